```python
import jax, jax.numpy as jnp
from jax import lax
import numpy as np

D_MODEL = 1024
BATCH = 8
SEQ = 2048
DEPTH = 2

GRID_W = 64
CTX_LEN = 256
N_MIXERS = 2
EPS = 1e-6
N_ADA = 6
POOL_WINDOWS = (2, 4, 8, 16)
N_POOL_GROUPS = 4
POOL_GROUP_DIM = D_MODEL // N_POOL_GROUPS
MLSTM_HEADS = 8
QK_DIM = D_MODEL // 2
V_DIM = D_MODEL
DK = QK_DIM // MLSTM_HEADS
DV = V_DIM // MLSTM_HEADS
CHUNK = 64
IN_PROJ_DIM = 2 * QK_DIM + 2 * V_DIM + 4 * MLSTM_HEADS
N_EXPERTS = 16
CAPACITY_FACTOR = 2
EXPERT_HIDDEN = 2 * D_MODEL
N_POOL_LAYERS = (DEPTH + 1) // 2
N_MLSTM_LAYERS = DEPTH // 2

kernel_name = "hybrid_pool_mlstm_ecmoe_dit"


def _rmsnorm(x, g):
    xf = x.astype(jnp.float32)
    y = xf * lax.rsqrt(jnp.mean(xf * xf, axis=-1, keepdims=True) + EPS)
    return (y * g.astype(jnp.float32)).astype(x.dtype)


def _modulate(h, shift, scale):
    return h * (1 + scale) + shift


def _box_mean(x, axis, w):
    n = x.shape[axis]
    s = jnp.cumsum(x.astype(jnp.float32), axis=axis)
    s = jnp.concatenate([jnp.zeros_like(lax.slice_in_dim(s, 0, 1, axis=axis)), s], axis=axis)
    t = np.arange(n)
    lo = np.clip(t - w // 2, 0, n)
    hi = np.clip(t + w - w // 2, 0, n)
    tot = jnp.take(s, hi, axis=axis) - jnp.take(s, lo, axis=axis)
    cshape = [1] * x.ndim
    cshape[axis] = n
    cnt = jnp.asarray((hi - lo).astype(np.float32)).reshape(cshape)
    return (tot / cnt).astype(x.dtype)


def _pool_mixer(hx, hc, w_pool, scale, rows):
    B, L, D = hx.shape
    g = hx.reshape(B, rows, GRID_W, N_POOL_GROUPS, POOL_GROUP_DIM)
    pooled = jnp.stack([_box_mean(_box_mean(g[..., j, :], 2, w), 1, w)
                        for j, w in enumerate(POOL_WINDOWS)], axis=3)
    yx = jnp.einsum('brwgc,gcd->brwgd', pooled - g, w_pool).reshape(B, L, D) * scale
    if hc is None:
        return yx, None
    Lc = hc.shape[1]
    gc = hc.reshape(B, Lc, N_POOL_GROUPS, POOL_GROUP_DIM)
    pooled_c = jnp.stack([_box_mean(gc[..., j, :], 1, w) for j, w in enumerate(POOL_WINDOWS)], axis=2)
    yc = jnp.einsum('blgc,gcd->blgd', pooled_c - gc, w_pool).reshape(B, Lc, D) * scale
    return yx, yc


def _mlstm_scan(q, k, v, ig, lf, state):
    B, H, T, _ = q.shape
    nc = T // CHUNK

    def to_chunks(a):
        return jnp.moveaxis(a.reshape(a.shape[:2] + (nc, CHUNK) + a.shape[3:]), 2, 0)

    tril = jnp.asarray(np.tril(np.ones((CHUNK, CHUNK), dtype=bool)))

    def step(carry, inp):
        C, n, m = carry
        qc, kc, vc, ic, fc = inp
        b = jnp.cumsum(fc, axis=-1)
        a = b + m[..., None]
        dmat = jnp.where(tril, b[..., :, None] - b[..., None, :] + ic[..., None, :], -jnp.inf)
        mj = jnp.maximum(a, jnp.max(dmat, axis=-1))
        w_inter = jnp.exp(a - mj)
        s = jnp.einsum('bhjd,bhsd->bhjs', qc, kc) * jnp.exp(dmat - mj[..., None])
        num = jnp.einsum('bhjs,bhsv->bhjv', s, vc) + w_inter[..., None] * jnp.einsum('bhvd,bhjd->bhjv', C, qc)
        den = jnp.sum(s, axis=-1) + w_inter * jnp.einsum('bhd,bhjd->bhj', n, qc)
        h = num / jnp.maximum(jnp.abs(den), jnp.exp(-mj))[..., None]
        b_last = b[..., -1]
        gl = b_last[..., None] - b + ic
        m_new = jnp.maximum(b_last + m, jnp.max(gl, axis=-1))
        ws = jnp.exp(gl - m_new[..., None])
        decay = jnp.exp(b_last + m - m_new)
        C_new = decay[..., None, None] * C + jnp.einsum('bhs,bhsv,bhsd->bhvd', ws, vc, kc)
        n_new = decay[..., None] * n + jnp.einsum('bhs,bhsd->bhd', ws, kc)
        return (C_new, n_new, m_new), h

    state, hs = lax.scan(step, state, tuple(to_chunks(a) for a in (q, k, v, ig, lf)))
    h = jnp.moveaxis(hs, 0, 2).reshape(B, H, T, -1)
    return h, state


def _mlstm_project(h, w_in, b_gates):
    B, T, _ = h.shape
    p = h @ w_in

    def heads(a, d):
        return jnp.swapaxes(a.reshape(B, T, MLSTM_HEADS, d), 1, 2).astype(jnp.float32)

    q = heads(p[..., :QK_DIM], DK)
    k = heads(p[..., QK_DIM:2 * QK_DIM], DK) * (DK ** -0.5)
    v = heads(p[..., 2 * QK_DIM:2 * QK_DIM + V_DIM], DV)
    o = jax.nn.sigmoid(p[..., 2 * QK_DIM + V_DIM:2 * QK_DIM + 2 * V_DIM])
    gates = (p[..., 2 * QK_DIM + 2 * V_DIM:] + b_gates).astype(jnp.float32)
    gates = gates.reshape(B, T, 4, MLSTM_HEADS).transpose(2, 0, 3, 1)
    return q, k, v, o, gates


def _mlstm_out(h, o, norm_g, w_out):
    B, H, T, _ = h.shape
    h = h * lax.rsqrt(jnp.mean(h * h, axis=-1, keepdims=True) + EPS)
    h = jnp.swapaxes(h, 1, 2).reshape(B, T, V_DIM) * norm_g.astype(jnp.float32)
    return (h.astype(o.dtype) * o) @ w_out


def _mlstm_mixer(hx, hc, w_in, b_gates, norm_g, w_out, ctx_out):
    B = hx.shape[0]
    qx, kx, vx, ox, gx = _mlstm_project(hx, w_in, b_gates)
    qc, kc, vc, oc, gc = _mlstm_project(hc, w_in, b_gates)
    zero = (jnp.zeros((B, MLSTM_HEADS, DV, DK), jnp.float32),
            jnp.zeros((B, MLSTM_HEADS, DK), jnp.float32),
            jnp.zeros((B, MLSTM_HEADS), jnp.float32))
    ls = jax.nn.log_sigmoid

    def flip(a):
        return jnp.flip(a, axis=2)

    hc_f, st_f = _mlstm_scan(qc, kc, vc, gc[0], ls(gc[1]), zero)
    hx_f, _ = _mlstm_scan(qx, kx, vx, gx[0], ls(gx[1]), st_f)
    hc_b, st_b = _mlstm_scan(flip(qc), flip(kc), flip(vc), flip(gc[2]), flip(ls(gc[3])), zero)
    hx_b, _ = _mlstm_scan(flip(qx), flip(kx), flip(vx), flip(gx[2]), flip(ls(gx[3])), st_b)
    yx = _mlstm_out(hx_f + flip(hx_b), ox, norm_g, w_out)
    yc = _mlstm_out(hc_f + flip(hc_b), oc, norm_g, w_out) if ctx_out else None
    return yx, yc


def _ec_moe(h, w_router, w_gate, w_up, w_down):
    B, T, D = h.shape
    cap = CAPACITY_FACTOR * T // N_EXPERTS
    aff = jax.nn.softmax(jnp.einsum('btd,de->bte', h, w_router).astype(jnp.float32), axis=-1)
    gate, idx = lax.top_k(jnp.swapaxes(aff, 1, 2), cap)
    xs = jax.vmap(lambda hb, ib: hb[ib])(h, idx)
    hid = jax.nn.silu(jnp.einsum('becd,edf->becf', xs, w_gate)) * jnp.einsum('becd,edf->becf', xs, w_up)
    ys = jnp.einsum('becf,efd->becd', hid, w_down) * gate[..., None].astype(h.dtype)
    return jax.vmap(lambda yb, ib: jnp.zeros((T, D), yb.dtype).at[ib.reshape(-1)].add(yb.reshape(-1, D)))(ys, idx)


def setup_inputs(seed: int = 0) -> dict:
    key = jax.random.key(seed)
    ks = jax.random.split(key, 19)

    def nrm(k, shape, s):
        return jax.random.normal(k, shape, jnp.float32) * s

    D = D_MODEL
    gate_offset = jnp.repeat(jnp.array([0.0, 3.0, 0.0, 3.0], jnp.float32), MLSTM_HEADS)
    return {
        "x": nrm(ks[0], (BATCH, SEQ, D), 1.0),
        "c": nrm(ks[1], (BATCH, D), 1.0),
        "ctx": nrm(ks[2], (BATCH, CTX_LEN, D), 1.0),
        "c_ctx": nrm(ks[3], (D,), 1.0),
        "ada_w": nrm(ks[4], (DEPTH, D, N_ADA * D), 0.5 * D ** -0.5),
        "ada_b": nrm(ks[5], (DEPTH, N_ADA * D), 0.02),
        "norm_mix": 1.0 + nrm(ks[6], (DEPTH, D), 0.02),
        "norm_ffn": 1.0 + nrm(ks[7], (DEPTH, D), 0.02),
        "pool_w": nrm(ks[8], (N_POOL_LAYERS, N_POOL_GROUPS, POOL_GROUP_DIM, POOL_GROUP_DIM), POOL_GROUP_DIM ** -0.5),
        "pool_scale": 1.0 + nrm(ks[9], (N_POOL_LAYERS, D), 0.1),
        "mlstm_w_in": nrm(ks[10], (N_MLSTM_LAYERS, D, IN_PROJ_DIM), D ** -0.5),
        "mlstm_b_gates": gate_offset[None] + nrm(ks[11], (N_MLSTM_LAYERS, 4 * MLSTM_HEADS), 0.1),
        "mlstm_norm": 1.0 + nrm(ks[12], (N_MLSTM_LAYERS, V_DIM), 0.02),
        "mlstm_w_out": nrm(ks[13], (N_MLSTM_LAYERS, V_DIM, D), V_DIM ** -0.5),
        "moe_router": nrm(ks[14], (DEPTH, D, N_EXPERTS), D ** -0.5),
        "moe_w_gate": nrm(ks[15], (DEPTH, N_EXPERTS, D, EXPERT_HIDDEN), D ** -0.5),
        "moe_w_up": nrm(ks[16], (DEPTH, N_EXPERTS, D, EXPERT_HIDDEN), D ** -0.5),
        "moe_w_down": nrm(ks[17], (DEPTH, N_EXPERTS, EXPERT_HIDDEN, D), EXPERT_HIDDEN ** -0.5),
        "final_norm": 1.0 + nrm(ks[18], (D,), 0.02),
    }


def reference(x, c, ctx, c_ctx, ada_w, ada_b, norm_mix, norm_ffn, pool_w, pool_scale,
              mlstm_w_in, mlstm_b_gates, mlstm_norm, mlstm_w_out,
              moe_router, moe_w_gate, moe_w_up, moe_w_down, final_norm):
    rows = x.shape[1] // GRID_W
    for i in range(DEPTH):
        last = i == DEPTH - 1
        j = i // N_MIXERS
        mx = jnp.split((jax.nn.silu(c) @ ada_w[i] + ada_b[i])[:, None, :], N_ADA, axis=-1)
        mc = jnp.split(jax.nn.silu(c_ctx) @ ada_w[i] + ada_b[i], N_ADA, axis=-1)
        hx = _modulate(_rmsnorm(x, norm_mix[i]), mx[0], mx[1])
        if i % N_MIXERS == 0:
            hc = None if last else _modulate(_rmsnorm(ctx, norm_mix[i]), mc[0], mc[1])
            yx, yc = _pool_mixer(hx, hc, pool_w[j], pool_scale[j], rows)
        else:
            hc = _modulate(_rmsnorm(ctx, norm_mix[i]), mc[0], mc[1])
            yx, yc = _mlstm_mixer(hx, hc, mlstm_w_in[j], mlstm_b_gates[j], mlstm_norm[j], mlstm_w_out[j],
                                  ctx_out=not last)
        x = x + mx[2] * yx
        hx = _modulate(_rmsnorm(x, norm_ffn[i]), mx[3], mx[4])
        x = x + mx[5] * _ec_moe(hx, moe_router[i], moe_w_gate[i], moe_w_up[i], moe_w_down[i])
        if not last:
            ctx = ctx + mc[2] * yc
            hc = _modulate(_rmsnorm(ctx, norm_ffn[i]), mc[3], mc[4])
            ctx = ctx + mc[5] * _ec_moe(hc, moe_router[i], moe_w_gate[i], moe_w_up[i], moe_w_down[i])
    return _rmsnorm(x, final_norm)
```

```python
import functools

import jax
import jax.numpy as jnp
import numpy as np
from jax import lax
from jax.experimental import pallas as pl
from jax.experimental.pallas import tpu as pltpu

F32 = jnp.float32
BF16 = jnp.bfloat16
HIGHEST = lax.Precision.HIGHEST

GRID_W = 64
EPS = 1e-6
N_ADA = 6
POOL_WINDOWS = (2, 4, 8, 16)
N_POOL_GROUPS = 4
MLSTM_HEADS = 8
N_EXPERTS = 16
CAPACITY_FACTOR = 2

MIB = 1024 * 1024
MXU_DIM = 256
SCAN_CHUNK = 128
POOL_PAD_ROWS = max(POOL_WINDOWS) // 2


def _params(sem, vmem_mib):
    return pltpu.CompilerParams(dimension_semantics=sem, vmem_limit_bytes=vmem_mib * MIB)


def _dot(a, b, precision=None):
    return jnp.dot(a, b, preferred_element_type=F32, precision=precision)


def _dot_nt(a, b, precision=None):
    return lax.dot_general(a, b, (((1,), (1,)), ((), ())), preferred_element_type=F32, precision=precision)


def _iota(shape, dim, dtype=jnp.int32):
    return lax.broadcasted_iota(dtype, shape, dim)


def _sigmoid(x):
    return 1.0 / (1.0 + jnp.exp(-x))


def _log_sigmoid(x):
    return jnp.minimum(x, 0.0) - jnp.log1p(jnp.exp(-jnp.abs(x)))


def _norm_mod(x, g, shift, scale):
    inv = lax.rsqrt(jnp.mean(x * x, axis=-1, keepdims=True) + EPS)
    return (x * inv * g) * (1.0 + scale) + shift


def _ada_kernel(c_ref, w_ref, b_ref, o_ref):
    c = c_ref[...]
    s = c * _sigmoid(c)
    o_ref[0] = _dot(s, w_ref[0], HIGHEST) + b_ref[0]


def _ada(cc, ada_w, ada_b):
    depth, d, n = ada_w.shape
    rows = cc.shape[0]
    tn = n // 4
    return pl.pallas_call(
        _ada_kernel,
        grid=(depth, n // tn),
        in_specs=[pl.BlockSpec((rows, d), lambda i, j: (0, 0)),
                  pl.BlockSpec((1, d, tn), lambda i, j: (i, 0, j)),
                  pl.BlockSpec((1, 1, tn), lambda i, j: (i, 0, j))],
        out_specs=pl.BlockSpec((1, rows, tn), lambda i, j: (i, 0, j)),
        out_shape=jax.ShapeDtypeStruct((depth, rows, n), F32),
        compiler_params=_params(("arbitrary", "arbitrary"), 40),
        name="ada",
    )(cc, ada_w, ada_b.reshape(depth, 1, n))


def _ffn_pre(xn, mod, nffn, wr_t):
    h2 = _norm_mod(xn, nffn, mod[3:4], mod[4:5])
    return h2.astype(BF16), _dot_nt(wr_t, h2, HIGHEST)


def _pool_kernel(x_ref, mod_ref, nmix_ref, nffn_ref, cmat_ref, cnt_ref, pw_ref, ps_ref, wrt_ref,
                 x1_ref, hx2_ref, lg_ref, inv_ref, *pad, two_d, tc):
    t, d = x_ref.shape[1], x_ref.shape[2]
    gd = d // N_POOL_GROUPS
    mod = mod_ref[0]
    pad_tok = POOL_PAD_ROWS * GRID_W

    for c0 in range(0, t, tc):
        xs = x_ref[0, c0:c0 + tc, :]
        inv_ref[c0:c0 + tc, :] = lax.rsqrt(jnp.mean(xs * xs, axis=-1, keepdims=True) + EPS)

    def hx_of(r0, rn, j):
        cs = slice(j * gd, (j + 1) * gd)
        xs = x_ref[0, r0:r0 + rn, cs]
        g = nmix_ref[:, cs] * (1.0 + mod[1:2, cs])
        return xs * inv_ref[r0:r0 + rn, :] * g + mod[0:1, cs]

    if two_d:
        pad_ref, = pad
        pad_ref[0:pad_tok, :] = jnp.zeros((pad_tok, gd), F32)
        pad_ref[pad_tok + t:pad_tok + t + pad_tok, :] = jnp.zeros((pad_tok, gd), F32)

    for j, w in enumerate(POOL_WINDOWS):
        cs = slice(j * gd, (j + 1) * gd)
        cm = cmat_ref[j]
        sums = []
        for b0 in range(0, t, MXU_DIM):
            g = hx_of(b0, MXU_DIM, j)
            g_hi = g.astype(BF16)
            g_lo = (g - g_hi.astype(F32)).astype(BF16)
            csum = _dot(cm, g_hi) + _dot(cm, g_lo)
            if two_d:
                pad_ref[pad_tok + b0:pad_tok + b0 + MXU_DIM, :] = csum
            else:
                sums.append(csum)
        for c0 in range(0, t, tc):
            if two_d:
                tot = None
                for dr in range(-(w // 2), w - w // 2):
                    o = pad_tok + c0 + dr * GRID_W
                    sl = pad_ref[o:o + tc, :]
                    tot = sl if tot is None else tot + sl
            else:
                tot = sums[c0 // MXU_DIM]
            mean = tot / cnt_ref[c0:c0 + tc, j:j + 1]
            diff = mean - hx_of(c0, tc, j)
            y = _dot(diff.astype(BF16), pw_ref[j]) * ps_ref[:, cs]
            x1_ref[0, c0:c0 + tc, cs] = x_ref[0, c0:c0 + tc, cs] + mod[2:3, cs] * y

    for c0 in range(0, t, tc):
        hb, lg = _ffn_pre(x1_ref[0, c0:c0 + tc, :], mod, nffn_ref[...], wrt_ref[...])
        hx2_ref[0, c0:c0 + tc, :] = hb
        lg_ref[0, :, c0:c0 + tc] = lg


def _pool_consts(t, two_d):
    cm = np.zeros((len(POOL_WINDOWS), MXU_DIM, MXU_DIM), np.float32)
    cnt = np.zeros((t, len(POOL_WINDOWS)), np.float32)
    n = GRID_W if two_d else t
    pos = np.arange(MXU_DIM)
    for j, w in enumerate(POOL_WINDOWS):
        col = pos % n
        lo = np.clip(col - w // 2, 0, n)
        hi = np.clip(col + w - w // 2, 0, n)
        same = (pos[:, None] // n) == (pos[None, :] // n)
        cm[j] = (same & (col[None, :] >= lo[:, None]) & (col[None, :] < hi[:, None])).astype(np.float32)
        tt = np.arange(t)
        c = tt % n
        ccnt = np.clip(c + w - w // 2, 0, n) - np.clip(c - w // 2, 0, n)
        if two_d:
            rows = t // GRID_W
            r = tt // GRID_W
            rcnt = np.clip(r + w - w // 2, 0, rows) - np.clip(r - w // 2, 0, rows)
            cnt[:, j] = ccnt * rcnt
        else:
            cnt[:, j] = ccnt
    return jnp.asarray(cm, BF16), jnp.asarray(cnt)


def _pool_mixer(x, mod, nmix, nffn, pool_w, pool_scale, wr_t, two_d):
    b, t, d = x.shape
    gd = d // N_POOL_GROUPS
    if not two_d:
        assert t == MXU_DIM
    cmat, cnt = _pool_consts(t, two_d)
    tc = min(t, 512)
    pad_tok = POOL_PAD_ROWS * GRID_W
    scratch = [pltpu.VMEM((t, 1), F32)]
    if two_d:
        scratch.append(pltpu.VMEM((t + 2 * pad_tok, gd), F32))
    e = wr_t.shape[0]
    full = lambda shape: pl.BlockSpec(shape, lambda i: (0,) * len(shape))
    return pl.pallas_call(
        functools.partial(_pool_kernel, two_d=two_d, tc=tc),
        grid=(b,),
        in_specs=[pl.BlockSpec((1, t, d), lambda i: (i, 0, 0)),
                  pl.BlockSpec((1, 8, d), lambda i: (i, 0, 0)),
                  full((1, d)), full((1, d)), full(cmat.shape), full(cnt.shape),
                  full(pool_w.shape), full((1, d)), full(wr_t.shape)],
        out_specs=[pl.BlockSpec((1, t, d), lambda i: (i, 0, 0)),
                   pl.BlockSpec((1, t, d), lambda i: (i, 0, 0)),
                   pl.BlockSpec((1, e, t), lambda i: (i, 0, 0))],
        out_shape=[jax.ShapeDtypeStruct((b, t, d), F32),
                   jax.ShapeDtypeStruct((b, t, d), BF16),
                   jax.ShapeDtypeStruct((b, e, t), F32)],
        scratch_shapes=scratch,
        compiler_params=_params(("arbitrary",), 60),
        name="pool_mixer_2d" if two_d else "pool_mixer_1d",
    )(x, mod, nmix, nffn, cmat, cnt, pool_w, pool_scale, wr_t)


def _route_kernel(lg_ref, slot_ref, gate_ref, *, cap):
    lg = lg_ref[0]
    e, t = lg.shape
    ex = jnp.exp(lg - jnp.max(lg, axis=0, keepdims=True))
    aff = ex / jnp.sum(ex, axis=0, keepdims=True)
    capf = jnp.float32(cap)

    def as_f32(v):
        return lax.bitcast_convert_type(v, F32)

    def bisect(i, v):
        cand = v | jnp.left_shift(jnp.int32(1), 30 - i)
        cnt = jnp.sum(jnp.where(aff >= as_f32(cand), 1.0, 0.0), axis=1, keepdims=True)
        return jnp.where(cnt >= capf, cand, v)

    kth = lax.fori_loop(0, 31, bisect, jnp.zeros((e, 1), jnp.int32))
    above = as_f32(kth + 1)
    gt = aff >= above
    eq = (aff >= as_f32(kth)) & jnp.logical_not(gt)
    need = capf - jnp.sum(jnp.where(gt, 1.0, 0.0), axis=1, keepdims=True)

    blk = min(t, MXU_DIM)
    before = (_iota((blk, blk), 0) < _iota((blk, blk), 1)).astype(BF16)

    def excl_cumsum(mask):
        ones = jnp.where(mask, 1.0, 0.0)
        outs, run = [], jnp.zeros((e, 1), F32)
        for b0 in range(0, t, blk):
            mb = ones[:, b0:b0 + blk]
            outs.append(_dot(mb.astype(BF16), before) + run)
            run = run + jnp.sum(mb, axis=1, keepdims=True)
        return jnp.concatenate(outs, axis=1) if len(outs) > 1 else outs[0]

    sel = gt | (eq & (excl_cumsum(eq) < need))
    slot_ref[0] = jnp.where(sel, excl_cumsum(sel), -1.0)
    gate_ref[0] = aff


def _route(logits, cap):
    b, e, t = logits.shape
    spec = pl.BlockSpec((1, e, t), lambda i: (i, 0, 0))
    return pl.pallas_call(
        functools.partial(_route_kernel, cap=cap),
        grid=(b,),
        in_specs=[spec],
        out_specs=[spec, spec],
        out_shape=[jax.ShapeDtypeStruct((b, e, t), F32)] * 2,
        compiler_params=_params(("arbitrary",), 32),
        name="route",
    )(logits)


def _gather_kernel(h_ref, slot_ref, o_ref, *, cap):
    hx = h_ref[0]
    t = hx.shape[0]
    row = _iota((cap, t), 0).astype(F32)
    for i in range(o_ref.shape[0]):
        onehot = jnp.where(slot_ref[0, i:i + 1, :] == row, 1.0, 0.0).astype(BF16)
        o_ref[i] = _dot(onehot, hx).astype(BF16)


def _gather(hx2, slot, cap):
    b, t, d = hx2.shape
    e = slot.shape[1]
    eb = 8
    return pl.pallas_call(
        functools.partial(_gather_kernel, cap=cap),
        grid=(b, e // eb),
        in_specs=[pl.BlockSpec((1, t, d), lambda i, j: (i, 0, 0)),
                  pl.BlockSpec((1, eb, t), lambda i, j: (i, j, 0))],
        out_specs=pl.BlockSpec((eb, cap, d), lambda i, j: (j, i, 0)),
        out_shape=jax.ShapeDtypeStruct((e, b * cap, d), BF16),
        compiler_params=_params(("arbitrary", "arbitrary"), 48),
        name="gather",
    )(hx2, slot)


def _ffn_kernel(*refs, n_sets, mc):
    x_refs = refs[:n_sets]
    wg_ref, wu_ref, wd_ref = refs[n_sets:n_sets + 3]
    y_refs = refs[n_sets + 3:2 * n_sets + 3]
    acc_refs = refs[2 * n_sets + 3:3 * n_sets + 3]
    wgb, wub, wdb = refs[3 * n_sets + 3:]
    f = pl.program_id(1)
    wgb[...] = wg_ref[0].astype(BF16)
    wub[...] = wu_ref[0].astype(BF16)
    wdb[...] = wd_ref[0].astype(BF16)
    for x_ref, y_ref, acc in zip(x_refs, y_refs, acc_refs):
        m = x_ref.shape[1]
        step = min(m, mc)
        for m0 in range(0, m, step):
            xs = x_ref[0, m0:m0 + step, :]
            hg = _dot(xs, wgb[...])
            hu = _dot(xs, wub[...])
            hid = (hg * _sigmoid(hg) * hu).astype(BF16)
            part = _dot(hid, wdb[...])

            @pl.when(f == 0)
            def _():
                acc[m0:m0 + step, :] = part

            @pl.when(f > 0)
            def _():
                acc[m0:m0 + step, :] += part

        @pl.when(f == pl.num_programs(1) - 1)
        def _():
            y_ref[0] = acc[...].astype(BF16)


def _expert_ffn(xs, w_gate, w_up, w_down):
    e, d, hidden = w_gate.shape
    tf = 512
    n = len(xs)
    xspecs = [pl.BlockSpec((1, x.shape[1], d), lambda i, j: (i, 0, 0)) for x in xs]
    return pl.pallas_call(
        functools.partial(_ffn_kernel, n_sets=n, mc=512),
        grid=(e, hidden // tf),
        in_specs=xspecs + [pl.BlockSpec((1, d, tf), lambda i, j: (i, 0, j)),
                           pl.BlockSpec((1, d, tf), lambda i, j: (i, 0, j)),
                           pl.BlockSpec((1, tf, d), lambda i, j: (i, j, 0))],
        out_specs=xspecs,
        out_shape=[jax.ShapeDtypeStruct(x.shape, BF16) for x in xs],
        scratch_shapes=[pltpu.VMEM((x.shape[1], d), F32) for x in xs]
        + [pltpu.VMEM((d, tf), BF16), pltpu.VMEM((d, tf), BF16), pltpu.VMEM((tf, d), BF16)],
        compiler_params=_params(("arbitrary", "arbitrary"), 60),
        name="expert_ffn",
    )(*xs, w_gate, w_up, w_down)


def _scatter_kernel(x_ref, y_ref, slot_ref, gate_ref, mod_ref, fn_ref, o_ref, *, final):
    tt = x_ref.shape[1]
    e, cap, _ = y_ref.shape
    eye = (_iota((tt, tt), 0) == _iota((tt, tt), 1)).astype(BF16)
    slot_t = _dot_nt(eye, (slot_ref[0] + 1.0).astype(BF16))
    gate_t = _dot_nt(eye, gate_ref[0].astype(BF16))
    lane = _iota((tt, cap), 1).astype(F32) + 1.0
    acc = None
    for i in range(e):
        pt = jnp.where(slot_t[:, i:i + 1] == lane, gate_t[:, i:i + 1], 0.0).astype(BF16)
        part = _dot(pt, y_ref[i])
        acc = part if acc is None else acc + part
    out = x_ref[0] + mod_ref[0][5:6] * acc
    if final:
        out = out * lax.rsqrt(jnp.mean(out * out, axis=-1, keepdims=True) + EPS) * fn_ref[...]
    o_ref[0] = out


def _scatter(x, y, slot, gate, mod, final_norm, cap, final):
    b, t, d = x.shape
    e = slot.shape[1]
    tt = min(t, 512)
    return pl.pallas_call(
        functools.partial(_scatter_kernel, final=final),
        grid=(b, t // tt),
        in_specs=[pl.BlockSpec((1, tt, d), lambda i, j: (i, j, 0)),
                  pl.BlockSpec((e, cap, d), lambda i, j: (0, i, 0)),
                  pl.BlockSpec((1, e, tt), lambda i, j: (i, 0, j)),
                  pl.BlockSpec((1, e, tt), lambda i, j: (i, 0, j)),
                  pl.BlockSpec((1, 8, d), lambda i, j: (i, 0, 0)),
                  pl.BlockSpec((1, d), lambda i, j: (0, 0))],
        out_specs=pl.BlockSpec((1, tt, d), lambda i, j: (i, j, 0)),
        out_shape=jax.ShapeDtypeStruct((b, t, d), F32),
        compiler_params=_params(("arbitrary", "arbitrary"), 48),
        name="scatter",
    )(x, y, slot, gate, mod, final_norm)


def _moe(streams, w_router_unused, w_gate, w_up, w_down, final_norm, final):
    routed = []
    for x1, hx2, logits, mod in streams:
        t = x1.shape[1]
        cap = CAPACITY_FACTOR * t // N_EXPERTS
        slot, gate = _route(logits, cap)
        routed.append((slot, gate, cap, _gather(hx2, slot, cap)))
    ys = _expert_ffn([r[3] for r in routed], w_gate, w_up, w_down)
    outs = []
    for (x1, _, _, mod), (slot, gate, cap, _), y in zip(streams, routed, ys):
        outs.append(_scatter(x1, y, slot, gate, mod, final_norm, cap, final))
    return outs


def _inproj_kernel(x_ref, mod_ref, nmix_ref, win_ref, wg_ref, wgt_ref, bg_ref, bgt_ref,
                   q_ref, k_ref, v_ref, o_ref, gc_ref, gr_ref, *, chunk, qk, dk):
    tt = x_ref.shape[1]
    mod = mod_ref[0]
    h = _norm_mod(x_ref[0], nmix_ref[...], mod[0:1], mod[1:2])
    p = _dot(h.astype(BF16), win_ref[...])
    vd = (p.shape[1] - 2 * qk) // 2
    q_ref[0] = p[:, :qk].astype(BF16)
    k_ref[0] = (p[:, qk:2 * qk] * (dk ** -0.5)).astype(BF16)
    v_ref[0] = p[:, 2 * qk:2 * qk + vd].astype(BF16)
    o_ref[0] = _sigmoid(p[:, 2 * qk + vd:])

    nh = MLSTM_HEADS
    g = _dot(h, wg_ref[...], HIGHEST) + bg_ref[...]
    g_t = _dot_nt(wgt_ref[...], h, HIGHEST) + bgt_ref[...]
    col = _iota(g.shape, 1)
    row = _iota(g_t.shape, 0)
    is_f_col = (col & nh) == nh
    is_f_row = (row & nh) == nh
    lf = jnp.where(is_f_col, _log_sigmoid(g), 0.0)
    lf_t = jnp.where(is_f_row, _log_sigmoid(g_t), 0.0)
    a = _iota((tt, tt), 0)
    c = _iota((tt, tt), 1)
    same = (a & -chunk) == (c & -chunk)
    le = jnp.where(same & (a <= c), 1.0, 0.0)
    ge = jnp.where(same & (a >= c), 1.0, 0.0)
    pre_c = _dot(ge, lf, HIGHEST)
    suf_c = _dot(le, lf, HIGHEST)
    gc_ref[0] = jnp.where(col < 2 * nh, pre_c, suf_c)
    pre_r = _dot(lf_t, le, HIGHEST)
    suf_r = _dot(lf_t, ge, HIGHEST)
    gr_ref[0] = jnp.where(is_f_row, jnp.where(row < 2 * nh, pre_r, suf_r), g_t)


def _inproj(x, mod, nmix, w_main, w_g, w_gt, b_g, b_gt, qk, vd):
    b, t, d = x.shape
    tt = min(t, 512)
    ng = w_g.shape[1]
    full = lambda shape: pl.BlockSpec(shape, lambda i, j: (0,) * len(shape))
    tok = lambda n: pl.BlockSpec((1, tt, n), lambda i, j: (i, j, 0))
    return pl.pallas_call(
        functools.partial(_inproj_kernel, chunk=SCAN_CHUNK, qk=qk, dk=qk // MLSTM_HEADS),
        grid=(b, t // tt),
        in_specs=[tok(d), pl.BlockSpec((1, 8, d), lambda i, j: (i, 0, 0)), full((1, d)),
                  full(w_main.shape), full(w_g.shape), full(w_gt.shape), full(b_g.shape), full(b_gt.shape)],
        out_specs=[tok(qk), tok(qk), tok(vd), tok(vd), tok(ng),
                   pl.BlockSpec((1, ng, tt), lambda i, j: (i, 0, j))],
        out_shape=[jax.ShapeDtypeStruct((b, t, qk), BF16), jax.ShapeDtypeStruct((b, t, qk), BF16),
                   jax.ShapeDtypeStruct((b, t, vd), BF16), jax.ShapeDtypeStruct((b, t, vd), F32),
                   jax.ShapeDtypeStruct((b, t, ng), F32), jax.ShapeDtypeStruct((b, ng, t), F32)],
        compiler_params=_params(("arbitrary", "arbitrary"), 56),
        name="mlstm_inproj",
    )(x, mod, nmix, w_main, w_g, w_gt, b_g, b_gt)


def _scan_kernel(qc_ref, kc_ref, vc_ref, gcc_ref, grc_ref, qx_ref, kx_ref, vx_ref, gcx_ref, grx_ref,
                 out_ref, hacc_ref, *, chunk):
    ln = chunk
    nh = MLSTM_HEADS
    hd = pl.program_id(1)
    dv = vx_ref.shape[2]
    kl = kx_ref.shape[2]
    lane = _iota((ln, kl), 1)
    half = (hd % 2) * (kl // 2)
    kmask = (lane >= half) & (lane < half + kl // 2)
    ones_col = jnp.where(_iota((ln, dv), 1) == 0, 1.0, 0.0).astype(BF16)
    ri = _iota((ln, ln), 0)
    ci = _iota((ln, ln), 1)
    gcol = _iota((ln, 4 * nh), 1)

    def step(refs, c, carry, fwd, emit):
        q_ref, k_ref, v_ref, gc_ref, gr_ref = refs
        s_state, m = carry
        t0 = pl.multiple_of(c * ln, ln)
        q = q_ref[0, pl.ds(t0, ln), :]
        kh = jnp.where(kmask, k_ref[0, pl.ds(t0, ln), :], jnp.zeros((), BF16))
        vaug = jnp.concatenate([v_ref[0, pl.ds(t0, ln), :], ones_col], axis=1)
        base = 0 if fwd else 2 * nh
        b_col = jnp.sum(jnp.where(gcol == base + nh + hd, gc_ref[0, pl.ds(t0, ln), :], 0.0),
                        axis=1, keepdims=True)
        ig_row = gr_ref[0, base + hd, pl.ds(c, 1), :]
        b_row = gr_ref[0, base + nh + hd, pl.ds(c, 1), :]
        a = b_col + m
        dm = jnp.where((ci <= ri) if fwd else (ci >= ri), b_col - b_row + ig_row, -jnp.inf)
        mj = jnp.maximum(a, jnp.max(dm, axis=1, keepdims=True))
        sm = (_dot_nt(q, kh) * jnp.exp(dm - mj)).astype(BF16)
        num = _dot(sm, vaug) + jnp.exp(a - mj) * _dot(q, s_state.astype(BF16))
        b_end = b_row[:, ln - 1:ln] if fwd else b_row[:, 0:1]
        gl = b_end - b_row + ig_row
        m_new = jnp.maximum(b_end + m, jnp.max(gl, axis=1, keepdims=True))
        ws = jnp.exp(gl - m_new)
        k_t = kh.astype(F32).T
        s_new = jnp.exp(b_end + m - m_new) * s_state + _dot((k_t * ws).astype(BF16), vaug)
        if emit:
            den = num[:, dv:dv + 1]
            h = num[:, :dv] / jnp.maximum(jnp.abs(den), jnp.exp(-mj))
            if fwd:
                hacc_ref[pl.ds(t0, ln), :] = h
            else:
                hacc_ref[pl.ds(t0, ln), :] += h
        return s_new, m_new

    ctx = (qc_ref, kc_ref, vc_ref, gcc_ref, grc_ref)
    lat = (qx_ref, kx_ref, vx_ref, gcx_ref, grx_ref)
    ncc = qc_ref.shape[1] // ln
    ncx = qx_ref.shape[1] // ln
    for fwd in (True, False):
        carry = (jnp.zeros((kl, 2 * dv), F32), jnp.zeros((1, 1), F32))
        for refs, n, emit in ((ctx, ncc, False), (lat, ncx, True)):
            carry = lax.fori_loop(
                0, n,
                lambda i, cr, refs=refs, n=n, emit=emit, fwd=fwd: step(refs, i if fwd else n - 1 - i, cr, fwd, emit),
                carry)
    hs = hacc_ref[...]
    out_ref[0] = hs * lax.rsqrt(jnp.mean(hs * hs, axis=-1, keepdims=True) + EPS)


def _scan(ctx_parts, lat_parts):
    qc, kc, vc, gcc, grc = ctx_parts
    qx, kx, vx, gcx, grx = lat_parts
    b, t, vd = vx.shape
    nh = MLSTM_HEADS
    dv = vd // nh
    kl = 2 * (qx.shape[2] // nh)

    def specs(q, v, gc, gr):
        tq = q.shape[1]
        return [pl.BlockSpec((1, tq, kl), lambda i, j: (i, 0, j // 2)),
                pl.BlockSpec((1, tq, kl), lambda i, j: (i, 0, j // 2)),
                pl.BlockSpec((1, tq, dv), lambda i, j: (i, 0, j)),
                pl.BlockSpec((1, tq, gc.shape[2]), lambda i, j: (i, 0, 0)),
                pl.BlockSpec((1,) + gr.shape[1:], lambda i, j: (i, 0, 0, 0))]

    return pl.pallas_call(
        functools.partial(_scan_kernel, chunk=SCAN_CHUNK),
        grid=(b, nh),
        in_specs=specs(qc, vc, gcc, grc) + specs(qx, vx, gcx, grx),
        out_specs=pl.BlockSpec((1, t, dv), lambda i, j: (i, 0, j)),
        out_shape=jax.ShapeDtypeStruct((b, t, vd), F32),
        scratch_shapes=[pltpu.VMEM((t, dv), F32)],
        compiler_params=_params(("arbitrary", "arbitrary"), 32),
        name="mlstm_scan",
    )(qc, kc, vc, gcc, grc, qx, kx, vx, gcx, grx)


def _mlstm_out_kernel(hn_ref, og_ref, x_ref, mod_ref, mn_ref, wout_ref, nffn_ref, wrt_ref,
                      x1_ref, hx2_ref, lg_ref):
    mod = mod_ref[0]
    a = (hn_ref[0] * mn_ref[...]) * og_ref[0]
    xn = x_ref[0] + mod[2:3] * _dot(a.astype(BF16), wout_ref[...])
    x1_ref[0] = xn
    hb, lg = _ffn_pre(xn, mod, nffn_ref[...], wrt_ref[...])
    hx2_ref[0] = hb
    lg_ref[0] = lg


def _mlstm_out(hn, og, x, mod, mnorm, w_out, nffn, wr_t):
    b, t, d = x.shape
    vd = hn.shape[2]
    e = wr_t.shape[0]
    tt = min(t, 512)
    full = lambda shape: pl.BlockSpec(shape, lambda i, j: (0,) * len(shape))
    tok = lambda n: pl.BlockSpec((1, tt, n), lambda i, j: (i, j, 0))
    return pl.pallas_call(
        _mlstm_out_kernel,
        grid=(b, t // tt),
        in_specs=[tok(vd), tok(vd), tok(d), pl.BlockSpec((1, 8, d), lambda i, j: (i, 0, 0)),
                  full((1, vd)), full(w_out.shape), full((1, d)), full(wr_t.shape)],
        out_specs=[tok(d), tok(d), pl.BlockSpec((1, e, tt), lambda i, j: (i, 0, j))],
        out_shape=[jax.ShapeDtypeStruct((b, t, d), F32), jax.ShapeDtypeStruct((b, t, d), BF16),
                   jax.ShapeDtypeStruct((b, e, t), F32)],
        compiler_params=_params(("arbitrary", "arbitrary"), 48),
        name="mlstm_out",
    )(hn, og, x, mod, mnorm, w_out, nffn, wr_t)


def _chunked_rows(gr):
    b, n, t = gr.shape
    return gr.reshape(b, n, t // SCAN_CHUNK, SCAN_CHUNK)


def kernel(x, c, ctx, c_ctx, ada_w, ada_b, norm_mix, norm_ffn, pool_w, pool_scale, mlstm_w_in, mlstm_b_gates,
           mlstm_norm, mlstm_w_out, moe_router, moe_w_gate, moe_w_up, moe_w_down, final_norm):
    bsz, seq, d = x.shape
    depth = ada_w.shape[0]
    n_mixers = 2

    cc = jnp.concatenate([c, c_ctx[None, :], jnp.zeros((16 - bsz - 1, d), F32)], axis=0)
    ada = _ada(cc, ada_w, ada_b).reshape(depth, 16, N_ADA, d)
    pad = jnp.zeros((bsz, 8 - N_ADA, d), F32)

    fn = final_norm.reshape(1, d)
    for i in range(depth):
        last = i == depth - 1
        j = i // n_mixers
        mod_x = jnp.concatenate([ada[i, :bsz], pad], axis=1)
        mod_c = jnp.concatenate([jnp.broadcast_to(ada[i, bsz][None], (bsz, N_ADA, d)), pad], axis=1)
        nmix = norm_mix[i].reshape(1, d)
        nffn = norm_ffn[i].reshape(1, d)
        wr_t = moe_router[i].T
        streams = []
        if i % n_mixers == 0:
            pw = pool_w[j].astype(BF16)
            ps = pool_scale[j].reshape(1, d)
            x1, hx2, lg = _pool_mixer(x, mod_x, nmix, nffn, pw, ps, wr_t, True)
            streams.append((x1, hx2, lg, mod_x))
            if not last:
                c1, hc2, lgc = _pool_mixer(ctx, mod_c, nmix, nffn, pw, ps, wr_t, False)
                streams.append((c1, hc2, lgc, mod_c))
        else:
            qk = mlstm_w_in.shape[2] - 2 * mlstm_w_out.shape[1] - 4 * MLSTM_HEADS
            qk //= 2
            vd = mlstm_w_out.shape[1]
            w_main = mlstm_w_in[j][:, :2 * qk + 2 * vd].astype(BF16)
            w_g = mlstm_w_in[j][:, 2 * qk + 2 * vd:]
            b_g = mlstm_b_gates[j].reshape(1, -1)
            proj = lambda s, m: _inproj(s, m, nmix, w_main, w_g, w_g.T, b_g, b_g.T, qk, vd)
            qc, kc, vc, oc, gcc, grc = proj(ctx, mod_c)
            qx, kx, vx, ox, gcx, grx = proj(x, mod_x)
            hn = _scan((qc, kc, vc, gcc, _chunked_rows(grc)), (qx, kx, vx, gcx, _chunked_rows(grx)))
            x1, hx2, lg = _mlstm_out(hn, ox, x, mod_x, mlstm_norm[j].reshape(1, vd),
                                     mlstm_w_out[j].astype(BF16), nffn, wr_t)
            streams.append((x1, hx2, lg, mod_x))
            assert last, "context output of the mLSTM mixer is only needed by a following layer"
        outs = _moe(streams, None, moe_w_gate[i], moe_w_up[i], moe_w_down[i], fn, last)
        x = outs[0]
        if not last:
            ctx = outs[1]
    return x
```

```python
import functools

import jax
import jax.numpy as jnp
import numpy as np
from jax import lax
from jax.experimental import pallas as pl
from jax.experimental.pallas import tpu as pltpu

F32 = jnp.float32
BF16 = jnp.bfloat16
HIGHEST = lax.Precision.HIGHEST

GRID_W = 64
EPS = 1e-6
N_ADA = 6
POOL_WINDOWS = (2, 4, 8, 16)
N_POOL_GROUPS = 4
MLSTM_HEADS = 8
N_EXPERTS = 16
CAPACITY_FACTOR = 2

MIB = 1024 * 1024
MXU_DIM = 256
SCAN_CHUNK = 128
POOL_PAD_ROWS = max(POOL_WINDOWS) // 2


def _params(sem, vmem_mib):
    return pltpu.CompilerParams(dimension_semantics=sem, vmem_limit_bytes=vmem_mib * MIB)


def _dot(a, b, precision=None):
    return jnp.dot(a, b, preferred_element_type=F32, precision=precision)


def _dot_nt(a, b, precision=None):
    return lax.dot_general(a, b, (((1,), (1,)), ((), ())), preferred_element_type=F32, precision=precision)


def _iota(shape, dim, dtype=jnp.int32):
    return lax.broadcasted_iota(dtype, shape, dim)


def _sigmoid(x):
    return 1.0 / (1.0 + jnp.exp(-x))


def _log_sigmoid(x):
    return jnp.minimum(x, 0.0) - jnp.log1p(jnp.exp(-jnp.abs(x)))


def _norm_mod(x, g, shift, scale):
    inv = lax.rsqrt(jnp.mean(x * x, axis=-1, keepdims=True) + EPS)
    return (x * inv * g) * (1.0 + scale) + shift


def _ada_kernel(c_ref, w_ref, b_ref, o_ref):
    c = c_ref[...]
    s = c * _sigmoid(c)
    o_ref[0] = _dot(s, w_ref[0], HIGHEST) + b_ref[0]


def _ada(cc, ada_w, ada_b):
    depth, d, n = ada_w.shape
    rows = cc.shape[0]
    tn = n // 4
    return pl.pallas_call(
        _ada_kernel,
        grid=(depth, n // tn),
        in_specs=[pl.BlockSpec((rows, d), lambda i, j: (0, 0)),
                  pl.BlockSpec((1, d, tn), lambda i, j: (i, 0, j)),
                  pl.BlockSpec((1, 1, tn), lambda i, j: (i, 0, j))],
        out_specs=pl.BlockSpec((1, rows, tn), lambda i, j: (i, 0, j)),
        out_shape=jax.ShapeDtypeStruct((depth, rows, n), F32),
        compiler_params=_params(("arbitrary", "arbitrary"), 40),
        name="ada",
    )(cc, ada_w, ada_b.reshape(depth, 1, n))


def _split_bf16(x, pieces):
    out = []
    for _ in range(pieces):
        p = x.astype(BF16)
        out.append(p)
        x = x - p.astype(F32)
    return out


def _dot_split(a, b):
    a_hi, a_lo = _split_bf16(a, 2)
    b_hi, b_lo = _split_bf16(b, 2)
    return _dot(a_hi, b_hi) + (_dot(a_hi, b_lo) + _dot(a_lo, b_hi))


def _transpose_exact(x):
    m = x.shape[1]
    eye = (_iota((m, m), 0) == _iota((m, m), 1)).astype(BF16)
    hi, mid, lo = _split_bf16(x, 3)
    return _dot_nt(eye, hi) + (_dot_nt(eye, mid) + _dot_nt(eye, lo))


def _ffn_pre(xn, mod, nffn, wr):
    h2 = _norm_mod(xn, nffn, mod[3:4], mod[4:5])
    return h2.astype(BF16), _transpose_exact(_dot_split(h2, wr))


def _pool_kernel(x_ref, mod_ref, nmix_ref, nffn_ref, cmat_ref, cnt_ref, pw_ref, ps_ref, wr_ref,
                 x1_ref, hx2_ref, lg_ref, inv_ref, *pad, two_d, tc):
    t, d = x_ref.shape[1], x_ref.shape[2]
    gd = d // N_POOL_GROUPS
    mod = mod_ref[0]
    pad_tok = POOL_PAD_ROWS * GRID_W

    for c0 in range(0, t, tc):
        xs = x_ref[0, c0:c0 + tc, :]
        inv_ref[c0:c0 + tc, :] = lax.rsqrt(jnp.mean(xs * xs, axis=-1, keepdims=True) + EPS)

    def hx_of(r0, rn, j):
        cs = slice(j * gd, (j + 1) * gd)
        xs = x_ref[0, r0:r0 + rn, cs]
        g = nmix_ref[:, cs] * (1.0 + mod[1:2, cs])
        return xs * inv_ref[r0:r0 + rn, :] * g + mod[0:1, cs]

    if two_d:
        pad_ref, = pad
        pad_ref[0:pad_tok, :] = jnp.zeros((pad_tok, gd), F32)
        pad_ref[pad_tok + t:pad_tok + t + pad_tok, :] = jnp.zeros((pad_tok, gd), F32)

    for j, w in enumerate(POOL_WINDOWS):
        cs = slice(j * gd, (j + 1) * gd)
        cm = cmat_ref[j]
        sums = []
        for b0 in range(0, t, MXU_DIM):
            g = hx_of(b0, MXU_DIM, j)
            g_hi = g.astype(BF16)
            g_lo = (g - g_hi.astype(F32)).astype(BF16)
            csum = _dot(cm, g_hi) + _dot(cm, g_lo)
            if two_d:
                pad_ref[pad_tok + b0:pad_tok + b0 + MXU_DIM, :] = csum
            else:
                sums.append(csum)
        for c0 in range(0, t, tc):
            if two_d:
                tot = None
                for dr in range(-(w // 2), w - w // 2):
                    o = pad_tok + c0 + dr * GRID_W
                    sl = pad_ref[o:o + tc, :]
                    tot = sl if tot is None else tot + sl
            else:
                tot = sums[c0 // MXU_DIM]
            mean = tot / cnt_ref[c0:c0 + tc, j:j + 1]
            diff = mean - hx_of(c0, tc, j)
            y = _dot(diff.astype(BF16), pw_ref[j]) * ps_ref[:, cs]
            x1_ref[0, c0:c0 + tc, cs] = x_ref[0, c0:c0 + tc, cs] + mod[2:3, cs] * y

    for c0 in range(0, t, tc):
        hb, lg = _ffn_pre(x1_ref[0, c0:c0 + tc, :], mod, nffn_ref[...], wr_ref[...])
        hx2_ref[0, c0:c0 + tc, :] = hb
        lg_ref[0, :, c0:c0 + tc] = lg


def _pool_consts(t, two_d):
    cm = np.zeros((len(POOL_WINDOWS), MXU_DIM, MXU_DIM), np.float32)
    cnt = np.zeros((t, len(POOL_WINDOWS)), np.float32)
    n = GRID_W if two_d else t
    pos = np.arange(MXU_DIM)
    for j, w in enumerate(POOL_WINDOWS):
        col = pos % n
        lo = np.clip(col - w // 2, 0, n)
        hi = np.clip(col + w - w // 2, 0, n)
        same = (pos[:, None] // n) == (pos[None, :] // n)
        cm[j] = (same & (col[None, :] >= lo[:, None]) & (col[None, :] < hi[:, None])).astype(np.float32)
        tt = np.arange(t)
        c = tt % n
        ccnt = np.clip(c + w - w // 2, 0, n) - np.clip(c - w // 2, 0, n)
        if two_d:
            rows = t // GRID_W
            r = tt // GRID_W
            rcnt = np.clip(r + w - w // 2, 0, rows) - np.clip(r - w // 2, 0, rows)
            cnt[:, j] = ccnt * rcnt
        else:
            cnt[:, j] = ccnt
    return jnp.asarray(cm, BF16), jnp.asarray(cnt)


def _pool_mixer(x, mod, nmix, nffn, pool_w, pool_scale, w_router, two_d):
    b, t, d = x.shape
    gd = d // N_POOL_GROUPS
    if not two_d:
        assert t == MXU_DIM
    cmat, cnt = _pool_consts(t, two_d)
    tc = min(t, 512)
    pad_tok = POOL_PAD_ROWS * GRID_W
    scratch = [pltpu.VMEM((t, 1), F32)]
    if two_d:
        scratch.append(pltpu.VMEM((t + 2 * pad_tok, gd), F32))
    e = w_router.shape[1]
    full = lambda shape: pl.BlockSpec(shape, lambda i: (0,) * len(shape))
    return pl.pallas_call(
        functools.partial(_pool_kernel, two_d=two_d, tc=tc),
        grid=(b,),
        in_specs=[pl.BlockSpec((1, t, d), lambda i: (i, 0, 0)),
                  pl.BlockSpec((1, 8, d), lambda i: (i, 0, 0)),
                  full((1, d)), full((1, d)), full(cmat.shape), full(cnt.shape),
                  full(pool_w.shape), full((1, d)), full(w_router.shape)],
        out_specs=[pl.BlockSpec((1, t, d), lambda i: (i, 0, 0)),
                   pl.BlockSpec((1, t, d), lambda i: (i, 0, 0)),
                   pl.BlockSpec((1, e, t), lambda i: (i, 0, 0))],
        out_shape=[jax.ShapeDtypeStruct((b, t, d), F32),
                   jax.ShapeDtypeStruct((b, t, d), BF16),
                   jax.ShapeDtypeStruct((b, e, t), F32)],
        scratch_shapes=scratch,
        compiler_params=_params(("arbitrary",), 60),
        name="pool_mixer_2d" if two_d else "pool_mixer_1d",
    )(x, mod, nmix, nffn, cmat, cnt, pool_w, pool_scale, w_router)


def _route_kernel(lg_ref, slot_ref, gate_ref, *, cap):
    lg = lg_ref[0]
    e, t = lg.shape
    ex = jnp.exp(lg - jnp.max(lg, axis=0, keepdims=True))
    aff = ex / jnp.sum(ex, axis=0, keepdims=True)
    capf = jnp.float32(cap)

    def as_f32(v):
        return lax.bitcast_convert_type(v, F32)

    def bisect(i, v):
        cand = v | jnp.left_shift(jnp.int32(1), 30 - i)
        cnt = jnp.sum(jnp.where(aff >= as_f32(cand), 1.0, 0.0), axis=1, keepdims=True)
        return jnp.where(cnt >= capf, cand, v)

    kth = lax.fori_loop(0, 31, bisect, jnp.zeros((e, 1), jnp.int32))
    above = as_f32(kth + 1)
    gt = aff >= above
    eq = (aff >= as_f32(kth)) & jnp.logical_not(gt)
    need = capf - jnp.sum(jnp.where(gt, 1.0, 0.0), axis=1, keepdims=True)

    blk = min(t, MXU_DIM)
    before = (_iota((blk, blk), 0) < _iota((blk, blk), 1)).astype(BF16)

    def excl_cumsum(mask):
        ones = jnp.where(mask, 1.0, 0.0)
        outs, run = [], jnp.zeros((e, 1), F32)
        for b0 in range(0, t, blk):
            mb = ones[:, b0:b0 + blk]
            outs.append(_dot(mb.astype(BF16), before) + run)
            run = run + jnp.sum(mb, axis=1, keepdims=True)
        return jnp.concatenate(outs, axis=1) if len(outs) > 1 else outs[0]

    sel = gt | (eq & (excl_cumsum(eq) < need))
    slot_ref[0] = jnp.where(sel, excl_cumsum(sel), -1.0)
    gate_ref[0] = aff


def _route(logits, cap):
    b, e, t = logits.shape
    spec = pl.BlockSpec((1, e, t), lambda i: (i, 0, 0))
    return pl.pallas_call(
        functools.partial(_route_kernel, cap=cap),
        grid=(b,),
        in_specs=[spec],
        out_specs=[spec, spec],
        out_shape=[jax.ShapeDtypeStruct((b, e, t), F32)] * 2,
        compiler_params=_params(("arbitrary",), 32),
        name="route",
    )(logits)


def _gather_kernel(h_ref, slot_ref, o_ref, *, cap):
    hx = h_ref[0]
    t = hx.shape[0]
    row = _iota((cap, t), 0).astype(F32)
    for i in range(o_ref.shape[0]):
        onehot = jnp.where(slot_ref[0, i:i + 1, :] == row, 1.0, 0.0).astype(BF16)
        o_ref[i] = _dot(onehot, hx).astype(BF16)


def _gather(hx2, slot, cap):
    b, t, d = hx2.shape
    e = slot.shape[1]
    eb = 8
    return pl.pallas_call(
        functools.partial(_gather_kernel, cap=cap),
        grid=(b, e // eb),
        in_specs=[pl.BlockSpec((1, t, d), lambda i, j: (i, 0, 0)),
                  pl.BlockSpec((1, eb, t), lambda i, j: (i, j, 0))],
        out_specs=pl.BlockSpec((eb, cap, d), lambda i, j: (j, i, 0)),
        out_shape=jax.ShapeDtypeStruct((e, b * cap, d), BF16),
        compiler_params=_params(("arbitrary", "arbitrary"), 48),
        name="gather",
    )(hx2, slot)


def _ffn_kernel(*refs, n_sets, mc):
    x_refs = refs[:n_sets]
    wg_ref, wu_ref, wd_ref = refs[n_sets:n_sets + 3]
    y_refs = refs[n_sets + 3:2 * n_sets + 3]
    acc_refs = refs[2 * n_sets + 3:3 * n_sets + 3]
    wgb, wub, wdb = refs[3 * n_sets + 3:]
    f = pl.program_id(1)
    wgb[...] = wg_ref[0, 0].astype(BF16)
    wub[...] = wu_ref[0, 0].astype(BF16)
    wdb[...] = wd_ref[0, 0].astype(BF16)
    for x_ref, y_ref, acc in zip(x_refs, y_refs, acc_refs):
        m = x_ref.shape[1]
        step = min(m, mc)
        for m0 in range(0, m, step):
            xs = x_ref[0, m0:m0 + step, :]
            hg = _dot(xs, wgb[...])
            hu = _dot(xs, wub[...])
            hid = (hg * _sigmoid(hg) * hu).astype(BF16)
            part = _dot(hid, wdb[...])

            @pl.when(f == 0)
            def _():
                acc[m0:m0 + step, :] = part

            @pl.when(f > 0)
            def _():
                acc[m0:m0 + step, :] += part

        @pl.when(f == pl.num_programs(1) - 1)
        def _():
            y_ref[0] = acc[...].astype(BF16)


def _expert_ffn(xs, layer, w_gate, w_up, w_down):
    _, e, d, hidden = w_gate.shape
    tf = 512
    n = len(xs)
    xspecs = [pl.BlockSpec((1, x.shape[1], d), lambda i, j: (i, 0, 0)) for x in xs]
    return pl.pallas_call(
        functools.partial(_ffn_kernel, n_sets=n, mc=512),
        grid=(e, hidden // tf),
        in_specs=xspecs + [pl.BlockSpec((1, 1, d, tf), lambda i, j: (layer, i, 0, j)),
                           pl.BlockSpec((1, 1, d, tf), lambda i, j: (layer, i, 0, j)),
                           pl.BlockSpec((1, 1, tf, d), lambda i, j: (layer, i, j, 0))],
        out_specs=xspecs,
        out_shape=[jax.ShapeDtypeStruct(x.shape, BF16) for x in xs],
        scratch_shapes=[pltpu.VMEM((x.shape[1], d), F32) for x in xs]
        + [pltpu.VMEM((d, tf), BF16), pltpu.VMEM((d, tf), BF16), pltpu.VMEM((tf, d), BF16)],
        compiler_params=_params(("arbitrary", "arbitrary"), 60),
        name="expert_ffn",
    )(*xs, w_gate, w_up, w_down)


def _scatter_kernel(x_ref, y_ref, slot_ref, gate_ref, mod_ref, fn_ref, o_ref, *, final):
    tt = x_ref.shape[1]
    e, cap, _ = y_ref.shape
    eye = (_iota((tt, tt), 0) == _iota((tt, tt), 1)).astype(BF16)
    slot_t = _dot_nt(eye, (slot_ref[0] + 1.0).astype(BF16))
    gate_t = _dot_nt(eye, gate_ref[0].astype(BF16))
    lane = _iota((tt, cap), 1).astype(F32) + 1.0
    acc = None
    for i in range(e):
        pt = jnp.where(slot_t[:, i:i + 1] == lane, gate_t[:, i:i + 1], 0.0).astype(BF16)
        part = _dot(pt, y_ref[i])
        acc = part if acc is None else acc + part
    out = x_ref[0] + mod_ref[0][5:6] * acc
    if final:
        out = out * lax.rsqrt(jnp.mean(out * out, axis=-1, keepdims=True) + EPS) * fn_ref[...]
    o_ref[0] = out


def _scatter(x, y, slot, gate, mod, final_norm, cap, final):
    b, t, d = x.shape
    e = slot.shape[1]
    tt = min(t, 512)
    return pl.pallas_call(
        functools.partial(_scatter_kernel, final=final),
        grid=(b, t // tt),
        in_specs=[pl.BlockSpec((1, tt, d), lambda i, j: (i, j, 0)),
                  pl.BlockSpec((e, cap, d), lambda i, j: (0, i, 0)),
                  pl.BlockSpec((1, e, tt), lambda i, j: (i, 0, j)),
                  pl.BlockSpec((1, e, tt), lambda i, j: (i, 0, j)),
                  pl.BlockSpec((1, 8, d), lambda i, j: (i, 0, 0)),
                  pl.BlockSpec((1, d), lambda i, j: (0, 0))],
        out_specs=pl.BlockSpec((1, tt, d), lambda i, j: (i, j, 0)),
        out_shape=jax.ShapeDtypeStruct((b, t, d), F32),
        compiler_params=_params(("arbitrary", "arbitrary"), 48),
        name="scatter",
    )(x, y, slot, gate, mod, final_norm)


def _moe(streams, layer, w_gate, w_up, w_down, final_norm, final):
    routed = []
    for x1, hx2, logits, mod in streams:
        t = x1.shape[1]
        cap = CAPACITY_FACTOR * t // N_EXPERTS
        slot, gate = _route(logits, cap)
        routed.append((slot, gate, cap, _gather(hx2, slot, cap)))
    ys = _expert_ffn([r[3] for r in routed], layer, w_gate, w_up, w_down)
    outs = []
    for (x1, _, _, mod), (slot, gate, cap, _), y in zip(streams, routed, ys):
        outs.append(_scatter(x1, y, slot, gate, mod, final_norm, cap, final))
    return outs


def _inproj_kernel(x_ref, mod_ref, nmix_ref, win_ref, wg_ref, bg_ref,
                   q_ref, k_ref, v_ref, o_ref, gr_ref, *, chunk, qk, dk):
    tt = x_ref.shape[1]
    mod = mod_ref[0]
    h = _norm_mod(x_ref[0], nmix_ref[...], mod[0:1], mod[1:2])
    p = _dot(h.astype(BF16), win_ref[...])
    vd = (p.shape[1] - 2 * qk) // 2
    q_ref[0] = p[:, :qk].astype(BF16)
    k_ref[0] = (p[:, qk:2 * qk] * (dk ** -0.5)).astype(BF16)
    v_ref[0] = p[:, 2 * qk:2 * qk + vd].astype(BF16)
    o_ref[0] = _sigmoid(p[:, 2 * qk + vd:])

    nh = MLSTM_HEADS
    g_t = _transpose_exact(_dot_split(h, wg_ref[...]) + bg_ref[...])
    row = _iota((4 * nh, chunk), 0)
    is_f = (row & nh) == nh
    a = _iota((chunk, chunk), 0)
    c = _iota((chunk, chunk), 1)
    le = (a <= c).astype(BF16)
    ge = (a >= c).astype(BF16)
    for c0 in range(0, tt, chunk):
        gc = g_t[:, c0:c0 + chunk]
        pieces = _split_bf16(jnp.where(is_f, _log_sigmoid(gc), 0.0), 3)
        pre = _dot(pieces[0], le) + (_dot(pieces[1], le) + _dot(pieces[2], le))
        suf = _dot(pieces[0], ge) + (_dot(pieces[1], ge) + _dot(pieces[2], ge))
        gr_ref[0, :, c0:c0 + chunk] = jnp.where(is_f, jnp.where(row < 2 * nh, pre, suf), gc)


def _inproj(x, mod, nmix, w_main, w_g, b_g, qk, vd):
    b, t, d = x.shape
    tt = min(t, 512)
    ng = w_g.shape[1]
    full = lambda shape: pl.BlockSpec(shape, lambda i, j: (0,) * len(shape))
    tok = lambda n: pl.BlockSpec((1, tt, n), lambda i, j: (i, j, 0))
    return pl.pallas_call(
        functools.partial(_inproj_kernel, chunk=SCAN_CHUNK, qk=qk, dk=qk // MLSTM_HEADS),
        grid=(b, t // tt),
        in_specs=[tok(d), pl.BlockSpec((1, 8, d), lambda i, j: (i, 0, 0)), full((1, d)),
                  full(w_main.shape), full(w_g.shape), full(b_g.shape)],
        out_specs=[tok(qk), tok(qk), tok(vd), tok(vd),
                   pl.BlockSpec((1, ng, tt), lambda i, j: (i, 0, j))],
        out_shape=[jax.ShapeDtypeStruct((b, t, qk), BF16), jax.ShapeDtypeStruct((b, t, qk), BF16),
                   jax.ShapeDtypeStruct((b, t, vd), BF16), jax.ShapeDtypeStruct((b, t, vd), F32),
                   jax.ShapeDtypeStruct((b, ng, t), F32)],
        compiler_params=_params(("arbitrary", "arbitrary"), 56),
        name="mlstm_inproj",
    )(x, mod, nmix, w_main, w_g, b_g)


def _scan_kernel(qc_ref, kc_ref, vc_ref, grc_ref, qx_ref, kx_ref, vx_ref, grx_ref,
                 out_ref, s_ref, sall_ref, mall_ref, *, chunk):
    ln = chunk
    nh = MLSTM_HEADS
    pair = pl.program_id(1)
    kl = kx_ref.shape[2]
    dv = vx_ref.shape[2] // 2
    lane = _iota((ln, kl), 1)
    kmask = (lane < kl // 2, lane >= kl // 2)
    ones = jnp.ones((ln, dv), BF16)
    ri = _iota((ln, ln), 0)
    ci = _iota((ln, ln), 1)
    causal = (ci <= ri, ci >= ri)
    ncc = qc_ref.shape[1] // ln
    ncx = qx_ref.shape[1] // ln

    def gate_rows(gr_ref, hh, dirn, c):
        base = 2 * nh * dirn + 2 * pair + hh
        return gr_ref[0, base, pl.ds(c, 1), :], gr_ref[0, base + nh, pl.ds(c, 1), :]

    def load(q_ref, k_ref, v_ref, c):
        t0 = pl.multiple_of(c * ln, ln)
        q = q_ref[0, pl.ds(t0, ln), :]
        k = k_ref[0, pl.ds(t0, ln), :]
        v = v_ref[0, pl.ds(t0, ln), :]
        khs = [jnp.where(kmask[hh], k, jnp.zeros((), BF16)) for hh in range(2)]
        vaugs = [jnp.concatenate([v[:, hh * dv:(hh + 1) * dv], ones], axis=1) for hh in range(2)]
        return t0, q, khs, vaugs

    def advance(refs, n, record):
        q_ref, k_ref, v_ref, gr_ref = refs

        def body(i, ms):
            new_ms = []
            loaded = [load(q_ref, k_ref, v_ref, c) for c in (i, n - 1 - i)]
            for hh in range(2):
                k_ts = [ld[2][hh].astype(F32).T for ld in loaded]
                for dirn in range(2):
                    ch = 2 * hh + dirn
                    c = i if dirn == 0 else n - 1 - i
                    ig, bc = gate_rows(gr_ref, hh, dirn, c)
                    m = ms[ch]
                    s_old = s_ref[ch]
                    if record:
                        sall_ref[ch * ncx + c] = s_old.astype(BF16)
                        mall_ref[ch * ncx + c] = jnp.broadcast_to(m, (8, ln))
                    b_end = bc[:, ln - 1:ln] if dirn == 0 else bc[:, 0:1]
                    gl = b_end - bc + ig
                    m_new = jnp.maximum(b_end + m, jnp.max(gl, axis=1, keepdims=True))
                    kw = (k_ts[dirn] * jnp.exp(gl - m_new)).astype(BF16)
                    s_ref[ch] = jnp.exp(b_end + m - m_new) * s_old + _dot(kw, loaded[dirn][3][hh])
                    new_ms.append(m_new)
            return tuple(new_ms)

        return body

    s_ref[...] = jnp.zeros(s_ref.shape, F32)
    ms = tuple(jnp.zeros((1, 1), F32) for _ in range(4))
    ms = lax.fori_loop(0, ncc, advance((qc_ref, kc_ref, vc_ref, grc_ref), ncc, False), ms)
    lax.fori_loop(0, ncx, advance((qx_ref, kx_ref, vx_ref, grx_ref), ncx, True), ms)

    def emit(c, _):
        t0, q, khs, vaugs = load(qx_ref, kx_ref, vx_ref, c)
        s_pair = _dot_nt(q, jnp.concatenate(khs, axis=0))
        for hh in range(2):
            s_qk = s_pair[:, hh * ln:(hh + 1) * ln]
            sms, mjs, inters, wins = [], [], [], []
            for dirn in range(2):
                ch = 2 * hh + dirn
                ig, bc = gate_rows(grx_ref, hh, dirn, c)
                b_col = jnp.broadcast_to(bc, (ln, ln)).T
                a = b_col + mall_ref[ch * ncx + c][0:1, :]
                dm = jnp.where(causal[dirn], b_col - bc + ig, -jnp.inf)
                mj = jnp.maximum(a, jnp.max(dm, axis=1, keepdims=True))
                sms.append((s_qk * jnp.exp(dm - mj)).astype(BF16))
                mjs.append(mj)
                wins.append(jnp.exp(a - mj))
                inters.append(_dot(q, sall_ref[ch * ncx + c]))
            intra = _dot(jnp.concatenate(sms, axis=0), vaugs[hh])
            hsum = None
            for dirn in range(2):
                num = intra[dirn * ln:(dirn + 1) * ln, :] + jnp.concatenate([wins[dirn]] * 2, axis=1) * inters[dirn]
                h = num[:, :dv] / jnp.maximum(jnp.abs(num[:, dv:]), jnp.exp(-mjs[dirn]))
                hsum = h if hsum is None else hsum + h
            hn = hsum * lax.rsqrt(jnp.mean(hsum * hsum, axis=-1, keepdims=True) + EPS)
            out_ref[0, pl.ds(t0, ln), hh * dv:(hh + 1) * dv] = hn
        return 0

    lax.fori_loop(0, ncx, emit, 0)


def _scan(ctx_parts, lat_parts):
    qc, kc, vc, grc = ctx_parts
    qx, kx, vx, grx = lat_parts
    b, t, vd = vx.shape
    nh = MLSTM_HEADS
    dv = vd // nh
    kl = 2 * (qx.shape[2] // nh)
    assert kl == SCAN_CHUNK and dv == SCAN_CHUNK, "scan kernel assumes key-pair lanes = head value dim = chunk"
    ncx = t // SCAN_CHUNK

    def specs(q, gr):
        tq = q.shape[1]
        return [pl.BlockSpec((1, tq, kl), lambda i, j: (i, 0, j)),
                pl.BlockSpec((1, tq, kl), lambda i, j: (i, 0, j)),
                pl.BlockSpec((1, tq, 2 * dv), lambda i, j: (i, 0, j)),
                pl.BlockSpec((1,) + gr.shape[1:], lambda i, j: (i, 0, 0, 0))]

    return pl.pallas_call(
        functools.partial(_scan_kernel, chunk=SCAN_CHUNK),
        grid=(b, nh // 2),
        in_specs=specs(qc, grc) + specs(qx, grx),
        out_specs=pl.BlockSpec((1, t, 2 * dv), lambda i, j: (i, 0, j)),
        out_shape=jax.ShapeDtypeStruct((b, t, vd), F32),
        scratch_shapes=[pltpu.VMEM((4, kl, 2 * dv), F32),
                        pltpu.VMEM((4 * ncx, kl, 2 * dv), BF16),
                        pltpu.VMEM((4 * ncx, 8, SCAN_CHUNK), F32)],
        compiler_params=_params(("arbitrary", "arbitrary"), 40),
        name="mlstm_scan",
    )(qc, kc, vc, grc, qx, kx, vx, grx)


def _mlstm_out_kernel(hn_ref, og_ref, x_ref, mod_ref, mn_ref, wout_ref, nffn_ref, wr_ref,
                      x1_ref, hx2_ref, lg_ref):
    mod = mod_ref[0]
    a = (hn_ref[0] * mn_ref[...]) * og_ref[0]
    xn = x_ref[0] + mod[2:3] * _dot(a.astype(BF16), wout_ref[...])
    x1_ref[0] = xn
    hb, lg = _ffn_pre(xn, mod, nffn_ref[...], wr_ref[...])
    hx2_ref[0] = hb
    lg_ref[0] = lg


def _mlstm_out(hn, og, x, mod, mnorm, w_out, nffn, w_router):
    b, t, d = x.shape
    vd = hn.shape[2]
    e = w_router.shape[1]
    tt = min(t, 512)
    full = lambda shape: pl.BlockSpec(shape, lambda i, j: (0,) * len(shape))
    tok = lambda n: pl.BlockSpec((1, tt, n), lambda i, j: (i, j, 0))
    return pl.pallas_call(
        _mlstm_out_kernel,
        grid=(b, t // tt),
        in_specs=[tok(vd), tok(vd), tok(d), pl.BlockSpec((1, 8, d), lambda i, j: (i, 0, 0)),
                  full((1, vd)), full(w_out.shape), full((1, d)), full(w_router.shape)],
        out_specs=[tok(d), tok(d), pl.BlockSpec((1, e, tt), lambda i, j: (i, 0, j))],
        out_shape=[jax.ShapeDtypeStruct((b, t, d), F32), jax.ShapeDtypeStruct((b, t, d), BF16),
                   jax.ShapeDtypeStruct((b, e, t), F32)],
        compiler_params=_params(("arbitrary", "arbitrary"), 48),
        name="mlstm_out",
    )(hn, og, x, mod, mnorm, w_out, nffn, w_router)


def _chunked_rows(gr):
    b, n, t = gr.shape
    return gr.reshape(b, n, t // SCAN_CHUNK, SCAN_CHUNK)


def kernel(x, c, ctx, c_ctx, ada_w, ada_b, norm_mix, norm_ffn, pool_w, pool_scale, mlstm_w_in, mlstm_b_gates,
           mlstm_norm, mlstm_w_out, moe_router, moe_w_gate, moe_w_up, moe_w_down, final_norm):
    bsz, seq, d = x.shape
    depth = ada_w.shape[0]
    n_mixers = 2

    cc = jnp.concatenate([c, c_ctx[None, :], jnp.zeros((16 - bsz - 1, d), F32)], axis=0)
    ada = _ada(cc, ada_w, ada_b).reshape(depth, 16, N_ADA, d)
    pad = jnp.zeros((bsz, 8 - N_ADA, d), F32)

    fn = final_norm.reshape(1, d)
    for i in range(depth):
        last = i == depth - 1
        j = i // n_mixers
        mod_x = jnp.concatenate([ada[i, :bsz], pad], axis=1)
        mod_c = jnp.concatenate([jnp.broadcast_to(ada[i, bsz][None], (bsz, N_ADA, d)), pad], axis=1)
        nmix = norm_mix[i].reshape(1, d)
        nffn = norm_ffn[i].reshape(1, d)
        w_router = moe_router[i]
        streams = []
        if i % n_mixers == 0:
            pw = pool_w[j].astype(BF16)
            ps = pool_scale[j].reshape(1, d)
            x1, hx2, lg = _pool_mixer(x, mod_x, nmix, nffn, pw, ps, w_router, True)
            streams.append((x1, hx2, lg, mod_x))
            if not last:
                c1, hc2, lgc = _pool_mixer(ctx, mod_c, nmix, nffn, pw, ps, w_router, False)
                streams.append((c1, hc2, lgc, mod_c))
        else:
            qk = mlstm_w_in.shape[2] - 2 * mlstm_w_out.shape[1] - 4 * MLSTM_HEADS
            qk //= 2
            vd = mlstm_w_out.shape[1]
            w_main = mlstm_w_in[j][:, :2 * qk + 2 * vd].astype(BF16)
            w_g = mlstm_w_in[j][:, 2 * qk + 2 * vd:]
            b_g = mlstm_b_gates[j].reshape(1, -1)
            proj = lambda s, m: _inproj(s, m, nmix, w_main, w_g, b_g, qk, vd)
            qc, kc, vc, _, grc = proj(ctx, mod_c)
            qx, kx, vx, ox, grx = proj(x, mod_x)
            hn = _scan((qc, kc, vc, _chunked_rows(grc)), (qx, kx, vx, _chunked_rows(grx)))
            x1, hx2, lg = _mlstm_out(hn, ox, x, mod_x, mlstm_norm[j].reshape(1, vd),
                                     mlstm_w_out[j].astype(BF16), nffn, w_router)
            streams.append((x1, hx2, lg, mod_x))
            assert last, "context output of the mLSTM mixer is only needed by a following layer"
        outs = _moe(streams, i, moe_w_gate, moe_w_up, moe_w_down, fn, last)
        x = outs[0]
        if not last:
            ctx = outs[1]
    return x
```

```python
import functools

import jax
import jax.numpy as jnp
import numpy as np
from jax import lax
from jax.experimental import pallas as pl
from jax.experimental.pallas import tpu as pltpu

F32 = jnp.float32
BF16 = jnp.bfloat16
HIGHEST = lax.Precision.HIGHEST

GRID_W = 64
EPS = 1e-6
N_ADA = 6
POOL_WINDOWS = (2, 4, 8, 16)
N_POOL_GROUPS = 4
MLSTM_HEADS = 8
N_EXPERTS = 16
CAPACITY_FACTOR = 2

MIB = 1024 * 1024
MXU_DIM = 256
SCAN_CHUNK = 128
SCAN_UNROLL = 4
POOL_PAD_ROWS = max(POOL_WINDOWS) // 2


def _params(sem, vmem_mib):
    return pltpu.CompilerParams(dimension_semantics=sem, vmem_limit_bytes=vmem_mib * MIB)


def _dot(a, b, precision=None):
    return jnp.dot(a, b, preferred_element_type=F32, precision=precision)


def _dot_nt(a, b, precision=None):
    return lax.dot_general(a, b, (((1,), (1,)), ((), ())), preferred_element_type=F32, precision=precision)


def _iota(shape, dim, dtype=jnp.int32):
    return lax.broadcasted_iota(dtype, shape, dim)


def _sigmoid(x):
    return 1.0 / (1.0 + jnp.exp(-x))


def _log_sigmoid(x):
    return jnp.minimum(x, 0.0) - jnp.log1p(jnp.exp(-jnp.abs(x)))


def _norm_mod(x, g, shift, scale):
    inv = lax.rsqrt(jnp.mean(x * x, axis=-1, keepdims=True) + EPS)
    return (x * inv * g) * (1.0 + scale) + shift


def _ada_kernel(c_ref, w_ref, b_ref, o_ref):
    c = c_ref[...]
    s = c * _sigmoid(c)
    o_ref[0] = _dot(s, w_ref[0], HIGHEST) + b_ref[0]


def _ada(cc, ada_w, ada_b):
    depth, d, n = ada_w.shape
    rows = cc.shape[0]
    tn = n // 4
    return pl.pallas_call(
        _ada_kernel,
        grid=(depth, n // tn),
        in_specs=[pl.BlockSpec((rows, d), lambda i, j: (0, 0)),
                  pl.BlockSpec((1, d, tn), lambda i, j: (i, 0, j)),
                  pl.BlockSpec((1, 1, tn), lambda i, j: (i, 0, j))],
        out_specs=pl.BlockSpec((1, rows, tn), lambda i, j: (i, 0, j)),
        out_shape=jax.ShapeDtypeStruct((depth, rows, n), F32),
        compiler_params=_params(("arbitrary", "arbitrary"), 40),
        name="ada",
    )(cc, ada_w, ada_b.reshape(depth, 1, n))


def _split_bf16(x, pieces):
    out = []
    for _ in range(pieces):
        p = x.astype(BF16)
        out.append(p)
        x = x - p.astype(F32)
    return out


def _dot_split(a, b):
    a_hi, a_lo = _split_bf16(a, 2)
    b_hi, b_lo = _split_bf16(b, 2)
    return _dot(a_hi, b_hi) + (_dot(a_hi, b_lo) + _dot(a_lo, b_hi))


def _transpose_exact(x):
    m = x.shape[1]
    eye = (_iota((m, m), 0) == _iota((m, m), 1)).astype(BF16)
    hi, mid, lo = _split_bf16(x, 3)
    return _dot_nt(eye, hi) + (_dot_nt(eye, mid) + _dot_nt(eye, lo))


def _ffn_pre(xn, mod, nffn, wr):
    h2 = _norm_mod(xn, nffn, mod[3:4], mod[4:5])
    return h2.astype(BF16), _transpose_exact(_dot_split(h2, wr))


def _pool_kernel(x_ref, mod_ref, nmix_ref, nffn_ref, cmat_ref, cnt_ref, pw_ref, ps_ref, wr_ref,
                 x1_ref, hx2_ref, lg_ref, inv_ref, *pad, two_d, tc):
    t, d = x_ref.shape[1], x_ref.shape[2]
    gd = d // N_POOL_GROUPS
    mod = mod_ref[0]
    pad_tok = POOL_PAD_ROWS * GRID_W

    for c0 in range(0, t, tc):
        xs = x_ref[0, c0:c0 + tc, :]
        inv_ref[c0:c0 + tc, :] = lax.rsqrt(jnp.mean(xs * xs, axis=-1, keepdims=True) + EPS)

    def hx_of(r0, rn, j):
        cs = slice(j * gd, (j + 1) * gd)
        xs = x_ref[0, r0:r0 + rn, cs]
        g = nmix_ref[:, cs] * (1.0 + mod[1:2, cs])
        return xs * inv_ref[r0:r0 + rn, :] * g + mod[0:1, cs]

    if two_d:
        pad_ref, = pad
        pad_ref[0:pad_tok, :] = jnp.zeros((pad_tok, gd), F32)
        pad_ref[pad_tok + t:pad_tok + t + pad_tok, :] = jnp.zeros((pad_tok, gd), F32)

    for j, w in enumerate(POOL_WINDOWS):
        cs = slice(j * gd, (j + 1) * gd)
        cm = cmat_ref[j]
        sums = []
        for b0 in range(0, t, MXU_DIM):
            g = hx_of(b0, MXU_DIM, j)
            g_hi = g.astype(BF16)
            g_lo = (g - g_hi.astype(F32)).astype(BF16)
            csum = _dot(cm, g_hi) + _dot(cm, g_lo)
            if two_d:
                pad_ref[pad_tok + b0:pad_tok + b0 + MXU_DIM, :] = csum
            else:
                sums.append(csum)
        for c0 in range(0, t, tc):
            if two_d:
                tot = None
                for dr in range(-(w // 2), w - w // 2):
                    o = pad_tok + c0 + dr * GRID_W
                    sl = pad_ref[o:o + tc, :]
                    tot = sl if tot is None else tot + sl
            else:
                tot = sums[c0 // MXU_DIM]
            mean = tot / cnt_ref[c0:c0 + tc, j:j + 1]
            diff = mean - hx_of(c0, tc, j)
            y = _dot(diff.astype(BF16), pw_ref[j]) * ps_ref[:, cs]
            x1_ref[0, c0:c0 + tc, cs] = x_ref[0, c0:c0 + tc, cs] + mod[2:3, cs] * y

    for c0 in range(0, t, tc):
        hb, lg = _ffn_pre(x1_ref[0, c0:c0 + tc, :], mod, nffn_ref[...], wr_ref[...])
        hx2_ref[0, c0:c0 + tc, :] = hb
        lg_ref[0, :, c0:c0 + tc] = lg


def _pool_consts(t, two_d):
    cm = np.zeros((len(POOL_WINDOWS), MXU_DIM, MXU_DIM), np.float32)
    cnt = np.zeros((t, len(POOL_WINDOWS)), np.float32)
    n = GRID_W if two_d else t
    pos = np.arange(MXU_DIM)
    for j, w in enumerate(POOL_WINDOWS):
        col = pos % n
        lo = np.clip(col - w // 2, 0, n)
        hi = np.clip(col + w - w // 2, 0, n)
        same = (pos[:, None] // n) == (pos[None, :] // n)
        cm[j] = (same & (col[None, :] >= lo[:, None]) & (col[None, :] < hi[:, None])).astype(np.float32)
        tt = np.arange(t)
        c = tt % n
        ccnt = np.clip(c + w - w // 2, 0, n) - np.clip(c - w // 2, 0, n)
        if two_d:
            rows = t // GRID_W
            r = tt // GRID_W
            rcnt = np.clip(r + w - w // 2, 0, rows) - np.clip(r - w // 2, 0, rows)
            cnt[:, j] = ccnt * rcnt
        else:
            cnt[:, j] = ccnt
    return jnp.asarray(cm, BF16), jnp.asarray(cnt)


def _pool_mixer(x, mod, nmix, nffn, pool_w, pool_scale, w_router, two_d):
    b, t, d = x.shape
    gd = d // N_POOL_GROUPS
    if not two_d:
        assert t == MXU_DIM
    cmat, cnt = _pool_consts(t, two_d)
    tc = min(t, 512)
    pad_tok = POOL_PAD_ROWS * GRID_W
    scratch = [pltpu.VMEM((t, 1), F32)]
    if two_d:
        scratch.append(pltpu.VMEM((t + 2 * pad_tok, gd), F32))
    e = w_router.shape[1]
    full = lambda shape: pl.BlockSpec(shape, lambda i: (0,) * len(shape))
    return pl.pallas_call(
        functools.partial(_pool_kernel, two_d=two_d, tc=tc),
        grid=(b,),
        in_specs=[pl.BlockSpec((1, t, d), lambda i: (i, 0, 0)),
                  pl.BlockSpec((1, 8, d), lambda i: (i, 0, 0)),
                  full((1, d)), full((1, d)), full(cmat.shape), full(cnt.shape),
                  full(pool_w.shape), full((1, d)), full(w_router.shape)],
        out_specs=[pl.BlockSpec((1, t, d), lambda i: (i, 0, 0)),
                   pl.BlockSpec((1, t, d), lambda i: (i, 0, 0)),
                   pl.BlockSpec((1, e, t), lambda i: (i, 0, 0))],
        out_shape=[jax.ShapeDtypeStruct((b, t, d), F32),
                   jax.ShapeDtypeStruct((b, t, d), BF16),
                   jax.ShapeDtypeStruct((b, e, t), F32)],
        scratch_shapes=scratch,
        compiler_params=_params(("arbitrary",), 60),
        name="pool_mixer_2d" if two_d else "pool_mixer_1d",
    )(x, mod, nmix, nffn, cmat, cnt, pool_w, pool_scale, w_router)


def _route_kernel(lg_ref, slot_ref, gate_ref, *, cap):
    lg = lg_ref[0]
    e, t = lg.shape
    ex = jnp.exp(lg - jnp.max(lg, axis=0, keepdims=True))
    aff = ex / jnp.sum(ex, axis=0, keepdims=True)
    capf = jnp.float32(cap)

    def as_f32(v):
        return lax.bitcast_convert_type(v, F32)

    def bisect(i, v):
        cand = v | jnp.left_shift(jnp.int32(1), 30 - i)
        cnt = jnp.sum(jnp.where(aff >= as_f32(cand), 1.0, 0.0), axis=1, keepdims=True)
        return jnp.where(cnt >= capf, cand, v)

    kth = lax.fori_loop(0, 31, bisect, jnp.zeros((e, 1), jnp.int32))
    above = as_f32(kth + 1)
    gt = aff >= above
    eq = (aff >= as_f32(kth)) & jnp.logical_not(gt)
    need = capf - jnp.sum(jnp.where(gt, 1.0, 0.0), axis=1, keepdims=True)

    blk = min(t, MXU_DIM)
    before = (_iota((blk, blk), 0) < _iota((blk, blk), 1)).astype(BF16)

    def excl_cumsum(mask):
        ones = jnp.where(mask, 1.0, 0.0)
        outs, run = [], jnp.zeros((e, 1), F32)
        for b0 in range(0, t, blk):
            mb = ones[:, b0:b0 + blk]
            outs.append(_dot(mb.astype(BF16), before) + run)
            run = run + jnp.sum(mb, axis=1, keepdims=True)
        return jnp.concatenate(outs, axis=1) if len(outs) > 1 else outs[0]

    sel = gt | (eq & (excl_cumsum(eq) < need))
    slot_ref[0] = jnp.where(sel, excl_cumsum(sel), -1.0)
    gate_ref[0] = aff


def _route(logits, cap):
    b, e, t = logits.shape
    spec = pl.BlockSpec((1, e, t), lambda i: (i, 0, 0))
    return pl.pallas_call(
        functools.partial(_route_kernel, cap=cap),
        grid=(b,),
        in_specs=[spec],
        out_specs=[spec, spec],
        out_shape=[jax.ShapeDtypeStruct((b, e, t), F32)] * 2,
        compiler_params=_params(("arbitrary",), 32),
        name="route",
    )(logits)


def _gather_kernel(h_ref, slot_ref, o_ref, *, cap):
    hx = h_ref[0]
    t = hx.shape[0]
    row = _iota((cap, t), 0).astype(F32)
    for i in range(o_ref.shape[0]):
        onehot = jnp.where(slot_ref[0, i:i + 1, :] == row, 1.0, 0.0).astype(BF16)
        o_ref[i] = _dot(onehot, hx).astype(BF16)


def _gather(hx2, slot, cap):
    b, t, d = hx2.shape
    e = slot.shape[1]
    eb = 8
    return pl.pallas_call(
        functools.partial(_gather_kernel, cap=cap),
        grid=(b, e // eb),
        in_specs=[pl.BlockSpec((1, t, d), lambda i, j: (i, 0, 0)),
                  pl.BlockSpec((1, eb, t), lambda i, j: (i, j, 0))],
        out_specs=pl.BlockSpec((eb, cap, d), lambda i, j: (j, i, 0)),
        out_shape=jax.ShapeDtypeStruct((e, b * cap, d), BF16),
        compiler_params=_params(("arbitrary", "arbitrary"), 48),
        name="gather",
    )(hx2, slot)


def _ffn_kernel(*refs, n_sets, mc):
    x_refs = refs[:n_sets]
    wg_ref, wu_ref, wd_ref = refs[n_sets:n_sets + 3]
    y_refs = refs[n_sets + 3:2 * n_sets + 3]
    acc_refs = refs[2 * n_sets + 3:3 * n_sets + 3]
    wgb, wub, wdb = refs[3 * n_sets + 3:]
    f = pl.program_id(1)
    wgb[...] = wg_ref[0, 0].astype(BF16)
    wub[...] = wu_ref[0, 0].astype(BF16)
    wdb[...] = wd_ref[0, 0].astype(BF16)
    last = f == pl.num_programs(1) - 1

    @pl.when((pl.program_id(0) == 0) & (f == 0))
    def _():
        for acc in acc_refs:
            acc[...] = jnp.zeros(acc.shape, F32)

    for x_ref, y_ref, acc in zip(x_refs, y_refs, acc_refs):
        m = x_ref.shape[1]
        step = min(m, mc)
        for m0 in range(0, m, step):
            xs = x_ref[0, m0:m0 + step, :]
            hg = _dot(xs, wgb[...])
            hu = _dot(xs, wub[...])
            hid = (hg * _sigmoid(hg) * hu).astype(BF16)
            tot = acc[m0:m0 + step, :] + _dot(hid, wdb[...])
            acc[m0:m0 + step, :] = jnp.where(last, 0.0, tot)
            y_ref[0, m0:m0 + step, :] = tot.astype(BF16)


def _expert_ffn(xs, layer, w_gate, w_up, w_down):
    _, e, d, hidden = w_gate.shape
    tf = 512
    n = len(xs)
    xspecs = [pl.BlockSpec((1, x.shape[1], d), lambda i, j: (i, 0, 0)) for x in xs]
    return pl.pallas_call(
        functools.partial(_ffn_kernel, n_sets=n, mc=2048),
        grid=(e, hidden // tf),
        in_specs=xspecs + [pl.BlockSpec((1, 1, d, tf), lambda i, j: (layer, i, 0, j)),
                           pl.BlockSpec((1, 1, d, tf), lambda i, j: (layer, i, 0, j)),
                           pl.BlockSpec((1, 1, tf, d), lambda i, j: (layer, i, j, 0))],
        out_specs=xspecs,
        out_shape=[jax.ShapeDtypeStruct(x.shape, BF16) for x in xs],
        scratch_shapes=[pltpu.VMEM((x.shape[1], d), F32) for x in xs]
        + [pltpu.VMEM((d, tf), BF16), pltpu.VMEM((d, tf), BF16), pltpu.VMEM((tf, d), BF16)],
        compiler_params=_params(("arbitrary", "arbitrary"), 60),
        name="expert_ffn",
    )(*xs, w_gate, w_up, w_down)


def _scatter_kernel(x_ref, y_ref, slot_ref, gate_ref, mod_ref, fn_ref, o_ref, *, final):
    tt = x_ref.shape[1]
    e, cap, _ = y_ref.shape
    eye = (_iota((tt, tt), 0) == _iota((tt, tt), 1)).astype(BF16)
    slot_t = _dot_nt(eye, (slot_ref[0] + 1.0).astype(BF16))
    gate_t = _dot_nt(eye, gate_ref[0].astype(BF16))
    lane = _iota((tt, cap), 1).astype(F32) + 1.0
    acc = None
    for i in range(e):
        pt = jnp.where(slot_t[:, i:i + 1] == lane, gate_t[:, i:i + 1], 0.0).astype(BF16)
        part = _dot(pt, y_ref[i])
        acc = part if acc is None else acc + part
    out = x_ref[0] + mod_ref[0][5:6] * acc
    if final:
        out = out * lax.rsqrt(jnp.mean(out * out, axis=-1, keepdims=True) + EPS) * fn_ref[...]
    o_ref[0] = out


def _scatter(x, y, slot, gate, mod, final_norm, cap, final):
    b, t, d = x.shape
    e = slot.shape[1]
    tt = min(t, 512)
    return pl.pallas_call(
        functools.partial(_scatter_kernel, final=final),
        grid=(b, t // tt),
        in_specs=[pl.BlockSpec((1, tt, d), lambda i, j: (i, j, 0)),
                  pl.BlockSpec((e, cap, d), lambda i, j: (0, i, 0)),
                  pl.BlockSpec((1, e, tt), lambda i, j: (i, 0, j)),
                  pl.BlockSpec((1, e, tt), lambda i, j: (i, 0, j)),
                  pl.BlockSpec((1, 8, d), lambda i, j: (i, 0, 0)),
                  pl.BlockSpec((1, d), lambda i, j: (0, 0))],
        out_specs=pl.BlockSpec((1, tt, d), lambda i, j: (i, j, 0)),
        out_shape=jax.ShapeDtypeStruct((b, t, d), F32),
        compiler_params=_params(("arbitrary", "arbitrary"), 48),
        name="scatter",
    )(x, y, slot, gate, mod, final_norm)


def _moe(streams, layer, w_gate, w_up, w_down, final_norm, final):
    routed = []
    for x1, hx2, logits, mod in streams:
        t = x1.shape[1]
        cap = CAPACITY_FACTOR * t // N_EXPERTS
        slot, gate = _route(logits, cap)
        routed.append((slot, gate, cap, _gather(hx2, slot, cap)))
    ys = _expert_ffn([r[3] for r in routed], layer, w_gate, w_up, w_down)
    outs = []
    for (x1, _, _, mod), (slot, gate, cap, _), y in zip(streams, routed, ys):
        outs.append(_scatter(x1, y, slot, gate, mod, final_norm, cap, final))
    return outs


def _inproj_kernel(x_ref, mod_ref, nmix_ref, wko_ref, wqvt_ref, wg_ref, bg_ref,
                   qt_ref, k_ref, vt_ref, o_ref, gr_ref, *, chunk, qk, dk):
    tt = x_ref.shape[1]
    mod = mod_ref[0]
    h = _norm_mod(x_ref[0], nmix_ref[...], mod[0:1], mod[1:2])
    hb = h.astype(BF16)
    p = _dot(hb, wko_ref[...])
    k_ref[0] = (p[:, :qk] * (dk ** -0.5)).astype(BF16)
    o_ref[0] = _sigmoid(p[:, qk:])
    p_t = _dot_nt(wqvt_ref[...], hb)
    for ci in range(tt // chunk):
        qt_ref[0, ci] = p_t[:qk, ci * chunk:(ci + 1) * chunk].astype(BF16)
        vt_ref[0, ci] = p_t[qk:, ci * chunk:(ci + 1) * chunk].astype(BF16)

    nh = MLSTM_HEADS
    g_t = _transpose_exact(_dot_split(h, wg_ref[...]) + bg_ref[...])
    row = _iota((4 * nh, chunk), 0)
    is_f = (row & nh) == nh
    a = _iota((chunk, chunk), 0)
    c = _iota((chunk, chunk), 1)
    le = (a <= c).astype(BF16)
    ge = (a >= c).astype(BF16)
    for c0 in range(0, tt, chunk):
        gc = g_t[:, c0:c0 + chunk]
        pieces = _split_bf16(jnp.where(is_f, _log_sigmoid(gc), 0.0), 3)
        pre = _dot(pieces[0], le) + (_dot(pieces[1], le) + _dot(pieces[2], le))
        suf = _dot(pieces[0], ge) + (_dot(pieces[1], ge) + _dot(pieces[2], ge))
        gr_ref[0, :, c0:c0 + chunk] = jnp.where(is_f, jnp.where(row < 2 * nh, pre, suf), gc)


def _inproj(x, mod, nmix, w_ko, w_qvt, w_g, b_g, qk, vd):
    b, t, d = x.shape
    tt = min(t, 512)
    ng = w_g.shape[1]
    ln = SCAN_CHUNK
    full = lambda shape: pl.BlockSpec(shape, lambda i, j: (0,) * len(shape))
    tok = lambda n: pl.BlockSpec((1, tt, n), lambda i, j: (i, j, 0))
    slab = lambda n: pl.BlockSpec((1, tt // ln, n, ln), lambda i, j: (i, j, 0, 0))
    return pl.pallas_call(
        functools.partial(_inproj_kernel, chunk=ln, qk=qk, dk=qk // MLSTM_HEADS),
        grid=(b, t // tt),
        in_specs=[tok(d), pl.BlockSpec((1, 8, d), lambda i, j: (i, 0, 0)), full((1, d)),
                  full(w_ko.shape), full(w_qvt.shape), full(w_g.shape), full(b_g.shape)],
        out_specs=[slab(qk), tok(qk), slab(vd), tok(vd),
                   pl.BlockSpec((1, ng, tt), lambda i, j: (i, 0, j))],
        out_shape=[jax.ShapeDtypeStruct((b, t // ln, qk, ln), BF16), jax.ShapeDtypeStruct((b, t, qk), BF16),
                   jax.ShapeDtypeStruct((b, t // ln, vd, ln), BF16), jax.ShapeDtypeStruct((b, t, vd), F32),
                   jax.ShapeDtypeStruct((b, ng, t), F32)],
        compiler_params=_params(("arbitrary", "arbitrary"), 56),
        name="mlstm_inproj",
    )(x, mod, nmix, w_ko, w_qvt, w_g, b_g)


def _scan_kernel(kc_ref, vc_ref, grc_ref, qx_ref, kx_ref, vx_ref, grx_ref,
                 out_ref, s_ref, sall_ref, mall_ref, *, chunk):
    ln = chunk
    nh = MLSTM_HEADS
    pair = pl.program_id(1)
    kl = kx_ref.shape[2]
    dv = vx_ref.shape[2] // 2
    lane = _iota((ln, kl), 1)
    kmask = (lane < kl // 2, lane >= kl // 2)
    ones = jnp.ones((dv, ln), BF16)
    si = _iota((ln, ln), 0)
    ji = _iota((ln, ln), 1)
    causal = (si <= ji, si >= ji)
    ncc = kc_ref.shape[1] // ln
    ncx = kx_ref.shape[1] // ln

    def gate_rows(gr_ref, hh, dirn, c):
        base = 2 * nh * dirn + 2 * pair + hh
        return gr_ref[0, base, pl.ds(c, 1), :], gr_ref[0, base + nh, pl.ds(c, 1), :]

    def keys(k_ref, c):
        k = k_ref[0, pl.ds(pl.multiple_of(c * ln, ln), ln), :]
        return [jnp.where(kmask[hh], k, jnp.zeros((), BF16)) for hh in range(2)]

    def values_t(v_ref, c, hh):
        return jnp.concatenate([v_ref[0, c, hh * dv:(hh + 1) * dv, :], ones], axis=0)

    def advance(refs, n, record):
        k_ref, v_ref, gr_ref = refs

        def body(i, ms):
            new_ms = []
            cs = (i, n - 1 - i)
            khs = [keys(k_ref, c) for c in cs]
            for hh in range(2):
                for dirn in range(2):
                    ch = 2 * hh + dirn
                    c = cs[dirn]
                    ig, bc = gate_rows(gr_ref, hh, dirn, c)
                    m = ms[ch]
                    s_old = s_ref[ch]
                    if record:
                        sall_ref[ch * ncx + c] = s_old.astype(BF16)
                        mall_ref[ch * ncx + c] = jnp.broadcast_to(m, (8, ln))
                    b_end = bc[:, ln - 1:ln] if dirn == 0 else bc[:, 0:1]
                    gl = b_end - bc + ig
                    m_new = jnp.maximum(b_end + m, jnp.max(gl, axis=1, keepdims=True))
                    vw = (values_t(v_ref, c, hh) * jnp.exp(gl - m_new)).astype(BF16)
                    s_ref[ch] = jnp.exp(b_end + m - m_new) * s_old + _dot(vw, khs[dirn][hh])
                    new_ms.append(m_new)
            return tuple(new_ms)

        return body

    s_ref[...] = jnp.zeros(s_ref.shape, F32)
    ms = tuple(jnp.zeros((1, 1), F32) for _ in range(4))
    ms = lax.fori_loop(0, ncc, advance((kc_ref, vc_ref, grc_ref), ncc, False), ms, unroll=min(ncc, SCAN_UNROLL))
    lax.fori_loop(0, ncx, advance((kx_ref, vx_ref, grx_ref), ncx, True), ms, unroll=min(ncx, SCAN_UNROLL))

    def emit(c, _):
        q_t = qx_ref[0, c]
        q_tf = q_t.astype(F32)
        s_pair = _dot(jnp.concatenate(keys(kx_ref, c), axis=0), q_t)
        for hh in range(2):
            s_kq = s_pair[hh * ln:(hh + 1) * ln, :]
            v_t = values_t(vx_ref, c, hh)
            hsum = None
            for dirn in range(2):
                ch = 2 * hh + dirn
                ig, bc = gate_rows(grx_ref, hh, dirn, c)
                u_col = jnp.broadcast_to(ig - bc, (ln, ln)).T
                a = bc + mall_ref[ch * ncx + c][0:1, :]
                dm = jnp.where(causal[dirn], u_col + bc, -jnp.inf)
                mj = jnp.maximum(a, jnp.max(dm, axis=0, keepdims=True))
                sm = (s_kq * jnp.exp(dm - mj)).astype(BF16)
                qw = (q_tf * jnp.exp(a - mj)).astype(BF16)
                num = _dot(jnp.concatenate([v_t, sall_ref[ch * ncx + c]], axis=1),
                           jnp.concatenate([sm, qw], axis=0))
                h = num[:dv, :] * (1.0 / jnp.maximum(jnp.abs(num[dv:dv + 1, :]), jnp.exp(-mj)))
                hsum = h if hsum is None else hsum + h
            hn = hsum * lax.rsqrt(jnp.mean(hsum * hsum, axis=0, keepdims=True) + EPS)
            out_ref[0, pl.ds(pl.multiple_of(c * ln, ln), ln), hh * dv:(hh + 1) * dv] = hn.T
        return 0

    lax.fori_loop(0, ncx, emit, 0, unroll=min(ncx, SCAN_UNROLL))


def _scan(ctx_parts, lat_parts):
    kc, vc, grc = ctx_parts
    qx, kx, vx, grx = lat_parts
    b, t, qk = kx.shape
    vd = vx.shape[2]
    nh = MLSTM_HEADS
    ln = SCAN_CHUNK
    dv = vd // nh
    kl = 2 * (qk // nh)
    assert kl == ln and dv == ln, "scan kernel assumes key-pair lanes = head value dim = chunk"
    ncx = t // ln

    def specs(k, v, gr):
        tk = k.shape[1]
        return [pl.BlockSpec((1, tk, kl), lambda i, j: (i, 0, j)),
                pl.BlockSpec((1, tk // ln, 2 * dv, ln), lambda i, j: (i, 0, j, 0)),
                pl.BlockSpec((1,) + gr.shape[1:], lambda i, j: (i, 0, 0, 0))]

    return pl.pallas_call(
        functools.partial(_scan_kernel, chunk=ln),
        grid=(b, nh // 2),
        in_specs=specs(kc, vc, grc) + [pl.BlockSpec((1, ncx, kl, ln), lambda i, j: (i, 0, j, 0))] + specs(kx, vx, grx),
        out_specs=pl.BlockSpec((1, t, 2 * dv), lambda i, j: (i, 0, j)),
        out_shape=jax.ShapeDtypeStruct((b, t, vd), F32),
        scratch_shapes=[pltpu.VMEM((4, 2 * dv, kl), F32),
                        pltpu.VMEM((4 * ncx, 2 * dv, kl), BF16),
                        pltpu.VMEM((4 * ncx, 8, ln), F32)],
        compiler_params=_params(("arbitrary", "arbitrary"), 40),
        name="mlstm_scan",
    )(kc, vc, grc, qx, kx, vx, grx)


def _mlstm_out_kernel(hn_ref, og_ref, x_ref, mod_ref, mn_ref, wout_ref, nffn_ref, wr_ref,
                      x1_ref, hx2_ref, lg_ref):
    mod = mod_ref[0]
    a = (hn_ref[0] * mn_ref[...]) * og_ref[0]
    xn = x_ref[0] + mod[2:3] * _dot(a.astype(BF16), wout_ref[...])
    x1_ref[0] = xn
    hb, lg = _ffn_pre(xn, mod, nffn_ref[...], wr_ref[...])
    hx2_ref[0] = hb
    lg_ref[0] = lg


def _mlstm_out(hn, og, x, mod, mnorm, w_out, nffn, w_router):
    b, t, d = x.shape
    vd = hn.shape[2]
    e = w_router.shape[1]
    tt = min(t, 512)
    full = lambda shape: pl.BlockSpec(shape, lambda i, j: (0,) * len(shape))
    tok = lambda n: pl.BlockSpec((1, tt, n), lambda i, j: (i, j, 0))
    return pl.pallas_call(
        _mlstm_out_kernel,
        grid=(b, t // tt),
        in_specs=[tok(vd), tok(vd), tok(d), pl.BlockSpec((1, 8, d), lambda i, j: (i, 0, 0)),
                  full((1, vd)), full(w_out.shape), full((1, d)), full(w_router.shape)],
        out_specs=[tok(d), tok(d), pl.BlockSpec((1, e, tt), lambda i, j: (i, 0, j))],
        out_shape=[jax.ShapeDtypeStruct((b, t, d), F32), jax.ShapeDtypeStruct((b, t, d), BF16),
                   jax.ShapeDtypeStruct((b, e, t), F32)],
        compiler_params=_params(("arbitrary", "arbitrary"), 48),
        name="mlstm_out",
    )(hn, og, x, mod, mnorm, w_out, nffn, w_router)


def _chunked_rows(gr):
    b, n, t = gr.shape
    return gr.reshape(b, n, t // SCAN_CHUNK, SCAN_CHUNK)


def kernel(x, c, ctx, c_ctx, ada_w, ada_b, norm_mix, norm_ffn, pool_w, pool_scale, mlstm_w_in, mlstm_b_gates,
           mlstm_norm, mlstm_w_out, moe_router, moe_w_gate, moe_w_up, moe_w_down, final_norm):
    bsz, seq, d = x.shape
    depth = ada_w.shape[0]
    n_mixers = 2

    cc = jnp.concatenate([c, c_ctx[None, :], jnp.zeros((16 - bsz - 1, d), F32)], axis=0)
    ada = _ada(cc, ada_w, ada_b).reshape(depth, 16, N_ADA, d)
    pad = jnp.zeros((bsz, 8 - N_ADA, d), F32)

    fn = final_norm.reshape(1, d)
    for i in range(depth):
        last = i == depth - 1
        j = i // n_mixers
        mod_x = jnp.concatenate([ada[i, :bsz], pad], axis=1)
        mod_c = jnp.concatenate([jnp.broadcast_to(ada[i, bsz][None], (bsz, N_ADA, d)), pad], axis=1)
        nmix = norm_mix[i].reshape(1, d)
        nffn = norm_ffn[i].reshape(1, d)
        w_router = moe_router[i]
        streams = []
        if i % n_mixers == 0:
            pw = pool_w[j].astype(BF16)
            ps = pool_scale[j].reshape(1, d)
            x1, hx2, lg = _pool_mixer(x, mod_x, nmix, nffn, pw, ps, w_router, True)
            streams.append((x1, hx2, lg, mod_x))
            if not last:
                c1, hc2, lgc = _pool_mixer(ctx, mod_c, nmix, nffn, pw, ps, w_router, False)
                streams.append((c1, hc2, lgc, mod_c))
        else:
            qk = mlstm_w_in.shape[2] - 2 * mlstm_w_out.shape[1] - 4 * MLSTM_HEADS
            qk //= 2
            vd = mlstm_w_out.shape[1]
            w_in = mlstm_w_in[j]
            w_ko = jnp.concatenate([w_in[:, qk:2 * qk], w_in[:, 2 * qk + vd:2 * qk + 2 * vd]], axis=1).astype(BF16)
            w_qvt = jnp.concatenate([w_in[:, :qk], w_in[:, 2 * qk:2 * qk + vd]], axis=1).T.astype(BF16)
            w_g = w_in[:, 2 * qk + 2 * vd:]
            b_g = mlstm_b_gates[j].reshape(1, -1)
            proj = lambda s, m: _inproj(s, m, nmix, w_ko, w_qvt, w_g, b_g, qk, vd)
            _, kc, vc, _, grc = proj(ctx, mod_c)
            qx, kx, vx, ox, grx = proj(x, mod_x)
            hn = _scan((kc, vc, _chunked_rows(grc)), (qx, kx, vx, _chunked_rows(grx)))
            x1, hx2, lg = _mlstm_out(hn, ox, x, mod_x, mlstm_norm[j].reshape(1, vd),
                                     mlstm_w_out[j].astype(BF16), nffn, w_router)
            streams.append((x1, hx2, lg, mod_x))
            assert last, "context output of the mLSTM mixer is only needed by a following layer"
        outs = _moe(streams, i, moe_w_gate, moe_w_up, moe_w_down, fn, last)
        x = outs[0]
        if not last:
            ctx = outs[1]
    return x
```

```python
import functools

import jax
import jax.numpy as jnp
import numpy as np
from jax import lax
from jax.experimental import pallas as pl
from jax.experimental.pallas import tpu as pltpu

F32 = jnp.float32
BF16 = jnp.bfloat16
HIGHEST = lax.Precision.HIGHEST

GRID_W = 64
EPS = 1e-6
N_ADA = 6
POOL_WINDOWS = (2, 4, 8, 16)
N_POOL_GROUPS = 4
MLSTM_HEADS = 8
N_EXPERTS = 16
CAPACITY_FACTOR = 2

MIB = 1024 * 1024
MXU_DIM = 256
SCAN_CHUNK = 128
SCAN_UNROLL = 4
POOL_PAD_ROWS = max(POOL_WINDOWS) // 2


def _params(sem, vmem_mib):
    return pltpu.CompilerParams(dimension_semantics=sem, vmem_limit_bytes=vmem_mib * MIB)


def _dot(a, b, precision=None):
    return jnp.dot(a, b, preferred_element_type=F32, precision=precision)


def _dot_nt(a, b, precision=None):
    return lax.dot_general(a, b, (((1,), (1,)), ((), ())), preferred_element_type=F32, precision=precision)


def _iota(shape, dim, dtype=jnp.int32):
    return lax.broadcasted_iota(dtype, shape, dim)


def _sigmoid(x):
    return 1.0 / (1.0 + jnp.exp(-x))


def _log_sigmoid(x):
    return jnp.minimum(x, 0.0) - jnp.log1p(jnp.exp(-jnp.abs(x)))


def _norm_mod(x, g, shift, scale):
    inv = lax.rsqrt(jnp.mean(x * x, axis=-1, keepdims=True) + EPS)
    return (x * inv) * (g * (1.0 + scale)) + shift


def _ada_kernel(c_ref, w_ref, b_ref, o_ref):
    c = c_ref[...]
    s = c * _sigmoid(c)
    o_ref[0] = _dot(s, w_ref[0], HIGHEST) + b_ref[0]


def _ada(cc, ada_w, ada_b):
    depth, d, n = ada_w.shape
    rows = cc.shape[0]
    tn = n // 4
    return pl.pallas_call(
        _ada_kernel,
        grid=(depth, n // tn),
        in_specs=[pl.BlockSpec((rows, d), lambda i, j: (0, 0)),
                  pl.BlockSpec((1, d, tn), lambda i, j: (i, 0, j)),
                  pl.BlockSpec((1, 1, tn), lambda i, j: (i, 0, j))],
        out_specs=pl.BlockSpec((1, rows, tn), lambda i, j: (i, 0, j)),
        out_shape=jax.ShapeDtypeStruct((depth, rows, n), F32),
        compiler_params=_params(("arbitrary", "arbitrary"), 40),
        name="ada",
    )(cc, ada_w, ada_b.reshape(depth, 1, n))


def _split_bf16(x, pieces):
    out = []
    for _ in range(pieces):
        p = x.astype(BF16)
        out.append(p)
        x = x - p.astype(F32)
    return out


def _dot_split(a, b):
    n = b.shape[1]
    a_hi, a_lo = _split_bf16(a, 2)
    b_hi, b_lo = _split_bf16(b, 2)
    hi = _dot(a_hi, jnp.concatenate([b_hi, b_lo], axis=1))
    return hi[:, :n] + (hi[:, n:] + _dot(a_lo, b_hi))


def _transpose_exact(x):
    m = x.shape[1]
    eye = (_iota((m, m), 0) == _iota((m, m), 1)).astype(BF16)
    hi, mid, lo = _split_bf16(x, 3)
    return _dot_nt(eye, hi) + (_dot_nt(eye, mid) + _dot_nt(eye, lo))


def _ffn_pre(xn, mod, nffn, wr):
    h2 = _norm_mod(xn, nffn, mod[3:4], mod[4:5])
    return h2.astype(BF16), _transpose_exact(_dot_split(h2, wr))


def _pool_kernel(x_ref, mod_ref, nmix_ref, nffn_ref, cmat_ref, cnt_ref, pw_ref, ps_ref, wr_ref,
                 x1_ref, hx2_ref, lg_ref, inv_ref, *pad, two_d, tc):
    t, d = x_ref.shape[1], x_ref.shape[2]
    gd = d // N_POOL_GROUPS
    mod = mod_ref[0]
    pad_tok = POOL_PAD_ROWS * GRID_W

    for c0 in range(0, t, tc):
        xs = x_ref[0, c0:c0 + tc, :]
        inv_ref[c0:c0 + tc, :] = lax.rsqrt(jnp.mean(xs * xs, axis=-1, keepdims=True) + EPS)

    def hx_of(r0, rn, j):
        cs = slice(j * gd, (j + 1) * gd)
        xs = x_ref[0, r0:r0 + rn, cs]
        g = nmix_ref[:, cs] * (1.0 + mod[1:2, cs])
        return xs * inv_ref[r0:r0 + rn, :] * g + mod[0:1, cs]

    if two_d:
        pad_ref, = pad
        pad_ref[0:pad_tok, :] = jnp.zeros((pad_tok, gd), F32)
        pad_ref[pad_tok + t:pad_tok + t + pad_tok, :] = jnp.zeros((pad_tok, gd), F32)

    for j, w in enumerate(POOL_WINDOWS):
        cs = slice(j * gd, (j + 1) * gd)
        cm = cmat_ref[j]
        sums = []
        for b0 in range(0, t, MXU_DIM):
            g = hx_of(b0, MXU_DIM, j)
            g_hi = g.astype(BF16)
            g_lo = (g - g_hi.astype(F32)).astype(BF16)
            csum = _dot(cm, g_hi) + _dot(cm, g_lo)
            if two_d:
                pad_ref[pad_tok + b0:pad_tok + b0 + MXU_DIM, :] = csum
            else:
                sums.append(csum)
        for c0 in range(0, t, tc):
            if two_d:
                tot = None
                for dr in range(-(w // 2), w - w // 2):
                    o = pad_tok + c0 + dr * GRID_W
                    sl = pad_ref[o:o + tc, :]
                    tot = sl if tot is None else tot + sl
            else:
                tot = sums[c0 // MXU_DIM]
            mean = tot / cnt_ref[c0:c0 + tc, j:j + 1]
            diff = mean - hx_of(c0, tc, j)
            y = _dot(diff.astype(BF16), pw_ref[j]) * ps_ref[:, cs]
            x1_ref[0, c0:c0 + tc, cs] = x_ref[0, c0:c0 + tc, cs] + mod[2:3, cs] * y

    for c0 in range(0, t, tc):
        hb, lg = _ffn_pre(x1_ref[0, c0:c0 + tc, :], mod, nffn_ref[...], wr_ref[...])
        hx2_ref[0, c0:c0 + tc, :] = hb
        lg_ref[0, :, c0:c0 + tc] = lg


def _pool_consts(t, two_d):
    cm = np.zeros((len(POOL_WINDOWS), MXU_DIM, MXU_DIM), np.float32)
    cnt = np.zeros((t, len(POOL_WINDOWS)), np.float32)
    n = GRID_W if two_d else t
    pos = np.arange(MXU_DIM)
    for j, w in enumerate(POOL_WINDOWS):
        col = pos % n
        lo = np.clip(col - w // 2, 0, n)
        hi = np.clip(col + w - w // 2, 0, n)
        same = (pos[:, None] // n) == (pos[None, :] // n)
        cm[j] = (same & (col[None, :] >= lo[:, None]) & (col[None, :] < hi[:, None])).astype(np.float32)
        tt = np.arange(t)
        c = tt % n
        ccnt = np.clip(c + w - w // 2, 0, n) - np.clip(c - w // 2, 0, n)
        if two_d:
            rows = t // GRID_W
            r = tt // GRID_W
            rcnt = np.clip(r + w - w // 2, 0, rows) - np.clip(r - w // 2, 0, rows)
            cnt[:, j] = ccnt * rcnt
        else:
            cnt[:, j] = ccnt
    return jnp.asarray(cm, BF16), jnp.asarray(cnt)


def _pool_mixer(x, mod, nmix, nffn, pool_w, pool_scale, w_router, two_d):
    b, t, d = x.shape
    gd = d // N_POOL_GROUPS
    if not two_d:
        assert t == MXU_DIM
    cmat, cnt = _pool_consts(t, two_d)
    tc = min(t, 512)
    pad_tok = POOL_PAD_ROWS * GRID_W
    scratch = [pltpu.VMEM((t, 1), F32)]
    if two_d:
        scratch.append(pltpu.VMEM((t + 2 * pad_tok, gd), F32))
    e = w_router.shape[1]
    full = lambda shape: pl.BlockSpec(shape, lambda i: (0,) * len(shape))
    return pl.pallas_call(
        functools.partial(_pool_kernel, two_d=two_d, tc=tc),
        grid=(b,),
        in_specs=[pl.BlockSpec((1, t, d), lambda i: (i, 0, 0)),
                  pl.BlockSpec((1, 8, d), lambda i: (i, 0, 0)),
                  full((1, d)), full((1, d)), full(cmat.shape), full(cnt.shape),
                  full(pool_w.shape), full((1, d)), full(w_router.shape)],
        out_specs=[pl.BlockSpec((1, t, d), lambda i: (i, 0, 0)),
                   pl.BlockSpec((1, t, d), lambda i: (i, 0, 0)),
                   pl.BlockSpec((1, e, t), lambda i: (i, 0, 0))],
        out_shape=[jax.ShapeDtypeStruct((b, t, d), F32),
                   jax.ShapeDtypeStruct((b, t, d), BF16),
                   jax.ShapeDtypeStruct((b, e, t), F32)],
        scratch_shapes=scratch,
        compiler_params=_params(("arbitrary",), 60),
        name="pool_mixer_2d" if two_d else "pool_mixer_1d",
    )(x, mod, nmix, nffn, cmat, cnt, pool_w, pool_scale, w_router)


def _route_kernel(lg_ref, slot_ref, gate_ref, *, cap):
    lg = lg_ref[0]
    e, t = lg.shape
    ex = jnp.exp(lg - jnp.max(lg, axis=0, keepdims=True))
    aff = ex / jnp.sum(ex, axis=0, keepdims=True)
    capf = jnp.float32(cap)

    def as_f32(v):
        return lax.bitcast_convert_type(v, F32)

    def bisect(i, v):
        cand = v | jnp.left_shift(jnp.int32(1), 30 - i)
        cnt = jnp.sum(jnp.where(aff >= as_f32(cand), 1.0, 0.0), axis=1, keepdims=True)
        return jnp.where(cnt >= capf, cand, v)

    kth = lax.fori_loop(0, 31, bisect, jnp.zeros((e, 1), jnp.int32))
    above = as_f32(kth + 1)
    gt = aff >= above
    eq = (aff >= as_f32(kth)) & jnp.logical_not(gt)
    need = capf - jnp.sum(jnp.where(gt, 1.0, 0.0), axis=1, keepdims=True)

    blk = min(t, MXU_DIM)
    before = (_iota((blk, blk), 0) < _iota((blk, blk), 1)).astype(BF16)

    def excl_cumsum(mask):
        ones = jnp.where(mask, 1.0, 0.0)
        outs, run = [], jnp.zeros((e, 1), F32)
        for b0 in range(0, t, blk):
            mb = ones[:, b0:b0 + blk]
            outs.append(_dot(mb.astype(BF16), before) + run)
            run = run + jnp.sum(mb, axis=1, keepdims=True)
        return jnp.concatenate(outs, axis=1) if len(outs) > 1 else outs[0]

    sel = gt | (eq & (excl_cumsum(eq) < need))
    slot_ref[0] = jnp.where(sel, excl_cumsum(sel), -1.0)
    gate_ref[0] = aff


def _route(logits, cap):
    b, e, t = logits.shape
    spec = pl.BlockSpec((1, e, t), lambda i: (i, 0, 0))
    return pl.pallas_call(
        functools.partial(_route_kernel, cap=cap),
        grid=(b,),
        in_specs=[spec],
        out_specs=[spec, spec],
        out_shape=[jax.ShapeDtypeStruct((b, e, t), F32)] * 2,
        compiler_params=_params(("arbitrary",), 32),
        name="route",
    )(logits)


def _gather_kernel(h_ref, slot_ref, o_ref, *, cap):
    hx = h_ref[0]
    t = hx.shape[0]
    row = _iota((cap, t), 0).astype(F32)
    for i in range(o_ref.shape[0]):
        onehot = jnp.where(slot_ref[0, i:i + 1, :] == row, 1.0, 0.0).astype(BF16)
        o_ref[i] = _dot(onehot, hx).astype(BF16)


def _gather(hx2, slot, cap):
    b, t, d = hx2.shape
    e = slot.shape[1]
    eb = 8
    return pl.pallas_call(
        functools.partial(_gather_kernel, cap=cap),
        grid=(b, e // eb),
        in_specs=[pl.BlockSpec((1, t, d), lambda i, j: (i, 0, 0)),
                  pl.BlockSpec((1, eb, t), lambda i, j: (i, j, 0))],
        out_specs=pl.BlockSpec((eb, cap, d), lambda i, j: (j, i, 0)),
        out_shape=jax.ShapeDtypeStruct((e, b * cap, d), BF16),
        compiler_params=_params(("arbitrary", "arbitrary"), 48),
        name="gather",
    )(hx2, slot)


def _ffn_kernel(*refs, n_sets, mc):
    x_refs = refs[:n_sets]
    wg_ref, wu_ref, wd_ref = refs[n_sets:n_sets + 3]
    y_refs = refs[n_sets + 3:2 * n_sets + 3]
    acc_refs = refs[2 * n_sets + 3:3 * n_sets + 3]
    wgb, wub, wdb = refs[3 * n_sets + 3:]
    f = pl.program_id(1)
    wgb[...] = wg_ref[0, 0].astype(BF16)
    wub[...] = wu_ref[0, 0].astype(BF16)
    wdb[...] = wd_ref[0, 0].astype(BF16)
    last = f == pl.num_programs(1) - 1

    @pl.when((pl.program_id(0) == 0) & (f == 0))
    def _():
        for acc in acc_refs:
            acc[...] = jnp.zeros(acc.shape, F32)

    for x_ref, y_ref, acc in zip(x_refs, y_refs, acc_refs):
        m = x_ref.shape[1]
        step = min(m, mc)
        for m0 in range(0, m, step):
            xs = x_ref[0, m0:m0 + step, :]
            hg = _dot(xs, wgb[...])
            hu = _dot(xs, wub[...])
            hid = (hg * _sigmoid(hg) * hu).astype(BF16)
            tot = acc[m0:m0 + step, :] + _dot(hid, wdb[...])
            acc[m0:m0 + step, :] = jnp.where(last, 0.0, tot)
            y_ref[0, m0:m0 + step, :] = tot.astype(BF16)


def _expert_ffn(xs, layer, w_gate, w_up, w_down):
    _, e, d, hidden = w_gate.shape
    tf = 512
    n = len(xs)
    xspecs = [pl.BlockSpec((1, x.shape[1], d), lambda i, j: (i, 0, 0)) for x in xs]
    return pl.pallas_call(
        functools.partial(_ffn_kernel, n_sets=n, mc=2048),
        grid=(e, hidden // tf),
        in_specs=xspecs + [pl.BlockSpec((1, 1, d, tf), lambda i, j: (layer, i, 0, j)),
                           pl.BlockSpec((1, 1, d, tf), lambda i, j: (layer, i, 0, j)),
                           pl.BlockSpec((1, 1, tf, d), lambda i, j: (layer, i, j, 0))],
        out_specs=xspecs,
        out_shape=[jax.ShapeDtypeStruct(x.shape, BF16) for x in xs],
        scratch_shapes=[pltpu.VMEM((x.shape[1], d), F32) for x in xs]
        + [pltpu.VMEM((d, tf), BF16), pltpu.VMEM((d, tf), BF16), pltpu.VMEM((tf, d), BF16)],
        compiler_params=_params(("arbitrary", "arbitrary"), 60),
        name="expert_ffn",
    )(*xs, w_gate, w_up, w_down)


def _scatter_kernel(x_ref, y_ref, slot_ref, gate_ref, mod_ref, fn_ref, o_ref, *, final):
    tt = x_ref.shape[1]
    e, cap, _ = y_ref.shape
    eye = (_iota((tt, tt), 0) == _iota((tt, tt), 1)).astype(BF16)
    slot_t = _dot_nt(eye, (slot_ref[0] + 1.0).astype(BF16))
    gate_t = _dot_nt(eye, gate_ref[0].astype(BF16))
    lane = _iota((tt, cap), 1).astype(F32) + 1.0
    acc = None
    for i in range(e):
        pt = jnp.where(slot_t[:, i:i + 1] == lane, gate_t[:, i:i + 1], 0.0).astype(BF16)
        part = _dot(pt, y_ref[i])
        acc = part if acc is None else acc + part
    out = x_ref[0] + mod_ref[0][5:6] * acc
    if final:
        out = out * lax.rsqrt(jnp.mean(out * out, axis=-1, keepdims=True) + EPS) * fn_ref[...]
    o_ref[0] = out


def _scatter(x, y, slot, gate, mod, final_norm, cap, final):
    b, t, d = x.shape
    e = slot.shape[1]
    tt = min(t, 512)
    return pl.pallas_call(
        functools.partial(_scatter_kernel, final=final),
        grid=(b, t // tt),
        in_specs=[pl.BlockSpec((1, tt, d), lambda i, j: (i, j, 0)),
                  pl.BlockSpec((e, cap, d), lambda i, j: (0, i, 0)),
                  pl.BlockSpec((1, e, tt), lambda i, j: (i, 0, j)),
                  pl.BlockSpec((1, e, tt), lambda i, j: (i, 0, j)),
                  pl.BlockSpec((1, 8, d), lambda i, j: (i, 0, 0)),
                  pl.BlockSpec((1, d), lambda i, j: (0, 0))],
        out_specs=pl.BlockSpec((1, tt, d), lambda i, j: (i, j, 0)),
        out_shape=jax.ShapeDtypeStruct((b, t, d), F32),
        compiler_params=_params(("arbitrary", "arbitrary"), 48),
        name="scatter",
    )(x, y, slot, gate, mod, final_norm)


def _moe(streams, layer, w_gate, w_up, w_down, final_norm, final):
    routed = []
    for x1, hx2, logits, mod in streams:
        t = x1.shape[1]
        cap = CAPACITY_FACTOR * t // N_EXPERTS
        slot, gate = _route(logits, cap)
        routed.append((slot, gate, cap, _gather(hx2, slot, cap)))
    ys = _expert_ffn([r[3] for r in routed], layer, w_gate, w_up, w_down)
    outs = []
    for (x1, _, _, mod), (slot, gate, cap, _), y in zip(streams, routed, ys):
        outs.append(_scatter(x1, y, slot, gate, mod, final_norm, cap, final))
    return outs


def _inproj_kernel(x_ref, mod_ref, nmix_ref, wko_ref, wqvt_ref, wg_ref, bg_ref,
                   qt_ref, k_ref, vt_ref, o_ref, gr_ref, *, chunk, qk, dk):
    tt = x_ref.shape[1]
    mod = mod_ref[0]
    h = _norm_mod(x_ref[0], nmix_ref[...], mod[0:1], mod[1:2])
    hb = h.astype(BF16)
    p = _dot(hb, wko_ref[...])
    k_ref[0] = (p[:, :qk] * (dk ** -0.5)).astype(BF16)
    o_ref[0] = _sigmoid(p[:, qk:]).astype(o_ref.dtype)
    p_t = _dot_nt(wqvt_ref[...], hb)
    for ci in range(tt // chunk):
        qt_ref[0, ci] = p_t[:qk, ci * chunk:(ci + 1) * chunk].astype(BF16)
        vt_ref[0, ci] = p_t[qk:, ci * chunk:(ci + 1) * chunk].astype(BF16)

    nh = MLSTM_HEADS
    g_t = _transpose_exact(_dot_split(h, wg_ref[...]) + bg_ref[...])
    row = _iota((4 * nh, chunk), 0)
    is_f = (row & nh) == nh
    a = _iota((chunk, chunk), 0)
    c = _iota((chunk, chunk), 1)
    le = (a <= c).astype(BF16)
    ge = (a >= c).astype(BF16)
    for c0 in range(0, tt, chunk):
        gc = g_t[:, c0:c0 + chunk]
        pieces = _split_bf16(jnp.where(is_f, _log_sigmoid(gc), 0.0), 3)
        pre = _dot(pieces[0], le) + (_dot(pieces[1], le) + _dot(pieces[2], le))
        suf = _dot(pieces[0], ge) + (_dot(pieces[1], ge) + _dot(pieces[2], ge))
        gr_ref[0, :, c0:c0 + chunk] = jnp.where(is_f, jnp.where(row < 2 * nh, pre, suf), gc)


def _inproj(x, mod, nmix, w_ko, w_qvt, w_g, b_g, qk, vd):
    b, t, d = x.shape
    tt = min(t, 512)
    ng = w_g.shape[1]
    ln = SCAN_CHUNK
    full = lambda shape: pl.BlockSpec(shape, lambda i, j: (0,) * len(shape))
    tok = lambda n: pl.BlockSpec((1, tt, n), lambda i, j: (i, j, 0))
    slab = lambda n: pl.BlockSpec((1, tt // ln, n, ln), lambda i, j: (i, j, 0, 0))
    return pl.pallas_call(
        functools.partial(_inproj_kernel, chunk=ln, qk=qk, dk=qk // MLSTM_HEADS),
        grid=(b, t // tt),
        in_specs=[tok(d), pl.BlockSpec((1, 8, d), lambda i, j: (i, 0, 0)), full((1, d)),
                  full(w_ko.shape), full(w_qvt.shape), full(w_g.shape), full(b_g.shape)],
        out_specs=[slab(qk), tok(qk), slab(vd), tok(vd),
                   pl.BlockSpec((1, ng, tt), lambda i, j: (i, 0, j))],
        out_shape=[jax.ShapeDtypeStruct((b, t // ln, qk, ln), BF16), jax.ShapeDtypeStruct((b, t, qk), BF16),
                   jax.ShapeDtypeStruct((b, t // ln, vd, ln), BF16), jax.ShapeDtypeStruct((b, t, vd), BF16),
                   jax.ShapeDtypeStruct((b, ng, t), F32)],
        compiler_params=_params(("arbitrary", "arbitrary"), 56),
        name="mlstm_inproj",
    )(x, mod, nmix, w_ko, w_qvt, w_g, b_g)


def _scan_kernel(kc_ref, vc_ref, grc_ref, qx_ref, kx_ref, vx_ref, grx_ref,
                 out_ref, s_ref, sall_ref, mall_ref, *, chunk):
    ln = chunk
    nh = MLSTM_HEADS
    pair = pl.program_id(1)
    kl = kx_ref.shape[2]
    dv = vx_ref.shape[2] // 2
    lane = _iota((ln, kl), 1)
    kmask = (lane < kl // 2, lane >= kl // 2)
    ones = jnp.ones((dv, ln), BF16)
    si = _iota((ln, ln), 0)
    ji = _iota((ln, ln), 1)
    causal = (si <= ji, si >= ji)
    ncc = kc_ref.shape[1] // ln
    ncx = kx_ref.shape[1] // ln

    def gate_rows(gr_ref, hh, dirn, c):
        base = 2 * nh * dirn + 2 * pair + hh
        return gr_ref[0, base, pl.ds(c, 1), :], gr_ref[0, base + nh, pl.ds(c, 1), :]

    def keys(k_ref, c):
        k = k_ref[0, pl.ds(pl.multiple_of(c * ln, ln), ln), :]
        return [jnp.where(kmask[hh], k, jnp.zeros((), BF16)) for hh in range(2)]

    def values_t(v_ref, c, hh):
        return jnp.concatenate([v_ref[0, c, hh * dv:(hh + 1) * dv, :], ones], axis=0)

    def advance(refs, n, record):
        k_ref, v_ref, gr_ref = refs

        def body(i, ms):
            new_ms = []
            cs = (i, n - 1 - i)
            khs = [keys(k_ref, c) for c in cs]
            for hh in range(2):
                for dirn in range(2):
                    ch = 2 * hh + dirn
                    c = cs[dirn]
                    ig, bc = gate_rows(gr_ref, hh, dirn, c)
                    m = ms[ch]
                    s_old = s_ref[ch]
                    if record:
                        sall_ref[ch * ncx + c] = s_old.astype(BF16)
                        mall_ref[ch * ncx + c] = jnp.broadcast_to(m, (8, ln))
                    b_end = bc[:, ln - 1:ln] if dirn == 0 else bc[:, 0:1]
                    gl = b_end - bc + ig
                    m_new = jnp.maximum(b_end + m, jnp.max(gl, axis=1, keepdims=True))
                    vw = (values_t(v_ref, c, hh) * jnp.exp(gl - m_new)).astype(BF16)
                    s_ref[ch] = jnp.exp(b_end + m - m_new) * s_old + _dot(vw, khs[dirn][hh])
                    new_ms.append(m_new)
            return tuple(new_ms)

        return body

    s_ref[...] = jnp.zeros(s_ref.shape, F32)
    ms = tuple(jnp.zeros((1, 1), F32) for _ in range(4))
    ms = lax.fori_loop(0, ncc, advance((kc_ref, vc_ref, grc_ref), ncc, False), ms, unroll=min(ncc, SCAN_UNROLL))
    lax.fori_loop(0, ncx, advance((kx_ref, vx_ref, grx_ref), ncx, True), ms, unroll=min(ncx, SCAN_UNROLL))

    def emit(c, _):
        q_t = qx_ref[0, c]
        q_tf = q_t.astype(F32)
        s_pair = _dot(jnp.concatenate(keys(kx_ref, c), axis=0), q_t)
        for hh in range(2):
            s_kq = s_pair[hh * ln:(hh + 1) * ln, :]
            v_t = values_t(vx_ref, c, hh)
            hsum = None
            for dirn in range(2):
                ch = 2 * hh + dirn
                ig, bc = gate_rows(grx_ref, hh, dirn, c)
                u_col = jnp.broadcast_to(ig - bc, (ln, ln)).T
                a = bc + mall_ref[ch * ncx + c][0:1, :]
                dm = jnp.where(causal[dirn], u_col + bc, -jnp.inf)
                mj = jnp.maximum(a, jnp.max(dm, axis=0, keepdims=True))
                sm = (s_kq * jnp.exp(dm - mj)).astype(BF16)
                qw = (q_tf * jnp.exp(a - mj)).astype(BF16)
                num = _dot(jnp.concatenate([v_t, sall_ref[ch * ncx + c]], axis=1),
                           jnp.concatenate([sm, qw], axis=0))
                h = num[:dv, :] * (1.0 / jnp.maximum(jnp.abs(num[dv:dv + 1, :]), jnp.exp(-mj)))
                hsum = h if hsum is None else hsum + h
            hn = hsum * lax.rsqrt(jnp.mean(hsum * hsum, axis=0, keepdims=True) + EPS)
            out_ref[0, pl.ds(pl.multiple_of(c * ln, ln), ln), hh * dv:(hh + 1) * dv] = hn.T.astype(out_ref.dtype)
        return 0

    lax.fori_loop(0, ncx, emit, 0, unroll=min(ncx, SCAN_UNROLL))


def _scan(ctx_parts, lat_parts):
    kc, vc, grc = ctx_parts
    qx, kx, vx, grx = lat_parts
    b, t, qk = kx.shape
    vd = vx.shape[2]
    nh = MLSTM_HEADS
    ln = SCAN_CHUNK
    dv = vd // nh
    kl = 2 * (qk // nh)
    assert kl == ln and dv == ln, "scan kernel assumes key-pair lanes = head value dim = chunk"
    ncx = t // ln

    def specs(k, v, gr):
        tk = k.shape[1]
        return [pl.BlockSpec((1, tk, kl), lambda i, j: (i, 0, j)),
                pl.BlockSpec((1, tk // ln, 2 * dv, ln), lambda i, j: (i, 0, j, 0)),
                pl.BlockSpec((1,) + gr.shape[1:], lambda i, j: (i, 0, 0, 0))]

    return pl.pallas_call(
        functools.partial(_scan_kernel, chunk=ln),
        grid=(b, nh // 2),
        in_specs=specs(kc, vc, grc) + [pl.BlockSpec((1, ncx, kl, ln), lambda i, j: (i, 0, j, 0))] + specs(kx, vx, grx),
        out_specs=pl.BlockSpec((1, t, 2 * dv), lambda i, j: (i, 0, j)),
        out_shape=jax.ShapeDtypeStruct((b, t, vd), BF16),
        scratch_shapes=[pltpu.VMEM((4, 2 * dv, kl), F32),
                        pltpu.VMEM((4 * ncx, 2 * dv, kl), BF16),
                        pltpu.VMEM((4 * ncx, 8, ln), F32)],
        compiler_params=_params(("arbitrary", "arbitrary"), 40),
        name="mlstm_scan",
    )(kc, vc, grc, qx, kx, vx, grx)


def _mlstm_out_kernel(hn_ref, og_ref, x_ref, mod_ref, mn_ref, wout_ref, nffn_ref, wr_ref,
                      x1_ref, hx2_ref, lg_ref):
    mod = mod_ref[0]
    a = (hn_ref[0].astype(F32) * mn_ref[...]) * og_ref[0].astype(F32)
    xn = x_ref[0] + mod[2:3] * _dot(a.astype(BF16), wout_ref[...])
    x1_ref[0] = xn
    hb, lg = _ffn_pre(xn, mod, nffn_ref[...], wr_ref[...])
    hx2_ref[0] = hb
    lg_ref[0] = lg


def _mlstm_out(hn, og, x, mod, mnorm, w_out, nffn, w_router):
    b, t, d = x.shape
    vd = hn.shape[2]
    e = w_router.shape[1]
    tt = min(t, 512)
    full = lambda shape: pl.BlockSpec(shape, lambda i, j: (0,) * len(shape))
    tok = lambda n: pl.BlockSpec((1, tt, n), lambda i, j: (i, j, 0))
    return pl.pallas_call(
        _mlstm_out_kernel,
        grid=(b, t // tt),
        in_specs=[tok(vd), tok(vd), tok(d), pl.BlockSpec((1, 8, d), lambda i, j: (i, 0, 0)),
                  full((1, vd)), full(w_out.shape), full((1, d)), full(w_router.shape)],
        out_specs=[tok(d), tok(d), pl.BlockSpec((1, e, tt), lambda i, j: (i, 0, j))],
        out_shape=[jax.ShapeDtypeStruct((b, t, d), F32), jax.ShapeDtypeStruct((b, t, d), BF16),
                   jax.ShapeDtypeStruct((b, e, t), F32)],
        compiler_params=_params(("arbitrary", "arbitrary"), 48),
        name="mlstm_out",
    )(hn, og, x, mod, mnorm, w_out, nffn, w_router)


def _chunked_rows(gr):
    b, n, t = gr.shape
    return gr.reshape(b, n, t // SCAN_CHUNK, SCAN_CHUNK)


def kernel(x, c, ctx, c_ctx, ada_w, ada_b, norm_mix, norm_ffn, pool_w, pool_scale, mlstm_w_in, mlstm_b_gates,
           mlstm_norm, mlstm_w_out, moe_router, moe_w_gate, moe_w_up, moe_w_down, final_norm):
    bsz, seq, d = x.shape
    depth = ada_w.shape[0]
    n_mixers = 2

    cc = jnp.concatenate([c, c_ctx[None, :], jnp.zeros((16 - bsz - 1, d), F32)], axis=0)
    ada = _ada(cc, ada_w, ada_b).reshape(depth, 16, N_ADA, d)
    pad = jnp.zeros((bsz, 8 - N_ADA, d), F32)

    fn = final_norm.reshape(1, d)
    for i in range(depth):
        last = i == depth - 1
        j = i // n_mixers
        mod_x = jnp.concatenate([ada[i, :bsz], pad], axis=1)
        mod_c = jnp.concatenate([jnp.broadcast_to(ada[i, bsz][None], (bsz, N_ADA, d)), pad], axis=1)
        nmix = norm_mix[i].reshape(1, d)
        nffn = norm_ffn[i].reshape(1, d)
        w_router = moe_router[i]
        streams = []
        if i % n_mixers == 0:
            pw = pool_w[j].astype(BF16)
            ps = pool_scale[j].reshape(1, d)
            x1, hx2, lg = _pool_mixer(x, mod_x, nmix, nffn, pw, ps, w_router, True)
            streams.append((x1, hx2, lg, mod_x))
            if not last:
                c1, hc2, lgc = _pool_mixer(ctx, mod_c, nmix, nffn, pw, ps, w_router, False)
                streams.append((c1, hc2, lgc, mod_c))
        else:
            qk = mlstm_w_in.shape[2] - 2 * mlstm_w_out.shape[1] - 4 * MLSTM_HEADS
            qk //= 2
            vd = mlstm_w_out.shape[1]
            w_in = mlstm_w_in[j]
            w_ko = jnp.concatenate([w_in[:, qk:2 * qk], w_in[:, 2 * qk + vd:2 * qk + 2 * vd]], axis=1).astype(BF16)
            w_qvt = jnp.concatenate([w_in[:, :qk], w_in[:, 2 * qk:2 * qk + vd]], axis=1).T.astype(BF16)
            w_g = w_in[:, 2 * qk + 2 * vd:]
            b_g = mlstm_b_gates[j].reshape(1, -1)
            proj = lambda s, m: _inproj(s, m, nmix, w_ko, w_qvt, w_g, b_g, qk, vd)
            _, kc, vc, _, grc = proj(ctx, mod_c)
            qx, kx, vx, ox, grx = proj(x, mod_x)
            hn = _scan((kc, vc, _chunked_rows(grc)), (qx, kx, vx, _chunked_rows(grx)))
            x1, hx2, lg = _mlstm_out(hn, ox, x, mod_x, mlstm_norm[j].reshape(1, vd),
                                     mlstm_w_out[j].astype(BF16), nffn, w_router)
            streams.append((x1, hx2, lg, mod_x))
            assert last, "context output of the mLSTM mixer is only needed by a following layer"
        outs = _moe(streams, i, moe_w_gate, moe_w_up, moe_w_down, fn, last)
        x = outs[0]
        if not last:
            ctx = outs[1]
    return x
```

```python
import functools

import jax
import jax.numpy as jnp
import numpy as np
from jax import lax
from jax.experimental import pallas as pl
from jax.experimental.pallas import tpu as pltpu

F32 = jnp.float32
BF16 = jnp.bfloat16
HIGHEST = lax.Precision.HIGHEST

GRID_W = 64
EPS = 1e-6
N_ADA = 6
POOL_WINDOWS = (2, 4, 8, 16)
N_POOL_GROUPS = 4
MLSTM_HEADS = 8
N_EXPERTS = 16
CAPACITY_FACTOR = 2

MIB = 1024 * 1024
MXU_DIM = 256
LANES = 128
TOKEN_BLOCK = 256
SLOT_GROUP = 32
PAIRS_PER_DOT = 16
SCAN_CHUNK = 128
SCAN_UNROLL = 4
POOL_PAD_ROWS = max(POOL_WINDOWS) // 2


def _params(sem, vmem_mib):
    return pltpu.CompilerParams(dimension_semantics=sem, vmem_limit_bytes=vmem_mib * MIB)


def _dot(a, b, precision=None):
    return jnp.dot(a, b, preferred_element_type=F32, precision=precision)


def _dot_nt(a, b, precision=None):
    return lax.dot_general(a, b, (((1,), (1,)), ((), ())), preferred_element_type=F32, precision=precision)


def _iota(shape, dim, dtype=jnp.int32):
    return lax.broadcasted_iota(dtype, shape, dim)


def _sigmoid(x):
    return 1.0 / (1.0 + jnp.exp(-x))


def _log_sigmoid(x):
    return jnp.minimum(x, 0.0) - jnp.log1p(jnp.exp(-jnp.abs(x)))


def _norm_mod(x, g, shift, scale):
    inv = lax.rsqrt(jnp.mean(x * x, axis=-1, keepdims=True) + EPS)
    return (x * inv) * (g * (1.0 + scale)) + shift


def _ada_kernel(c_ref, w_ref, b_ref, o_ref):
    c = c_ref[...]
    s = c * _sigmoid(c)
    o_ref[0] = _dot(s, w_ref[0], HIGHEST) + b_ref[0]


def _ada(cc, ada_w, ada_b):
    depth, d, n = ada_w.shape
    rows = cc.shape[0]
    tn = n // 4
    return pl.pallas_call(
        _ada_kernel,
        grid=(depth, n // tn),
        in_specs=[pl.BlockSpec((rows, d), lambda i, j: (0, 0)),
                  pl.BlockSpec((1, d, tn), lambda i, j: (i, 0, j)),
                  pl.BlockSpec((1, 1, tn), lambda i, j: (i, 0, j))],
        out_specs=pl.BlockSpec((1, rows, tn), lambda i, j: (i, 0, j)),
        out_shape=jax.ShapeDtypeStruct((depth, rows, n), F32),
        compiler_params=_params(("arbitrary", "arbitrary"), 40),
        name="ada",
    )(cc, ada_w, ada_b.reshape(depth, 1, n))


def _split_bf16(x, pieces):
    out = []
    for _ in range(pieces):
        p = x.astype(BF16)
        out.append(p)
        x = x - p.astype(F32)
    return out


def _dot_split(a, b):
    n = b.shape[1]
    a_hi, a_lo = _split_bf16(a, 2)
    b_hi, b_lo = _split_bf16(b, 2)
    hi = _dot(a_hi, jnp.concatenate([b_hi, b_lo], axis=1))
    return hi[:, :n] + (hi[:, n:] + _dot(a_lo, b_hi))


def _transpose_exact(x):
    m = x.shape[1]
    eye = (_iota((m, m), 0) == _iota((m, m), 1)).astype(BF16)
    hi, mid, lo = _split_bf16(x, 3)
    return _dot_nt(eye, hi) + (_dot_nt(eye, mid) + _dot_nt(eye, lo))


def _ffn_pre(xn, mod, nffn, wr):
    h2 = _norm_mod(xn, nffn, mod[3:4], mod[4:5])
    return h2.astype(BF16), _transpose_exact(_dot_split(h2, wr))


def _pool_kernel(x_ref, mod_ref, nmix_ref, nffn_ref, cmat_ref, cnt_ref, pw_ref, ps_ref, wr_ref,
                 x1_ref, hx2_ref, lg_ref, inv_ref, *pad, two_d, tc):
    t, d = x_ref.shape[1], x_ref.shape[2]
    gd = d // N_POOL_GROUPS
    mod = mod_ref[0]
    pad_tok = POOL_PAD_ROWS * GRID_W

    for c0 in range(0, t, tc):
        xs = x_ref[0, c0:c0 + tc, :]
        inv_ref[c0:c0 + tc, :] = lax.rsqrt(jnp.mean(xs * xs, axis=-1, keepdims=True) + EPS)

    def hx_of(r0, rn, j):
        cs = slice(j * gd, (j + 1) * gd)
        xs = x_ref[0, r0:r0 + rn, cs]
        g = nmix_ref[:, cs] * (1.0 + mod[1:2, cs])
        return xs * inv_ref[r0:r0 + rn, :] * g + mod[0:1, cs]

    if two_d:
        pad_ref, = pad
        pad_ref[0:pad_tok, :] = jnp.zeros((pad_tok, gd), F32)
        pad_ref[pad_tok + t:pad_tok + t + pad_tok, :] = jnp.zeros((pad_tok, gd), F32)

    for j, w in enumerate(POOL_WINDOWS):
        cs = slice(j * gd, (j + 1) * gd)
        cm = cmat_ref[j]
        sums = []
        for b0 in range(0, t, MXU_DIM):
            g = hx_of(b0, MXU_DIM, j)
            g_hi = g.astype(BF16)
            g_lo = (g - g_hi.astype(F32)).astype(BF16)
            csum = _dot(cm, g_hi) + _dot(cm, g_lo)
            if two_d:
                pad_ref[pad_tok + b0:pad_tok + b0 + MXU_DIM, :] = csum
            else:
                sums.append(csum)
        for c0 in range(0, t, tc):
            if two_d:
                tot = None
                for dr in range(-(w // 2), w - w // 2):
                    o = pad_tok + c0 + dr * GRID_W
                    sl = pad_ref[o:o + tc, :]
                    tot = sl if tot is None else tot + sl
            else:
                tot = sums[c0 // MXU_DIM]
            mean = tot / cnt_ref[c0:c0 + tc, j:j + 1]
            diff = mean - hx_of(c0, tc, j)
            y = _dot(diff.astype(BF16), pw_ref[j]) * ps_ref[:, cs]
            x1_ref[0, c0:c0 + tc, cs] = x_ref[0, c0:c0 + tc, cs] + mod[2:3, cs] * y

    for c0 in range(0, t, tc):
        hb, lg = _ffn_pre(x1_ref[0, c0:c0 + tc, :], mod, nffn_ref[...], wr_ref[...])
        hx2_ref[0, c0:c0 + tc, :] = hb
        lg_ref[0, :, c0:c0 + tc] = lg


def _pool_consts(t, two_d):
    cm = np.zeros((len(POOL_WINDOWS), MXU_DIM, MXU_DIM), np.float32)
    cnt = np.zeros((t, len(POOL_WINDOWS)), np.float32)
    n = GRID_W if two_d else t
    pos = np.arange(MXU_DIM)
    for j, w in enumerate(POOL_WINDOWS):
        col = pos % n
        lo = np.clip(col - w // 2, 0, n)
        hi = np.clip(col + w - w // 2, 0, n)
        same = (pos[:, None] // n) == (pos[None, :] // n)
        cm[j] = (same & (col[None, :] >= lo[:, None]) & (col[None, :] < hi[:, None])).astype(np.float32)
        tt = np.arange(t)
        c = tt % n
        ccnt = np.clip(c + w - w // 2, 0, n) - np.clip(c - w // 2, 0, n)
        if two_d:
            rows = t // GRID_W
            r = tt // GRID_W
            rcnt = np.clip(r + w - w // 2, 0, rows) - np.clip(r - w // 2, 0, rows)
            cnt[:, j] = ccnt * rcnt
        else:
            cnt[:, j] = ccnt
    return jnp.asarray(cm, BF16), jnp.asarray(cnt)


def _pool_mixer(x, mod, nmix, nffn, pool_w, pool_scale, w_router, two_d):
    b, t, d = x.shape
    gd = d // N_POOL_GROUPS
    if not two_d:
        assert t == MXU_DIM
    cmat, cnt = _pool_consts(t, two_d)
    tc = min(t, 512)
    pad_tok = POOL_PAD_ROWS * GRID_W
    scratch = [pltpu.VMEM((t, 1), F32)]
    if two_d:
        scratch.append(pltpu.VMEM((t + 2 * pad_tok, gd), F32))
    e = w_router.shape[1]
    full = lambda shape: pl.BlockSpec(shape, lambda i: (0,) * len(shape))
    return pl.pallas_call(
        functools.partial(_pool_kernel, two_d=two_d, tc=tc),
        grid=(b,),
        in_specs=[pl.BlockSpec((1, t, d), lambda i: (i, 0, 0)),
                  pl.BlockSpec((1, 8, d), lambda i: (i, 0, 0)),
                  full((1, d)), full((1, d)), full(cmat.shape), full(cnt.shape),
                  full(pool_w.shape), full((1, d)), full(w_router.shape)],
        out_specs=[pl.BlockSpec((1, t, d), lambda i: (i, 0, 0)),
                   pl.BlockSpec((1, t, d), lambda i: (i, 0, 0)),
                   pl.BlockSpec((1, e, t), lambda i: (i, 0, 0))],
        out_shape=[jax.ShapeDtypeStruct((b, t, d), F32),
                   jax.ShapeDtypeStruct((b, t, d), BF16),
                   jax.ShapeDtypeStruct((b, e, t), F32)],
        scratch_shapes=scratch,
        compiler_params=_params(("arbitrary",), 60),
        name="pool_mixer_2d" if two_d else "pool_mixer_1d",
    )(x, mod, nmix, nffn, cmat, cnt, pool_w, pool_scale, w_router)


def _route_kernel(lg_ref, slot_ref, gate_ref, offs_ref, *, cap):
    lg = lg_ref[0]
    e, t = lg.shape
    ex = jnp.exp(lg - jnp.max(lg, axis=0, keepdims=True))
    aff = ex / jnp.sum(ex, axis=0, keepdims=True)
    capf = jnp.float32(cap)

    def as_f32(v):
        return lax.bitcast_convert_type(v, F32)

    def bisect(i, v):
        cand = v | jnp.left_shift(jnp.int32(1), 30 - i)
        cnt = jnp.sum(jnp.where(aff >= as_f32(cand), 1.0, 0.0), axis=1, keepdims=True)
        return jnp.where(cnt >= capf, cand, v)

    kth = lax.fori_loop(0, 31, bisect, jnp.zeros((e, 1), jnp.int32))
    above = as_f32(kth + 1)
    gt = aff >= above
    eq = (aff >= as_f32(kth)) & jnp.logical_not(gt)
    need = capf - jnp.sum(jnp.where(gt, 1.0, 0.0), axis=1, keepdims=True)

    blk = min(t, MXU_DIM)
    before = (_iota((blk, blk), 0) < _iota((blk, blk), 1)).astype(BF16)

    def excl_cumsum(mask):
        ones = jnp.where(mask, 1.0, 0.0)
        outs, run = [], jnp.zeros((e, 1), F32)
        for b0 in range(0, t, blk):
            mb = ones[:, b0:b0 + blk]
            outs.append(_dot(mb.astype(BF16), before) + run)
            run = run + jnp.sum(mb, axis=1, keepdims=True)
        return jnp.concatenate(outs, axis=1) if len(outs) > 1 else outs[0]

    sel = gt | (eq & (excl_cumsum(eq) < need))
    slot_ref[0] = jnp.where(sel, excl_cumsum(sel), -1.0)
    gate_ref[0] = aff
    nl = offs_ref.shape[2]
    starts = (_iota((t, nl), 0) < _iota((t, nl), 1) * TOKEN_BLOCK).astype(BF16)
    offs_ref[0] = _dot(jnp.where(sel, 1.0, 0.0).astype(BF16), starts)


def _route(logits, cap):
    b, e, t = logits.shape
    spec = pl.BlockSpec((1, e, t), lambda i: (i, 0, 0))
    ospec = pl.BlockSpec((1, e, LANES), lambda i: (i, 0, 0))
    return pl.pallas_call(
        functools.partial(_route_kernel, cap=cap),
        grid=(b,),
        in_specs=[spec],
        out_specs=[spec, spec, ospec],
        out_shape=[jax.ShapeDtypeStruct((b, e, t), F32)] * 2 + [jax.ShapeDtypeStruct((b, e, LANES), F32)],
        compiler_params=_params(("arbitrary",), 32),
        name="route",
    )(logits)


def _gather_kernel(h_ref, slot_ref, o_ref, *, cap):
    hx = h_ref[0]
    t = hx.shape[0]
    row = _iota((cap, t), 0).astype(F32)
    for i in range(o_ref.shape[0]):
        onehot = jnp.where(slot_ref[0, i:i + 1, :] == row, 1.0, 0.0).astype(BF16)
        o_ref[i] = _dot(onehot, hx).astype(BF16)


def _gather(hx2, slot, cap):
    b, t, d = hx2.shape
    e = slot.shape[1]
    eb = 8
    return pl.pallas_call(
        functools.partial(_gather_kernel, cap=cap),
        grid=(b, e // eb),
        in_specs=[pl.BlockSpec((1, t, d), lambda i, j: (i, 0, 0)),
                  pl.BlockSpec((1, eb, t), lambda i, j: (i, j, 0))],
        out_specs=pl.BlockSpec((eb, cap, d), lambda i, j: (j, i, 0)),
        out_shape=jax.ShapeDtypeStruct((e, b * cap, d), BF16),
        compiler_params=_params(("arbitrary", "arbitrary"), 48),
        name="gather",
    )(hx2, slot)


STATIC_DOTS = 2


def _pair_list(offs_ref, n_exp, nblk, le_ref, lg_ref):
    b, kb = pl.program_id(0), pl.program_id(1)
    shift = SLOT_GROUP.bit_length() - 1
    max_groups = le_ref.shape[0] // (n_exp + 1)

    cnt = jnp.int32(0)
    for e in range(n_exp):
        o0 = offs_ref[b, e * (nblk + 1) + kb]
        o1 = offs_ref[b, e * (nblk + 1) + kb + 1]
        lo = jnp.right_shift(o0, shift)
        ng = jnp.where(o1 > o0, jnp.right_shift(o1 - 1, shift) - lo + 1, 0)
        for i in range(max_groups):
            le_ref[cnt + i] = e
            lg_ref[cnt + i] = lo + i
        cnt = cnt + ng
    return cnt


def _pair(le_ref, lg_ref, idx, cnt):
    valid = idx < cnt
    safe = jnp.minimum(idx, jnp.maximum(cnt - 1, 0))
    e_p = jnp.where(valid, le_ref[safe], 0)
    g_p = jnp.where(valid, lg_ref[safe], 0)
    base = jnp.where(valid, g_p * SLOT_GROUP, -2 * SLOT_GROUP).astype(F32)
    return e_p, pl.multiple_of(g_p * SLOT_GROUP, SLOT_GROUP), base


def _for_each_dot(cnt, body):
    for c in range(STATIC_DOTS):
        body(c, 0)
    lax.fori_loop(STATIC_DOTS, (cnt + PAIRS_PER_DOT - 1) // PAIRS_PER_DOT, body, 0)


def _sparse_gather_kernel(offs_ref, h_ref, slot_ref, o_ref, le_ref, lg_ref, p_ref, *, n_exp, nblk):
    sg = SLOT_GROUP

    @pl.when(pl.program_id(1) == 0)
    def _():
        o_ref[...] = jnp.zeros(o_ref.shape, o_ref.dtype)

    cnt = _pair_list(offs_ref, n_exp, nblk, le_ref, lg_ref)
    sub = _iota((sg, TOKEN_BLOCK), 0).astype(F32)

    def dot_batch(c, _):
        dst = []
        for p in range(PAIRS_PER_DOT):
            e_p, s0, base = _pair(le_ref, lg_ref, c * PAIRS_PER_DOT + p, cnt)
            row = slot_ref[0, pl.ds(e_p, 1), :]
            p_ref[p * sg:(p + 1) * sg, :] = jnp.where(row == base + sub, 1.0, 0.0).astype(BF16)
            dst.append((e_p, s0))
        z = _dot(p_ref[...], h_ref[0])
        for p, (e_p, s0) in enumerate(dst):
            o_ref[e_p, pl.ds(s0, sg), :] += z[p * sg:(p + 1) * sg, :].astype(o_ref.dtype)
        return 0

    _for_each_dot(cnt, dot_batch)


def _sparse_gather(hx2, slot, offs, cap):
    b, t, d = hx2.shape
    e = slot.shape[1]
    nblk = t // TOKEN_BLOCK
    return pl.pallas_call(
        functools.partial(_sparse_gather_kernel, n_exp=e, nblk=nblk),
        grid_spec=pltpu.PrefetchScalarGridSpec(
            num_scalar_prefetch=1,
            grid=(b, nblk),
            in_specs=[pl.BlockSpec((1, TOKEN_BLOCK, d), lambda i, j, o: (i, j, 0)),
                      pl.BlockSpec((1, e, TOKEN_BLOCK), lambda i, j, o: (i, 0, j))],
            out_specs=pl.BlockSpec((e, cap, d), lambda i, j, o: (0, i, 0)),
            scratch_shapes=[pltpu.SMEM(((e + 1) * (cap // SLOT_GROUP),), jnp.int32),
                            pltpu.SMEM(((e + 1) * (cap // SLOT_GROUP),), jnp.int32),
                            pltpu.VMEM((PAIRS_PER_DOT * SLOT_GROUP, TOKEN_BLOCK), BF16)]),
        out_shape=jax.ShapeDtypeStruct((e, b * cap, d), BF16),
        compiler_params=_params(("arbitrary", "arbitrary"), 40),
        name="sparse_gather",
    )(offs, hx2, slot)


def _sparse_scatter_kernel(offs_ref, x_ref, y_ref, slot_ref, gate_ref, mod_ref, fn_ref, o_ref,
                           le_ref, lg_ref, p_ref, yc_ref, acc_ref, *, n_exp, nblk, final):
    sg = SLOT_GROUP
    cnt = _pair_list(offs_ref, n_exp, nblk, le_ref, lg_ref)
    sub = _iota((sg, TOKEN_BLOCK), 0).astype(F32)
    acc_ref[...] = jnp.zeros(acc_ref.shape, F32)

    def dot_batch(c, _):
        for p in range(PAIRS_PER_DOT):
            e_p, s0, base = _pair(le_ref, lg_ref, c * PAIRS_PER_DOT + p, cnt)
            row = slot_ref[0, pl.ds(e_p, 1), :]
            gate = gate_ref[0, pl.ds(e_p, 1), :]
            p_ref[p * sg:(p + 1) * sg, :] = jnp.where(row == base + sub, gate, 0.0)
            yc_ref[p * sg:(p + 1) * sg, :] = y_ref[e_p, pl.ds(s0, sg), :]
        acc_ref[...] += _dot(p_ref[...].T.astype(BF16), yc_ref[...])
        return 0

    _for_each_dot(cnt, dot_batch)
    out = x_ref[0] + mod_ref[0][5:6] * acc_ref[...]
    if final:
        out = out * lax.rsqrt(jnp.mean(out * out, axis=-1, keepdims=True) + EPS) * fn_ref[...]
    o_ref[0] = out


def _sparse_scatter(x, y, slot, gate, offs, mod, final_norm, cap, final):
    b, t, d = x.shape
    e = slot.shape[1]
    nblk = t // TOKEN_BLOCK
    rows = PAIRS_PER_DOT * SLOT_GROUP
    tok = lambda n: pl.BlockSpec((1, TOKEN_BLOCK, n), lambda i, j, o: (i, j, 0))
    exp = pl.BlockSpec((1, e, TOKEN_BLOCK), lambda i, j, o: (i, 0, j))
    return pl.pallas_call(
        functools.partial(_sparse_scatter_kernel, n_exp=e, nblk=nblk, final=final),
        grid_spec=pltpu.PrefetchScalarGridSpec(
            num_scalar_prefetch=1,
            grid=(b, nblk),
            in_specs=[tok(d), pl.BlockSpec((e, cap, d), lambda i, j, o: (0, i, 0)), exp, exp,
                      pl.BlockSpec((1, 8, d), lambda i, j, o: (i, 0, 0)),
                      pl.BlockSpec((1, d), lambda i, j, o: (0, 0))],
            out_specs=tok(d),
            scratch_shapes=[pltpu.SMEM(((e + 1) * (cap // SLOT_GROUP),), jnp.int32),
                            pltpu.SMEM(((e + 1) * (cap // SLOT_GROUP),), jnp.int32),
                            pltpu.VMEM((rows, TOKEN_BLOCK), F32),
                            pltpu.VMEM((rows, d), BF16),
                            pltpu.VMEM((TOKEN_BLOCK, d), F32)]),
        out_shape=jax.ShapeDtypeStruct((b, t, d), F32),
        compiler_params=_params(("arbitrary", "arbitrary"), 48),
        name="sparse_scatter",
    )(offs, x, y, slot, gate, mod, final_norm)


def _ffn_kernel(*refs, n_sets, mc):
    x_refs = refs[:n_sets]
    wg_ref, wu_ref, wd_ref = refs[n_sets:n_sets + 3]
    y_refs = refs[n_sets + 3:2 * n_sets + 3]
    acc_refs = refs[2 * n_sets + 3:3 * n_sets + 3]
    wgb, wub, wdb = refs[3 * n_sets + 3:]
    f = pl.program_id(1)
    wgb[...] = wg_ref[0, 0].astype(BF16)
    wub[...] = wu_ref[0, 0].astype(BF16)
    wdb[...] = wd_ref[0, 0].astype(BF16)
    last = f == pl.num_programs(1) - 1

    @pl.when((pl.program_id(0) == 0) & (f == 0))
    def _():
        for acc in acc_refs:
            acc[...] = jnp.zeros(acc.shape, F32)

    for x_ref, y_ref, acc in zip(x_refs, y_refs, acc_refs):
        m = x_ref.shape[1]
        step = min(m, mc)
        for m0 in range(0, m, step):
            xs = x_ref[0, m0:m0 + step, :]
            hg = _dot(xs, wgb[...])
            hu = _dot(xs, wub[...])
            hid = (hg * _sigmoid(hg) * hu).astype(BF16)
            tot = acc[m0:m0 + step, :] + _dot(hid, wdb[...])
            acc[m0:m0 + step, :] = jnp.where(last, 0.0, tot)
            y_ref[0, m0:m0 + step, :] = tot.astype(BF16)


def _expert_ffn(xs, layer, w_gate, w_up, w_down):
    _, e, d, hidden = w_gate.shape
    tf = 512
    n = len(xs)
    xspecs = [pl.BlockSpec((1, x.shape[1], d), lambda i, j: (i, 0, 0)) for x in xs]
    return pl.pallas_call(
        functools.partial(_ffn_kernel, n_sets=n, mc=2048),
        grid=(e, hidden // tf),
        in_specs=xspecs + [pl.BlockSpec((1, 1, d, tf), lambda i, j: (layer, i, 0, j)),
                           pl.BlockSpec((1, 1, d, tf), lambda i, j: (layer, i, 0, j)),
                           pl.BlockSpec((1, 1, tf, d), lambda i, j: (layer, i, j, 0))],
        out_specs=xspecs,
        out_shape=[jax.ShapeDtypeStruct(x.shape, BF16) for x in xs],
        scratch_shapes=[pltpu.VMEM((x.shape[1], d), F32) for x in xs]
        + [pltpu.VMEM((d, tf), BF16), pltpu.VMEM((d, tf), BF16), pltpu.VMEM((tf, d), BF16)],
        compiler_params=_params(("arbitrary", "arbitrary"), 60),
        name="expert_ffn",
    )(*xs, w_gate, w_up, w_down)


def _scatter_kernel(x_ref, y_ref, slot_ref, gate_ref, mod_ref, fn_ref, o_ref, *, final):
    tt = x_ref.shape[1]
    e, cap, _ = y_ref.shape
    eye = (_iota((tt, tt), 0) == _iota((tt, tt), 1)).astype(BF16)
    slot_t = _dot_nt(eye, (slot_ref[0] + 1.0).astype(BF16))
    gate_t = _dot_nt(eye, gate_ref[0].astype(BF16))
    lane = _iota((tt, cap), 1).astype(F32) + 1.0
    acc = None
    for i in range(e):
        pt = jnp.where(slot_t[:, i:i + 1] == lane, gate_t[:, i:i + 1], 0.0).astype(BF16)
        part = _dot(pt, y_ref[i])
        acc = part if acc is None else acc + part
    out = x_ref[0] + mod_ref[0][5:6] * acc
    if final:
        out = out * lax.rsqrt(jnp.mean(out * out, axis=-1, keepdims=True) + EPS) * fn_ref[...]
    o_ref[0] = out


def _scatter(x, y, slot, gate, mod, final_norm, cap, final):
    b, t, d = x.shape
    e = slot.shape[1]
    tt = min(t, 512)
    return pl.pallas_call(
        functools.partial(_scatter_kernel, final=final),
        grid=(b, t // tt),
        in_specs=[pl.BlockSpec((1, tt, d), lambda i, j: (i, j, 0)),
                  pl.BlockSpec((e, cap, d), lambda i, j: (0, i, 0)),
                  pl.BlockSpec((1, e, tt), lambda i, j: (i, 0, j)),
                  pl.BlockSpec((1, e, tt), lambda i, j: (i, 0, j)),
                  pl.BlockSpec((1, 8, d), lambda i, j: (i, 0, 0)),
                  pl.BlockSpec((1, d), lambda i, j: (0, 0))],
        out_specs=pl.BlockSpec((1, tt, d), lambda i, j: (i, j, 0)),
        out_shape=jax.ShapeDtypeStruct((b, t, d), F32),
        compiler_params=_params(("arbitrary", "arbitrary"), 48),
        name="scatter",
    )(x, y, slot, gate, mod, final_norm)


def _moe(streams, layer, w_gate, w_up, w_down, final_norm, final):
    routed = []
    for x1, hx2, logits, mod in streams:
        t = x1.shape[1]
        cap = CAPACITY_FACTOR * t // N_EXPERTS
        slot, gate, offs = _route(logits, cap)
        nblk = t // TOKEN_BLOCK
        if nblk > 1:
            offs = offs[:, :, :nblk + 1].astype(jnp.int32).reshape(offs.shape[0], -1)
            xg = _sparse_gather(hx2, slot, offs, cap)
        else:
            offs = None
            xg = _gather(hx2, slot, cap)
        routed.append((slot, gate, offs, cap, xg))
    ys = _expert_ffn([r[4] for r in routed], layer, w_gate, w_up, w_down)
    outs = []
    for (x1, _, _, mod), (slot, gate, offs, cap, _), y in zip(streams, routed, ys):
        if offs is None:
            outs.append(_scatter(x1, y, slot, gate, mod, final_norm, cap, final))
        else:
            outs.append(_sparse_scatter(x1, y, slot, gate, offs, mod, final_norm, cap, final))
    return outs


def _inproj_kernel(x_ref, mod_ref, nmix_ref, wko_ref, wqvt_ref, wg_ref, bg_ref,
                   qt_ref, k_ref, vt_ref, o_ref, gr_ref, *, chunk, qk, dk):
    tt = x_ref.shape[1]
    mod = mod_ref[0]
    h = _norm_mod(x_ref[0], nmix_ref[...], mod[0:1], mod[1:2])
    hb = h.astype(BF16)
    p = _dot(hb, wko_ref[...])
    k_ref[0] = (p[:, :qk] * (dk ** -0.5)).astype(BF16)
    o_ref[0] = _sigmoid(p[:, qk:]).astype(o_ref.dtype)
    p_t = _dot_nt(wqvt_ref[...], hb)
    for ci in range(tt // chunk):
        qt_ref[0, ci] = p_t[:qk, ci * chunk:(ci + 1) * chunk].astype(BF16)
        vt_ref[0, ci] = p_t[qk:, ci * chunk:(ci + 1) * chunk].astype(BF16)

    nh = MLSTM_HEADS
    g_t = _transpose_exact(_dot_split(h, wg_ref[...]) + bg_ref[...])
    row = _iota((4 * nh, chunk), 0)
    is_f = (row & nh) == nh
    a = _iota((chunk, chunk), 0)
    c = _iota((chunk, chunk), 1)
    le = (a <= c).astype(BF16)
    ge = (a >= c).astype(BF16)
    for c0 in range(0, tt, chunk):
        gc = g_t[:, c0:c0 + chunk]
        pieces = _split_bf16(jnp.where(is_f, _log_sigmoid(gc), 0.0), 3)
        pre = _dot(pieces[0], le) + (_dot(pieces[1], le) + _dot(pieces[2], le))
        suf = _dot(pieces[0], ge) + (_dot(pieces[1], ge) + _dot(pieces[2], ge))
        gr_ref[0, :, c0:c0 + chunk] = jnp.where(is_f, jnp.where(row < 2 * nh, pre, suf), gc)


def _inproj(x, mod, nmix, w_ko, w_qvt, w_g, b_g, qk, vd):
    b, t, d = x.shape
    tt = min(t, 512)
    ng = w_g.shape[1]
    ln = SCAN_CHUNK
    full = lambda shape: pl.BlockSpec(shape, lambda i, j: (0,) * len(shape))
    tok = lambda n: pl.BlockSpec((1, tt, n), lambda i, j: (i, j, 0))
    slab = lambda n: pl.BlockSpec((1, tt // ln, n, ln), lambda i, j: (i, j, 0, 0))
    return pl.pallas_call(
        functools.partial(_inproj_kernel, chunk=ln, qk=qk, dk=qk // MLSTM_HEADS),
        grid=(b, t // tt),
        in_specs=[tok(d), pl.BlockSpec((1, 8, d), lambda i, j: (i, 0, 0)), full((1, d)),
                  full(w_ko.shape), full(w_qvt.shape), full(w_g.shape), full(b_g.shape)],
        out_specs=[slab(qk), tok(qk), slab(vd), tok(vd),
                   pl.BlockSpec((1, ng, tt), lambda i, j: (i, 0, j))],
        out_shape=[jax.ShapeDtypeStruct((b, t // ln, qk, ln), BF16), jax.ShapeDtypeStruct((b, t, qk), BF16),
                   jax.ShapeDtypeStruct((b, t // ln, vd, ln), BF16), jax.ShapeDtypeStruct((b, t, vd), BF16),
                   jax.ShapeDtypeStruct((b, ng, t), F32)],
        compiler_params=_params(("arbitrary", "arbitrary"), 56),
        name="mlstm_inproj",
    )(x, mod, nmix, w_ko, w_qvt, w_g, b_g)


def _scan_kernel(kc_ref, vc_ref, grc_ref, qx_ref, kx_ref, vx_ref, grx_ref,
                 out_ref, s_ref, sall_ref, mall_ref, *, chunk):
    ln = chunk
    nh = MLSTM_HEADS
    pair = pl.program_id(1)
    kl = kx_ref.shape[2]
    dv = vx_ref.shape[2] // 2
    lane = _iota((ln, kl), 1)
    kmask = (lane < kl // 2, lane >= kl // 2)
    ones = jnp.ones((dv, ln), BF16)
    si = _iota((ln, ln), 0)
    ji = _iota((ln, ln), 1)
    causal = (si <= ji, si >= ji)
    ncc = kc_ref.shape[1] // ln
    ncx = kx_ref.shape[1] // ln

    def gate_rows(gr_ref, hh, dirn, c):
        base = 2 * nh * dirn + 2 * pair + hh
        return gr_ref[0, base, pl.ds(c, 1), :], gr_ref[0, base + nh, pl.ds(c, 1), :]

    def keys(k_ref, c):
        k = k_ref[0, pl.ds(pl.multiple_of(c * ln, ln), ln), :]
        return [jnp.where(kmask[hh], k, jnp.zeros((), BF16)) for hh in range(2)]

    def values_t(v_ref, c, hh):
        return jnp.concatenate([v_ref[0, c, hh * dv:(hh + 1) * dv, :], ones], axis=0)

    def advance(refs, n, record):
        k_ref, v_ref, gr_ref = refs

        def body(i, ms):
            new_ms = []
            cs = (i, n - 1 - i)
            khs = [keys(k_ref, c) for c in cs]
            for hh in range(2):
                for dirn in range(2):
                    ch = 2 * hh + dirn
                    c = cs[dirn]
                    ig, bc = gate_rows(gr_ref, hh, dirn, c)
                    m = ms[ch]
                    s_old = s_ref[ch]
                    if record:
                        sall_ref[ch * ncx + c] = s_old.astype(BF16)
                        mall_ref[ch * ncx + c] = jnp.broadcast_to(m, (8, ln))
                    b_end = bc[:, ln - 1:ln] if dirn == 0 else bc[:, 0:1]
                    gl = b_end - bc + ig
                    m_new = jnp.maximum(b_end + m, jnp.max(gl, axis=1, keepdims=True))
                    vw = (values_t(v_ref, c, hh) * jnp.exp(gl - m_new)).astype(BF16)
                    s_ref[ch] = jnp.exp(b_end + m - m_new) * s_old + _dot(vw, khs[dirn][hh])
                    new_ms.append(m_new)
            return tuple(new_ms)

        return body

    s_ref[...] = jnp.zeros(s_ref.shape, F32)
    ms = tuple(jnp.zeros((1, 1), F32) for _ in range(4))
    ms = lax.fori_loop(0, ncc, advance((kc_ref, vc_ref, grc_ref), ncc, False), ms, unroll=min(ncc, SCAN_UNROLL))
    lax.fori_loop(0, ncx, advance((kx_ref, vx_ref, grx_ref), ncx, True), ms, unroll=min(ncx, SCAN_UNROLL))

    def emit(c, _):
        q_t = qx_ref[0, c]
        q_tf = q_t.astype(F32)
        s_pair = _dot(jnp.concatenate(keys(kx_ref, c), axis=0), q_t)
        for hh in range(2):
            s_kq = s_pair[hh * ln:(hh + 1) * ln, :]
            v_t = values_t(vx_ref, c, hh)
            hsum = None
            for dirn in range(2):
                ch = 2 * hh + dirn
                ig, bc = gate_rows(grx_ref, hh, dirn, c)
                u_col = jnp.broadcast_to(ig - bc, (ln, ln)).T
                a = bc + mall_ref[ch * ncx + c][0:1, :]
                dm = jnp.where(causal[dirn], u_col + bc, -jnp.inf)
                mj = jnp.maximum(a, jnp.max(dm, axis=0, keepdims=True))
                sm = (s_kq * jnp.exp(dm - mj)).astype(BF16)
                qw = (q_tf * jnp.exp(a - mj)).astype(BF16)
                num = _dot(jnp.concatenate([v_t, sall_ref[ch * ncx + c]], axis=1),
                           jnp.concatenate([sm, qw], axis=0))
                h = num[:dv, :] * (1.0 / jnp.maximum(jnp.abs(num[dv:dv + 1, :]), jnp.exp(-mj)))
                hsum = h if hsum is None else hsum + h
            hn = hsum * lax.rsqrt(jnp.mean(hsum * hsum, axis=0, keepdims=True) + EPS)
            out_ref[0, pl.ds(pl.multiple_of(c * ln, ln), ln), hh * dv:(hh + 1) * dv] = hn.T.astype(out_ref.dtype)
        return 0

    lax.fori_loop(0, ncx, emit, 0, unroll=min(ncx, SCAN_UNROLL))


def _scan(ctx_parts, lat_parts):
    kc, vc, grc = ctx_parts
    qx, kx, vx, grx = lat_parts
    b, t, qk = kx.shape
    vd = vx.shape[2]
    nh = MLSTM_HEADS
    ln = SCAN_CHUNK
    dv = vd // nh
    kl = 2 * (qk // nh)
    assert kl == ln and dv == ln, "scan kernel assumes key-pair lanes = head value dim = chunk"
    ncx = t // ln

    def specs(k, v, gr):
        tk = k.shape[1]
        return [pl.BlockSpec((1, tk, kl), lambda i, j: (i, 0, j)),
                pl.BlockSpec((1, tk // ln, 2 * dv, ln), lambda i, j: (i, 0, j, 0)),
                pl.BlockSpec((1,) + gr.shape[1:], lambda i, j: (i, 0, 0, 0))]

    return pl.pallas_call(
        functools.partial(_scan_kernel, chunk=ln),
        grid=(b, nh // 2),
        in_specs=specs(kc, vc, grc) + [pl.BlockSpec((1, ncx, kl, ln), lambda i, j: (i, 0, j, 0))] + specs(kx, vx, grx),
        out_specs=pl.BlockSpec((1, t, 2 * dv), lambda i, j: (i, 0, j)),
        out_shape=jax.ShapeDtypeStruct((b, t, vd), BF16),
        scratch_shapes=[pltpu.VMEM((4, 2 * dv, kl), F32),
                        pltpu.VMEM((4 * ncx, 2 * dv, kl), BF16),
                        pltpu.VMEM((4 * ncx, 8, ln), F32)],
        compiler_params=_params(("arbitrary", "arbitrary"), 40),
        name="mlstm_scan",
    )(kc, vc, grc, qx, kx, vx, grx)


def _mlstm_out_kernel(hn_ref, og_ref, x_ref, mod_ref, mn_ref, wout_ref, nffn_ref, wr_ref,
                      x1_ref, hx2_ref, lg_ref):
    mod = mod_ref[0]
    a = (hn_ref[0].astype(F32) * mn_ref[...]) * og_ref[0].astype(F32)
    xn = x_ref[0] + mod[2:3] * _dot(a.astype(BF16), wout_ref[...])
    x1_ref[0] = xn
    hb, lg = _ffn_pre(xn, mod, nffn_ref[...], wr_ref[...])
    hx2_ref[0] = hb
    lg_ref[0] = lg


def _mlstm_out(hn, og, x, mod, mnorm, w_out, nffn, w_router):
    b, t, d = x.shape
    vd = hn.shape[2]
    e = w_router.shape[1]
    tt = min(t, 512)
    full = lambda shape: pl.BlockSpec(shape, lambda i, j: (0,) * len(shape))
    tok = lambda n: pl.BlockSpec((1, tt, n), lambda i, j: (i, j, 0))
    return pl.pallas_call(
        _mlstm_out_kernel,
        grid=(b, t // tt),
        in_specs=[tok(vd), tok(vd), tok(d), pl.BlockSpec((1, 8, d), lambda i, j: (i, 0, 0)),
                  full((1, vd)), full(w_out.shape), full((1, d)), full(w_router.shape)],
        out_specs=[tok(d), tok(d), pl.BlockSpec((1, e, tt), lambda i, j: (i, 0, j))],
        out_shape=[jax.ShapeDtypeStruct((b, t, d), F32), jax.ShapeDtypeStruct((b, t, d), BF16),
                   jax.ShapeDtypeStruct((b, e, t), F32)],
        compiler_params=_params(("arbitrary", "arbitrary"), 48),
        name="mlstm_out",
    )(hn, og, x, mod, mnorm, w_out, nffn, w_router)


def _chunked_rows(gr):
    b, n, t = gr.shape
    return gr.reshape(b, n, t // SCAN_CHUNK, SCAN_CHUNK)


def kernel(x, c, ctx, c_ctx, ada_w, ada_b, norm_mix, norm_ffn, pool_w, pool_scale, mlstm_w_in, mlstm_b_gates,
           mlstm_norm, mlstm_w_out, moe_router, moe_w_gate, moe_w_up, moe_w_down, final_norm):
    bsz, seq, d = x.shape
    depth = ada_w.shape[0]
    n_mixers = 2

    cc = jnp.concatenate([c, c_ctx[None, :], jnp.zeros((16 - bsz - 1, d), F32)], axis=0)
    ada = _ada(cc, ada_w, ada_b).reshape(depth, 16, N_ADA, d)
    pad = jnp.zeros((bsz, 8 - N_ADA, d), F32)

    fn = final_norm.reshape(1, d)
    for i in range(depth):
        last = i == depth - 1
        j = i // n_mixers
        mod_x = jnp.concatenate([ada[i, :bsz], pad], axis=1)
        mod_c = jnp.concatenate([jnp.broadcast_to(ada[i, bsz][None], (bsz, N_ADA, d)), pad], axis=1)
        nmix = norm_mix[i].reshape(1, d)
        nffn = norm_ffn[i].reshape(1, d)
        w_router = moe_router[i]
        streams = []
        if i % n_mixers == 0:
            pw = pool_w[j].astype(BF16)
            ps = pool_scale[j].reshape(1, d)
            x1, hx2, lg = _pool_mixer(x, mod_x, nmix, nffn, pw, ps, w_router, True)
            streams.append((x1, hx2, lg, mod_x))
            if not last:
                c1, hc2, lgc = _pool_mixer(ctx, mod_c, nmix, nffn, pw, ps, w_router, False)
                streams.append((c1, hc2, lgc, mod_c))
        else:
            qk = mlstm_w_in.shape[2] - 2 * mlstm_w_out.shape[1] - 4 * MLSTM_HEADS
            qk //= 2
            vd = mlstm_w_out.shape[1]
            w_in = mlstm_w_in[j]
            w_ko = jnp.concatenate([w_in[:, qk:2 * qk], w_in[:, 2 * qk + vd:2 * qk + 2 * vd]], axis=1).astype(BF16)
            w_qvt = jnp.concatenate([w_in[:, :qk], w_in[:, 2 * qk:2 * qk + vd]], axis=1).T.astype(BF16)
            w_g = w_in[:, 2 * qk + 2 * vd:]
            b_g = mlstm_b_gates[j].reshape(1, -1)
            proj = lambda s, m: _inproj(s, m, nmix, w_ko, w_qvt, w_g, b_g, qk, vd)
            _, kc, vc, _, grc = proj(ctx, mod_c)
            qx, kx, vx, ox, grx = proj(x, mod_x)
            hn = _scan((kc, vc, _chunked_rows(grc)), (qx, kx, vx, _chunked_rows(grx)))
            x1, hx2, lg = _mlstm_out(hn, ox, x, mod_x, mlstm_norm[j].reshape(1, vd),
                                     mlstm_w_out[j].astype(BF16), nffn, w_router)
            streams.append((x1, hx2, lg, mod_x))
            assert last, "context output of the mLSTM mixer is only needed by a following layer"
        outs = _moe(streams, i, moe_w_gate, moe_w_up, moe_w_down, fn, last)
        x = outs[0]
        if not last:
            ctx = outs[1]
    return x
```

```python
import functools

import jax
import jax.numpy as jnp
import numpy as np
from jax import lax
from jax.experimental import pallas as pl
from jax.experimental.pallas import tpu as pltpu

F32 = jnp.float32
BF16 = jnp.bfloat16
HIGHEST = lax.Precision.HIGHEST

GRID_W = 64
EPS = 1e-6
N_ADA = 6
POOL_WINDOWS = (2, 4, 8, 16)
N_POOL_GROUPS = 4
MLSTM_HEADS = 8
N_EXPERTS = 16
CAPACITY_FACTOR = 2

MIB = 1024 * 1024
MXU_DIM = 256
LANES = 128
TOKEN_BLOCK = 256
SLOT_GROUP = 32
PAIRS_PER_DOT = 16
SCAN_CHUNK = 128
SCAN_UNROLL = 8
POOL_PAD_ROWS = max(POOL_WINDOWS) // 2


def _params(sem, vmem_mib):
    return pltpu.CompilerParams(dimension_semantics=sem, vmem_limit_bytes=vmem_mib * MIB)


def _dot(a, b, precision=None):
    return jnp.dot(a, b, preferred_element_type=F32, precision=precision)


def _dot_nt(a, b, precision=None):
    return lax.dot_general(a, b, (((1,), (1,)), ((), ())), preferred_element_type=F32, precision=precision)


def _iota(shape, dim, dtype=jnp.int32):
    return lax.broadcasted_iota(dtype, shape, dim)


def _sigmoid(x):
    return 1.0 / (1.0 + jnp.exp(-x))


def _log_sigmoid(x):
    return jnp.minimum(x, 0.0) - jnp.log1p(jnp.exp(-jnp.abs(x)))


def _norm_mod(x, g, shift, scale):
    inv = lax.rsqrt(jnp.mean(x * x, axis=-1, keepdims=True) + EPS)
    return (x * inv) * (g * (1.0 + scale)) + shift


def _ada_kernel(c_ref, w_ref, b_ref, o_ref):
    c = c_ref[...]
    s = c * _sigmoid(c)
    s_hi, s_lo = _split_bf16(s, 2)
    w_hi, w_lo = _split_bf16(w_ref[0], 2)
    o_ref[0] = _dot(s_hi, w_hi) + (_dot(s_hi, w_lo) + _dot(s_lo, w_hi)) + b_ref[0]


def _ada(cc, ada_w, ada_b):
    depth, d, n = ada_w.shape
    rows = cc.shape[0]
    tn = n // 4
    return pl.pallas_call(
        _ada_kernel,
        grid=(depth, n // tn),
        in_specs=[pl.BlockSpec((rows, d), lambda i, j: (0, 0)),
                  pl.BlockSpec((1, d, tn), lambda i, j: (i, 0, j)),
                  pl.BlockSpec((1, 1, tn), lambda i, j: (i, 0, j))],
        out_specs=pl.BlockSpec((1, rows, tn), lambda i, j: (i, 0, j)),
        out_shape=jax.ShapeDtypeStruct((depth, rows, n), F32),
        compiler_params=_params(("arbitrary", "arbitrary"), 40),
        name="ada",
    )(cc, ada_w, ada_b.reshape(depth, 1, n))


def _split_bf16(x, pieces):
    out = []
    for _ in range(pieces):
        p = x.astype(BF16)
        out.append(p)
        x = x - p.astype(F32)
    return out


def _dot_split(a, b):
    n = b.shape[1]
    a_hi, a_lo = _split_bf16(a, 2)
    b_hi, b_lo = _split_bf16(b, 2)
    hi = _dot(a_hi, jnp.concatenate([b_hi, b_lo], axis=1))
    return hi[:, :n] + (hi[:, n:] + _dot(a_lo, b_hi))


def _transpose_exact(x):
    m = x.shape[1]
    eye = (_iota((m, m), 0) == _iota((m, m), 1)).astype(BF16)
    hi, mid, lo = _split_bf16(x, 3)
    return _dot_nt(eye, hi) + (_dot_nt(eye, mid) + _dot_nt(eye, lo))


def _ffn_pre(xn, mod, nffn, wr):
    h2 = _norm_mod(xn, nffn, mod[3:4], mod[4:5])
    return h2.astype(BF16), _transpose_exact(_dot_split(h2, wr))


def _pool_kernel(x_ref, mod_ref, nmix_ref, nffn_ref, cmat_ref, cnt_ref, pw_ref, ps_ref, wr_ref,
                 x1_ref, hx2_ref, lg_ref, inv_ref, *pad, two_d, tc):
    t, d = x_ref.shape[1], x_ref.shape[2]
    gd = d // N_POOL_GROUPS
    mod = mod_ref[0]
    pad_tok = POOL_PAD_ROWS * GRID_W

    for c0 in range(0, t, tc):
        xs = x_ref[0, c0:c0 + tc, :]
        inv_ref[c0:c0 + tc, :] = lax.rsqrt(jnp.mean(xs * xs, axis=-1, keepdims=True) + EPS)

    def hx_of(r0, rn, j):
        cs = slice(j * gd, (j + 1) * gd)
        xs = x_ref[0, r0:r0 + rn, cs]
        g = nmix_ref[:, cs] * (1.0 + mod[1:2, cs])
        return xs * inv_ref[r0:r0 + rn, :] * g + mod[0:1, cs]

    if two_d:
        pad_ref, = pad
        pad_ref[0:pad_tok, :] = jnp.zeros((pad_tok, gd), F32)
        pad_ref[pad_tok + t:pad_tok + t + pad_tok, :] = jnp.zeros((pad_tok, gd), F32)

    for j, w in enumerate(POOL_WINDOWS):
        cs = slice(j * gd, (j + 1) * gd)
        cm = cmat_ref[j]
        sums = []
        for b0 in range(0, t, MXU_DIM):
            g = hx_of(b0, MXU_DIM, j)
            g_hi = g.astype(BF16)
            g_lo = (g - g_hi.astype(F32)).astype(BF16)
            csum = _dot(cm, g_hi) + _dot(cm, g_lo)
            if two_d:
                pad_ref[pad_tok + b0:pad_tok + b0 + MXU_DIM, :] = csum
            else:
                sums.append(csum)
        for c0 in range(0, t, tc):
            if two_d:
                tot = None
                for dr in range(-(w // 2), w - w // 2):
                    o = pad_tok + c0 + dr * GRID_W
                    sl = pad_ref[o:o + tc, :]
                    tot = sl if tot is None else tot + sl
            else:
                tot = sums[c0 // MXU_DIM]
            mean = tot / cnt_ref[c0:c0 + tc, j:j + 1]
            diff = mean - hx_of(c0, tc, j)
            y = _dot(diff.astype(BF16), pw_ref[j]) * ps_ref[:, cs]
            x1_ref[0, c0:c0 + tc, cs] = x_ref[0, c0:c0 + tc, cs] + mod[2:3, cs] * y

    for c0 in range(0, t, tc):
        hb, lg = _ffn_pre(x1_ref[0, c0:c0 + tc, :], mod, nffn_ref[...], wr_ref[...])
        hx2_ref[0, c0:c0 + tc, :] = hb
        lg_ref[0, :, c0:c0 + tc] = lg


def _pool_consts(t, two_d):
    cm = np.zeros((len(POOL_WINDOWS), MXU_DIM, MXU_DIM), np.float32)
    cnt = np.zeros((t, len(POOL_WINDOWS)), np.float32)
    n = GRID_W if two_d else t
    pos = np.arange(MXU_DIM)
    for j, w in enumerate(POOL_WINDOWS):
        col = pos % n
        lo = np.clip(col - w // 2, 0, n)
        hi = np.clip(col + w - w // 2, 0, n)
        same = (pos[:, None] // n) == (pos[None, :] // n)
        cm[j] = (same & (col[None, :] >= lo[:, None]) & (col[None, :] < hi[:, None])).astype(np.float32)
        tt = np.arange(t)
        c = tt % n
        ccnt = np.clip(c + w - w // 2, 0, n) - np.clip(c - w // 2, 0, n)
        if two_d:
            rows = t // GRID_W
            r = tt // GRID_W
            rcnt = np.clip(r + w - w // 2, 0, rows) - np.clip(r - w // 2, 0, rows)
            cnt[:, j] = ccnt * rcnt
        else:
            cnt[:, j] = ccnt
    return jnp.asarray(cm, BF16), jnp.asarray(cnt)


def _pool_mixer(x, mod, nmix, nffn, pool_w, pool_scale, w_router, two_d):
    b, t, d = x.shape
    gd = d // N_POOL_GROUPS
    if not two_d:
        assert t == MXU_DIM
    cmat, cnt = _pool_consts(t, two_d)
    tc = min(t, 512)
    pad_tok = POOL_PAD_ROWS * GRID_W
    scratch = [pltpu.VMEM((t, 1), F32)]
    if two_d:
        scratch.append(pltpu.VMEM((t + 2 * pad_tok, gd), F32))
    e = w_router.shape[1]
    full = lambda shape: pl.BlockSpec(shape, lambda i: (0,) * len(shape))
    return pl.pallas_call(
        functools.partial(_pool_kernel, two_d=two_d, tc=tc),
        grid=(b,),
        in_specs=[pl.BlockSpec((1, t, d), lambda i: (i, 0, 0)),
                  pl.BlockSpec((1, 8, d), lambda i: (i, 0, 0)),
                  full((1, d)), full((1, d)), full(cmat.shape), full(cnt.shape),
                  full(pool_w.shape), full((1, d)), full(w_router.shape)],
        out_specs=[pl.BlockSpec((1, t, d), lambda i: (i, 0, 0)),
                   pl.BlockSpec((1, t, d), lambda i: (i, 0, 0)),
                   pl.BlockSpec((1, e, t), lambda i: (i, 0, 0))],
        out_shape=[jax.ShapeDtypeStruct((b, t, d), F32),
                   jax.ShapeDtypeStruct((b, t, d), BF16),
                   jax.ShapeDtypeStruct((b, e, t), F32)],
        scratch_shapes=scratch,
        compiler_params=_params(("arbitrary",), 60),
        name="pool_mixer_2d" if two_d else "pool_mixer_1d",
    )(x, mod, nmix, nffn, cmat, cnt, pool_w, pool_scale, w_router)


def _route_kernel(lg_ref, slot_ref, gate_ref, offs_ref, *, cap):
    lg = lg_ref[0]
    e, t = lg.shape
    ex = jnp.exp(lg - jnp.max(lg, axis=0, keepdims=True))
    aff = ex / jnp.sum(ex, axis=0, keepdims=True)
    capf = jnp.float32(cap)

    def as_f32(v):
        return lax.bitcast_convert_type(v, F32)

    def enough(cand):
        return jnp.sum(jnp.where(aff >= as_f32(cand), 1.0, 0.0), axis=1, keepdims=True) >= capf

    def bisect2(i, v):
        hi = jnp.left_shift(jnp.int32(1), 30 - 2 * i)
        lo = jnp.left_shift(jnp.int32(1), 29 - 2 * i)
        c_hi, c_lo, c_both = v | hi, v | lo, v | hi | lo
        return jnp.where(enough(c_both), c_both, jnp.where(enough(c_hi), c_hi, jnp.where(enough(c_lo), c_lo, v)))

    kth = lax.fori_loop(0, 15, bisect2, jnp.zeros((e, 1), jnp.int32))
    kth = jnp.where(enough(kth | 1), kth | 1, kth)
    above = as_f32(kth + 1)
    gt = aff >= above
    eq = (aff >= as_f32(kth)) & jnp.logical_not(gt)
    need = capf - jnp.sum(jnp.where(gt, 1.0, 0.0), axis=1, keepdims=True)

    blk = min(t, MXU_DIM)
    before = (_iota((blk, blk), 0) < _iota((blk, blk), 1)).astype(BF16)

    def excl_cumsum(mask):
        ones = jnp.where(mask, 1.0, 0.0)
        outs, run = [], jnp.zeros((e, 1), F32)
        for b0 in range(0, t, blk):
            mb = ones[:, b0:b0 + blk]
            outs.append(_dot(mb.astype(BF16), before) + run)
            run = run + jnp.sum(mb, axis=1, keepdims=True)
        return jnp.concatenate(outs, axis=1) if len(outs) > 1 else outs[0]

    sel = gt | (eq & (excl_cumsum(eq) < need))
    slot_ref[0] = jnp.where(sel, excl_cumsum(sel), -1.0)
    gate_ref[0] = aff
    nl = offs_ref.shape[2]
    starts = (_iota((t, nl), 0) < _iota((t, nl), 1) * TOKEN_BLOCK).astype(BF16)
    offs_ref[0] = _dot(jnp.where(sel, 1.0, 0.0).astype(BF16), starts)


def _route(logits, cap):
    b, e, t = logits.shape
    spec = pl.BlockSpec((1, e, t), lambda i: (i, 0, 0))
    ospec = pl.BlockSpec((1, e, LANES), lambda i: (i, 0, 0))
    return pl.pallas_call(
        functools.partial(_route_kernel, cap=cap),
        grid=(b,),
        in_specs=[spec],
        out_specs=[spec, spec, ospec],
        out_shape=[jax.ShapeDtypeStruct((b, e, t), F32)] * 2 + [jax.ShapeDtypeStruct((b, e, LANES), F32)],
        compiler_params=_params(("arbitrary",), 32),
        name="route",
    )(logits)


def _gather_kernel(h_ref, slot_ref, o_ref, *, cap):
    hx = h_ref[0]
    t = hx.shape[0]
    row = _iota((cap, t), 0).astype(F32)
    for i in range(o_ref.shape[0]):
        onehot = jnp.where(slot_ref[0, i:i + 1, :] == row, 1.0, 0.0).astype(BF16)
        o_ref[i] = _dot(onehot, hx).astype(BF16)


def _gather(hx2, slot, cap):
    b, t, d = hx2.shape
    e = slot.shape[1]
    eb = 8
    return pl.pallas_call(
        functools.partial(_gather_kernel, cap=cap),
        grid=(b, e // eb),
        in_specs=[pl.BlockSpec((1, t, d), lambda i, j: (i, 0, 0)),
                  pl.BlockSpec((1, eb, t), lambda i, j: (i, j, 0))],
        out_specs=pl.BlockSpec((eb, cap, d), lambda i, j: (j, i, 0)),
        out_shape=jax.ShapeDtypeStruct((e, b * cap, d), BF16),
        compiler_params=_params(("arbitrary", "arbitrary"), 48),
        name="gather",
    )(hx2, slot)


STATIC_DOTS = 2


def _pair_list(offs_ref, n_exp, nblk, le_ref, lg_ref):
    b, kb = pl.program_id(0), pl.program_id(1)
    shift = SLOT_GROUP.bit_length() - 1
    max_groups = le_ref.shape[0] // (n_exp + 1)

    cnt = jnp.int32(0)
    for e in range(n_exp):
        o0 = offs_ref[b, e * (nblk + 1) + kb]
        o1 = offs_ref[b, e * (nblk + 1) + kb + 1]
        lo = jnp.right_shift(o0, shift)
        ng = jnp.where(o1 > o0, jnp.right_shift(o1 - 1, shift) - lo + 1, 0)
        for i in range(max_groups):
            le_ref[cnt + i] = e
            lg_ref[cnt + i] = lo + i
        cnt = cnt + ng
    return cnt


def _pair(le_ref, lg_ref, idx, cnt):
    valid = idx < cnt
    safe = jnp.minimum(idx, jnp.maximum(cnt - 1, 0))
    e_p = jnp.where(valid, le_ref[safe], 0)
    g_p = jnp.where(valid, lg_ref[safe], 0)
    base = jnp.where(valid, g_p * SLOT_GROUP, -2 * SLOT_GROUP).astype(F32)
    return e_p, pl.multiple_of(g_p * SLOT_GROUP, SLOT_GROUP), base


def _for_each_dot(cnt, body):
    for c in range(STATIC_DOTS):
        body(c, 0)
    lax.fori_loop(STATIC_DOTS, (cnt + PAIRS_PER_DOT - 1) // PAIRS_PER_DOT, body, 0)


def _sparse_gather_kernel(offs_ref, h_ref, slot_ref, o_ref, le_ref, lg_ref, p_ref, *, n_exp, nblk):
    sg = SLOT_GROUP

    @pl.when(pl.program_id(1) == 0)
    def _():
        o_ref[...] = jnp.zeros(o_ref.shape, o_ref.dtype)

    cnt = _pair_list(offs_ref, n_exp, nblk, le_ref, lg_ref)
    sub = _iota((sg, TOKEN_BLOCK), 0).astype(F32)

    def dot_batch(c, _):
        dst = []
        for p in range(PAIRS_PER_DOT):
            e_p, s0, base = _pair(le_ref, lg_ref, c * PAIRS_PER_DOT + p, cnt)
            row = slot_ref[0, pl.ds(e_p, 1), :]
            p_ref[p * sg:(p + 1) * sg, :] = jnp.where(row == base + sub, 1.0, 0.0).astype(BF16)
            dst.append((e_p, s0))
        z = _dot(p_ref[...], h_ref[0])
        for p, (e_p, s0) in enumerate(dst):
            o_ref[e_p, pl.ds(s0, sg), :] += z[p * sg:(p + 1) * sg, :].astype(o_ref.dtype)
        return 0

    _for_each_dot(cnt, dot_batch)


def _sparse_gather(hx2, slot, offs, cap):
    b, t, d = hx2.shape
    e = slot.shape[1]
    nblk = t // TOKEN_BLOCK
    return pl.pallas_call(
        functools.partial(_sparse_gather_kernel, n_exp=e, nblk=nblk),
        grid_spec=pltpu.PrefetchScalarGridSpec(
            num_scalar_prefetch=1,
            grid=(b, nblk),
            in_specs=[pl.BlockSpec((1, TOKEN_BLOCK, d), lambda i, j, o: (i, j, 0)),
                      pl.BlockSpec((1, e, TOKEN_BLOCK), lambda i, j, o: (i, 0, j))],
            out_specs=pl.BlockSpec((e, cap, d), lambda i, j, o: (0, i, 0)),
            scratch_shapes=[pltpu.SMEM(((e + 1) * (cap // SLOT_GROUP),), jnp.int32),
                            pltpu.SMEM(((e + 1) * (cap // SLOT_GROUP),), jnp.int32),
                            pltpu.VMEM((PAIRS_PER_DOT * SLOT_GROUP, TOKEN_BLOCK), BF16)]),
        out_shape=jax.ShapeDtypeStruct((e, b * cap, d), BF16),
        compiler_params=_params(("arbitrary", "arbitrary"), 40),
        name="sparse_gather",
    )(offs, hx2, slot)


def _sparse_scatter_kernel(offs_ref, x_ref, y_ref, slot_ref, gate_ref, mod_ref, fn_ref, o_ref,
                           le_ref, lg_ref, p_ref, yc_ref, acc_ref, *, n_exp, nblk, final):
    sg = SLOT_GROUP
    cnt = _pair_list(offs_ref, n_exp, nblk, le_ref, lg_ref)
    sub = _iota((sg, TOKEN_BLOCK), 0).astype(F32)
    acc_ref[...] = jnp.zeros(acc_ref.shape, F32)

    def dot_batch(c, _):
        for p in range(PAIRS_PER_DOT):
            e_p, s0, base = _pair(le_ref, lg_ref, c * PAIRS_PER_DOT + p, cnt)
            row = slot_ref[0, pl.ds(e_p, 1), :]
            gate = gate_ref[0, pl.ds(e_p, 1), :]
            p_ref[p * sg:(p + 1) * sg, :] = jnp.where(row == base + sub, gate, 0.0)
            yc_ref[p * sg:(p + 1) * sg, :] = y_ref[e_p, pl.ds(s0, sg), :]
        acc_ref[...] += _dot(p_ref[...].T.astype(BF16), yc_ref[...])
        return 0

    _for_each_dot(cnt, dot_batch)
    out = x_ref[0] + mod_ref[0][5:6] * acc_ref[...]
    if final:
        out = out * lax.rsqrt(jnp.mean(out * out, axis=-1, keepdims=True) + EPS) * fn_ref[...]
    o_ref[0] = out


def _sparse_scatter(x, y, slot, gate, offs, mod, final_norm, cap, final):
    b, t, d = x.shape
    e = slot.shape[1]
    nblk = t // TOKEN_BLOCK
    rows = PAIRS_PER_DOT * SLOT_GROUP
    tok = lambda n: pl.BlockSpec((1, TOKEN_BLOCK, n), lambda i, j, o: (i, j, 0))
    exp = pl.BlockSpec((1, e, TOKEN_BLOCK), lambda i, j, o: (i, 0, j))
    return pl.pallas_call(
        functools.partial(_sparse_scatter_kernel, n_exp=e, nblk=nblk, final=final),
        grid_spec=pltpu.PrefetchScalarGridSpec(
            num_scalar_prefetch=1,
            grid=(b, nblk),
            in_specs=[tok(d), pl.BlockSpec((e, cap, d), lambda i, j, o: (0, i, 0)), exp, exp,
                      pl.BlockSpec((1, 8, d), lambda i, j, o: (i, 0, 0)),
                      pl.BlockSpec((1, d), lambda i, j, o: (0, 0))],
            out_specs=tok(d),
            scratch_shapes=[pltpu.SMEM(((e + 1) * (cap // SLOT_GROUP),), jnp.int32),
                            pltpu.SMEM(((e + 1) * (cap // SLOT_GROUP),), jnp.int32),
                            pltpu.VMEM((rows, TOKEN_BLOCK), F32),
                            pltpu.VMEM((rows, d), BF16),
                            pltpu.VMEM((TOKEN_BLOCK, d), F32)]),
        out_shape=jax.ShapeDtypeStruct((b, t, d), F32),
        compiler_params=_params(("arbitrary", "arbitrary"), 48),
        name="sparse_scatter",
    )(offs, x, y, slot, gate, mod, final_norm)


def _ffn_kernel(*refs, n_sets, mc):
    x_refs = refs[:n_sets]
    wg_ref, wu_ref, wd_ref = refs[n_sets:n_sets + 3]
    y_refs = refs[n_sets + 3:2 * n_sets + 3]
    acc_refs = refs[2 * n_sets + 3:3 * n_sets + 3]
    wgb, wub, wdb = refs[3 * n_sets + 3:]
    f = pl.program_id(1)
    wgb[...] = wg_ref[0, 0].astype(BF16)
    wub[...] = wu_ref[0, 0].astype(BF16)
    wdb[...] = wd_ref[0, 0].astype(BF16)
    last = f == pl.num_programs(1) - 1

    @pl.when((pl.program_id(0) == 0) & (f == 0))
    def _():
        for acc in acc_refs:
            acc[...] = jnp.zeros(acc.shape, F32)

    for x_ref, y_ref, acc in zip(x_refs, y_refs, acc_refs):
        m = x_ref.shape[1]
        step = min(m, mc)
        for m0 in range(0, m, step):
            xs = x_ref[0, m0:m0 + step, :]
            hg = _dot(xs, wgb[...])
            hu = _dot(xs, wub[...])
            hid = (hg * _sigmoid(hg) * hu).astype(BF16)
            tot = acc[m0:m0 + step, :] + _dot(hid, wdb[...])
            acc[m0:m0 + step, :] = jnp.where(last, 0.0, tot)
            y_ref[0, m0:m0 + step, :] = tot.astype(BF16)


def _expert_ffn(xs, layer, w_gate, w_up, w_down):
    _, e, d, hidden = w_gate.shape
    tf = 512
    n = len(xs)
    xspecs = [pl.BlockSpec((1, x.shape[1], d), lambda i, j: (i, 0, 0)) for x in xs]
    return pl.pallas_call(
        functools.partial(_ffn_kernel, n_sets=n, mc=2048),
        grid=(e, hidden // tf),
        in_specs=xspecs + [pl.BlockSpec((1, 1, d, tf), lambda i, j: (layer, i, 0, j)),
                           pl.BlockSpec((1, 1, d, tf), lambda i, j: (layer, i, 0, j)),
                           pl.BlockSpec((1, 1, tf, d), lambda i, j: (layer, i, j, 0))],
        out_specs=xspecs,
        out_shape=[jax.ShapeDtypeStruct(x.shape, BF16) for x in xs],
        scratch_shapes=[pltpu.VMEM((x.shape[1], d), F32) for x in xs]
        + [pltpu.VMEM((d, tf), BF16), pltpu.VMEM((d, tf), BF16), pltpu.VMEM((tf, d), BF16)],
        compiler_params=_params(("arbitrary", "arbitrary"), 60),
        name="expert_ffn",
    )(*xs, w_gate, w_up, w_down)


def _scatter_kernel(x_ref, y_ref, slot_ref, gate_ref, mod_ref, fn_ref, o_ref, *, final):
    tt = x_ref.shape[1]
    e, cap, _ = y_ref.shape
    eye = (_iota((tt, tt), 0) == _iota((tt, tt), 1)).astype(BF16)
    slot_t = _dot_nt(eye, (slot_ref[0] + 1.0).astype(BF16))
    gate_t = _dot_nt(eye, gate_ref[0].astype(BF16))
    lane = _iota((tt, cap), 1).astype(F32) + 1.0
    acc = None
    for i in range(e):
        pt = jnp.where(slot_t[:, i:i + 1] == lane, gate_t[:, i:i + 1], 0.0).astype(BF16)
        part = _dot(pt, y_ref[i])
        acc = part if acc is None else acc + part
    out = x_ref[0] + mod_ref[0][5:6] * acc
    if final:
        out = out * lax.rsqrt(jnp.mean(out * out, axis=-1, keepdims=True) + EPS) * fn_ref[...]
    o_ref[0] = out


def _scatter(x, y, slot, gate, mod, final_norm, cap, final):
    b, t, d = x.shape
    e = slot.shape[1]
    tt = min(t, 512)
    return pl.pallas_call(
        functools.partial(_scatter_kernel, final=final),
        grid=(b, t // tt),
        in_specs=[pl.BlockSpec((1, tt, d), lambda i, j: (i, j, 0)),
                  pl.BlockSpec((e, cap, d), lambda i, j: (0, i, 0)),
                  pl.BlockSpec((1, e, tt), lambda i, j: (i, 0, j)),
                  pl.BlockSpec((1, e, tt), lambda i, j: (i, 0, j)),
                  pl.BlockSpec((1, 8, d), lambda i, j: (i, 0, 0)),
                  pl.BlockSpec((1, d), lambda i, j: (0, 0))],
        out_specs=pl.BlockSpec((1, tt, d), lambda i, j: (i, j, 0)),
        out_shape=jax.ShapeDtypeStruct((b, t, d), F32),
        compiler_params=_params(("arbitrary", "arbitrary"), 48),
        name="scatter",
    )(x, y, slot, gate, mod, final_norm)


def _moe(streams, layer, w_gate, w_up, w_down, final_norm, final):
    routed = []
    for x1, hx2, logits, mod in streams:
        t = x1.shape[1]
        cap = CAPACITY_FACTOR * t // N_EXPERTS
        slot, gate, offs = _route(logits, cap)
        nblk = t // TOKEN_BLOCK
        if nblk > 1:
            offs = offs[:, :, :nblk + 1].astype(jnp.int32).reshape(offs.shape[0], -1)
            xg = _sparse_gather(hx2, slot, offs, cap)
        else:
            offs = None
            xg = _gather(hx2, slot, cap)
        routed.append((slot, gate, offs, cap, xg))
    ys = _expert_ffn([r[4] for r in routed], layer, w_gate, w_up, w_down)
    outs = []
    for (x1, _, _, mod), (slot, gate, offs, cap, _), y in zip(streams, routed, ys):
        if offs is None:
            outs.append(_scatter(x1, y, slot, gate, mod, final_norm, cap, final))
        else:
            outs.append(_sparse_scatter(x1, y, slot, gate, offs, mod, final_norm, cap, final))
    return outs


def _inproj_kernel(x_ref, mod_ref, nmix_ref, wko_ref, wqvt_ref, wg_ref, bg_ref,
                   qt_ref, k_ref, vt_ref, o_ref, gr_ref, *, chunk, qk, dk):
    tt = x_ref.shape[1]
    mod = mod_ref[0]
    h = _norm_mod(x_ref[0], nmix_ref[...], mod[0:1], mod[1:2])
    hb = h.astype(BF16)
    p = _dot(hb, wko_ref[...])
    k_ref[0] = (p[:, :qk] * (dk ** -0.5)).astype(BF16)
    o_ref[0] = _sigmoid(p[:, qk:]).astype(o_ref.dtype)
    p_t = _dot_nt(wqvt_ref[...], hb)
    for ci in range(tt // chunk):
        qt_ref[0, ci] = p_t[:qk, ci * chunk:(ci + 1) * chunk].astype(BF16)
        vt_ref[0, ci] = p_t[qk:, ci * chunk:(ci + 1) * chunk].astype(BF16)

    nh = MLSTM_HEADS
    g_t = _transpose_exact(_dot_split(h, wg_ref[...]) + bg_ref[...])
    row = _iota((4 * nh, chunk), 0)
    is_f = (row & nh) == nh
    a = _iota((chunk, chunk), 0)
    c = _iota((chunk, chunk), 1)
    le = (a <= c).astype(BF16)
    ge = (a >= c).astype(BF16)
    for c0 in range(0, tt, chunk):
        gc = g_t[:, c0:c0 + chunk]
        pieces = _split_bf16(jnp.where(is_f, _log_sigmoid(gc), 0.0), 3)
        pre = _dot(pieces[0], le) + (_dot(pieces[1], le) + _dot(pieces[2], le))
        suf = _dot(pieces[0], ge) + (_dot(pieces[1], ge) + _dot(pieces[2], ge))
        gr_ref[0, :, c0:c0 + chunk] = jnp.where(is_f, jnp.where(row < 2 * nh, pre, suf), gc)


def _inproj(x, mod, nmix, w_ko, w_qvt, w_g, b_g, qk, vd):
    b, t, d = x.shape
    tt = min(t, 512)
    ng = w_g.shape[1]
    ln = SCAN_CHUNK
    full = lambda shape: pl.BlockSpec(shape, lambda i, j: (0,) * len(shape))
    tok = lambda n: pl.BlockSpec((1, tt, n), lambda i, j: (i, j, 0))
    slab = lambda n: pl.BlockSpec((1, tt // ln, n, ln), lambda i, j: (i, j, 0, 0))
    return pl.pallas_call(
        functools.partial(_inproj_kernel, chunk=ln, qk=qk, dk=qk // MLSTM_HEADS),
        grid=(b, t // tt),
        in_specs=[tok(d), pl.BlockSpec((1, 8, d), lambda i, j: (i, 0, 0)), full((1, d)),
                  full(w_ko.shape), full(w_qvt.shape), full(w_g.shape), full(b_g.shape)],
        out_specs=[slab(qk), tok(qk), slab(vd), tok(vd),
                   pl.BlockSpec((1, ng, tt), lambda i, j: (i, 0, j))],
        out_shape=[jax.ShapeDtypeStruct((b, t // ln, qk, ln), BF16), jax.ShapeDtypeStruct((b, t, qk), BF16),
                   jax.ShapeDtypeStruct((b, t // ln, vd, ln), BF16), jax.ShapeDtypeStruct((b, t, vd), BF16),
                   jax.ShapeDtypeStruct((b, ng, t), F32)],
        compiler_params=_params(("arbitrary", "arbitrary"), 56),
        name="mlstm_inproj",
    )(x, mod, nmix, w_ko, w_qvt, w_g, b_g)


def _scan_kernel(kc_ref, vc_ref, grc_ref, qx_ref, kx_ref, vx_ref, grx_ref,
                 out_ref, s_ref, sall_ref, mall_ref, *, chunk):
    ln = chunk
    nh = MLSTM_HEADS
    pair = pl.program_id(1)
    kl = kx_ref.shape[2]
    dv = vx_ref.shape[2] // 2
    lane = _iota((ln, kl), 1)
    kmask = (lane < kl // 2, lane >= kl // 2)
    ones = jnp.ones((dv, ln), BF16)
    si = _iota((ln, ln), 0)
    ji = _iota((ln, ln), 1)
    causal = (si <= ji, si >= ji)
    ncc = kc_ref.shape[1] // ln
    ncx = kx_ref.shape[1] // ln

    def gate_rows(gr_ref, hh, dirn, c):
        base = 2 * nh * dirn + 2 * pair + hh
        return gr_ref[0, base, pl.ds(c, 1), :], gr_ref[0, base + nh, pl.ds(c, 1), :]

    def keys(k_ref, c):
        k = k_ref[0, pl.ds(pl.multiple_of(c * ln, ln), ln), :]
        return [jnp.where(kmask[hh], k, jnp.zeros((), BF16)) for hh in range(2)]

    def values_t(v_ref, c, hh):
        return jnp.concatenate([v_ref[0, c, hh * dv:(hh + 1) * dv, :], ones], axis=0)

    def advance(refs, n, record):
        k_ref, v_ref, gr_ref = refs

        def body(i, ms):
            new_ms = []
            cs = (i, n - 1 - i)
            khs = [keys(k_ref, c) for c in cs]
            for hh in range(2):
                for dirn in range(2):
                    ch = 2 * hh + dirn
                    c = cs[dirn]
                    ig, bc = gate_rows(gr_ref, hh, dirn, c)
                    m = ms[ch]
                    s_old = s_ref[ch]
                    if record:
                        sall_ref[ch * ncx + c] = s_old.astype(BF16)
                        mall_ref[ch * ncx + c] = jnp.broadcast_to(m, (8, ln))
                    b_end = bc[:, ln - 1:ln] if dirn == 0 else bc[:, 0:1]
                    gl = b_end - bc + ig
                    m_new = jnp.maximum(b_end + m, jnp.max(gl, axis=1, keepdims=True))
                    vw = (values_t(v_ref, c, hh) * jnp.exp(gl - m_new)).astype(BF16)
                    s_ref[ch] = jnp.exp(b_end + m - m_new) * s_old + _dot(vw, khs[dirn][hh])
                    new_ms.append(m_new)
            return tuple(new_ms)

        return body

    s_ref[...] = jnp.zeros(s_ref.shape, F32)
    ms = tuple(jnp.zeros((1, 1), F32) for _ in range(4))
    ms = lax.fori_loop(0, ncc, advance((kc_ref, vc_ref, grc_ref), ncc, False), ms, unroll=min(ncc, SCAN_UNROLL))
    lax.fori_loop(0, ncx, advance((kx_ref, vx_ref, grx_ref), ncx, True), ms, unroll=min(ncx, SCAN_UNROLL))

    def emit(c, _):
        q_t = qx_ref[0, c]
        q_tf = q_t.astype(F32)
        s_pair = _dot(jnp.concatenate(keys(kx_ref, c), axis=0), q_t)
        for hh in range(2):
            s_kq = s_pair[hh * ln:(hh + 1) * ln, :]
            v_t = values_t(vx_ref, c, hh)
            hsum = None
            for dirn in range(2):
                ch = 2 * hh + dirn
                ig, bc = gate_rows(grx_ref, hh, dirn, c)
                u_col = jnp.broadcast_to(ig - bc, (ln, ln)).T
                a = bc + mall_ref[ch * ncx + c][0:1, :]
                dm = jnp.where(causal[dirn], u_col + bc, -jnp.inf)
                mj = jnp.maximum(a, jnp.max(dm, axis=0, keepdims=True))
                sm = (s_kq * jnp.exp(dm - mj)).astype(BF16)
                qw = (q_tf * jnp.exp(a - mj)).astype(BF16)
                num = _dot(jnp.concatenate([v_t, sall_ref[ch * ncx + c]], axis=1),
                           jnp.concatenate([sm, qw], axis=0))
                h = num[:dv, :] * (1.0 / jnp.maximum(jnp.abs(num[dv:dv + 1, :]), jnp.exp(-mj)))
                hsum = h if hsum is None else hsum + h
            hn = hsum * lax.rsqrt(jnp.mean(hsum * hsum, axis=0, keepdims=True) + EPS)
            out_ref[0, pl.ds(pl.multiple_of(c * ln, ln), ln), hh * dv:(hh + 1) * dv] = hn.T.astype(out_ref.dtype)
        return 0

    lax.fori_loop(0, ncx, emit, 0, unroll=min(ncx, SCAN_UNROLL))


def _scan(ctx_parts, lat_parts):
    kc, vc, grc = ctx_parts
    qx, kx, vx, grx = lat_parts
    b, t, qk = kx.shape
    vd = vx.shape[2]
    nh = MLSTM_HEADS
    ln = SCAN_CHUNK
    dv = vd // nh
    kl = 2 * (qk // nh)
    assert kl == ln and dv == ln, "scan kernel assumes key-pair lanes = head value dim = chunk"
    ncx = t // ln

    def specs(k, v, gr):
        tk = k.shape[1]
        return [pl.BlockSpec((1, tk, kl), lambda i, j: (i, 0, j)),
                pl.BlockSpec((1, tk // ln, 2 * dv, ln), lambda i, j: (i, 0, j, 0)),
                pl.BlockSpec((1,) + gr.shape[1:], lambda i, j: (i, 0, 0, 0))]

    return pl.pallas_call(
        functools.partial(_scan_kernel, chunk=ln),
        grid=(b, nh // 2),
        in_specs=specs(kc, vc, grc) + [pl.BlockSpec((1, ncx, kl, ln), lambda i, j: (i, 0, j, 0))] + specs(kx, vx, grx),
        out_specs=pl.BlockSpec((1, t, 2 * dv), lambda i, j: (i, 0, j)),
        out_shape=jax.ShapeDtypeStruct((b, t, vd), BF16),
        scratch_shapes=[pltpu.VMEM((4, 2 * dv, kl), F32),
                        pltpu.VMEM((4 * ncx, 2 * dv, kl), BF16),
                        pltpu.VMEM((4 * ncx, 8, ln), F32)],
        compiler_params=_params(("arbitrary", "arbitrary"), 40),
        name="mlstm_scan",
    )(kc, vc, grc, qx, kx, vx, grx)


def _mlstm_out_kernel(hn_ref, og_ref, x_ref, mod_ref, mn_ref, wout_ref, nffn_ref, wr_ref,
                      x1_ref, hx2_ref, lg_ref):
    mod = mod_ref[0]
    a = (hn_ref[0].astype(F32) * mn_ref[...]) * og_ref[0].astype(F32)
    xn = x_ref[0] + mod[2:3] * _dot(a.astype(BF16), wout_ref[...])
    x1_ref[0] = xn
    hb, lg = _ffn_pre(xn, mod, nffn_ref[...], wr_ref[...])
    hx2_ref[0] = hb
    lg_ref[0] = lg


def _mlstm_out(hn, og, x, mod, mnorm, w_out, nffn, w_router):
    b, t, d = x.shape
    vd = hn.shape[2]
    e = w_router.shape[1]
    tt = min(t, 512)
    full = lambda shape: pl.BlockSpec(shape, lambda i, j: (0,) * len(shape))
    tok = lambda n: pl.BlockSpec((1, tt, n), lambda i, j: (i, j, 0))
    return pl.pallas_call(
        _mlstm_out_kernel,
        grid=(b, t // tt),
        in_specs=[tok(vd), tok(vd), tok(d), pl.BlockSpec((1, 8, d), lambda i, j: (i, 0, 0)),
                  full((1, vd)), full(w_out.shape), full((1, d)), full(w_router.shape)],
        out_specs=[tok(d), tok(d), pl.BlockSpec((1, e, tt), lambda i, j: (i, 0, j))],
        out_shape=[jax.ShapeDtypeStruct((b, t, d), F32), jax.ShapeDtypeStruct((b, t, d), BF16),
                   jax.ShapeDtypeStruct((b, e, t), F32)],
        compiler_params=_params(("arbitrary", "arbitrary"), 48),
        name="mlstm_out",
    )(hn, og, x, mod, mnorm, w_out, nffn, w_router)


def _chunked_rows(gr):
    b, n, t = gr.shape
    return gr.reshape(b, n, t // SCAN_CHUNK, SCAN_CHUNK)


def kernel(x, c, ctx, c_ctx, ada_w, ada_b, norm_mix, norm_ffn, pool_w, pool_scale, mlstm_w_in, mlstm_b_gates,
           mlstm_norm, mlstm_w_out, moe_router, moe_w_gate, moe_w_up, moe_w_down, final_norm):
    bsz, seq, d = x.shape
    depth = ada_w.shape[0]
    n_mixers = 2

    cc = jnp.concatenate([c, c_ctx[None, :], jnp.zeros((16 - bsz - 1, d), F32)], axis=0)
    ada = _ada(cc, ada_w, ada_b).reshape(depth, 16, N_ADA, d)
    pad = jnp.zeros((bsz, 8 - N_ADA, d), F32)

    fn = final_norm.reshape(1, d)
    for i in range(depth):
        last = i == depth - 1
        j = i // n_mixers
        mod_x = jnp.concatenate([ada[i, :bsz], pad], axis=1)
        mod_c = jnp.concatenate([jnp.broadcast_to(ada[i, bsz][None], (bsz, N_ADA, d)), pad], axis=1)
        nmix = norm_mix[i].reshape(1, d)
        nffn = norm_ffn[i].reshape(1, d)
        w_router = moe_router[i]
        streams = []
        if i % n_mixers == 0:
            pw = pool_w[j].astype(BF16)
            ps = pool_scale[j].reshape(1, d)
            x1, hx2, lg = _pool_mixer(x, mod_x, nmix, nffn, pw, ps, w_router, True)
            streams.append((x1, hx2, lg, mod_x))
            if not last:
                c1, hc2, lgc = _pool_mixer(ctx, mod_c, nmix, nffn, pw, ps, w_router, False)
                streams.append((c1, hc2, lgc, mod_c))
        else:
            qk = mlstm_w_in.shape[2] - 2 * mlstm_w_out.shape[1] - 4 * MLSTM_HEADS
            qk //= 2
            vd = mlstm_w_out.shape[1]
            w_in = mlstm_w_in[j]
            w_ko = jnp.concatenate([w_in[:, qk:2 * qk], w_in[:, 2 * qk + vd:2 * qk + 2 * vd]], axis=1).astype(BF16)
            w_qvt = jnp.concatenate([w_in[:, :qk], w_in[:, 2 * qk:2 * qk + vd]], axis=1).T.astype(BF16)
            w_g = w_in[:, 2 * qk + 2 * vd:]
            b_g = mlstm_b_gates[j].reshape(1, -1)
            proj = lambda s, m: _inproj(s, m, nmix, w_ko, w_qvt, w_g, b_g, qk, vd)
            _, kc, vc, _, grc = proj(ctx, mod_c)
            qx, kx, vx, ox, grx = proj(x, mod_x)
            hn = _scan((kc, vc, _chunked_rows(grc)), (qx, kx, vx, _chunked_rows(grx)))
            x1, hx2, lg = _mlstm_out(hn, ox, x, mod_x, mlstm_norm[j].reshape(1, vd),
                                     mlstm_w_out[j].astype(BF16), nffn, w_router)
            streams.append((x1, hx2, lg, mod_x))
            assert last, "context output of the mLSTM mixer is only needed by a following layer"
        outs = _moe(streams, i, moe_w_gate, moe_w_up, moe_w_down, fn, last)
        x = outs[0]
        if not last:
            ctx = outs[1]
    return x
```

```python
import functools

import jax
import jax.numpy as jnp
import numpy as np
from jax import lax
from jax.experimental import pallas as pl
from jax.experimental.pallas import tpu as pltpu

F32 = jnp.float32
BF16 = jnp.bfloat16
HIGHEST = lax.Precision.HIGHEST

GRID_W = 64
EPS = 1e-6
N_ADA = 6
POOL_WINDOWS = (2, 4, 8, 16)
N_POOL_GROUPS = 4
MLSTM_HEADS = 8
N_EXPERTS = 16
CAPACITY_FACTOR = 2

MIB = 1024 * 1024
MXU_DIM = 256
LANES = 128
TOKEN_BLOCK = 256
SLOT_GROUP = 32
PAIRS_PER_DOT = 16
SCAN_CHUNK = 128
SCAN_UNROLL = 8
POOL_PAD_ROWS = max(POOL_WINDOWS) // 2


def _params(sem, vmem_mib):
    return pltpu.CompilerParams(dimension_semantics=sem, vmem_limit_bytes=vmem_mib * MIB)


def _dot(a, b, precision=None):
    return jnp.dot(a, b, preferred_element_type=F32, precision=precision)


def _dot_nt(a, b, precision=None):
    return lax.dot_general(a, b, (((1,), (1,)), ((), ())), preferred_element_type=F32, precision=precision)


def _iota(shape, dim, dtype=jnp.int32):
    return lax.broadcasted_iota(dtype, shape, dim)


def _sigmoid(x):
    return 1.0 / (1.0 + jnp.exp(-x))


def _log_sigmoid(x):
    return jnp.minimum(x, 0.0) - jnp.log1p(jnp.exp(-jnp.abs(x)))


def _norm_mod(x, g, shift, scale):
    inv = lax.rsqrt(jnp.mean(x * x, axis=-1, keepdims=True) + EPS)
    return (x * inv) * (g * (1.0 + scale)) + shift


def _ada_kernel(c_ref, w_ref, b_ref, o_ref):
    c = c_ref[...]
    s = c * _sigmoid(c)
    s_hi, s_lo = _split_bf16(s, 2)
    w_hi, w_lo = _split_bf16(w_ref[0], 2)
    o_ref[0] = _dot(s_hi, w_hi) + (_dot(s_hi, w_lo) + _dot(s_lo, w_hi)) + b_ref[0]


def _ada(cc, ada_w, ada_b):
    depth, d, n = ada_w.shape
    rows = cc.shape[0]
    tn = n // 4
    return pl.pallas_call(
        _ada_kernel,
        grid=(depth, n // tn),
        in_specs=[pl.BlockSpec((rows, d), lambda i, j: (0, 0)),
                  pl.BlockSpec((1, d, tn), lambda i, j: (i, 0, j)),
                  pl.BlockSpec((1, 1, tn), lambda i, j: (i, 0, j))],
        out_specs=pl.BlockSpec((1, rows, tn), lambda i, j: (i, 0, j)),
        out_shape=jax.ShapeDtypeStruct((depth, rows, n), F32),
        compiler_params=_params(("arbitrary", "arbitrary"), 40),
        name="ada",
    )(cc, ada_w, ada_b.reshape(depth, 1, n))


def _split_bf16(x, pieces):
    out = []
    for _ in range(pieces):
        p = x.astype(BF16)
        out.append(p)
        x = x - p.astype(F32)
    return out


def _dot_split(a, b):
    n = b.shape[1]
    a_hi, a_lo = _split_bf16(a, 2)
    b_hi, b_lo = _split_bf16(b, 2)
    rhs = jnp.concatenate([jnp.concatenate([b_hi, b_lo], axis=1),
                           jnp.concatenate([b_hi, jnp.zeros_like(b_lo)], axis=1)], axis=0)
    both = _dot(jnp.concatenate([a_hi, a_lo], axis=1), rhs)
    return both[:, :n] + both[:, n:]


def _transpose_exact(x):
    m = x.shape[1]
    pieces = 3
    assert m & (m - 1) == 0
    eyes = ((_iota((m, pieces * m), 1) & (m - 1)) == _iota((m, pieces * m), 0)).astype(BF16)
    return _dot_nt(eyes, jnp.concatenate(_split_bf16(x, pieces), axis=1))


def _ffn_pre(xn, mod, nffn, wr):
    h2 = _norm_mod(xn, nffn, mod[3:4], mod[4:5])
    return h2.astype(BF16), _transpose_exact(_dot_split(h2, wr))


def _pool_kernel(x_ref, mod_ref, nmix_ref, nffn_ref, cmat_ref, cnt_ref, pw_ref, ps_ref, wr_ref,
                 x1_ref, hx2_ref, lg_ref, inv_ref, *pad, two_d, tc):
    t, d = x_ref.shape[1], x_ref.shape[2]
    gd = d // N_POOL_GROUPS
    mod = mod_ref[0]
    pad_tok = POOL_PAD_ROWS * GRID_W

    for c0 in range(0, t, tc):
        xs = x_ref[0, c0:c0 + tc, :]
        inv_ref[c0:c0 + tc, :] = lax.rsqrt(jnp.mean(xs * xs, axis=-1, keepdims=True) + EPS)

    def hx_of(r0, rn, j):
        cs = slice(j * gd, (j + 1) * gd)
        xs = x_ref[0, r0:r0 + rn, cs]
        g = nmix_ref[:, cs] * (1.0 + mod[1:2, cs])
        return xs * inv_ref[r0:r0 + rn, :] * g + mod[0:1, cs]

    if two_d:
        pad_ref, = pad
        pad_ref[0:pad_tok, :] = jnp.zeros((pad_tok, gd), F32)
        pad_ref[pad_tok + t:pad_tok + t + pad_tok, :] = jnp.zeros((pad_tok, gd), F32)

    for j, w in enumerate(POOL_WINDOWS):
        cs = slice(j * gd, (j + 1) * gd)
        cm = cmat_ref[j]
        sums = []
        for b0 in range(0, t, MXU_DIM):
            g = hx_of(b0, MXU_DIM, j)
            g_hi = g.astype(BF16)
            g_lo = (g - g_hi.astype(F32)).astype(BF16)
            csum = _dot(cm, g_hi) + _dot(cm, g_lo)
            if two_d:
                pad_ref[pad_tok + b0:pad_tok + b0 + MXU_DIM, :] = csum
            else:
                sums.append(csum)
        for c0 in range(0, t, tc):
            if two_d:
                tot = None
                for dr in range(-(w // 2), w - w // 2):
                    o = pad_tok + c0 + dr * GRID_W
                    sl = pad_ref[o:o + tc, :]
                    tot = sl if tot is None else tot + sl
            else:
                tot = sums[c0 // MXU_DIM]
            mean = tot / cnt_ref[c0:c0 + tc, j:j + 1]
            diff = mean - hx_of(c0, tc, j)
            y = _dot(diff.astype(BF16), pw_ref[j]) * ps_ref[:, cs]
            x1_ref[0, c0:c0 + tc, cs] = x_ref[0, c0:c0 + tc, cs] + mod[2:3, cs] * y

    for c0 in range(0, t, tc):
        hb, lg = _ffn_pre(x1_ref[0, c0:c0 + tc, :], mod, nffn_ref[...], wr_ref[...])
        hx2_ref[0, c0:c0 + tc, :] = hb
        lg_ref[0, :, c0:c0 + tc] = lg


def _pool_consts(t, two_d):
    cm = np.zeros((len(POOL_WINDOWS), MXU_DIM, MXU_DIM), np.float32)
    cnt = np.zeros((t, len(POOL_WINDOWS)), np.float32)
    n = GRID_W if two_d else t
    pos = np.arange(MXU_DIM)
    for j, w in enumerate(POOL_WINDOWS):
        col = pos % n
        lo = np.clip(col - w // 2, 0, n)
        hi = np.clip(col + w - w // 2, 0, n)
        same = (pos[:, None] // n) == (pos[None, :] // n)
        cm[j] = (same & (col[None, :] >= lo[:, None]) & (col[None, :] < hi[:, None])).astype(np.float32)
        tt = np.arange(t)
        c = tt % n
        ccnt = np.clip(c + w - w // 2, 0, n) - np.clip(c - w // 2, 0, n)
        if two_d:
            rows = t // GRID_W
            r = tt // GRID_W
            rcnt = np.clip(r + w - w // 2, 0, rows) - np.clip(r - w // 2, 0, rows)
            cnt[:, j] = ccnt * rcnt
        else:
            cnt[:, j] = ccnt
    return jnp.asarray(cm, BF16), jnp.asarray(cnt)


def _pool_mixer(x, mod, nmix, nffn, pool_w, pool_scale, w_router, two_d):
    b, t, d = x.shape
    gd = d // N_POOL_GROUPS
    if not two_d:
        assert t == MXU_DIM
    cmat, cnt = _pool_consts(t, two_d)
    tc = min(t, 512)
    pad_tok = POOL_PAD_ROWS * GRID_W
    scratch = [pltpu.VMEM((t, 1), F32)]
    if two_d:
        scratch.append(pltpu.VMEM((t + 2 * pad_tok, gd), F32))
    e = w_router.shape[1]
    full = lambda shape: pl.BlockSpec(shape, lambda i: (0,) * len(shape))
    return pl.pallas_call(
        functools.partial(_pool_kernel, two_d=two_d, tc=tc),
        grid=(b,),
        in_specs=[pl.BlockSpec((1, t, d), lambda i: (i, 0, 0)),
                  pl.BlockSpec((1, 8, d), lambda i: (i, 0, 0)),
                  full((1, d)), full((1, d)), full(cmat.shape), full(cnt.shape),
                  full(pool_w.shape), full((1, d)), full(w_router.shape)],
        out_specs=[pl.BlockSpec((1, t, d), lambda i: (i, 0, 0)),
                   pl.BlockSpec((1, t, d), lambda i: (i, 0, 0)),
                   pl.BlockSpec((1, e, t), lambda i: (i, 0, 0))],
        out_shape=[jax.ShapeDtypeStruct((b, t, d), F32),
                   jax.ShapeDtypeStruct((b, t, d), BF16),
                   jax.ShapeDtypeStruct((b, e, t), F32)],
        scratch_shapes=scratch,
        compiler_params=_params(("arbitrary",), 60),
        name="pool_mixer_2d" if two_d else "pool_mixer_1d",
    )(x, mod, nmix, nffn, cmat, cnt, pool_w, pool_scale, w_router)


def _route_kernel(lg_ref, slot_ref, gate_ref, offs_ref, *, cap):
    lg = lg_ref[0]
    e, t = lg.shape
    ex = jnp.exp(lg - jnp.max(lg, axis=0, keepdims=True))
    aff = ex / jnp.sum(ex, axis=0, keepdims=True)
    capf = jnp.float32(cap)

    def as_f32(v):
        return lax.bitcast_convert_type(v, F32)

    def enough(cand):
        return jnp.sum(jnp.where(aff >= as_f32(cand), 1.0, 0.0), axis=1, keepdims=True) >= capf

    def bisect2(i, v):
        hi = jnp.left_shift(jnp.int32(1), 30 - 2 * i)
        lo = jnp.left_shift(jnp.int32(1), 29 - 2 * i)
        c_hi, c_lo, c_both = v | hi, v | lo, v | hi | lo
        return jnp.where(enough(c_both), c_both, jnp.where(enough(c_hi), c_hi, jnp.where(enough(c_lo), c_lo, v)))

    kth = lax.fori_loop(0, 15, bisect2, jnp.zeros((e, 1), jnp.int32))
    kth = jnp.where(enough(kth | 1), kth | 1, kth)
    above = as_f32(kth + 1)
    gt = aff >= above
    eq = (aff >= as_f32(kth)) & jnp.logical_not(gt)
    need = capf - jnp.sum(jnp.where(gt, 1.0, 0.0), axis=1, keepdims=True)

    blk = min(t, MXU_DIM)
    before = (_iota((blk, blk), 0) < _iota((blk, blk), 1)).astype(BF16)

    def excl_cumsum(mask):
        ones = jnp.where(mask, 1.0, 0.0)
        outs, run = [], jnp.zeros((e, 1), F32)
        for b0 in range(0, t, blk):
            mb = ones[:, b0:b0 + blk]
            outs.append(_dot(mb.astype(BF16), before) + run)
            run = run + jnp.sum(mb, axis=1, keepdims=True)
        return jnp.concatenate(outs, axis=1) if len(outs) > 1 else outs[0]

    sel = gt | (eq & (excl_cumsum(eq) < need))
    slot_ref[0] = jnp.where(sel, excl_cumsum(sel), -1.0)
    gate_ref[0] = aff
    nl = offs_ref.shape[2]
    starts = (_iota((t, nl), 0) < _iota((t, nl), 1) * TOKEN_BLOCK).astype(BF16)
    offs_ref[0] = _dot(jnp.where(sel, 1.0, 0.0).astype(BF16), starts)


def _route(logits, cap):
    b, e, t = logits.shape
    spec = pl.BlockSpec((1, e, t), lambda i: (i, 0, 0))
    ospec = pl.BlockSpec((1, e, LANES), lambda i: (i, 0, 0))
    return pl.pallas_call(
        functools.partial(_route_kernel, cap=cap),
        grid=(b,),
        in_specs=[spec],
        out_specs=[spec, spec, ospec],
        out_shape=[jax.ShapeDtypeStruct((b, e, t), F32)] * 2 + [jax.ShapeDtypeStruct((b, e, LANES), F32)],
        compiler_params=_params(("arbitrary",), 32),
        name="route",
    )(logits)


def _gather_kernel(h_ref, slot_ref, o_ref, *, cap):
    hx = h_ref[0]
    t = hx.shape[0]
    row = _iota((cap, t), 0).astype(F32)
    for i in range(o_ref.shape[0]):
        onehot = jnp.where(slot_ref[0, i:i + 1, :] == row, 1.0, 0.0).astype(BF16)
        o_ref[i] = _dot(onehot, hx).astype(BF16)


def _gather(hx2, slot, cap):
    b, t, d = hx2.shape
    e = slot.shape[1]
    eb = 8
    return pl.pallas_call(
        functools.partial(_gather_kernel, cap=cap),
        grid=(b, e // eb),
        in_specs=[pl.BlockSpec((1, t, d), lambda i, j: (i, 0, 0)),
                  pl.BlockSpec((1, eb, t), lambda i, j: (i, j, 0))],
        out_specs=pl.BlockSpec((eb, cap, d), lambda i, j: (j, i, 0)),
        out_shape=jax.ShapeDtypeStruct((e, b * cap, d), BF16),
        compiler_params=_params(("arbitrary", "arbitrary"), 48),
        name="gather",
    )(hx2, slot)


STATIC_DOTS = 2


def _pair_list(offs_ref, kb, n_exp, nblk, le_ref, lg_ref):
    b = pl.program_id(0)
    shift = SLOT_GROUP.bit_length() - 1
    max_groups = le_ref.shape[0] // (n_exp + 1)

    cnt = jnp.int32(0)
    for e in range(n_exp):
        o0 = offs_ref[b, e * (nblk + 1) + kb]
        o1 = offs_ref[b, e * (nblk + 1) + kb + 1]
        lo = jnp.right_shift(o0, shift)
        ng = jnp.where(o1 > o0, jnp.right_shift(o1 - 1, shift) - lo + 1, 0)
        for i in range(max_groups):
            le_ref[cnt + i] = e
            lg_ref[cnt + i] = lo + i
        cnt = cnt + ng
    return cnt


def _pair(le_ref, lg_ref, idx, cnt):
    valid = idx < cnt
    safe = jnp.minimum(idx, jnp.maximum(cnt - 1, 0))
    e_p = jnp.where(valid, le_ref[safe], 0)
    g_p = jnp.where(valid, lg_ref[safe], 0)
    base = jnp.where(valid, g_p * SLOT_GROUP, -2 * SLOT_GROUP).astype(F32)
    return e_p, pl.multiple_of(g_p * SLOT_GROUP, SLOT_GROUP), base


def _for_each_dot(cnt, body):
    for c in range(STATIC_DOTS):
        body(c, 0)
    lax.fori_loop(STATIC_DOTS, (cnt + PAIRS_PER_DOT - 1) // PAIRS_PER_DOT, body, 0)


def _sparse_gather_kernel(offs_ref, h_ref, slot_ref, o_ref, le_ref, lg_ref, p_ref, *, n_exp, nblk):
    sg = SLOT_GROUP
    o_ref[...] = jnp.zeros(o_ref.shape, o_ref.dtype)
    sub = _iota((sg, TOKEN_BLOCK), 0).astype(F32)

    def per_block(kb, _):
        cnt = _pair_list(offs_ref, kb, n_exp, nblk, le_ref, lg_ref)
        tok = pl.ds(pl.multiple_of(kb * TOKEN_BLOCK, TOKEN_BLOCK), TOKEN_BLOCK)

        def dot_batch(c, _):
            dst = []
            for p in range(PAIRS_PER_DOT):
                e_p, s0, base = _pair(le_ref, lg_ref, c * PAIRS_PER_DOT + p, cnt)
                row = slot_ref[0, kb, pl.ds(e_p, 1), :]
                p_ref[p * sg:(p + 1) * sg, :] = jnp.where(row == base + sub, 1.0, 0.0).astype(BF16)
                dst.append((e_p, s0))
            z = _dot(p_ref[...], h_ref[0, tok, :])
            for p, (e_p, s0) in enumerate(dst):
                o_ref[e_p, pl.ds(s0, sg), :] += z[p * sg:(p + 1) * sg, :].astype(o_ref.dtype)
            return 0

        _for_each_dot(cnt, dot_batch)
        return 0

    lax.fori_loop(0, nblk, per_block, 0)


def _by_token_block(a):
    b, e, t = a.shape
    return jnp.swapaxes(a.reshape(b, e, t // TOKEN_BLOCK, TOKEN_BLOCK), 1, 2)


def _sparse_gather(hx2, slot_blocks, offs, cap):
    b, t, d = hx2.shape
    _, nblk, e, _ = slot_blocks.shape
    return pl.pallas_call(
        functools.partial(_sparse_gather_kernel, n_exp=e, nblk=nblk),
        grid_spec=pltpu.PrefetchScalarGridSpec(
            num_scalar_prefetch=1,
            grid=(b,),
            in_specs=[pl.BlockSpec((1, t, d), lambda i, o: (i, 0, 0)),
                      pl.BlockSpec((1, nblk, e, TOKEN_BLOCK), lambda i, o: (i, 0, 0, 0))],
            out_specs=pl.BlockSpec((e, cap, d), lambda i, o: (0, i, 0)),
            scratch_shapes=[pltpu.SMEM(((e + 1) * (cap // SLOT_GROUP),), jnp.int32),
                            pltpu.SMEM(((e + 1) * (cap // SLOT_GROUP),), jnp.int32),
                            pltpu.VMEM((PAIRS_PER_DOT * SLOT_GROUP, TOKEN_BLOCK), BF16)]),
        out_shape=jax.ShapeDtypeStruct((e, b * cap, d), BF16),
        compiler_params=_params(("arbitrary",), 40),
        name="sparse_gather",
    )(offs, hx2, slot_blocks)


def _sparse_scatter_kernel(offs_ref, x_ref, y_ref, slot_ref, gate_ref, mod_ref, fn_ref, o_ref,
                           le_ref, lg_ref, p_ref, yc_ref, acc_ref, *, n_exp, nblk, bps, final):
    sg = SLOT_GROUP
    sub = _iota((sg, TOKEN_BLOCK), 0).astype(F32)

    def per_block(i, _):
        cnt = _pair_list(offs_ref, pl.program_id(1) * bps + i, n_exp, nblk, le_ref, lg_ref)
        tok = pl.ds(pl.multiple_of(i * TOKEN_BLOCK, TOKEN_BLOCK), TOKEN_BLOCK)
        acc_ref[...] = jnp.zeros(acc_ref.shape, F32)

        def dot_batch(c, _):
            for p in range(PAIRS_PER_DOT):
                e_p, s0, base = _pair(le_ref, lg_ref, c * PAIRS_PER_DOT + p, cnt)
                row = slot_ref[0, i, pl.ds(e_p, 1), :]
                gate = gate_ref[0, i, pl.ds(e_p, 1), :]
                p_ref[p * sg:(p + 1) * sg, :] = jnp.where(row == base + sub, gate, 0.0)
                yc_ref[p * sg:(p + 1) * sg, :] = y_ref[e_p, pl.ds(s0, sg), :]
            acc_ref[...] += _dot(p_ref[...].T.astype(BF16), yc_ref[...])
            return 0

        _for_each_dot(cnt, dot_batch)
        out = x_ref[0, tok, :] + mod_ref[0][5:6] * acc_ref[...]
        if final:
            out = out * lax.rsqrt(jnp.mean(out * out, axis=-1, keepdims=True) + EPS) * fn_ref[...]
        o_ref[0, tok, :] = out
        return 0

    lax.fori_loop(0, bps, per_block, 0)


def _sparse_scatter(x, y, slot_blocks, gate_blocks, offs, mod, final_norm, cap, final):
    b, t, d = x.shape
    _, nblk, e, _ = slot_blocks.shape
    bps = min(nblk, 4)
    rows = PAIRS_PER_DOT * SLOT_GROUP
    tok = lambda n: pl.BlockSpec((1, bps * TOKEN_BLOCK, n), lambda i, j, o: (i, j, 0))
    exp = pl.BlockSpec((1, bps, e, TOKEN_BLOCK), lambda i, j, o: (i, j, 0, 0))
    return pl.pallas_call(
        functools.partial(_sparse_scatter_kernel, n_exp=e, nblk=nblk, bps=bps, final=final),
        grid_spec=pltpu.PrefetchScalarGridSpec(
            num_scalar_prefetch=1,
            grid=(b, nblk // bps),
            in_specs=[tok(d), pl.BlockSpec((e, cap, d), lambda i, j, o: (0, i, 0)), exp, exp,
                      pl.BlockSpec((1, 8, d), lambda i, j, o: (i, 0, 0)),
                      pl.BlockSpec((1, d), lambda i, j, o: (0, 0))],
            out_specs=tok(d),
            scratch_shapes=[pltpu.SMEM(((e + 1) * (cap // SLOT_GROUP),), jnp.int32),
                            pltpu.SMEM(((e + 1) * (cap // SLOT_GROUP),), jnp.int32),
                            pltpu.VMEM((rows, TOKEN_BLOCK), F32),
                            pltpu.VMEM((rows, d), BF16),
                            pltpu.VMEM((TOKEN_BLOCK, d), F32)]),
        out_shape=jax.ShapeDtypeStruct((b, t, d), F32),
        compiler_params=_params(("arbitrary", "arbitrary"), 48),
        name="sparse_scatter",
    )(offs, x, y, slot_blocks, gate_blocks, mod, final_norm)


def _ffn_kernel(*refs, n_sets, mc):
    x_refs = refs[:n_sets]
    wg_ref, wu_ref, wd_ref = refs[n_sets:n_sets + 3]
    y_refs = refs[n_sets + 3:2 * n_sets + 3]
    acc_refs = refs[2 * n_sets + 3:3 * n_sets + 3]
    wgb, wub, wdb = refs[3 * n_sets + 3:]
    f = pl.program_id(1)
    wgb[...] = wg_ref[0, 0].astype(BF16)
    wub[...] = wu_ref[0, 0].astype(BF16)
    wdb[...] = wd_ref[0, 0].astype(BF16)
    last = f == pl.num_programs(1) - 1

    @pl.when((pl.program_id(0) == 0) & (f == 0))
    def _():
        for acc in acc_refs:
            acc[...] = jnp.zeros(acc.shape, F32)

    for x_ref, y_ref, acc in zip(x_refs, y_refs, acc_refs):
        m = x_ref.shape[1]
        step = min(m, mc)
        for m0 in range(0, m, step):
            xs = x_ref[0, m0:m0 + step, :]
            hg = _dot(xs, wgb[...])
            hu = _dot(xs, wub[...])
            hid = (hg * _sigmoid(hg) * hu).astype(BF16)
            tot = acc[m0:m0 + step, :] + _dot(hid, wdb[...])
            acc[m0:m0 + step, :] = jnp.where(last, 0.0, tot)
            y_ref[0, m0:m0 + step, :] = tot.astype(BF16)


def _expert_ffn(xs, layer, w_gate, w_up, w_down):
    _, e, d, hidden = w_gate.shape
    tf = 512
    n = len(xs)
    xspecs = [pl.BlockSpec((1, x.shape[1], d), lambda i, j: (i, 0, 0)) for x in xs]
    return pl.pallas_call(
        functools.partial(_ffn_kernel, n_sets=n, mc=2048),
        grid=(e, hidden // tf),
        in_specs=xspecs + [pl.BlockSpec((1, 1, d, tf), lambda i, j: (layer, i, 0, j)),
                           pl.BlockSpec((1, 1, d, tf), lambda i, j: (layer, i, 0, j)),
                           pl.BlockSpec((1, 1, tf, d), lambda i, j: (layer, i, j, 0))],
        out_specs=xspecs,
        out_shape=[jax.ShapeDtypeStruct(x.shape, BF16) for x in xs],
        scratch_shapes=[pltpu.VMEM((x.shape[1], d), F32) for x in xs]
        + [pltpu.VMEM((d, tf), BF16), pltpu.VMEM((d, tf), BF16), pltpu.VMEM((tf, d), BF16)],
        compiler_params=_params(("arbitrary", "arbitrary"), 60),
        name="expert_ffn",
    )(*xs, w_gate, w_up, w_down)


def _scatter_kernel(x_ref, y_ref, slot_ref, gate_ref, mod_ref, fn_ref, o_ref, *, final):
    tt = x_ref.shape[1]
    e, cap, _ = y_ref.shape
    eye = (_iota((tt, tt), 0) == _iota((tt, tt), 1)).astype(BF16)
    slot_t = _dot_nt(eye, (slot_ref[0] + 1.0).astype(BF16))
    gate_t = _dot_nt(eye, gate_ref[0].astype(BF16))
    lane = _iota((tt, cap), 1).astype(F32) + 1.0
    acc = None
    for i in range(e):
        pt = jnp.where(slot_t[:, i:i + 1] == lane, gate_t[:, i:i + 1], 0.0).astype(BF16)
        part = _dot(pt, y_ref[i])
        acc = part if acc is None else acc + part
    out = x_ref[0] + mod_ref[0][5:6] * acc
    if final:
        out = out * lax.rsqrt(jnp.mean(out * out, axis=-1, keepdims=True) + EPS) * fn_ref[...]
    o_ref[0] = out


def _scatter(x, y, slot, gate, mod, final_norm, cap, final):
    b, t, d = x.shape
    e = slot.shape[1]
    tt = min(t, 512)
    return pl.pallas_call(
        functools.partial(_scatter_kernel, final=final),
        grid=(b, t // tt),
        in_specs=[pl.BlockSpec((1, tt, d), lambda i, j: (i, j, 0)),
                  pl.BlockSpec((e, cap, d), lambda i, j: (0, i, 0)),
                  pl.BlockSpec((1, e, tt), lambda i, j: (i, 0, j)),
                  pl.BlockSpec((1, e, tt), lambda i, j: (i, 0, j)),
                  pl.BlockSpec((1, 8, d), lambda i, j: (i, 0, 0)),
                  pl.BlockSpec((1, d), lambda i, j: (0, 0))],
        out_specs=pl.BlockSpec((1, tt, d), lambda i, j: (i, j, 0)),
        out_shape=jax.ShapeDtypeStruct((b, t, d), F32),
        compiler_params=_params(("arbitrary", "arbitrary"), 48),
        name="scatter",
    )(x, y, slot, gate, mod, final_norm)


def _moe(streams, layer, w_gate, w_up, w_down, final_norm, final):
    routed = []
    for x1, hx2, logits, mod in streams:
        t = x1.shape[1]
        cap = CAPACITY_FACTOR * t // N_EXPERTS
        slot, gate, offs = _route(logits, cap)
        nblk = t // TOKEN_BLOCK
        if nblk > 1:
            offs = offs[:, :, :nblk + 1].astype(jnp.int32).reshape(offs.shape[0], -1)
            slot, gate = _by_token_block(slot), _by_token_block(gate)
            xg = _sparse_gather(hx2, slot, offs, cap)
        else:
            offs = None
            xg = _gather(hx2, slot, cap)
        routed.append((slot, gate, offs, cap, xg))
    ys = _expert_ffn([r[4] for r in routed], layer, w_gate, w_up, w_down)
    outs = []
    for (x1, _, _, mod), (slot, gate, offs, cap, _), y in zip(streams, routed, ys):
        if offs is None:
            outs.append(_scatter(x1, y, slot, gate, mod, final_norm, cap, final))
        else:
            outs.append(_sparse_scatter(x1, y, slot, gate, offs, mod, final_norm, cap, final))
    return outs


def _inproj_kernel(x_ref, mod_ref, nmix_ref, wko_ref, wqvt_ref, wg_ref, bg_ref,
                   qt_ref, k_ref, vt_ref, o_ref, gr_ref, *, chunk, qk, dk):
    tt = x_ref.shape[1]
    mod = mod_ref[0]
    h = _norm_mod(x_ref[0], nmix_ref[...], mod[0:1], mod[1:2])
    hb = h.astype(BF16)
    p = _dot(hb, wko_ref[...])
    k_ref[0] = (p[:, :qk] * (dk ** -0.5)).astype(BF16)
    o_ref[0] = _sigmoid(p[:, qk:]).astype(o_ref.dtype)
    p_t = _dot_nt(wqvt_ref[...], hb)
    for ci in range(tt // chunk):
        qt_ref[0, ci] = p_t[:qk, ci * chunk:(ci + 1) * chunk].astype(BF16)
        vt_ref[0, ci] = p_t[qk:, ci * chunk:(ci + 1) * chunk].astype(BF16)

    nh = MLSTM_HEADS
    g_t = _transpose_exact(_dot_split(h, wg_ref[...]) + bg_ref[...])
    row = _iota((4 * nh, chunk), 0)
    is_f = (row & nh) == nh
    a = _iota((chunk, chunk), 0)
    c = _iota((chunk, chunk), 1)
    tri = jnp.concatenate([(a <= c).astype(BF16), (a >= c).astype(BF16)], axis=1)
    ng = 4 * nh
    for c0 in range(0, tt, chunk):
        gc = g_t[:, c0:c0 + chunk]
        pieces = _split_bf16(jnp.where(is_f, _log_sigmoid(gc), 0.0), 3)
        cum = _dot(jnp.concatenate(pieces, axis=0), tri)
        cum = cum[:ng] + (cum[ng:2 * ng] + cum[2 * ng:])
        gr_ref[0, :, c0:c0 + chunk] = jnp.where(is_f, jnp.where(row < 2 * nh, cum[:, :chunk], cum[:, chunk:]), gc)


def _inproj(x, mod, nmix, w_ko, w_qvt, w_g, b_g, qk, vd):
    b, t, d = x.shape
    tt = min(t, 512)
    ng = w_g.shape[1]
    ln = SCAN_CHUNK
    full = lambda shape: pl.BlockSpec(shape, lambda i, j: (0,) * len(shape))
    tok = lambda n: pl.BlockSpec((1, tt, n), lambda i, j: (i, j, 0))
    slab = lambda n: pl.BlockSpec((1, tt // ln, n, ln), lambda i, j: (i, j, 0, 0))
    return pl.pallas_call(
        functools.partial(_inproj_kernel, chunk=ln, qk=qk, dk=qk // MLSTM_HEADS),
        grid=(b, t // tt),
        in_specs=[tok(d), pl.BlockSpec((1, 8, d), lambda i, j: (i, 0, 0)), full((1, d)),
                  full(w_ko.shape), full(w_qvt.shape), full(w_g.shape), full(b_g.shape)],
        out_specs=[slab(qk), tok(qk), slab(vd), tok(vd),
                   pl.BlockSpec((1, ng, tt), lambda i, j: (i, 0, j))],
        out_shape=[jax.ShapeDtypeStruct((b, t // ln, qk, ln), BF16), jax.ShapeDtypeStruct((b, t, qk), BF16),
                   jax.ShapeDtypeStruct((b, t // ln, vd, ln), BF16), jax.ShapeDtypeStruct((b, t, vd), BF16),
                   jax.ShapeDtypeStruct((b, ng, t), F32)],
        compiler_params=_params(("arbitrary", "arbitrary"), 56),
        name="mlstm_inproj",
    )(x, mod, nmix, w_ko, w_qvt, w_g, b_g)


def _scan_kernel(kc_ref, vc_ref, grc_ref, qx_ref, kx_ref, vx_ref, grx_ref,
                 out_ref, s_ref, sall_ref, mall_ref, *, chunk):
    ln = chunk
    nh = MLSTM_HEADS
    pair = pl.program_id(1)
    kl = kx_ref.shape[2]
    dv = vx_ref.shape[2] // 2
    lane = _iota((ln, kl), 1)
    kmask = (lane < kl // 2, lane >= kl // 2)
    ones = jnp.ones((dv, ln), BF16)
    si = _iota((ln, ln), 0)
    ji = _iota((ln, ln), 1)
    causal = (si <= ji, si >= ji)
    ncc = kc_ref.shape[1] // ln
    ncx = kx_ref.shape[1] // ln

    def gate_rows(gr_ref, hh, dirn, c):
        base = 2 * nh * dirn + 2 * pair + hh
        return gr_ref[0, base, pl.ds(c, 1), :], gr_ref[0, base + nh, pl.ds(c, 1), :]

    def keys(k_ref, c):
        k = k_ref[0, pl.ds(pl.multiple_of(c * ln, ln), ln), :]
        return [jnp.where(kmask[hh], k, jnp.zeros((), BF16)) for hh in range(2)]

    def values_t(v_ref, c, hh):
        return jnp.concatenate([v_ref[0, c, hh * dv:(hh + 1) * dv, :], ones], axis=0)

    def advance(refs, n, record):
        k_ref, v_ref, gr_ref = refs

        def body(i, ms):
            new_ms = []
            cs = (i, n - 1 - i)
            khs = [keys(k_ref, c) for c in cs]
            for hh in range(2):
                for dirn in range(2):
                    ch = 2 * hh + dirn
                    c = cs[dirn]
                    ig, bc = gate_rows(gr_ref, hh, dirn, c)
                    m = ms[ch]
                    s_old = s_ref[ch]
                    if record:
                        sall_ref[ch * ncx + c] = s_old.astype(BF16)
                        mall_ref[ch * ncx + c] = jnp.broadcast_to(m, (8, ln))
                    b_end = bc[:, ln - 1:ln] if dirn == 0 else bc[:, 0:1]
                    gl = b_end - bc + ig
                    m_new = jnp.maximum(b_end + m, jnp.max(gl, axis=1, keepdims=True))
                    vw = (values_t(v_ref, c, hh) * jnp.exp(gl - m_new)).astype(BF16)
                    s_ref[ch] = jnp.exp(b_end + m - m_new) * s_old + _dot(vw, khs[dirn][hh])
                    new_ms.append(m_new)
            return tuple(new_ms)

        return body

    s_ref[...] = jnp.zeros(s_ref.shape, F32)
    ms = tuple(jnp.zeros((1, 1), F32) for _ in range(4))
    ms = lax.fori_loop(0, ncc, advance((kc_ref, vc_ref, grc_ref), ncc, False), ms, unroll=min(ncc, SCAN_UNROLL))
    lax.fori_loop(0, ncx, advance((kx_ref, vx_ref, grx_ref), ncx, True), ms, unroll=min(ncx, SCAN_UNROLL))

    def emit(c, _):
        q_t = qx_ref[0, c]
        q_tf = q_t.astype(F32)
        s_pair = _dot(jnp.concatenate(keys(kx_ref, c), axis=0), q_t)
        for hh in range(2):
            s_kq = s_pair[hh * ln:(hh + 1) * ln, :]
            v_t = values_t(vx_ref, c, hh)
            hsum = None
            for dirn in range(2):
                ch = 2 * hh + dirn
                ig, bc = gate_rows(grx_ref, hh, dirn, c)
                u_col = jnp.broadcast_to(ig - bc, (ln, ln)).T
                a = bc + mall_ref[ch * ncx + c][0:1, :]
                dm = jnp.where(causal[dirn], u_col + bc, -jnp.inf)
                mj = jnp.maximum(a, jnp.max(dm, axis=0, keepdims=True))
                sm = (s_kq * jnp.exp(dm - mj)).astype(BF16)
                qw = (q_tf * jnp.exp(a - mj)).astype(BF16)
                num = _dot(jnp.concatenate([v_t, sall_ref[ch * ncx + c]], axis=1),
                           jnp.concatenate([sm, qw], axis=0))
                h = num[:dv, :] * (1.0 / jnp.maximum(jnp.abs(num[dv:dv + 1, :]), jnp.exp(-mj)))
                hsum = h if hsum is None else hsum + h
            hn = hsum * lax.rsqrt(jnp.mean(hsum * hsum, axis=0, keepdims=True) + EPS)
            out_ref[0, pl.ds(pl.multiple_of(c * ln, ln), ln), hh * dv:(hh + 1) * dv] = hn.T.astype(out_ref.dtype)
        return 0

    lax.fori_loop(0, ncx, emit, 0, unroll=min(ncx, SCAN_UNROLL))


def _scan(ctx_parts, lat_parts):
    kc, vc, grc = ctx_parts
    qx, kx, vx, grx = lat_parts
    b, t, qk = kx.shape
    vd = vx.shape[2]
    nh = MLSTM_HEADS
    ln = SCAN_CHUNK
    dv = vd // nh
    kl = 2 * (qk // nh)
    assert kl == ln and dv == ln, "scan kernel assumes key-pair lanes = head value dim = chunk"
    ncx = t // ln

    def specs(k, v, gr):
        tk = k.shape[1]
        return [pl.BlockSpec((1, tk, kl), lambda i, j: (i, 0, j)),
                pl.BlockSpec((1, tk // ln, 2 * dv, ln), lambda i, j: (i, 0, j, 0)),
                pl.BlockSpec((1,) + gr.shape[1:], lambda i, j: (i, 0, 0, 0))]

    return pl.pallas_call(
        functools.partial(_scan_kernel, chunk=ln),
        grid=(b, nh // 2),
        in_specs=specs(kc, vc, grc) + [pl.BlockSpec((1, ncx, kl, ln), lambda i, j: (i, 0, j, 0))] + specs(kx, vx, grx),
        out_specs=pl.BlockSpec((1, t, 2 * dv), lambda i, j: (i, 0, j)),
        out_shape=jax.ShapeDtypeStruct((b, t, vd), BF16),
        scratch_shapes=[pltpu.VMEM((4, 2 * dv, kl), F32),
                        pltpu.VMEM((4 * ncx, 2 * dv, kl), BF16),
                        pltpu.VMEM((4 * ncx, 8, ln), F32)],
        compiler_params=_params(("arbitrary", "arbitrary"), 40),
        name="mlstm_scan",
    )(kc, vc, grc, qx, kx, vx, grx)


def _mlstm_out_kernel(hn_ref, og_ref, x_ref, mod_ref, mn_ref, wout_ref, nffn_ref, wr_ref,
                      x1_ref, hx2_ref, lg_ref):
    mod = mod_ref[0]
    a = (hn_ref[0].astype(F32) * mn_ref[...]) * og_ref[0].astype(F32)
    xn = x_ref[0] + mod[2:3] * _dot(a.astype(BF16), wout_ref[...])
    x1_ref[0] = xn
    hb, lg = _ffn_pre(xn, mod, nffn_ref[...], wr_ref[...])
    hx2_ref[0] = hb
    lg_ref[0] = lg


def _mlstm_out(hn, og, x, mod, mnorm, w_out, nffn, w_router):
    b, t, d = x.shape
    vd = hn.shape[2]
    e = w_router.shape[1]
    tt = min(t, 512)
    full = lambda shape: pl.BlockSpec(shape, lambda i, j: (0,) * len(shape))
    tok = lambda n: pl.BlockSpec((1, tt, n), lambda i, j: (i, j, 0))
    return pl.pallas_call(
        _mlstm_out_kernel,
        grid=(b, t // tt),
        in_specs=[tok(vd), tok(vd), tok(d), pl.BlockSpec((1, 8, d), lambda i, j: (i, 0, 0)),
                  full((1, vd)), full(w_out.shape), full((1, d)), full(w_router.shape)],
        out_specs=[tok(d), tok(d), pl.BlockSpec((1, e, tt), lambda i, j: (i, 0, j))],
        out_shape=[jax.ShapeDtypeStruct((b, t, d), F32), jax.ShapeDtypeStruct((b, t, d), BF16),
                   jax.ShapeDtypeStruct((b, e, t), F32)],
        compiler_params=_params(("arbitrary", "arbitrary"), 48),
        name="mlstm_out",
    )(hn, og, x, mod, mnorm, w_out, nffn, w_router)


def _chunked_rows(gr):
    b, n, t = gr.shape
    return gr.reshape(b, n, t // SCAN_CHUNK, SCAN_CHUNK)


def kernel(x, c, ctx, c_ctx, ada_w, ada_b, norm_mix, norm_ffn, pool_w, pool_scale, mlstm_w_in, mlstm_b_gates,
           mlstm_norm, mlstm_w_out, moe_router, moe_w_gate, moe_w_up, moe_w_down, final_norm):
    bsz, seq, d = x.shape
    depth = ada_w.shape[0]
    n_mixers = 2

    cc = jnp.concatenate([c, c_ctx[None, :], jnp.zeros((16 - bsz - 1, d), F32)], axis=0)
    ada = _ada(cc, ada_w, ada_b).reshape(depth, 16, N_ADA, d)
    pad = jnp.zeros((bsz, 8 - N_ADA, d), F32)

    fn = final_norm.reshape(1, d)
    for i in range(depth):
        last = i == depth - 1
        j = i // n_mixers
        mod_x = jnp.concatenate([ada[i, :bsz], pad], axis=1)
        mod_c = jnp.concatenate([jnp.broadcast_to(ada[i, bsz][None], (bsz, N_ADA, d)), pad], axis=1)
        nmix = norm_mix[i].reshape(1, d)
        nffn = norm_ffn[i].reshape(1, d)
        w_router = moe_router[i]
        streams = []
        if i % n_mixers == 0:
            pw = pool_w[j].astype(BF16)
            ps = pool_scale[j].reshape(1, d)
            x1, hx2, lg = _pool_mixer(x, mod_x, nmix, nffn, pw, ps, w_router, True)
            streams.append((x1, hx2, lg, mod_x))
            if not last:
                c1, hc2, lgc = _pool_mixer(ctx, mod_c, nmix, nffn, pw, ps, w_router, False)
                streams.append((c1, hc2, lgc, mod_c))
        else:
            qk = mlstm_w_in.shape[2] - 2 * mlstm_w_out.shape[1] - 4 * MLSTM_HEADS
            qk //= 2
            vd = mlstm_w_out.shape[1]
            w_in = mlstm_w_in[j]
            w_ko = jnp.concatenate([w_in[:, qk:2 * qk], w_in[:, 2 * qk + vd:2 * qk + 2 * vd]], axis=1).astype(BF16)
            w_qvt = jnp.concatenate([w_in[:, :qk], w_in[:, 2 * qk:2 * qk + vd]], axis=1).T.astype(BF16)
            w_g = w_in[:, 2 * qk + 2 * vd:]
            b_g = mlstm_b_gates[j].reshape(1, -1)
            proj = lambda s, m: _inproj(s, m, nmix, w_ko, w_qvt, w_g, b_g, qk, vd)
            _, kc, vc, _, grc = proj(ctx, mod_c)
            qx, kx, vx, ox, grx = proj(x, mod_x)
            hn = _scan((kc, vc, _chunked_rows(grc)), (qx, kx, vx, _chunked_rows(grx)))
            x1, hx2, lg = _mlstm_out(hn, ox, x, mod_x, mlstm_norm[j].reshape(1, vd),
                                     mlstm_w_out[j].astype(BF16), nffn, w_router)
            streams.append((x1, hx2, lg, mod_x))
            assert last, "context output of the mLSTM mixer is only needed by a following layer"
        outs = _moe(streams, i, moe_w_gate, moe_w_up, moe_w_down, fn, last)
        x = outs[0]
        if not last:
            ctx = outs[1]
    return x
```

```python
import functools

import jax
import jax.numpy as jnp
import numpy as np
from jax import lax
from jax.experimental import pallas as pl
from jax.experimental.pallas import tpu as pltpu

F32 = jnp.float32
BF16 = jnp.bfloat16
HIGHEST = lax.Precision.HIGHEST

GRID_W = 64
EPS = 1e-6
N_ADA = 6
POOL_WINDOWS = (2, 4, 8, 16)
N_POOL_GROUPS = 4
MLSTM_HEADS = 8
N_EXPERTS = 16
CAPACITY_FACTOR = 2

MIB = 1024 * 1024
MXU_DIM = 256
LANES = 128
TOKEN_BLOCK = 256
SLOT_GROUP = 32
PAIRS_PER_DOT = 16
SCAN_CHUNK = 128
SCAN_UNROLL = 8
POOL_PAD_ROWS = max(POOL_WINDOWS) // 2


def _params(sem, vmem_mib):
    return pltpu.CompilerParams(dimension_semantics=sem, vmem_limit_bytes=vmem_mib * MIB)


def _dot(a, b, precision=None):
    return jnp.dot(a, b, preferred_element_type=F32, precision=precision)


def _dot_nt(a, b, precision=None):
    return lax.dot_general(a, b, (((1,), (1,)), ((), ())), preferred_element_type=F32, precision=precision)


def _iota(shape, dim, dtype=jnp.int32):
    return lax.broadcasted_iota(dtype, shape, dim)


def _sigmoid(x):
    return 1.0 / (1.0 + jnp.exp(-x))


def _log_sigmoid(x):
    return jnp.minimum(x, 0.0) - jnp.log1p(jnp.exp(-jnp.abs(x)))


def _norm_mod(x, g, shift, scale):
    inv = lax.rsqrt(jnp.mean(x * x, axis=-1, keepdims=True) + EPS)
    return (x * inv) * (g * (1.0 + scale)) + shift


def _ada_kernel(c_ref, w_ref, b_ref, o_ref):
    c = c_ref[...]
    s = c * _sigmoid(c)
    s_hi, s_lo = _split_bf16(s, 2)
    w_hi, w_lo = _split_bf16(w_ref[0], 2)
    o_ref[0] = _dot(s_hi, w_hi) + (_dot(s_hi, w_lo) + _dot(s_lo, w_hi)) + b_ref[0]


def _ada(cc, ada_w, ada_b):
    depth, d, n = ada_w.shape
    rows = cc.shape[0]
    tn = n // 4
    return pl.pallas_call(
        _ada_kernel,
        grid=(depth, n // tn),
        in_specs=[pl.BlockSpec((rows, d), lambda i, j: (0, 0)),
                  pl.BlockSpec((1, d, tn), lambda i, j: (i, 0, j)),
                  pl.BlockSpec((1, 1, tn), lambda i, j: (i, 0, j))],
        out_specs=pl.BlockSpec((1, rows, tn), lambda i, j: (i, 0, j)),
        out_shape=jax.ShapeDtypeStruct((depth, rows, n), F32),
        compiler_params=_params(("arbitrary", "arbitrary"), 40),
        name="ada",
    )(cc, ada_w, ada_b.reshape(depth, 1, n))


def _split_bf16(x, pieces):
    out = []
    for _ in range(pieces):
        p = x.astype(BF16)
        out.append(p)
        x = x - p.astype(F32)
    return out


def _dot_split(a, b):
    n = b.shape[1]
    a_hi, a_lo = _split_bf16(a, 2)
    b_hi, b_lo = _split_bf16(b, 2)
    hi = _dot(a_hi, jnp.concatenate([b_hi, b_lo], axis=1))
    return hi[:, :n] + (hi[:, n:] + _dot(a_lo, b_hi))


def _transpose_exact(x):
    m = x.shape[1]
    eye = (_iota((m, m), 0) == _iota((m, m), 1)).astype(BF16)
    hi, mid, lo = _split_bf16(x, 3)
    return _dot_nt(eye, hi) + (_dot_nt(eye, mid) + _dot_nt(eye, lo))


def _ffn_pre(xn, mod, nffn, wr):
    h2 = _norm_mod(xn, nffn, mod[3:4], mod[4:5])
    return h2.astype(BF16), _transpose_exact(_dot_split(h2, wr))


def _pool_kernel(x_ref, mod_ref, nmix_ref, nffn_ref, cmat_ref, cnt_ref, pw_ref, ps_ref, wr_ref,
                 x1_ref, hx2_ref, lg_ref, inv_ref, *pad, two_d, tc):
    t, d = x_ref.shape[1], x_ref.shape[2]
    gd = d // N_POOL_GROUPS
    mod = mod_ref[0]
    pad_tok = POOL_PAD_ROWS * GRID_W

    for c0 in range(0, t, tc):
        xs = x_ref[0, c0:c0 + tc, :]
        inv_ref[c0:c0 + tc, :] = lax.rsqrt(jnp.mean(xs * xs, axis=-1, keepdims=True) + EPS)

    def hx_of(r0, rn, j):
        cs = slice(j * gd, (j + 1) * gd)
        xs = x_ref[0, r0:r0 + rn, cs]
        g = nmix_ref[:, cs] * (1.0 + mod[1:2, cs])
        return xs * inv_ref[r0:r0 + rn, :] * g + mod[0:1, cs]

    if two_d:
        pad_ref, = pad
        pad_ref[0:pad_tok, :] = jnp.zeros((pad_tok, gd), F32)
        pad_ref[pad_tok + t:pad_tok + t + pad_tok, :] = jnp.zeros((pad_tok, gd), F32)

    for j, w in enumerate(POOL_WINDOWS):
        cs = slice(j * gd, (j + 1) * gd)
        cm = cmat_ref[j]
        sums = []
        for b0 in range(0, t, MXU_DIM):
            g = hx_of(b0, MXU_DIM, j)
            g_hi = g.astype(BF16)
            g_lo = (g - g_hi.astype(F32)).astype(BF16)
            csum = _dot(cm, g_hi) + _dot(cm, g_lo)
            if two_d:
                pad_ref[pad_tok + b0:pad_tok + b0 + MXU_DIM, :] = csum
            else:
                sums.append(csum)
        for c0 in range(0, t, tc):
            if two_d:
                tot = None
                for dr in range(-(w // 2), w - w // 2):
                    o = pad_tok + c0 + dr * GRID_W
                    sl = pad_ref[o:o + tc, :]
                    tot = sl if tot is None else tot + sl
            else:
                tot = sums[c0 // MXU_DIM]
            mean = tot / cnt_ref[c0:c0 + tc, j:j + 1]
            diff = mean - hx_of(c0, tc, j)
            y = _dot(diff.astype(BF16), pw_ref[j]) * ps_ref[:, cs]
            x1_ref[0, c0:c0 + tc, cs] = x_ref[0, c0:c0 + tc, cs] + mod[2:3, cs] * y

    tail = min(t, 2 * tc)
    for c0 in range(0, t, tail):
        hb, lg = _ffn_pre(x1_ref[0, c0:c0 + tail, :], mod, nffn_ref[...], wr_ref[...])
        hx2_ref[0, c0:c0 + tail, :] = hb
        lg_ref[0, :, c0:c0 + tail] = lg


def _pool_consts(t, two_d):
    cm = np.zeros((len(POOL_WINDOWS), MXU_DIM, MXU_DIM), np.float32)
    cnt = np.zeros((t, len(POOL_WINDOWS)), np.float32)
    n = GRID_W if two_d else t
    pos = np.arange(MXU_DIM)
    for j, w in enumerate(POOL_WINDOWS):
        col = pos % n
        lo = np.clip(col - w // 2, 0, n)
        hi = np.clip(col + w - w // 2, 0, n)
        same = (pos[:, None] // n) == (pos[None, :] // n)
        cm[j] = (same & (col[None, :] >= lo[:, None]) & (col[None, :] < hi[:, None])).astype(np.float32)
        tt = np.arange(t)
        c = tt % n
        ccnt = np.clip(c + w - w // 2, 0, n) - np.clip(c - w // 2, 0, n)
        if two_d:
            rows = t // GRID_W
            r = tt // GRID_W
            rcnt = np.clip(r + w - w // 2, 0, rows) - np.clip(r - w // 2, 0, rows)
            cnt[:, j] = ccnt * rcnt
        else:
            cnt[:, j] = ccnt
    return jnp.asarray(cm, BF16), jnp.asarray(cnt)


def _pool_mixer(x, mod, nmix, nffn, pool_w, pool_scale, w_router, two_d):
    b, t, d = x.shape
    gd = d // N_POOL_GROUPS
    if not two_d:
        assert t == MXU_DIM
    cmat, cnt = _pool_consts(t, two_d)
    tc = min(t, 512)
    pad_tok = POOL_PAD_ROWS * GRID_W
    scratch = [pltpu.VMEM((t, 1), F32)]
    if two_d:
        scratch.append(pltpu.VMEM((t + 2 * pad_tok, gd), F32))
    e = w_router.shape[1]
    full = lambda shape: pl.BlockSpec(shape, lambda i: (0,) * len(shape))
    return pl.pallas_call(
        functools.partial(_pool_kernel, two_d=two_d, tc=tc),
        grid=(b,),
        in_specs=[pl.BlockSpec((1, t, d), lambda i: (i, 0, 0)),
                  pl.BlockSpec((1, 8, d), lambda i: (i, 0, 0)),
                  full((1, d)), full((1, d)), full(cmat.shape), full(cnt.shape),
                  full(pool_w.shape), full((1, d)), full(w_router.shape)],
        out_specs=[pl.BlockSpec((1, t, d), lambda i: (i, 0, 0)),
                   pl.BlockSpec((1, t, d), lambda i: (i, 0, 0)),
                   pl.BlockSpec((1, e, t), lambda i: (i, 0, 0))],
        out_shape=[jax.ShapeDtypeStruct((b, t, d), F32),
                   jax.ShapeDtypeStruct((b, t, d), BF16),
                   jax.ShapeDtypeStruct((b, e, t), F32)],
        scratch_shapes=scratch,
        compiler_params=_params(("arbitrary",), 60),
        name="pool_mixer_2d" if two_d else "pool_mixer_1d",
    )(x, mod, nmix, nffn, cmat, cnt, pool_w, pool_scale, w_router)


def _route_kernel(lg_ref, slot_ref, gate_ref, offs_ref, *, cap):
    lg = lg_ref[0]
    e, t = lg.shape
    ex = jnp.exp(lg - jnp.max(lg, axis=0, keepdims=True))
    aff = ex / jnp.sum(ex, axis=0, keepdims=True)
    capf = jnp.float32(cap)

    def as_f32(v):
        return lax.bitcast_convert_type(v, F32)

    def enough(cand):
        return jnp.sum(jnp.where(aff >= as_f32(cand), 1.0, 0.0), axis=1, keepdims=True) >= capf

    def bisect2(i, v):
        hi = jnp.left_shift(jnp.int32(1), 30 - 2 * i)
        lo = jnp.left_shift(jnp.int32(1), 29 - 2 * i)
        c_hi, c_lo, c_both = v | hi, v | lo, v | hi | lo
        return jnp.where(enough(c_both), c_both, jnp.where(enough(c_hi), c_hi, jnp.where(enough(c_lo), c_lo, v)))

    kth = lax.fori_loop(0, 15, bisect2, jnp.zeros((e, 1), jnp.int32))
    kth = jnp.where(enough(kth | 1), kth | 1, kth)
    above = as_f32(kth + 1)
    gt = aff >= above
    eq = (aff >= as_f32(kth)) & jnp.logical_not(gt)
    need = capf - jnp.sum(jnp.where(gt, 1.0, 0.0), axis=1, keepdims=True)

    blk = min(t, MXU_DIM)
    before = (_iota((blk, blk), 0) < _iota((blk, blk), 1)).astype(BF16)

    def excl_cumsum(mask):
        ones = jnp.where(mask, 1.0, 0.0)
        outs, run = [], jnp.zeros((e, 1), F32)
        for b0 in range(0, t, blk):
            mb = ones[:, b0:b0 + blk]
            outs.append(_dot(mb.astype(BF16), before) + run)
            run = run + jnp.sum(mb, axis=1, keepdims=True)
        return jnp.concatenate(outs, axis=1) if len(outs) > 1 else outs[0]

    sel = gt | (eq & (excl_cumsum(eq) < need))
    slot_ref[0] = jnp.where(sel, excl_cumsum(sel), -1.0)
    gate_ref[0] = aff
    nl = offs_ref.shape[2]
    starts = (_iota((t, nl), 0) < _iota((t, nl), 1) * TOKEN_BLOCK).astype(BF16)
    offs_ref[0] = _dot(jnp.where(sel, 1.0, 0.0).astype(BF16), starts)


def _route(logits, cap):
    b, e, t = logits.shape
    spec = pl.BlockSpec((1, e, t), lambda i: (i, 0, 0))
    ospec = pl.BlockSpec((1, e, LANES), lambda i: (i, 0, 0))
    return pl.pallas_call(
        functools.partial(_route_kernel, cap=cap),
        grid=(b,),
        in_specs=[spec],
        out_specs=[spec, spec, ospec],
        out_shape=[jax.ShapeDtypeStruct((b, e, t), F32)] * 2 + [jax.ShapeDtypeStruct((b, e, LANES), F32)],
        compiler_params=_params(("arbitrary",), 32),
        name="route",
    )(logits)


def _gather_kernel(h_ref, slot_ref, o_ref, *, cap):
    hx = h_ref[0]
    t = hx.shape[0]
    row = _iota((cap, t), 0).astype(F32)
    for i in range(o_ref.shape[0]):
        onehot = jnp.where(slot_ref[0, i:i + 1, :] == row, 1.0, 0.0).astype(BF16)
        o_ref[i] = _dot(onehot, hx).astype(BF16)


def _gather(hx2, slot, cap):
    b, t, d = hx2.shape
    e = slot.shape[1]
    eb = 8
    return pl.pallas_call(
        functools.partial(_gather_kernel, cap=cap),
        grid=(b, e // eb),
        in_specs=[pl.BlockSpec((1, t, d), lambda i, j: (i, 0, 0)),
                  pl.BlockSpec((1, eb, t), lambda i, j: (i, j, 0))],
        out_specs=pl.BlockSpec((eb, cap, d), lambda i, j: (j, i, 0)),
        out_shape=jax.ShapeDtypeStruct((e, b * cap, d), BF16),
        compiler_params=_params(("arbitrary", "arbitrary"), 48),
        name="gather",
    )(hx2, slot)


STATIC_DOTS = 2


def _pair_list(offs_ref, kb, n_exp, nblk, le_ref, lg_ref):
    b = pl.program_id(0)
    shift = SLOT_GROUP.bit_length() - 1
    max_groups = le_ref.shape[0] // (n_exp + 1)

    cnt = jnp.int32(0)
    for e in range(n_exp):
        o0 = offs_ref[b, e * (nblk + 1) + kb]
        o1 = offs_ref[b, e * (nblk + 1) + kb + 1]
        lo = jnp.right_shift(o0, shift)
        ng = jnp.where(o1 > o0, jnp.right_shift(o1 - 1, shift) - lo + 1, 0)
        for i in range(max_groups):
            le_ref[cnt + i] = e
            lg_ref[cnt + i] = lo + i
        cnt = cnt + ng
    return cnt


def _pair(le_ref, lg_ref, idx, cnt):
    valid = idx < cnt
    safe = jnp.minimum(idx, jnp.maximum(cnt - 1, 0))
    e_p = jnp.where(valid, le_ref[safe], 0)
    g_p = jnp.where(valid, lg_ref[safe], 0)
    base = jnp.where(valid, g_p * SLOT_GROUP, -2 * SLOT_GROUP).astype(F32)
    return e_p, pl.multiple_of(g_p * SLOT_GROUP, SLOT_GROUP), base


def _for_each_dot(cnt, body):
    for c in range(STATIC_DOTS):
        body(c, 0)
    lax.fori_loop(STATIC_DOTS, (cnt + PAIRS_PER_DOT - 1) // PAIRS_PER_DOT, body, 0)


def _sparse_gather_kernel(offs_ref, h_ref, slot_ref, o_ref, le_ref, lg_ref, p_ref, *, n_exp, nblk):
    sg = SLOT_GROUP
    o_ref[...] = jnp.zeros(o_ref.shape, o_ref.dtype)
    sub = _iota((sg, TOKEN_BLOCK), 0).astype(F32)

    def per_block(kb, _):
        cnt = _pair_list(offs_ref, kb, n_exp, nblk, le_ref, lg_ref)
        tok = pl.ds(pl.multiple_of(kb * TOKEN_BLOCK, TOKEN_BLOCK), TOKEN_BLOCK)

        def dot_batch(c, _):
            dst = []
            for p in range(PAIRS_PER_DOT):
                e_p, s0, base = _pair(le_ref, lg_ref, c * PAIRS_PER_DOT + p, cnt)
                row = slot_ref[0, kb, pl.ds(e_p, 1), :]
                p_ref[p * sg:(p + 1) * sg, :] = jnp.where(row == base + sub, 1.0, 0.0).astype(BF16)
                dst.append((e_p, s0))
            z = _dot(p_ref[...], h_ref[0, tok, :])
            for p, (e_p, s0) in enumerate(dst):
                o_ref[e_p, pl.ds(s0, sg), :] += z[p * sg:(p + 1) * sg, :].astype(o_ref.dtype)
            return 0

        _for_each_dot(cnt, dot_batch)
        return 0

    lax.fori_loop(0, nblk, per_block, 0)


def _by_token_block(a):
    b, e, t = a.shape
    return jnp.swapaxes(a.reshape(b, e, t // TOKEN_BLOCK, TOKEN_BLOCK), 1, 2)


def _sparse_gather(hx2, slot_blocks, offs, cap):
    b, t, d = hx2.shape
    _, nblk, e, _ = slot_blocks.shape
    return pl.pallas_call(
        functools.partial(_sparse_gather_kernel, n_exp=e, nblk=nblk),
        grid_spec=pltpu.PrefetchScalarGridSpec(
            num_scalar_prefetch=1,
            grid=(b,),
            in_specs=[pl.BlockSpec((1, t, d), lambda i, o: (i, 0, 0)),
                      pl.BlockSpec((1, nblk, e, TOKEN_BLOCK), lambda i, o: (i, 0, 0, 0))],
            out_specs=pl.BlockSpec((e, cap, d), lambda i, o: (0, i, 0)),
            scratch_shapes=[pltpu.SMEM(((e + 1) * (cap // SLOT_GROUP),), jnp.int32),
                            pltpu.SMEM(((e + 1) * (cap // SLOT_GROUP),), jnp.int32),
                            pltpu.VMEM((PAIRS_PER_DOT * SLOT_GROUP, TOKEN_BLOCK), BF16)]),
        out_shape=jax.ShapeDtypeStruct((e, b * cap, d), BF16),
        compiler_params=_params(("arbitrary",), 40),
        name="sparse_gather",
    )(offs, hx2, slot_blocks)


def _sparse_scatter_kernel(offs_ref, x_ref, y_ref, slot_ref, gate_ref, mod_ref, fn_ref, o_ref,
                           le_ref, lg_ref, p_ref, yc_ref, acc_ref, *, n_exp, nblk, bps, final):
    sg = SLOT_GROUP
    sub = _iota((sg, TOKEN_BLOCK), 0).astype(F32)

    def per_block(i, _):
        cnt = _pair_list(offs_ref, pl.program_id(1) * bps + i, n_exp, nblk, le_ref, lg_ref)
        tok = pl.ds(pl.multiple_of(i * TOKEN_BLOCK, TOKEN_BLOCK), TOKEN_BLOCK)
        acc_ref[...] = jnp.zeros(acc_ref.shape, F32)

        def dot_batch(c, _):
            for p in range(PAIRS_PER_DOT):
                e_p, s0, base = _pair(le_ref, lg_ref, c * PAIRS_PER_DOT + p, cnt)
                row = slot_ref[0, i, pl.ds(e_p, 1), :]
                gate = gate_ref[0, i, pl.ds(e_p, 1), :]
                p_ref[p * sg:(p + 1) * sg, :] = jnp.where(row == base + sub, gate, 0.0)
                yc_ref[p * sg:(p + 1) * sg, :] = y_ref[e_p, pl.ds(s0, sg), :]
            acc_ref[...] += _dot(p_ref[...].T.astype(BF16), yc_ref[...])
            return 0

        _for_each_dot(cnt, dot_batch)
        out = x_ref[0, tok, :] + mod_ref[0][5:6] * acc_ref[...]
        if final:
            out = out * lax.rsqrt(jnp.mean(out * out, axis=-1, keepdims=True) + EPS) * fn_ref[...]
        o_ref[0, tok, :] = out
        return 0

    lax.fori_loop(0, bps, per_block, 0)


def _sparse_scatter(x, y, slot_blocks, gate_blocks, offs, mod, final_norm, cap, final):
    b, t, d = x.shape
    _, nblk, e, _ = slot_blocks.shape
    bps = min(nblk, 4)
    rows = PAIRS_PER_DOT * SLOT_GROUP
    tok = lambda n: pl.BlockSpec((1, bps * TOKEN_BLOCK, n), lambda i, j, o: (i, j, 0))
    exp = pl.BlockSpec((1, bps, e, TOKEN_BLOCK), lambda i, j, o: (i, j, 0, 0))
    return pl.pallas_call(
        functools.partial(_sparse_scatter_kernel, n_exp=e, nblk=nblk, bps=bps, final=final),
        grid_spec=pltpu.PrefetchScalarGridSpec(
            num_scalar_prefetch=1,
            grid=(b, nblk // bps),
            in_specs=[tok(d), pl.BlockSpec((e, cap, d), lambda i, j, o: (0, i, 0)), exp, exp,
                      pl.BlockSpec((1, 8, d), lambda i, j, o: (i, 0, 0)),
                      pl.BlockSpec((1, d), lambda i, j, o: (0, 0))],
            out_specs=tok(d),
            scratch_shapes=[pltpu.SMEM(((e + 1) * (cap // SLOT_GROUP),), jnp.int32),
                            pltpu.SMEM(((e + 1) * (cap // SLOT_GROUP),), jnp.int32),
                            pltpu.VMEM((rows, TOKEN_BLOCK), F32),
                            pltpu.VMEM((rows, d), BF16),
                            pltpu.VMEM((TOKEN_BLOCK, d), F32)]),
        out_shape=jax.ShapeDtypeStruct((b, t, d), F32),
        compiler_params=_params(("arbitrary", "arbitrary"), 48),
        name="sparse_scatter",
    )(offs, x, y, slot_blocks, gate_blocks, mod, final_norm)


def _ffn_kernel(*refs, n_sets, mc):
    x_refs = refs[:n_sets]
    wg_ref, wu_ref, wd_ref = refs[n_sets:n_sets + 3]
    y_refs = refs[n_sets + 3:2 * n_sets + 3]
    acc_refs = refs[2 * n_sets + 3:3 * n_sets + 3]
    wgb, wub, wdb = refs[3 * n_sets + 3:]
    f = pl.program_id(1)
    wgb[...] = wg_ref[0, 0].astype(BF16)
    wub[...] = wu_ref[0, 0].astype(BF16)
    wdb[...] = wd_ref[0, 0].astype(BF16)
    last = f == pl.num_programs(1) - 1

    @pl.when((pl.program_id(0) == 0) & (f == 0))
    def _():
        for acc in acc_refs:
            acc[...] = jnp.zeros(acc.shape, F32)

    for x_ref, y_ref, acc in zip(x_refs, y_refs, acc_refs):
        m = x_ref.shape[1]
        step = min(m, mc)
        for m0 in range(0, m, step):
            xs = x_ref[0, m0:m0 + step, :]
            hg = _dot(xs, wgb[...])
            hu = _dot(xs, wub[...])
            hid = (hg * _sigmoid(hg) * hu).astype(BF16)
            tot = acc[m0:m0 + step, :] + _dot(hid, wdb[...])
            acc[m0:m0 + step, :] = jnp.where(last, 0.0, tot)
            y_ref[0, m0:m0 + step, :] = tot.astype(BF16)


def _expert_ffn(xs, layer, w_gate, w_up, w_down):
    _, e, d, hidden = w_gate.shape
    tf = 512
    n = len(xs)
    xspecs = [pl.BlockSpec((1, x.shape[1], d), lambda i, j: (i, 0, 0)) for x in xs]
    return pl.pallas_call(
        functools.partial(_ffn_kernel, n_sets=n, mc=1024),
        grid=(e, hidden // tf),
        in_specs=xspecs + [pl.BlockSpec((1, 1, d, tf), lambda i, j: (layer, i, 0, j)),
                           pl.BlockSpec((1, 1, d, tf), lambda i, j: (layer, i, 0, j)),
                           pl.BlockSpec((1, 1, tf, d), lambda i, j: (layer, i, j, 0))],
        out_specs=xspecs,
        out_shape=[jax.ShapeDtypeStruct(x.shape, BF16) for x in xs],
        scratch_shapes=[pltpu.VMEM((x.shape[1], d), F32) for x in xs]
        + [pltpu.VMEM((d, tf), BF16), pltpu.VMEM((d, tf), BF16), pltpu.VMEM((tf, d), BF16)],
        compiler_params=_params(("arbitrary", "arbitrary"), 60),
        name="expert_ffn",
    )(*xs, w_gate, w_up, w_down)


def _scatter_kernel(x_ref, y_ref, slot_ref, gate_ref, mod_ref, fn_ref, o_ref, *, final):
    tt = x_ref.shape[1]
    e, cap, _ = y_ref.shape
    eye = (_iota((tt, tt), 0) == _iota((tt, tt), 1)).astype(BF16)
    slot_t = _dot_nt(eye, (slot_ref[0] + 1.0).astype(BF16))
    gate_t = _dot_nt(eye, gate_ref[0].astype(BF16))
    lane = _iota((tt, cap), 1).astype(F32) + 1.0
    acc = None
    for i in range(e):
        pt = jnp.where(slot_t[:, i:i + 1] == lane, gate_t[:, i:i + 1], 0.0).astype(BF16)
        part = _dot(pt, y_ref[i])
        acc = part if acc is None else acc + part
    out = x_ref[0] + mod_ref[0][5:6] * acc
    if final:
        out = out * lax.rsqrt(jnp.mean(out * out, axis=-1, keepdims=True) + EPS) * fn_ref[...]
    o_ref[0] = out


def _scatter(x, y, slot, gate, mod, final_norm, cap, final):
    b, t, d = x.shape
    e = slot.shape[1]
    tt = min(t, 512)
    return pl.pallas_call(
        functools.partial(_scatter_kernel, final=final),
        grid=(b, t // tt),
        in_specs=[pl.BlockSpec((1, tt, d), lambda i, j: (i, j, 0)),
                  pl.BlockSpec((e, cap, d), lambda i, j: (0, i, 0)),
                  pl.BlockSpec((1, e, tt), lambda i, j: (i, 0, j)),
                  pl.BlockSpec((1, e, tt), lambda i, j: (i, 0, j)),
                  pl.BlockSpec((1, 8, d), lambda i, j: (i, 0, 0)),
                  pl.BlockSpec((1, d), lambda i, j: (0, 0))],
        out_specs=pl.BlockSpec((1, tt, d), lambda i, j: (i, j, 0)),
        out_shape=jax.ShapeDtypeStruct((b, t, d), F32),
        compiler_params=_params(("arbitrary", "arbitrary"), 48),
        name="scatter",
    )(x, y, slot, gate, mod, final_norm)


def _moe(streams, layer, w_gate, w_up, w_down, final_norm, final):
    routed = []
    for x1, hx2, logits, mod in streams:
        t = x1.shape[1]
        cap = CAPACITY_FACTOR * t // N_EXPERTS
        slot, gate, offs = _route(logits, cap)
        nblk = t // TOKEN_BLOCK
        if nblk > 1:
            offs = offs[:, :, :nblk + 1].astype(jnp.int32).reshape(offs.shape[0], -1)
            slot, gate = _by_token_block(slot), _by_token_block(gate)
            xg = _sparse_gather(hx2, slot, offs, cap)
        else:
            offs = None
            xg = _gather(hx2, slot, cap)
        routed.append((slot, gate, offs, cap, xg))
    ys = _expert_ffn([r[4] for r in routed], layer, w_gate, w_up, w_down)
    outs = []
    for (x1, _, _, mod), (slot, gate, offs, cap, _), y in zip(streams, routed, ys):
        if offs is None:
            outs.append(_scatter(x1, y, slot, gate, mod, final_norm, cap, final))
        else:
            outs.append(_sparse_scatter(x1, y, slot, gate, offs, mod, final_norm, cap, final))
    return outs


def _inproj_kernel(x_ref, mod_ref, nmix_ref, wko_ref, wqvt_ref, wg_ref, bg_ref,
                   qt_ref, k_ref, vt_ref, o_ref, gr_ref, *, chunk, qk, dk):
    tt = x_ref.shape[1]
    mod = mod_ref[0]
    h = _norm_mod(x_ref[0], nmix_ref[...], mod[0:1], mod[1:2])
    hb = h.astype(BF16)
    p = _dot(hb, wko_ref[...])
    k_ref[0] = (p[:, :qk] * (dk ** -0.5)).astype(BF16)
    o_ref[0] = _sigmoid(p[:, qk:]).astype(o_ref.dtype)
    p_t = _dot_nt(wqvt_ref[...], hb)
    for ci in range(tt // chunk):
        qt_ref[0, ci] = p_t[:qk, ci * chunk:(ci + 1) * chunk].astype(BF16)
        vt_ref[0, ci] = p_t[qk:, ci * chunk:(ci + 1) * chunk].astype(BF16)

    nh = MLSTM_HEADS
    g_t = _transpose_exact(_dot_split(h, wg_ref[...]) + bg_ref[...])
    row = _iota((4 * nh, chunk), 0)
    is_f = (row & nh) == nh
    a = _iota((chunk, chunk), 0)
    c = _iota((chunk, chunk), 1)
    tri = jnp.concatenate([(a <= c).astype(BF16), (a >= c).astype(BF16)], axis=1)
    ng = 4 * nh
    for c0 in range(0, tt, chunk):
        gc = g_t[:, c0:c0 + chunk]
        pieces = _split_bf16(jnp.where(is_f, _log_sigmoid(gc), 0.0), 3)
        cum = _dot(jnp.concatenate(pieces, axis=0), tri)
        cum = cum[:ng] + (cum[ng:2 * ng] + cum[2 * ng:])
        gr_ref[0, :, c0:c0 + chunk] = jnp.where(is_f, jnp.where(row < 2 * nh, cum[:, :chunk], cum[:, chunk:]), gc)


def _inproj(x, mod, nmix, w_ko, w_qvt, w_g, b_g, qk, vd):
    b, t, d = x.shape
    tt = min(t, 512)
    ng = w_g.shape[1]
    ln = SCAN_CHUNK
    full = lambda shape: pl.BlockSpec(shape, lambda i, j: (0,) * len(shape))
    tok = lambda n: pl.BlockSpec((1, tt, n), lambda i, j: (i, j, 0))
    slab = lambda n: pl.BlockSpec((1, tt // ln, n, ln), lambda i, j: (i, j, 0, 0))
    return pl.pallas_call(
        functools.partial(_inproj_kernel, chunk=ln, qk=qk, dk=qk // MLSTM_HEADS),
        grid=(b, t // tt),
        in_specs=[tok(d), pl.BlockSpec((1, 8, d), lambda i, j: (i, 0, 0)), full((1, d)),
                  full(w_ko.shape), full(w_qvt.shape), full(w_g.shape), full(b_g.shape)],
        out_specs=[slab(qk), tok(qk), slab(vd), tok(vd),
                   pl.BlockSpec((1, ng, tt), lambda i, j: (i, 0, j))],
        out_shape=[jax.ShapeDtypeStruct((b, t // ln, qk, ln), BF16), jax.ShapeDtypeStruct((b, t, qk), BF16),
                   jax.ShapeDtypeStruct((b, t // ln, vd, ln), BF16), jax.ShapeDtypeStruct((b, t, vd), BF16),
                   jax.ShapeDtypeStruct((b, ng, t), F32)],
        compiler_params=_params(("arbitrary", "arbitrary"), 56),
        name="mlstm_inproj",
    )(x, mod, nmix, w_ko, w_qvt, w_g, b_g)


def _scan_kernel(kc_ref, vc_ref, grc_ref, qx_ref, kx_ref, vx_ref, grx_ref,
                 out_ref, s_ref, sall_ref, mall_ref, *, chunk):
    ln = chunk
    nh = MLSTM_HEADS
    pair = pl.program_id(1)
    kl = kx_ref.shape[2]
    dv = vx_ref.shape[2] // 2
    lane = _iota((ln, kl), 1)
    kmask = (lane < kl // 2, lane >= kl // 2)
    klane = _iota((2 * dv, kl), 1)
    kcols = (klane < kl // 2, klane >= kl // 2)
    ones = jnp.ones((dv, ln), BF16)
    si = _iota((ln, ln), 0)
    ji = _iota((ln, ln), 1)
    causal = (si <= ji, si >= ji)
    ncc = kc_ref.shape[1] // ln
    ncx = kx_ref.shape[1] // ln

    def gate_rows(gr_ref, hh, dirn, c):
        base = 2 * nh * dirn + 2 * pair + hh
        return gr_ref[0, base, pl.ds(c, 1), :], gr_ref[0, base + nh, pl.ds(c, 1), :]

    def keys(k_ref, c):
        k = k_ref[0, pl.ds(pl.multiple_of(c * ln, ln), ln), :]
        return [jnp.where(kmask[hh], k, jnp.zeros((), BF16)) for hh in range(2)]

    def values_t(v_ref, c, hh):
        return jnp.concatenate([v_ref[0, c, hh * dv:(hh + 1) * dv, :], ones], axis=0)

    def advance(refs, n, record):
        k_ref, v_ref, gr_ref = refs

        def body(i, ms):
            new_ms = [None] * 4
            cs = (i, n - 1 - i)
            for dirn in range(2):
                c = cs[dirn]
                vws, decays = [], []
                for hh in range(2):
                    ch = 2 * hh + dirn
                    ig, bc = gate_rows(gr_ref, hh, dirn, c)
                    m = ms[ch]
                    if record:
                        s_own = jnp.where(kcols[hh], s_ref[ch], 0.0)
                        sall_ref[ch * ncx + c] = s_own.astype(BF16)
                        mall_ref[ch * ncx + c] = jnp.broadcast_to(m, (8, ln))
                    b_end = bc[:, ln - 1:ln] if dirn == 0 else bc[:, 0:1]
                    gl = b_end - bc + ig
                    m_new = jnp.maximum(b_end + m, jnp.max(gl, axis=1, keepdims=True))
                    vws.append((values_t(v_ref, c, hh) * jnp.exp(gl - m_new)).astype(BF16))
                    decays.append(jnp.exp(b_end + m - m_new))
                    new_ms[ch] = m_new
                upd = _dot(jnp.concatenate(vws, axis=0), k_ref[0, pl.ds(pl.multiple_of(c * ln, ln), ln), :])
                for hh in range(2):
                    ch = 2 * hh + dirn
                    s_ref[ch] = decays[hh] * s_ref[ch] + upd[hh * 2 * dv:(hh + 1) * 2 * dv, :]
            return tuple(new_ms)

        return body

    s_ref[...] = jnp.zeros(s_ref.shape, F32)
    ms = tuple(jnp.zeros((1, 1), F32) for _ in range(4))
    ms = lax.fori_loop(0, ncc, advance((kc_ref, vc_ref, grc_ref), ncc, False), ms, unroll=min(ncc, SCAN_UNROLL))
    lax.fori_loop(0, ncx, advance((kx_ref, vx_ref, grx_ref), ncx, True), ms, unroll=min(ncx, SCAN_UNROLL))

    def emit(c, _):
        q_t = qx_ref[0, c]
        q_tf = q_t.astype(F32)
        s_pair = _dot(jnp.concatenate(keys(kx_ref, c), axis=0), q_t)
        for hh in range(2):
            s_kq = s_pair[hh * ln:(hh + 1) * ln, :]
            v_t = values_t(vx_ref, c, hh)
            hsum = None
            for dirn in range(2):
                ch = 2 * hh + dirn
                ig, bc = gate_rows(grx_ref, hh, dirn, c)
                u_col = jnp.broadcast_to(ig - bc, (ln, ln)).T
                a = bc + mall_ref[ch * ncx + c][0:1, :]
                dm = jnp.where(causal[dirn], u_col + bc, -jnp.inf)
                mj = jnp.maximum(a, jnp.max(dm, axis=0, keepdims=True))
                sm = (s_kq * jnp.exp(dm - mj)).astype(BF16)
                qw = (q_tf * jnp.exp(a - mj)).astype(BF16)
                num = _dot(jnp.concatenate([v_t, sall_ref[ch * ncx + c]], axis=1),
                           jnp.concatenate([sm, qw], axis=0))
                h = num[:dv, :] * (1.0 / jnp.maximum(jnp.abs(num[dv:dv + 1, :]), jnp.exp(-mj)))
                hsum = h if hsum is None else hsum + h
            hn = hsum * lax.rsqrt(jnp.mean(hsum * hsum, axis=0, keepdims=True) + EPS)
            out_ref[0, pl.ds(pl.multiple_of(c * ln, ln), ln), hh * dv:(hh + 1) * dv] = hn.T.astype(out_ref.dtype)
        return 0

    lax.fori_loop(0, ncx, emit, 0, unroll=min(ncx, SCAN_UNROLL))


def _scan(ctx_parts, lat_parts):
    kc, vc, grc = ctx_parts
    qx, kx, vx, grx = lat_parts
    b, t, qk = kx.shape
    vd = vx.shape[2]
    nh = MLSTM_HEADS
    ln = SCAN_CHUNK
    dv = vd // nh
    kl = 2 * (qk // nh)
    assert kl == ln and dv == ln, "scan kernel assumes key-pair lanes = head value dim = chunk"
    ncx = t // ln

    def specs(k, v, gr):
        tk = k.shape[1]
        return [pl.BlockSpec((1, tk, kl), lambda i, j: (i, 0, j)),
                pl.BlockSpec((1, tk // ln, 2 * dv, ln), lambda i, j: (i, 0, j, 0)),
                pl.BlockSpec((1,) + gr.shape[1:], lambda i, j: (i, 0, 0, 0))]

    return pl.pallas_call(
        functools.partial(_scan_kernel, chunk=ln),
        grid=(b, nh // 2),
        in_specs=specs(kc, vc, grc) + [pl.BlockSpec((1, ncx, kl, ln), lambda i, j: (i, 0, j, 0))] + specs(kx, vx, grx),
        out_specs=pl.BlockSpec((1, t, 2 * dv), lambda i, j: (i, 0, j)),
        out_shape=jax.ShapeDtypeStruct((b, t, vd), BF16),
        scratch_shapes=[pltpu.VMEM((4, 2 * dv, kl), F32),
                        pltpu.VMEM((4 * ncx, 2 * dv, kl), BF16),
                        pltpu.VMEM((4 * ncx, 8, ln), F32)],
        compiler_params=_params(("arbitrary", "arbitrary"), 40),
        name="mlstm_scan",
    )(kc, vc, grc, qx, kx, vx, grx)


def _mlstm_out_kernel(hn_ref, og_ref, x_ref, mod_ref, mn_ref, wout_ref, nffn_ref, wr_ref,
                      x1_ref, hx2_ref, lg_ref):
    mod = mod_ref[0]
    a = (hn_ref[0].astype(F32) * mn_ref[...]) * og_ref[0].astype(F32)
    xn = x_ref[0] + mod[2:3] * _dot(a.astype(BF16), wout_ref[...])
    x1_ref[0] = xn
    hb, lg = _ffn_pre(xn, mod, nffn_ref[...], wr_ref[...])
    hx2_ref[0] = hb
    lg_ref[0] = lg


def _mlstm_out(hn, og, x, mod, mnorm, w_out, nffn, w_router):
    b, t, d = x.shape
    vd = hn.shape[2]
    e = w_router.shape[1]
    tt = min(t, 512)
    full = lambda shape: pl.BlockSpec(shape, lambda i, j: (0,) * len(shape))
    tok = lambda n: pl.BlockSpec((1, tt, n), lambda i, j: (i, j, 0))
    return pl.pallas_call(
        _mlstm_out_kernel,
        grid=(b, t // tt),
        in_specs=[tok(vd), tok(vd), tok(d), pl.BlockSpec((1, 8, d), lambda i, j: (i, 0, 0)),
                  full((1, vd)), full(w_out.shape), full((1, d)), full(w_router.shape)],
        out_specs=[tok(d), tok(d), pl.BlockSpec((1, e, tt), lambda i, j: (i, 0, j))],
        out_shape=[jax.ShapeDtypeStruct((b, t, d), F32), jax.ShapeDtypeStruct((b, t, d), BF16),
                   jax.ShapeDtypeStruct((b, e, t), F32)],
        compiler_params=_params(("arbitrary", "arbitrary"), 48),
        name="mlstm_out",
    )(hn, og, x, mod, mnorm, w_out, nffn, w_router)


def _chunked_rows(gr):
    b, n, t = gr.shape
    return gr.reshape(b, n, t // SCAN_CHUNK, SCAN_CHUNK)


def kernel(x, c, ctx, c_ctx, ada_w, ada_b, norm_mix, norm_ffn, pool_w, pool_scale, mlstm_w_in, mlstm_b_gates,
           mlstm_norm, mlstm_w_out, moe_router, moe_w_gate, moe_w_up, moe_w_down, final_norm):
    bsz, seq, d = x.shape
    depth = ada_w.shape[0]
    n_mixers = 2

    cc = jnp.concatenate([c, c_ctx[None, :], jnp.zeros((16 - bsz - 1, d), F32)], axis=0)
    ada = _ada(cc, ada_w, ada_b).reshape(depth, 16, N_ADA, d)
    pad = jnp.zeros((bsz, 8 - N_ADA, d), F32)

    fn = final_norm.reshape(1, d)
    for i in range(depth):
        last = i == depth - 1
        j = i // n_mixers
        mod_x = jnp.concatenate([ada[i, :bsz], pad], axis=1)
        mod_c = jnp.concatenate([jnp.broadcast_to(ada[i, bsz][None], (bsz, N_ADA, d)), pad], axis=1)
        nmix = norm_mix[i].reshape(1, d)
        nffn = norm_ffn[i].reshape(1, d)
        w_router = moe_router[i]
        streams = []
        if i % n_mixers == 0:
            pw = pool_w[j].astype(BF16)
            ps = pool_scale[j].reshape(1, d)
            x1, hx2, lg = _pool_mixer(x, mod_x, nmix, nffn, pw, ps, w_router, True)
            streams.append((x1, hx2, lg, mod_x))
            if not last:
                c1, hc2, lgc = _pool_mixer(ctx, mod_c, nmix, nffn, pw, ps, w_router, False)
                streams.append((c1, hc2, lgc, mod_c))
        else:
            qk = mlstm_w_in.shape[2] - 2 * mlstm_w_out.shape[1] - 4 * MLSTM_HEADS
            qk //= 2
            vd = mlstm_w_out.shape[1]
            w_in = mlstm_w_in[j]
            w_ko = jnp.concatenate([w_in[:, qk:2 * qk], w_in[:, 2 * qk + vd:2 * qk + 2 * vd]], axis=1).astype(BF16)
            w_qvt = jnp.concatenate([w_in[:, :qk], w_in[:, 2 * qk:2 * qk + vd]], axis=1).T.astype(BF16)
            w_g = w_in[:, 2 * qk + 2 * vd:]
            b_g = mlstm_b_gates[j].reshape(1, -1)
            proj = lambda s, m: _inproj(s, m, nmix, w_ko, w_qvt, w_g, b_g, qk, vd)
            _, kc, vc, _, grc = proj(ctx, mod_c)
            qx, kx, vx, ox, grx = proj(x, mod_x)
            hn = _scan((kc, vc, _chunked_rows(grc)), (qx, kx, vx, _chunked_rows(grx)))
            x1, hx2, lg = _mlstm_out(hn, ox, x, mod_x, mlstm_norm[j].reshape(1, vd),
                                     mlstm_w_out[j].astype(BF16), nffn, w_router)
            streams.append((x1, hx2, lg, mod_x))
            assert last, "context output of the mLSTM mixer is only needed by a following layer"
        outs = _moe(streams, i, moe_w_gate, moe_w_up, moe_w_down, fn, last)
        x = outs[0]
        if not last:
            ctx = outs[1]
    return x
```

```python
import functools

import jax
import jax.numpy as jnp
import numpy as np
from jax import lax
from jax.experimental import pallas as pl
from jax.experimental.pallas import tpu as pltpu

F32 = jnp.float32
BF16 = jnp.bfloat16
HIGHEST = lax.Precision.HIGHEST

GRID_W = 64
EPS = 1e-6
N_ADA = 6
POOL_WINDOWS = (2, 4, 8, 16)
N_POOL_GROUPS = 4
MLSTM_HEADS = 8
N_EXPERTS = 16
CAPACITY_FACTOR = 2

MIB = 1024 * 1024
MXU_DIM = 256
LANES = 128
TOKEN_BLOCK = 256
SLOT_GROUP = 32
PAIRS_PER_DOT = 16
SCAN_CHUNK = 128
SCAN_UNROLL = 8
POOL_PAD_ROWS = max(POOL_WINDOWS) // 2


def _params(sem, vmem_mib):
    return pltpu.CompilerParams(dimension_semantics=sem, vmem_limit_bytes=vmem_mib * MIB)


def _dot(a, b, precision=None):
    return jnp.dot(a, b, preferred_element_type=F32, precision=precision)


def _dot_nt(a, b, precision=None):
    return lax.dot_general(a, b, (((1,), (1,)), ((), ())), preferred_element_type=F32, precision=precision)


def _iota(shape, dim, dtype=jnp.int32):
    return lax.broadcasted_iota(dtype, shape, dim)


def _sigmoid(x):
    return 1.0 / (1.0 + jnp.exp(-x))


def _log_sigmoid(x):
    return jnp.minimum(x, 0.0) - jnp.log1p(jnp.exp(-jnp.abs(x)))


def _norm_mod(x, g, shift, scale):
    inv = lax.rsqrt(jnp.mean(x * x, axis=-1, keepdims=True) + EPS)
    return (x * inv) * (g * (1.0 + scale)) + shift


def _ada_kernel(c_ref, w_ref, b_ref, o_ref):
    c = c_ref[...]
    s = c * _sigmoid(c)
    s_hi, s_lo = _split_bf16(s, 2)
    w_hi, w_lo = _split_bf16(w_ref[0], 2)
    o_ref[0] = _dot(s_hi, w_hi) + (_dot(s_hi, w_lo) + _dot(s_lo, w_hi)) + b_ref[0]


def _ada(cc, ada_w, ada_b):
    depth, d, n = ada_w.shape
    rows = cc.shape[0]
    tn = n // 4
    return pl.pallas_call(
        _ada_kernel,
        grid=(depth, n // tn),
        in_specs=[pl.BlockSpec((rows, d), lambda i, j: (0, 0)),
                  pl.BlockSpec((1, d, tn), lambda i, j: (i, 0, j)),
                  pl.BlockSpec((1, 1, tn), lambda i, j: (i, 0, j))],
        out_specs=pl.BlockSpec((1, rows, tn), lambda i, j: (i, 0, j)),
        out_shape=jax.ShapeDtypeStruct((depth, rows, n), F32),
        compiler_params=_params(("arbitrary", "arbitrary"), 40),
        name="ada",
    )(cc, ada_w, ada_b.reshape(depth, 1, n))


def _split_bf16(x, pieces):
    out = []
    for _ in range(pieces):
        p = x.astype(BF16)
        out.append(p)
        x = x - p.astype(F32)
    return out


def _dot_split(a, b):
    n = b.shape[1]
    a_hi, a_lo = _split_bf16(a, 2)
    b_hi, b_lo = _split_bf16(b, 2)
    hi = _dot(a_hi, jnp.concatenate([b_hi, b_lo], axis=1))
    return hi[:, :n] + (hi[:, n:] + _dot(a_lo, b_hi))


def _transpose_exact(x):
    m = x.shape[1]
    eye = (_iota((m, m), 0) == _iota((m, m), 1)).astype(BF16)
    hi, mid, lo = _split_bf16(x, 3)
    return _dot_nt(eye, hi) + (_dot_nt(eye, mid) + _dot_nt(eye, lo))


def _ffn_pre(xn, mod, nffn, wr):
    h2 = _norm_mod(xn, nffn, mod[3:4], mod[4:5])
    return h2.astype(BF16), _transpose_exact(_dot_split(h2, wr))


def _pool_kernel(x_ref, mod_ref, nmix_ref, nffn_ref, cmat_ref, cnt_ref, pw_ref, ps_ref, wr_ref,
                 x1_ref, hx2_ref, lg_ref, inv_ref, *pad, two_d, tc):
    t, d = x_ref.shape[1], x_ref.shape[2]
    gd = d // N_POOL_GROUPS
    mod = mod_ref[0]
    pad_tok = POOL_PAD_ROWS * GRID_W

    for c0 in range(0, t, tc):
        xs = x_ref[0, c0:c0 + tc, :]
        inv_ref[c0:c0 + tc, :] = lax.rsqrt(jnp.mean(xs * xs, axis=-1, keepdims=True) + EPS)

    def hx_of(r0, rn, j):
        cs = slice(j * gd, (j + 1) * gd)
        xs = x_ref[0, r0:r0 + rn, cs]
        g = nmix_ref[:, cs] * (1.0 + mod[1:2, cs])
        return xs * inv_ref[r0:r0 + rn, :] * g + mod[0:1, cs]

    if two_d:
        pad_ref, = pad
        pad_ref[0:pad_tok, :] = jnp.zeros((pad_tok, gd), F32)
        pad_ref[pad_tok + t:pad_tok + t + pad_tok, :] = jnp.zeros((pad_tok, gd), F32)

    for j, w in enumerate(POOL_WINDOWS):
        cs = slice(j * gd, (j + 1) * gd)
        cm = cmat_ref[j]
        sums = []
        for b0 in range(0, t, MXU_DIM):
            g = hx_of(b0, MXU_DIM, j)
            g_hi = g.astype(BF16)
            g_lo = (g - g_hi.astype(F32)).astype(BF16)
            csum = _dot(cm, g_hi) + _dot(cm, g_lo)
            if two_d:
                pad_ref[pad_tok + b0:pad_tok + b0 + MXU_DIM, :] = csum
            else:
                sums.append(csum)
        for c0 in range(0, t, tc):
            if two_d:
                tot = None
                for dr in range(-(w // 2), w - w // 2):
                    o = pad_tok + c0 + dr * GRID_W
                    sl = pad_ref[o:o + tc, :]
                    tot = sl if tot is None else tot + sl
            else:
                tot = sums[c0 // MXU_DIM]
            mean = tot / cnt_ref[c0:c0 + tc, j:j + 1]
            diff = mean - hx_of(c0, tc, j)
            y = _dot(diff.astype(BF16), pw_ref[j]) * ps_ref[:, cs]
            x1_ref[0, c0:c0 + tc, cs] = x_ref[0, c0:c0 + tc, cs] + mod[2:3, cs] * y

    tail = min(t, 2 * tc)
    for c0 in range(0, t, tail):
        hb, lg = _ffn_pre(x1_ref[0, c0:c0 + tail, :], mod, nffn_ref[...], wr_ref[...])
        hx2_ref[0, c0:c0 + tail, :] = hb
        lg_ref[0, :, c0:c0 + tail] = lg


def _pool_consts(t, two_d):
    cm = np.zeros((len(POOL_WINDOWS), MXU_DIM, MXU_DIM), np.float32)
    cnt = np.zeros((t, len(POOL_WINDOWS)), np.float32)
    n = GRID_W if two_d else t
    pos = np.arange(MXU_DIM)
    for j, w in enumerate(POOL_WINDOWS):
        col = pos % n
        lo = np.clip(col - w // 2, 0, n)
        hi = np.clip(col + w - w // 2, 0, n)
        same = (pos[:, None] // n) == (pos[None, :] // n)
        cm[j] = (same & (col[None, :] >= lo[:, None]) & (col[None, :] < hi[:, None])).astype(np.float32)
        tt = np.arange(t)
        c = tt % n
        ccnt = np.clip(c + w - w // 2, 0, n) - np.clip(c - w // 2, 0, n)
        if two_d:
            rows = t // GRID_W
            r = tt // GRID_W
            rcnt = np.clip(r + w - w // 2, 0, rows) - np.clip(r - w // 2, 0, rows)
            cnt[:, j] = ccnt * rcnt
        else:
            cnt[:, j] = ccnt
    return jnp.asarray(cm, BF16), jnp.asarray(cnt)


def _pool_mixer(x, mod, nmix, nffn, pool_w, pool_scale, w_router, two_d):
    b, t, d = x.shape
    gd = d // N_POOL_GROUPS
    if not two_d:
        assert t == MXU_DIM
    cmat, cnt = _pool_consts(t, two_d)
    tc = min(t, 512)
    pad_tok = POOL_PAD_ROWS * GRID_W
    scratch = [pltpu.VMEM((t, 1), F32)]
    if two_d:
        scratch.append(pltpu.VMEM((t + 2 * pad_tok, gd), F32))
    e = w_router.shape[1]
    full = lambda shape: pl.BlockSpec(shape, lambda i: (0,) * len(shape))
    return pl.pallas_call(
        functools.partial(_pool_kernel, two_d=two_d, tc=tc),
        grid=(b,),
        in_specs=[pl.BlockSpec((1, t, d), lambda i: (i, 0, 0)),
                  pl.BlockSpec((1, 8, d), lambda i: (i, 0, 0)),
                  full((1, d)), full((1, d)), full(cmat.shape), full(cnt.shape),
                  full(pool_w.shape), full((1, d)), full(w_router.shape)],
        out_specs=[pl.BlockSpec((1, t, d), lambda i: (i, 0, 0)),
                   pl.BlockSpec((1, t, d), lambda i: (i, 0, 0)),
                   pl.BlockSpec((1, e, t), lambda i: (i, 0, 0))],
        out_shape=[jax.ShapeDtypeStruct((b, t, d), F32),
                   jax.ShapeDtypeStruct((b, t, d), BF16),
                   jax.ShapeDtypeStruct((b, e, t), F32)],
        scratch_shapes=scratch,
        compiler_params=_params(("arbitrary",), 60),
        name="pool_mixer_2d" if two_d else "pool_mixer_1d",
    )(x, mod, nmix, nffn, cmat, cnt, pool_w, pool_scale, w_router)


def _route_kernel(lg_ref, slot_ref, gate_ref, offs_ref, *, cap):
    lg = lg_ref[0]
    e, t = lg.shape
    ex = jnp.exp(lg - jnp.max(lg, axis=0, keepdims=True))
    aff = ex / jnp.sum(ex, axis=0, keepdims=True)
    capf = jnp.float32(cap)

    def as_f32(v):
        return lax.bitcast_convert_type(v, F32)

    def enough(cand):
        return jnp.sum(jnp.where(aff >= as_f32(cand), 1.0, 0.0), axis=1, keepdims=True) >= capf

    def bisect2(i, v):
        hi = jnp.left_shift(jnp.int32(1), 30 - 2 * i)
        lo = jnp.left_shift(jnp.int32(1), 29 - 2 * i)
        c_hi, c_lo, c_both = v | hi, v | lo, v | hi | lo
        return jnp.where(enough(c_both), c_both, jnp.where(enough(c_hi), c_hi, jnp.where(enough(c_lo), c_lo, v)))

    kth = lax.fori_loop(0, 15, bisect2, jnp.zeros((e, 1), jnp.int32))
    kth = jnp.where(enough(kth | 1), kth | 1, kth)
    above = as_f32(kth + 1)
    gt = aff >= above
    eq = (aff >= as_f32(kth)) & jnp.logical_not(gt)
    need = capf - jnp.sum(jnp.where(gt, 1.0, 0.0), axis=1, keepdims=True)

    blk = min(t, MXU_DIM)
    before = (_iota((blk, blk), 0) < _iota((blk, blk), 1)).astype(BF16)

    def excl_cumsum(mask):
        ones = jnp.where(mask, 1.0, 0.0)
        outs, run = [], jnp.zeros((e, 1), F32)
        for b0 in range(0, t, blk):
            mb = ones[:, b0:b0 + blk]
            outs.append(_dot(mb.astype(BF16), before) + run)
            run = run + jnp.sum(mb, axis=1, keepdims=True)
        return jnp.concatenate(outs, axis=1) if len(outs) > 1 else outs[0]

    sel = gt | (eq & (excl_cumsum(eq) < need))
    slot_ref[0] = jnp.where(sel, excl_cumsum(sel), -1.0)
    gate_ref[0] = aff
    nl = offs_ref.shape[2]
    starts = (_iota((t, nl), 0) < _iota((t, nl), 1) * TOKEN_BLOCK).astype(BF16)
    offs_ref[0] = _dot(jnp.where(sel, 1.0, 0.0).astype(BF16), starts)


def _route(logits, cap):
    b, e, t = logits.shape
    spec = pl.BlockSpec((1, e, t), lambda i: (i, 0, 0))
    ospec = pl.BlockSpec((1, e, LANES), lambda i: (i, 0, 0))
    return pl.pallas_call(
        functools.partial(_route_kernel, cap=cap),
        grid=(b,),
        in_specs=[spec],
        out_specs=[spec, spec, ospec],
        out_shape=[jax.ShapeDtypeStruct((b, e, t), F32)] * 2 + [jax.ShapeDtypeStruct((b, e, LANES), F32)],
        compiler_params=_params(("arbitrary",), 32),
        name="route",
    )(logits)


def _gather_kernel(h_ref, slot_ref, o_ref, *, cap):
    hx = h_ref[0]
    t = hx.shape[0]
    row = _iota((cap, t), 0).astype(F32)
    for i in range(o_ref.shape[0]):
        onehot = jnp.where(slot_ref[0, i:i + 1, :] == row, 1.0, 0.0).astype(BF16)
        o_ref[i] = _dot(onehot, hx).astype(BF16)


def _gather(hx2, slot, cap):
    b, t, d = hx2.shape
    e = slot.shape[1]
    eb = 8
    return pl.pallas_call(
        functools.partial(_gather_kernel, cap=cap),
        grid=(b, e // eb),
        in_specs=[pl.BlockSpec((1, t, d), lambda i, j: (i, 0, 0)),
                  pl.BlockSpec((1, eb, t), lambda i, j: (i, j, 0))],
        out_specs=pl.BlockSpec((eb, cap, d), lambda i, j: (j, i, 0)),
        out_shape=jax.ShapeDtypeStruct((e, b * cap, d), BF16),
        compiler_params=_params(("arbitrary", "arbitrary"), 48),
        name="gather",
    )(hx2, slot)


STATIC_DOTS = 2


def _pair_list(offs_ref, kb, n_exp, nblk, le_ref, lg_ref):
    b = pl.program_id(0)
    shift = SLOT_GROUP.bit_length() - 1
    max_groups = le_ref.shape[0] // (n_exp + 1)

    cnt = jnp.int32(0)
    for e in range(n_exp):
        o0 = offs_ref[b, e * (nblk + 1) + kb]
        o1 = offs_ref[b, e * (nblk + 1) + kb + 1]
        lo = jnp.right_shift(o0, shift)
        ng = jnp.where(o1 > o0, jnp.right_shift(o1 - 1, shift) - lo + 1, 0)
        for i in range(max_groups):
            le_ref[cnt + i] = e
            lg_ref[cnt + i] = lo + i
        cnt = cnt + ng
    return cnt


def _pair(le_ref, lg_ref, idx, cnt):
    valid = idx < cnt
    safe = jnp.minimum(idx, jnp.maximum(cnt - 1, 0))
    e_p = jnp.where(valid, le_ref[safe], 0)
    g_p = jnp.where(valid, lg_ref[safe], 0)
    base = jnp.where(valid, g_p * SLOT_GROUP, -2 * SLOT_GROUP).astype(F32)
    return e_p, pl.multiple_of(g_p * SLOT_GROUP, SLOT_GROUP), base


def _for_each_dot(cnt, body):
    for c in range(STATIC_DOTS):
        body(c, 0)
    lax.fori_loop(STATIC_DOTS, (cnt + PAIRS_PER_DOT - 1) // PAIRS_PER_DOT, body, 0)


def _sparse_gather_kernel(offs_ref, h_ref, slot_ref, o_ref, le_ref, lg_ref, p_ref, *, n_exp, nblk):
    sg = SLOT_GROUP
    o_ref[...] = jnp.zeros(o_ref.shape, o_ref.dtype)
    sub = _iota((sg, TOKEN_BLOCK), 0).astype(F32)

    def per_block(kb, _):
        cnt = _pair_list(offs_ref, kb, n_exp, nblk, le_ref, lg_ref)
        tok = pl.ds(pl.multiple_of(kb * TOKEN_BLOCK, TOKEN_BLOCK), TOKEN_BLOCK)

        def dot_batch(c, _):
            dst = []
            for p in range(PAIRS_PER_DOT):
                e_p, s0, base = _pair(le_ref, lg_ref, c * PAIRS_PER_DOT + p, cnt)
                row = slot_ref[0, kb, pl.ds(e_p, 1), :]
                p_ref[p * sg:(p + 1) * sg, :] = jnp.where(row == base + sub, 1.0, 0.0).astype(BF16)
                dst.append((e_p, s0))
            z = _dot(p_ref[...], h_ref[0, tok, :])
            for p, (e_p, s0) in enumerate(dst):
                o_ref[e_p, pl.ds(s0, sg), :] += z[p * sg:(p + 1) * sg, :].astype(o_ref.dtype)
            return 0

        _for_each_dot(cnt, dot_batch)
        return 0

    lax.fori_loop(0, nblk, per_block, 0)


def _by_token_block(a):
    b, e, t = a.shape
    return jnp.swapaxes(a.reshape(b, e, t // TOKEN_BLOCK, TOKEN_BLOCK), 1, 2)


def _sparse_gather(hx2, slot_blocks, offs, cap):
    b, t, d = hx2.shape
    _, nblk, e, _ = slot_blocks.shape
    return pl.pallas_call(
        functools.partial(_sparse_gather_kernel, n_exp=e, nblk=nblk),
        grid_spec=pltpu.PrefetchScalarGridSpec(
            num_scalar_prefetch=1,
            grid=(b,),
            in_specs=[pl.BlockSpec((1, t, d), lambda i, o: (i, 0, 0)),
                      pl.BlockSpec((1, nblk, e, TOKEN_BLOCK), lambda i, o: (i, 0, 0, 0))],
            out_specs=pl.BlockSpec((e, cap, d), lambda i, o: (0, i, 0)),
            scratch_shapes=[pltpu.SMEM(((e + 1) * (cap // SLOT_GROUP),), jnp.int32),
                            pltpu.SMEM(((e + 1) * (cap // SLOT_GROUP),), jnp.int32),
                            pltpu.VMEM((PAIRS_PER_DOT * SLOT_GROUP, TOKEN_BLOCK), BF16)]),
        out_shape=jax.ShapeDtypeStruct((e, b * cap, d), BF16),
        compiler_params=_params(("arbitrary",), 40),
        name="sparse_gather",
    )(offs, hx2, slot_blocks)


def _sparse_scatter_kernel(offs_ref, x_ref, y_ref, slot_ref, gate_ref, mod_ref, fn_ref, o_ref,
                           le_ref, lg_ref, p_ref, yc_ref, acc_ref, *, n_exp, nblk, bps, final):
    sg = SLOT_GROUP
    sub = _iota((sg, TOKEN_BLOCK), 0).astype(F32)

    def per_block(i, _):
        cnt = _pair_list(offs_ref, pl.program_id(1) * bps + i, n_exp, nblk, le_ref, lg_ref)
        tok = pl.ds(pl.multiple_of(i * TOKEN_BLOCK, TOKEN_BLOCK), TOKEN_BLOCK)
        acc_ref[...] = jnp.zeros(acc_ref.shape, F32)

        def dot_batch(c, _):
            for p in range(PAIRS_PER_DOT):
                e_p, s0, base = _pair(le_ref, lg_ref, c * PAIRS_PER_DOT + p, cnt)
                row = slot_ref[0, i, pl.ds(e_p, 1), :]
                gate = gate_ref[0, i, pl.ds(e_p, 1), :]
                p_ref[p * sg:(p + 1) * sg, :] = jnp.where(row == base + sub, gate, 0.0)
                yc_ref[p * sg:(p + 1) * sg, :] = y_ref[e_p, pl.ds(s0, sg), :]
            acc_ref[...] += _dot(p_ref[...].T.astype(BF16), yc_ref[...])
            return 0

        _for_each_dot(cnt, dot_batch)
        out = x_ref[0, tok, :] + mod_ref[0][5:6] * acc_ref[...]
        if final:
            out = out * lax.rsqrt(jnp.mean(out * out, axis=-1, keepdims=True) + EPS) * fn_ref[...]
        o_ref[0, tok, :] = out
        return 0

    lax.fori_loop(0, bps, per_block, 0)


def _sparse_scatter(x, y, slot_blocks, gate_blocks, offs, mod, final_norm, cap, final):
    b, t, d = x.shape
    _, nblk, e, _ = slot_blocks.shape
    bps = min(nblk, 4)
    rows = PAIRS_PER_DOT * SLOT_GROUP
    tok = lambda n: pl.BlockSpec((1, bps * TOKEN_BLOCK, n), lambda i, j, o: (i, j, 0))
    exp = pl.BlockSpec((1, bps, e, TOKEN_BLOCK), lambda i, j, o: (i, j, 0, 0))
    return pl.pallas_call(
        functools.partial(_sparse_scatter_kernel, n_exp=e, nblk=nblk, bps=bps, final=final),
        grid_spec=pltpu.PrefetchScalarGridSpec(
            num_scalar_prefetch=1,
            grid=(b, nblk // bps),
            in_specs=[tok(d), pl.BlockSpec((e, cap, d), lambda i, j, o: (0, i, 0)), exp, exp,
                      pl.BlockSpec((1, 8, d), lambda i, j, o: (i, 0, 0)),
                      pl.BlockSpec((1, d), lambda i, j, o: (0, 0))],
            out_specs=tok(d),
            scratch_shapes=[pltpu.SMEM(((e + 1) * (cap // SLOT_GROUP),), jnp.int32),
                            pltpu.SMEM(((e + 1) * (cap // SLOT_GROUP),), jnp.int32),
                            pltpu.VMEM((rows, TOKEN_BLOCK), F32),
                            pltpu.VMEM((rows, d), BF16),
                            pltpu.VMEM((TOKEN_BLOCK, d), F32)]),
        out_shape=jax.ShapeDtypeStruct((b, t, d), F32),
        compiler_params=_params(("arbitrary", "arbitrary"), 48),
        name="sparse_scatter",
    )(offs, x, y, slot_blocks, gate_blocks, mod, final_norm)


def _ffn_kernel(*refs, n_sets, mc):
    x_refs = refs[:n_sets]
    wg_ref, wu_ref, wd_ref = refs[n_sets:n_sets + 3]
    y_refs = refs[n_sets + 3:2 * n_sets + 3]
    acc_refs = refs[2 * n_sets + 3:3 * n_sets + 3]
    wgb, wub, wdb = refs[3 * n_sets + 3:]
    f = pl.program_id(1)
    wgb[...] = wg_ref[0, 0].astype(BF16)
    wub[...] = wu_ref[0, 0].astype(BF16)
    wdb[...] = wd_ref[0, 0].astype(BF16)
    last = f == pl.num_programs(1) - 1

    @pl.when((pl.program_id(0) == 0) & (f == 0))
    def _():
        for acc in acc_refs:
            acc[...] = jnp.zeros(acc.shape, F32)

    for x_ref, y_ref, acc in zip(x_refs, y_refs, acc_refs):
        m = x_ref.shape[1]
        step = min(m, mc)
        for m0 in range(0, m, step):
            xs = x_ref[0, m0:m0 + step, :]
            hg = _dot(xs, wgb[...])
            hu = _dot(xs, wub[...])
            hid = (hg * _sigmoid(hg) * hu).astype(BF16)
            tot = acc[m0:m0 + step, :] + _dot(hid, wdb[...])
            acc[m0:m0 + step, :] = jnp.where(last, 0.0, tot)
            y_ref[0, m0:m0 + step, :] = tot.astype(BF16)


def _expert_ffn(xs, layer, w_gate, w_up, w_down):
    _, e, d, hidden = w_gate.shape
    tf = 512
    n = len(xs)
    xspecs = [pl.BlockSpec((1, x.shape[1], d), lambda i, j: (i, 0, 0)) for x in xs]
    return pl.pallas_call(
        functools.partial(_ffn_kernel, n_sets=n, mc=1024),
        grid=(e, hidden // tf),
        in_specs=xspecs + [pl.BlockSpec((1, 1, d, tf), lambda i, j: (layer, i, 0, j)),
                           pl.BlockSpec((1, 1, d, tf), lambda i, j: (layer, i, 0, j)),
                           pl.BlockSpec((1, 1, tf, d), lambda i, j: (layer, i, j, 0))],
        out_specs=xspecs,
        out_shape=[jax.ShapeDtypeStruct(x.shape, BF16) for x in xs],
        scratch_shapes=[pltpu.VMEM((x.shape[1], d), F32) for x in xs]
        + [pltpu.VMEM((d, tf), BF16), pltpu.VMEM((d, tf), BF16), pltpu.VMEM((tf, d), BF16)],
        compiler_params=_params(("arbitrary", "arbitrary"), 60),
        name="expert_ffn",
    )(*xs, w_gate, w_up, w_down)


def _scatter_kernel(x_ref, y_ref, slot_ref, gate_ref, mod_ref, fn_ref, o_ref, *, final):
    tt = x_ref.shape[1]
    e, cap, _ = y_ref.shape
    eye = (_iota((tt, tt), 0) == _iota((tt, tt), 1)).astype(BF16)
    slot_t = _dot_nt(eye, (slot_ref[0] + 1.0).astype(BF16))
    gate_t = _dot_nt(eye, gate_ref[0].astype(BF16))
    lane = _iota((tt, cap), 1).astype(F32) + 1.0
    acc = None
    for i in range(e):
        pt = jnp.where(slot_t[:, i:i + 1] == lane, gate_t[:, i:i + 1], 0.0).astype(BF16)
        part = _dot(pt, y_ref[i])
        acc = part if acc is None else acc + part
    out = x_ref[0] + mod_ref[0][5:6] * acc
    if final:
        out = out * lax.rsqrt(jnp.mean(out * out, axis=-1, keepdims=True) + EPS) * fn_ref[...]
    o_ref[0] = out


def _scatter(x, y, slot, gate, mod, final_norm, cap, final):
    b, t, d = x.shape
    e = slot.shape[1]
    tt = min(t, 512)
    return pl.pallas_call(
        functools.partial(_scatter_kernel, final=final),
        grid=(b, t // tt),
        in_specs=[pl.BlockSpec((1, tt, d), lambda i, j: (i, j, 0)),
                  pl.BlockSpec((e, cap, d), lambda i, j: (0, i, 0)),
                  pl.BlockSpec((1, e, tt), lambda i, j: (i, 0, j)),
                  pl.BlockSpec((1, e, tt), lambda i, j: (i, 0, j)),
                  pl.BlockSpec((1, 8, d), lambda i, j: (i, 0, 0)),
                  pl.BlockSpec((1, d), lambda i, j: (0, 0))],
        out_specs=pl.BlockSpec((1, tt, d), lambda i, j: (i, j, 0)),
        out_shape=jax.ShapeDtypeStruct((b, t, d), F32),
        compiler_params=_params(("arbitrary", "arbitrary"), 48),
        name="scatter",
    )(x, y, slot, gate, mod, final_norm)


def _moe(streams, layer, w_gate, w_up, w_down, final_norm, final):
    routed = []
    for x1, hx2, logits, mod in streams:
        t = x1.shape[1]
        cap = CAPACITY_FACTOR * t // N_EXPERTS
        slot, gate, offs = _route(logits, cap)
        nblk = t // TOKEN_BLOCK
        if nblk > 1:
            offs = offs[:, :, :nblk + 1].astype(jnp.int32).reshape(offs.shape[0], -1)
            slot, gate = _by_token_block(slot), _by_token_block(gate)
            xg = _sparse_gather(hx2, slot, offs, cap)
        else:
            offs = None
            xg = _gather(hx2, slot, cap)
        routed.append((slot, gate, offs, cap, xg))
    ys = _expert_ffn([r[4] for r in routed], layer, w_gate, w_up, w_down)
    outs = []
    for (x1, _, _, mod), (slot, gate, offs, cap, _), y in zip(streams, routed, ys):
        if offs is None:
            outs.append(_scatter(x1, y, slot, gate, mod, final_norm, cap, final))
        else:
            outs.append(_sparse_scatter(x1, y, slot, gate, offs, mod, final_norm, cap, final))
    return outs


def _inproj_kernel(x_ref, mod_ref, nmix_ref, wko_ref, wqvt_ref, wg_ref, bg_ref,
                   qt_ref, k_ref, vt_ref, o_ref, gr_ref, *, chunk, qk, dk):
    tt = x_ref.shape[1]
    mod = mod_ref[0]
    h = _norm_mod(x_ref[0], nmix_ref[...], mod[0:1], mod[1:2])
    hb = h.astype(BF16)
    p = _dot(hb, wko_ref[...])
    k_ref[0] = (p[:, :qk] * (dk ** -0.5)).astype(BF16)
    o_ref[0] = _sigmoid(p[:, qk:]).astype(o_ref.dtype)
    p_t = _dot_nt(wqvt_ref[...], hb)
    for ci in range(tt // chunk):
        qt_ref[0, ci] = p_t[:qk, ci * chunk:(ci + 1) * chunk].astype(BF16)
        vt_ref[0, ci] = p_t[qk:, ci * chunk:(ci + 1) * chunk].astype(BF16)

    nh = MLSTM_HEADS
    g_t = _transpose_exact(_dot_split(h, wg_ref[...]) + bg_ref[...])
    row = _iota((4 * nh, chunk), 0)
    is_f = (row & nh) == nh
    a = _iota((chunk, chunk), 0)
    c = _iota((chunk, chunk), 1)
    tri = jnp.concatenate([(a <= c).astype(BF16), (a >= c).astype(BF16)], axis=1)
    ng = 4 * nh
    for c0 in range(0, tt, chunk):
        gc = g_t[:, c0:c0 + chunk]
        pieces = _split_bf16(jnp.where(is_f, _log_sigmoid(gc), 0.0), 3)
        cum = _dot(jnp.concatenate(pieces, axis=0), tri)
        cum = cum[:ng] + (cum[ng:2 * ng] + cum[2 * ng:])
        gr_ref[0, :, c0:c0 + chunk] = jnp.where(is_f, jnp.where(row < 2 * nh, cum[:, :chunk], cum[:, chunk:]), gc)


def _inproj(x, mod, nmix, w_ko, w_qvt, w_g, b_g, qk, vd):
    b, t, d = x.shape
    tt = min(t, 1024)
    ng = w_g.shape[1]
    ln = SCAN_CHUNK
    full = lambda shape: pl.BlockSpec(shape, lambda i, j: (0,) * len(shape))
    tok = lambda n: pl.BlockSpec((1, tt, n), lambda i, j: (i, j, 0))
    slab = lambda n: pl.BlockSpec((1, tt // ln, n, ln), lambda i, j: (i, j, 0, 0))
    return pl.pallas_call(
        functools.partial(_inproj_kernel, chunk=ln, qk=qk, dk=qk // MLSTM_HEADS),
        grid=(b, t // tt),
        in_specs=[tok(d), pl.BlockSpec((1, 8, d), lambda i, j: (i, 0, 0)), full((1, d)),
                  full(w_ko.shape), full(w_qvt.shape), full(w_g.shape), full(b_g.shape)],
        out_specs=[slab(qk), tok(qk), slab(vd), tok(vd),
                   pl.BlockSpec((1, ng, tt), lambda i, j: (i, 0, j))],
        out_shape=[jax.ShapeDtypeStruct((b, t // ln, qk, ln), BF16), jax.ShapeDtypeStruct((b, t, qk), BF16),
                   jax.ShapeDtypeStruct((b, t // ln, vd, ln), BF16), jax.ShapeDtypeStruct((b, t, vd), BF16),
                   jax.ShapeDtypeStruct((b, ng, t), F32)],
        compiler_params=_params(("arbitrary", "arbitrary"), 56),
        name="mlstm_inproj",
    )(x, mod, nmix, w_ko, w_qvt, w_g, b_g)


def _scan_kernel(kc_ref, vc_ref, grc_ref, qx_ref, kx_ref, vx_ref, grx_ref,
                 out_ref, s_ref, sall_ref, mall_ref, *, chunk):
    ln = chunk
    nh = MLSTM_HEADS
    pair = pl.program_id(1)
    kl = kx_ref.shape[2]
    dv = vx_ref.shape[2] // 2
    lane = _iota((ln, kl), 1)
    kmask = (lane < kl // 2, lane >= kl // 2)
    klane = _iota((2 * dv, kl), 1)
    kcols = (klane < kl // 2, klane >= kl // 2)
    ones = jnp.ones((dv, ln), BF16)
    si = _iota((ln, ln), 0)
    ji = _iota((ln, ln), 1)
    causal = (si <= ji, si >= ji)
    ncc = kc_ref.shape[1] // ln
    ncx = kx_ref.shape[1] // ln

    def gate_rows(gr_ref, hh, dirn, c):
        base = 2 * nh * dirn + 2 * pair + hh
        return gr_ref[0, base, pl.ds(c, 1), :], gr_ref[0, base + nh, pl.ds(c, 1), :]

    def keys(k_ref, c):
        k = k_ref[0, pl.ds(pl.multiple_of(c * ln, ln), ln), :]
        return [jnp.where(kmask[hh], k, jnp.zeros((), BF16)) for hh in range(2)]

    def values_t(v_ref, c, hh):
        return jnp.concatenate([v_ref[0, c, hh * dv:(hh + 1) * dv, :], ones], axis=0)

    def advance(refs, n, record):
        k_ref, v_ref, gr_ref = refs

        def body(i, ms):
            new_ms = [None] * 4
            cs = (i, n - 1 - i)
            for dirn in range(2):
                c = cs[dirn]
                vws, decays = [], []
                for hh in range(2):
                    ch = 2 * hh + dirn
                    ig, bc = gate_rows(gr_ref, hh, dirn, c)
                    m = ms[ch]
                    if record:
                        s_own = jnp.where(kcols[hh], s_ref[ch], 0.0)
                        sall_ref[ch * ncx + c] = s_own.astype(BF16)
                        mall_ref[ch * ncx + c] = jnp.broadcast_to(m, (8, ln))
                    b_end = bc[:, ln - 1:ln] if dirn == 0 else bc[:, 0:1]
                    gl = b_end - bc + ig
                    m_new = jnp.maximum(b_end + m, jnp.max(gl, axis=1, keepdims=True))
                    vws.append((values_t(v_ref, c, hh) * jnp.exp(gl - m_new)).astype(BF16))
                    decays.append(jnp.exp(b_end + m - m_new))
                    new_ms[ch] = m_new
                upd = _dot(jnp.concatenate(vws, axis=0), k_ref[0, pl.ds(pl.multiple_of(c * ln, ln), ln), :])
                for hh in range(2):
                    ch = 2 * hh + dirn
                    s_ref[ch] = decays[hh] * s_ref[ch] + upd[hh * 2 * dv:(hh + 1) * 2 * dv, :]
            return tuple(new_ms)

        return body

    s_ref[...] = jnp.zeros(s_ref.shape, F32)
    ms = tuple(jnp.zeros((1, 1), F32) for _ in range(4))
    ms = lax.fori_loop(0, ncc, advance((kc_ref, vc_ref, grc_ref), ncc, False), ms, unroll=min(ncc, SCAN_UNROLL))
    lax.fori_loop(0, ncx, advance((kx_ref, vx_ref, grx_ref), ncx, True), ms, unroll=min(ncx, SCAN_UNROLL))

    def emit(c, _):
        q_t = qx_ref[0, c]
        q_tf = q_t.astype(F32)
        s_pair = _dot(jnp.concatenate(keys(kx_ref, c), axis=0), q_t)
        for hh in range(2):
            s_kq = s_pair[hh * ln:(hh + 1) * ln, :]
            v_t = values_t(vx_ref, c, hh)
            hsum = None
            for dirn in range(2):
                ch = 2 * hh + dirn
                ig, bc = gate_rows(grx_ref, hh, dirn, c)
                u_col = jnp.broadcast_to(ig - bc, (ln, ln)).T
                a = bc + mall_ref[ch * ncx + c][0:1, :]
                dm = jnp.where(causal[dirn], u_col + bc, -jnp.inf)
                mj = jnp.maximum(a, jnp.max(dm, axis=0, keepdims=True))
                sm = (s_kq * jnp.exp(dm - mj)).astype(BF16)
                qw = (q_tf * jnp.exp(a - mj)).astype(BF16)
                num = _dot(jnp.concatenate([v_t, sall_ref[ch * ncx + c]], axis=1),
                           jnp.concatenate([sm, qw], axis=0))
                h = num[:dv, :] * (1.0 / jnp.maximum(jnp.abs(num[dv:dv + 1, :]), jnp.exp(-mj)))
                hsum = h if hsum is None else hsum + h
            hn = hsum * lax.rsqrt(jnp.mean(hsum * hsum, axis=0, keepdims=True) + EPS)
            out_ref[0, pl.ds(pl.multiple_of(c * ln, ln), ln), hh * dv:(hh + 1) * dv] = hn.T.astype(out_ref.dtype)
        return 0

    lax.fori_loop(0, ncx, emit, 0, unroll=min(ncx, SCAN_UNROLL))


def _scan(ctx_parts, lat_parts):
    kc, vc, grc = ctx_parts
    qx, kx, vx, grx = lat_parts
    b, t, qk = kx.shape
    vd = vx.shape[2]
    nh = MLSTM_HEADS
    ln = SCAN_CHUNK
    dv = vd // nh
    kl = 2 * (qk // nh)
    assert kl == ln and dv == ln, "scan kernel assumes key-pair lanes = head value dim = chunk"
    ncx = t // ln

    def specs(k, v, gr):
        tk = k.shape[1]
        return [pl.BlockSpec((1, tk, kl), lambda i, j: (i, 0, j)),
                pl.BlockSpec((1, tk // ln, 2 * dv, ln), lambda i, j: (i, 0, j, 0)),
                pl.BlockSpec((1,) + gr.shape[1:], lambda i, j: (i, 0, 0, 0))]

    return pl.pallas_call(
        functools.partial(_scan_kernel, chunk=ln),
        grid=(b, nh // 2),
        in_specs=specs(kc, vc, grc) + [pl.BlockSpec((1, ncx, kl, ln), lambda i, j: (i, 0, j, 0))] + specs(kx, vx, grx),
        out_specs=pl.BlockSpec((1, t, 2 * dv), lambda i, j: (i, 0, j)),
        out_shape=jax.ShapeDtypeStruct((b, t, vd), BF16),
        scratch_shapes=[pltpu.VMEM((4, 2 * dv, kl), F32),
                        pltpu.VMEM((4 * ncx, 2 * dv, kl), BF16),
                        pltpu.VMEM((4 * ncx, 8, ln), F32)],
        compiler_params=_params(("arbitrary", "arbitrary"), 40),
        name="mlstm_scan",
    )(kc, vc, grc, qx, kx, vx, grx)


def _mlstm_out_kernel(hn_ref, og_ref, x_ref, mod_ref, mn_ref, wout_ref, nffn_ref, wr_ref,
                      x1_ref, hx2_ref, lg_ref):
    mod = mod_ref[0]
    a = (hn_ref[0].astype(F32) * mn_ref[...]) * og_ref[0].astype(F32)
    xn = x_ref[0] + mod[2:3] * _dot(a.astype(BF16), wout_ref[...])
    x1_ref[0] = xn
    hb, lg = _ffn_pre(xn, mod, nffn_ref[...], wr_ref[...])
    hx2_ref[0] = hb
    lg_ref[0] = lg


def _mlstm_out(hn, og, x, mod, mnorm, w_out, nffn, w_router):
    b, t, d = x.shape
    vd = hn.shape[2]
    e = w_router.shape[1]
    tt = min(t, 1024)
    full = lambda shape: pl.BlockSpec(shape, lambda i, j: (0,) * len(shape))
    tok = lambda n: pl.BlockSpec((1, tt, n), lambda i, j: (i, j, 0))
    return pl.pallas_call(
        _mlstm_out_kernel,
        grid=(b, t // tt),
        in_specs=[tok(vd), tok(vd), tok(d), pl.BlockSpec((1, 8, d), lambda i, j: (i, 0, 0)),
                  full((1, vd)), full(w_out.shape), full((1, d)), full(w_router.shape)],
        out_specs=[tok(d), tok(d), pl.BlockSpec((1, e, tt), lambda i, j: (i, 0, j))],
        out_shape=[jax.ShapeDtypeStruct((b, t, d), F32), jax.ShapeDtypeStruct((b, t, d), BF16),
                   jax.ShapeDtypeStruct((b, e, t), F32)],
        compiler_params=_params(("arbitrary", "arbitrary"), 48),
        name="mlstm_out",
    )(hn, og, x, mod, mnorm, w_out, nffn, w_router)


def _chunked_rows(gr):
    b, n, t = gr.shape
    return gr.reshape(b, n, t // SCAN_CHUNK, SCAN_CHUNK)


def kernel(x, c, ctx, c_ctx, ada_w, ada_b, norm_mix, norm_ffn, pool_w, pool_scale, mlstm_w_in, mlstm_b_gates,
           mlstm_norm, mlstm_w_out, moe_router, moe_w_gate, moe_w_up, moe_w_down, final_norm):
    bsz, seq, d = x.shape
    depth = ada_w.shape[0]
    n_mixers = 2

    cc = jnp.concatenate([c, c_ctx[None, :], jnp.zeros((16 - bsz - 1, d), F32)], axis=0)
    ada = _ada(cc, ada_w, ada_b).reshape(depth, 16, N_ADA, d)
    pad = jnp.zeros((bsz, 8 - N_ADA, d), F32)

    fn = final_norm.reshape(1, d)
    for i in range(depth):
        last = i == depth - 1
        j = i // n_mixers
        mod_x = jnp.concatenate([ada[i, :bsz], pad], axis=1)
        mod_c = jnp.concatenate([jnp.broadcast_to(ada[i, bsz][None], (bsz, N_ADA, d)), pad], axis=1)
        nmix = norm_mix[i].reshape(1, d)
        nffn = norm_ffn[i].reshape(1, d)
        w_router = moe_router[i]
        streams = []
        if i % n_mixers == 0:
            pw = pool_w[j].astype(BF16)
            ps = pool_scale[j].reshape(1, d)
            x1, hx2, lg = _pool_mixer(x, mod_x, nmix, nffn, pw, ps, w_router, True)
            streams.append((x1, hx2, lg, mod_x))
            if not last:
                c1, hc2, lgc = _pool_mixer(ctx, mod_c, nmix, nffn, pw, ps, w_router, False)
                streams.append((c1, hc2, lgc, mod_c))
        else:
            qk = mlstm_w_in.shape[2] - 2 * mlstm_w_out.shape[1] - 4 * MLSTM_HEADS
            qk //= 2
            vd = mlstm_w_out.shape[1]
            w_in = mlstm_w_in[j]
            w_ko = jnp.concatenate([w_in[:, qk:2 * qk], w_in[:, 2 * qk + vd:2 * qk + 2 * vd]], axis=1).astype(BF16)
            w_qvt = jnp.concatenate([w_in[:, :qk], w_in[:, 2 * qk:2 * qk + vd]], axis=1).T.astype(BF16)
            w_g = w_in[:, 2 * qk + 2 * vd:]
            b_g = mlstm_b_gates[j].reshape(1, -1)
            proj = lambda s, m: _inproj(s, m, nmix, w_ko, w_qvt, w_g, b_g, qk, vd)
            _, kc, vc, _, grc = proj(ctx, mod_c)
            qx, kx, vx, ox, grx = proj(x, mod_x)
            hn = _scan((kc, vc, _chunked_rows(grc)), (qx, kx, vx, _chunked_rows(grx)))
            x1, hx2, lg = _mlstm_out(hn, ox, x, mod_x, mlstm_norm[j].reshape(1, vd),
                                     mlstm_w_out[j].astype(BF16), nffn, w_router)
            streams.append((x1, hx2, lg, mod_x))
            assert last, "context output of the mLSTM mixer is only needed by a following layer"
        outs = _moe(streams, i, moe_w_gate, moe_w_up, moe_w_down, fn, last)
        x = outs[0]
        if not last:
            ctx = outs[1]
    return x
```

```python
import functools

import jax
import jax.numpy as jnp
import numpy as np
from jax import lax
from jax.experimental import pallas as pl
from jax.experimental.pallas import tpu as pltpu

F32 = jnp.float32
BF16 = jnp.bfloat16
HIGHEST = lax.Precision.HIGHEST

GRID_W = 64
EPS = 1e-6
N_ADA = 6
POOL_WINDOWS = (2, 4, 8, 16)
N_POOL_GROUPS = 4
MLSTM_HEADS = 8
N_EXPERTS = 16
CAPACITY_FACTOR = 2

MIB = 1024 * 1024
MXU_DIM = 256
LANES = 128
TOKEN_BLOCK = 256
SLOT_GROUP = 32
PAIRS_PER_DOT = 16
SCAN_CHUNK = 128
SCAN_UNROLL = 8
POOL_PAD_ROWS = max(POOL_WINDOWS) // 2


def _params(sem, vmem_mib):
    return pltpu.CompilerParams(dimension_semantics=sem, vmem_limit_bytes=vmem_mib * MIB)


def _dot(a, b, precision=None):
    return jnp.dot(a, b, preferred_element_type=F32, precision=precision)


def _dot_nt(a, b, precision=None):
    return lax.dot_general(a, b, (((1,), (1,)), ((), ())), preferred_element_type=F32, precision=precision)


def _iota(shape, dim, dtype=jnp.int32):
    return lax.broadcasted_iota(dtype, shape, dim)


def _sigmoid(x):
    return 1.0 / (1.0 + jnp.exp(-x))


def _log_sigmoid(x):
    return jnp.minimum(x, 0.0) - jnp.log1p(jnp.exp(-jnp.abs(x)))


def _norm_mod(x, g, shift, scale):
    inv = lax.rsqrt(jnp.mean(x * x, axis=-1, keepdims=True) + EPS)
    return (x * inv) * (g * (1.0 + scale)) + shift


def _ada_kernel(c_ref, w_ref, b_ref, o_ref):
    c = c_ref[...]
    s = c * _sigmoid(c)
    s_hi, s_lo = _split_bf16(s, 2)
    w_hi, w_lo = _split_bf16(w_ref[0], 2)
    o_ref[0] = _dot(s_hi, w_hi) + (_dot(s_hi, w_lo) + _dot(s_lo, w_hi)) + b_ref[0]


def _ada(cc, ada_w, ada_b):
    depth, d, n = ada_w.shape
    rows = cc.shape[0]
    tn = n // 4
    return pl.pallas_call(
        _ada_kernel,
        grid=(depth, n // tn),
        in_specs=[pl.BlockSpec((rows, d), lambda i, j: (0, 0)),
                  pl.BlockSpec((1, d, tn), lambda i, j: (i, 0, j)),
                  pl.BlockSpec((1, 1, tn), lambda i, j: (i, 0, j))],
        out_specs=pl.BlockSpec((1, rows, tn), lambda i, j: (i, 0, j)),
        out_shape=jax.ShapeDtypeStruct((depth, rows, n), F32),
        compiler_params=_params(("arbitrary", "arbitrary"), 40),
        name="ada",
    )(cc, ada_w, ada_b.reshape(depth, 1, n))


def _split_bf16(x, pieces):
    out = []
    for _ in range(pieces):
        p = x.astype(BF16)
        out.append(p)
        x = x - p.astype(F32)
    return out


def _dot_split(a, b):
    n = b.shape[1]
    a_hi, a_lo = _split_bf16(a, 2)
    b_hi, b_lo = _split_bf16(b, 2)
    hi = _dot(a_hi, jnp.concatenate([b_hi, b_lo], axis=1))
    return hi[:, :n] + (hi[:, n:] + _dot(a_lo, b_hi))


def _transpose_exact(x):
    m = x.shape[1]
    eye = (_iota((m, m), 0) == _iota((m, m), 1)).astype(BF16)
    hi, mid, lo = _split_bf16(x, 3)
    return _dot_nt(eye, hi) + (_dot_nt(eye, mid) + _dot_nt(eye, lo))


def _ffn_pre(xn, mod, nffn, wr):
    h2 = _norm_mod(xn, nffn, mod[3:4], mod[4:5])
    return h2.astype(BF16), _transpose_exact(_dot_split(h2, wr))


def _pool_kernel(x_ref, mod_ref, nmix_ref, nffn_ref, cmat_ref, cnt_ref, pw_ref, ps_ref, wr_ref,
                 x1_ref, hx2_ref, lg_ref, *pad, two_d, tc):
    t, d = x_ref.shape[1], x_ref.shape[2]
    gd = d // N_POOL_GROUPS
    mod = mod_ref[0]
    pad_tok = POOL_PAD_ROWS * GRID_W

    for c0 in range(0, t, tc):
        x1_ref[0, c0:c0 + tc, :] = _norm_mod(x_ref[0, c0:c0 + tc, :], nmix_ref[...], mod[0:1], mod[1:2])

    def hx_of(r0, rn, j):
        return x1_ref[0, r0:r0 + rn, j * gd:(j + 1) * gd]

    if two_d:
        pad_ref, = pad
        pad_ref[0:pad_tok, :] = jnp.zeros((pad_tok, gd), F32)
        pad_ref[pad_tok + t:pad_tok + t + pad_tok, :] = jnp.zeros((pad_tok, gd), F32)

    for j, w in enumerate(POOL_WINDOWS):
        cs = slice(j * gd, (j + 1) * gd)
        cm = cmat_ref[j]
        sums = []
        for b0 in range(0, t, MXU_DIM):
            g = hx_of(b0, MXU_DIM, j)
            g_hi = g.astype(BF16)
            g_lo = (g - g_hi.astype(F32)).astype(BF16)
            csum = _dot(cm, g_hi) + _dot(cm, g_lo)
            if two_d:
                pad_ref[pad_tok + b0:pad_tok + b0 + MXU_DIM, :] = csum
            else:
                sums.append(csum)
        for c0 in range(0, t, tc):
            if two_d:
                tot = None
                for dr in range(-(w // 2), w - w // 2):
                    o = pad_tok + c0 + dr * GRID_W
                    sl = pad_ref[o:o + tc, :]
                    tot = sl if tot is None else tot + sl
            else:
                blocks = sums[c0 // MXU_DIM:(c0 + tc) // MXU_DIM]
                tot = blocks[0] if len(blocks) == 1 else jnp.concatenate(blocks, axis=0)
            mean = tot / cnt_ref[c0:c0 + tc, j:j + 1]
            diff = mean - hx_of(c0, tc, j)
            y = _dot(diff.astype(BF16), pw_ref[j]) * ps_ref[:, cs]
            x1_ref[0, c0:c0 + tc, cs] = x_ref[0, c0:c0 + tc, cs] + mod[2:3, cs] * y

    tail = min(t, 2 * tc)
    for c0 in range(0, t, tail):
        hb, lg = _ffn_pre(x1_ref[0, c0:c0 + tail, :], mod, nffn_ref[...], wr_ref[...])
        hx2_ref[0, c0:c0 + tail, :] = hb
        lg_ref[0, :, c0:c0 + tail] = lg


def _pool_consts(t, seq, two_d):
    cm = np.zeros((len(POOL_WINDOWS), MXU_DIM, MXU_DIM), np.float32)
    cnt = np.zeros((t, len(POOL_WINDOWS)), np.float32)
    n = seq
    assert MXU_DIM % n == 0
    pos = np.arange(MXU_DIM)
    for j, w in enumerate(POOL_WINDOWS):
        col = pos % n
        lo = np.clip(col - w // 2, 0, n)
        hi = np.clip(col + w - w // 2, 0, n)
        same = (pos[:, None] // n) == (pos[None, :] // n)
        cm[j] = (same & (col[None, :] >= lo[:, None]) & (col[None, :] < hi[:, None])).astype(np.float32)
        tt = np.arange(t)
        c = tt % n
        ccnt = np.clip(c + w - w // 2, 0, n) - np.clip(c - w // 2, 0, n)
        if two_d:
            rows = t // n
            r = tt // n
            rcnt = np.clip(r + w - w // 2, 0, rows) - np.clip(r - w // 2, 0, rows)
            cnt[:, j] = ccnt * rcnt
        else:
            cnt[:, j] = ccnt
    return jnp.asarray(cm, BF16), jnp.asarray(cnt)


def _pool_mixer(x, mod, nmix, nffn, pool_w, pool_scale, w_router, seq, two_d):
    b, t, d = x.shape
    gd = d // N_POOL_GROUPS
    assert seq == GRID_W or not two_d, "row-window shifts assume GRID_W tokens per grid row"
    cmat, cnt = _pool_consts(t, seq, two_d)
    tc = min(t, 512)
    pad_tok = POOL_PAD_ROWS * GRID_W
    scratch = []
    if two_d:
        scratch.append(pltpu.VMEM((t + 2 * pad_tok, gd), F32))
    e = w_router.shape[1]
    full = lambda shape: pl.BlockSpec(shape, lambda i: (0,) * len(shape))
    return pl.pallas_call(
        functools.partial(_pool_kernel, two_d=two_d, tc=tc),
        grid=(b,),
        in_specs=[pl.BlockSpec((1, t, d), lambda i: (i, 0, 0)),
                  pl.BlockSpec((1, 8, d), lambda i: (i, 0, 0)),
                  full((1, d)), full((1, d)), full(cmat.shape), full(cnt.shape),
                  full(pool_w.shape), full((1, d)), full(w_router.shape)],
        out_specs=[pl.BlockSpec((1, t, d), lambda i: (i, 0, 0)),
                   pl.BlockSpec((1, t, d), lambda i: (i, 0, 0)),
                   pl.BlockSpec((1, e, t), lambda i: (i, 0, 0))],
        out_shape=[jax.ShapeDtypeStruct((b, t, d), F32),
                   jax.ShapeDtypeStruct((b, t, d), BF16),
                   jax.ShapeDtypeStruct((b, e, t), F32)],
        scratch_shapes=scratch,
        compiler_params=_params(("arbitrary",), 60),
        name="pool_mixer_2d" if two_d else "pool_mixer_1d",
    )(x, mod, nmix, nffn, cmat, cnt, pool_w, pool_scale, w_router)


def _route_kernel(lg_ref, slot_ref, gate_ref, offs_ref, *, cap):
    lg = lg_ref[0]
    e, t = lg.shape
    ex = jnp.exp(lg - jnp.max(lg, axis=0, keepdims=True))
    aff = ex / jnp.sum(ex, axis=0, keepdims=True)
    capf = jnp.float32(cap)

    def as_f32(v):
        return lax.bitcast_convert_type(v, F32)

    def enough(cand):
        return jnp.sum(jnp.where(aff >= as_f32(cand), 1.0, 0.0), axis=1, keepdims=True) >= capf

    def bisect2(i, v):
        hi = jnp.left_shift(jnp.int32(1), 30 - 2 * i)
        lo = jnp.left_shift(jnp.int32(1), 29 - 2 * i)
        c_hi, c_lo, c_both = v | hi, v | lo, v | hi | lo
        return jnp.where(enough(c_both), c_both, jnp.where(enough(c_hi), c_hi, jnp.where(enough(c_lo), c_lo, v)))

    kth = lax.fori_loop(0, 15, bisect2, jnp.zeros((e, 1), jnp.int32))
    kth = jnp.where(enough(kth | 1), kth | 1, kth)
    above = as_f32(kth + 1)
    gt = aff >= above
    eq = (aff >= as_f32(kth)) & jnp.logical_not(gt)
    need = capf - jnp.sum(jnp.where(gt, 1.0, 0.0), axis=1, keepdims=True)

    blk = min(t, MXU_DIM)
    before = (_iota((blk, blk), 0) < _iota((blk, blk), 1)).astype(BF16)

    def excl_cumsum(mask):
        ones = jnp.where(mask, 1.0, 0.0)
        outs, run = [], jnp.zeros((e, 1), F32)
        for b0 in range(0, t, blk):
            mb = ones[:, b0:b0 + blk]
            outs.append(_dot(mb.astype(BF16), before) + run)
            run = run + jnp.sum(mb, axis=1, keepdims=True)
        return jnp.concatenate(outs, axis=1) if len(outs) > 1 else outs[0]

    sel = gt | (eq & (excl_cumsum(eq) < need))
    slot_ref[0] = jnp.where(sel, excl_cumsum(sel), -1.0)
    gate_ref[0] = aff
    nl = offs_ref.shape[2]
    starts = (_iota((t, nl), 0) < _iota((t, nl), 1) * TOKEN_BLOCK).astype(BF16)
    offs_ref[0] = _dot(jnp.where(sel, 1.0, 0.0).astype(BF16), starts)


def _route(logits, cap):
    b, e, t = logits.shape
    spec = pl.BlockSpec((1, e, t), lambda i: (i, 0, 0))
    ospec = pl.BlockSpec((1, e, LANES), lambda i: (i, 0, 0))
    return pl.pallas_call(
        functools.partial(_route_kernel, cap=cap),
        grid=(b,),
        in_specs=[spec],
        out_specs=[spec, spec, ospec],
        out_shape=[jax.ShapeDtypeStruct((b, e, t), F32)] * 2 + [jax.ShapeDtypeStruct((b, e, LANES), F32)],
        compiler_params=_params(("arbitrary",), 32),
        name="route",
    )(logits)


def _gather_kernel(h_ref, slot_ref, o_ref, *, cap):
    hx = h_ref[0]
    t = hx.shape[0]
    row = _iota((cap, t), 0).astype(F32)
    for i in range(o_ref.shape[0]):
        onehot = jnp.where(slot_ref[0, i:i + 1, :] == row, 1.0, 0.0).astype(BF16)
        o_ref[i] = _dot(onehot, hx).astype(BF16)


def _gather(hx2, slot, cap):
    b, t, d = hx2.shape
    e = slot.shape[1]
    eb = 8
    return pl.pallas_call(
        functools.partial(_gather_kernel, cap=cap),
        grid=(b, e // eb),
        in_specs=[pl.BlockSpec((1, t, d), lambda i, j: (i, 0, 0)),
                  pl.BlockSpec((1, eb, t), lambda i, j: (i, j, 0))],
        out_specs=pl.BlockSpec((eb, cap, d), lambda i, j: (j, i, 0)),
        out_shape=jax.ShapeDtypeStruct((e, b * cap, d), BF16),
        compiler_params=_params(("arbitrary", "arbitrary"), 48),
        name="gather",
    )(hx2, slot)


STATIC_DOTS = 2


def _pair_list(offs_ref, kb, n_exp, nblk, le_ref, lg_ref):
    b = pl.program_id(0)
    shift = SLOT_GROUP.bit_length() - 1
    max_groups = le_ref.shape[0] // (n_exp + 1)

    cnt = jnp.int32(0)
    for e in range(n_exp):
        o0 = offs_ref[b, e * (nblk + 1) + kb]
        o1 = offs_ref[b, e * (nblk + 1) + kb + 1]
        lo = jnp.right_shift(o0, shift)
        ng = jnp.where(o1 > o0, jnp.right_shift(o1 - 1, shift) - lo + 1, 0)
        for i in range(max_groups):
            le_ref[cnt + i] = e
            lg_ref[cnt + i] = lo + i
        cnt = cnt + ng
    return cnt


def _pair(le_ref, lg_ref, idx, cnt):
    valid = idx < cnt
    safe = jnp.minimum(idx, jnp.maximum(cnt - 1, 0))
    e_p = jnp.where(valid, le_ref[safe], 0)
    g_p = jnp.where(valid, lg_ref[safe], 0)
    base = jnp.where(valid, g_p * SLOT_GROUP, -2 * SLOT_GROUP).astype(F32)
    return e_p, pl.multiple_of(g_p * SLOT_GROUP, SLOT_GROUP), base


def _for_each_dot(cnt, body):
    for c in range(STATIC_DOTS):
        body(c, 0)
    lax.fori_loop(STATIC_DOTS, (cnt + PAIRS_PER_DOT - 1) // PAIRS_PER_DOT, body, 0)


def _sparse_gather_kernel(offs_ref, h_ref, slot_ref, o_ref, le_ref, lg_ref, p_ref, *, n_exp, nblk):
    sg = SLOT_GROUP
    o_ref[...] = jnp.zeros(o_ref.shape, o_ref.dtype)
    sub = _iota((sg, TOKEN_BLOCK), 0).astype(F32)

    def per_block(kb, _):
        cnt = _pair_list(offs_ref, kb, n_exp, nblk, le_ref, lg_ref)
        tok = pl.ds(pl.multiple_of(kb * TOKEN_BLOCK, TOKEN_BLOCK), TOKEN_BLOCK)

        def dot_batch(c, _):
            dst = []
            for p in range(PAIRS_PER_DOT):
                e_p, s0, base = _pair(le_ref, lg_ref, c * PAIRS_PER_DOT + p, cnt)
                row = slot_ref[0, kb, pl.ds(e_p, 1), :]
                p_ref[p * sg:(p + 1) * sg, :] = jnp.where(row == base + sub, 1.0, 0.0).astype(BF16)
                dst.append((e_p, s0))
            z = _dot(p_ref[...], h_ref[0, tok, :])
            for p, (e_p, s0) in enumerate(dst):
                o_ref[e_p, pl.ds(s0, sg), :] += z[p * sg:(p + 1) * sg, :].astype(o_ref.dtype)
            return 0

        _for_each_dot(cnt, dot_batch)
        return 0

    lax.fori_loop(0, nblk, per_block, 0)


def _by_token_block(a):
    b, e, t = a.shape
    return jnp.swapaxes(a.reshape(b, e, t // TOKEN_BLOCK, TOKEN_BLOCK), 1, 2)


def _sparse_gather(hx2, slot_blocks, offs, cap):
    b, t, d = hx2.shape
    _, nblk, e, _ = slot_blocks.shape
    return pl.pallas_call(
        functools.partial(_sparse_gather_kernel, n_exp=e, nblk=nblk),
        grid_spec=pltpu.PrefetchScalarGridSpec(
            num_scalar_prefetch=1,
            grid=(b,),
            in_specs=[pl.BlockSpec((1, t, d), lambda i, o: (i, 0, 0)),
                      pl.BlockSpec((1, nblk, e, TOKEN_BLOCK), lambda i, o: (i, 0, 0, 0))],
            out_specs=pl.BlockSpec((e, cap, d), lambda i, o: (0, i, 0)),
            scratch_shapes=[pltpu.SMEM(((e + 1) * (cap // SLOT_GROUP),), jnp.int32),
                            pltpu.SMEM(((e + 1) * (cap // SLOT_GROUP),), jnp.int32),
                            pltpu.VMEM((PAIRS_PER_DOT * SLOT_GROUP, TOKEN_BLOCK), BF16)]),
        out_shape=jax.ShapeDtypeStruct((e, b * cap, d), BF16),
        compiler_params=_params(("arbitrary",), 40),
        name="sparse_gather",
    )(offs, hx2, slot_blocks)


def _sparse_scatter_kernel(offs_ref, x_ref, y_ref, slot_ref, gate_ref, mod_ref, fn_ref, o_ref,
                           le_ref, lg_ref, p_ref, yc_ref, acc_ref, *, n_exp, nblk, bps, final):
    sg = SLOT_GROUP
    sub = _iota((sg, TOKEN_BLOCK), 0).astype(F32)

    def per_block(i, _):
        cnt = _pair_list(offs_ref, pl.program_id(1) * bps + i, n_exp, nblk, le_ref, lg_ref)
        tok = pl.ds(pl.multiple_of(i * TOKEN_BLOCK, TOKEN_BLOCK), TOKEN_BLOCK)
        acc_ref[...] = jnp.zeros(acc_ref.shape, F32)

        def dot_batch(c, _):
            for p in range(PAIRS_PER_DOT):
                e_p, s0, base = _pair(le_ref, lg_ref, c * PAIRS_PER_DOT + p, cnt)
                row = slot_ref[0, i, pl.ds(e_p, 1), :]
                gate = gate_ref[0, i, pl.ds(e_p, 1), :]
                p_ref[p * sg:(p + 1) * sg, :] = jnp.where(row == base + sub, gate, 0.0)
                yc_ref[p * sg:(p + 1) * sg, :] = y_ref[e_p, pl.ds(s0, sg), :]
            acc_ref[...] += _dot(p_ref[...].T.astype(BF16), yc_ref[...])
            return 0

        _for_each_dot(cnt, dot_batch)
        out = x_ref[0, tok, :] + mod_ref[0][5:6] * acc_ref[...]
        if final:
            out = out * lax.rsqrt(jnp.mean(out * out, axis=-1, keepdims=True) + EPS) * fn_ref[...]
        o_ref[0, tok, :] = out
        return 0

    lax.fori_loop(0, bps, per_block, 0)


def _sparse_scatter(x, y, slot_blocks, gate_blocks, offs, mod, final_norm, cap, final):
    b, t, d = x.shape
    _, nblk, e, _ = slot_blocks.shape
    bps = min(nblk, 4)
    rows = PAIRS_PER_DOT * SLOT_GROUP
    tok = lambda n: pl.BlockSpec((1, bps * TOKEN_BLOCK, n), lambda i, j, o: (i, j, 0))
    exp = pl.BlockSpec((1, bps, e, TOKEN_BLOCK), lambda i, j, o: (i, j, 0, 0))
    return pl.pallas_call(
        functools.partial(_sparse_scatter_kernel, n_exp=e, nblk=nblk, bps=bps, final=final),
        grid_spec=pltpu.PrefetchScalarGridSpec(
            num_scalar_prefetch=1,
            grid=(b, nblk // bps),
            in_specs=[tok(d), pl.BlockSpec((e, cap, d), lambda i, j, o: (0, i, 0)), exp, exp,
                      pl.BlockSpec((1, 8, d), lambda i, j, o: (i, 0, 0)),
                      pl.BlockSpec((1, d), lambda i, j, o: (0, 0))],
            out_specs=tok(d),
            scratch_shapes=[pltpu.SMEM(((e + 1) * (cap // SLOT_GROUP),), jnp.int32),
                            pltpu.SMEM(((e + 1) * (cap // SLOT_GROUP),), jnp.int32),
                            pltpu.VMEM((rows, TOKEN_BLOCK), F32),
                            pltpu.VMEM((rows, d), BF16),
                            pltpu.VMEM((TOKEN_BLOCK, d), F32)]),
        out_shape=jax.ShapeDtypeStruct((b, t, d), F32),
        compiler_params=_params(("arbitrary", "arbitrary"), 48),
        name="sparse_scatter",
    )(offs, x, y, slot_blocks, gate_blocks, mod, final_norm)


def _ffn_kernel(*refs, n_sets, mc):
    x_refs = refs[:n_sets]
    wg_ref, wu_ref, wd_ref = refs[n_sets:n_sets + 3]
    y_refs = refs[n_sets + 3:2 * n_sets + 3]
    acc_refs = refs[2 * n_sets + 3:3 * n_sets + 3]
    wgb, wub, wdb = refs[3 * n_sets + 3:]
    f = pl.program_id(1)
    wgb[...] = wg_ref[0, 0].astype(BF16)
    wub[...] = wu_ref[0, 0].astype(BF16)
    wdb[...] = wd_ref[0, 0].astype(BF16)
    last = f == pl.num_programs(1) - 1

    @pl.when((pl.program_id(0) == 0) & (f == 0))
    def _():
        for acc in acc_refs:
            acc[...] = jnp.zeros(acc.shape, F32)

    for x_ref, y_ref, acc in zip(x_refs, y_refs, acc_refs):
        m = x_ref.shape[1]
        step = min(m, mc)
        for m0 in range(0, m, step):
            xs = x_ref[0, m0:m0 + step, :]
            hg = _dot(xs, wgb[...])
            hu = _dot(xs, wub[...])
            hid = (hg * _sigmoid(hg) * hu).astype(BF16)
            tot = acc[m0:m0 + step, :] + _dot(hid, wdb[...])
            acc[m0:m0 + step, :] = jnp.where(last, 0.0, tot)
            y_ref[0, m0:m0 + step, :] = tot.astype(BF16)


def _expert_ffn(xs, layer, w_gate, w_up, w_down):
    _, e, d, hidden = w_gate.shape
    tf = 512
    n = len(xs)
    xspecs = [pl.BlockSpec((1, x.shape[1], d), lambda i, j: (i, 0, 0)) for x in xs]
    return pl.pallas_call(
        functools.partial(_ffn_kernel, n_sets=n, mc=1024),
        grid=(e, hidden // tf),
        in_specs=xspecs + [pl.BlockSpec((1, 1, d, tf), lambda i, j: (layer, i, 0, j)),
                           pl.BlockSpec((1, 1, d, tf), lambda i, j: (layer, i, 0, j)),
                           pl.BlockSpec((1, 1, tf, d), lambda i, j: (layer, i, j, 0))],
        out_specs=xspecs,
        out_shape=[jax.ShapeDtypeStruct(x.shape, BF16) for x in xs],
        scratch_shapes=[pltpu.VMEM((x.shape[1], d), F32) for x in xs]
        + [pltpu.VMEM((d, tf), BF16), pltpu.VMEM((d, tf), BF16), pltpu.VMEM((tf, d), BF16)],
        compiler_params=_params(("arbitrary", "arbitrary"), 60),
        name="expert_ffn",
    )(*xs, w_gate, w_up, w_down)


def _scatter_kernel(x_ref, y_ref, slot_ref, gate_ref, mod_ref, fn_ref, o_ref, *, final):
    tt = x_ref.shape[1]
    e, cap, _ = y_ref.shape
    eye = (_iota((tt, tt), 0) == _iota((tt, tt), 1)).astype(BF16)
    slot_t = _dot_nt(eye, (slot_ref[0] + 1.0).astype(BF16))
    gate_t = _dot_nt(eye, gate_ref[0].astype(BF16))
    lane = _iota((tt, cap), 1).astype(F32) + 1.0
    acc = None
    for i in range(e):
        pt = jnp.where(slot_t[:, i:i + 1] == lane, gate_t[:, i:i + 1], 0.0).astype(BF16)
        part = _dot(pt, y_ref[i])
        acc = part if acc is None else acc + part
    out = x_ref[0] + mod_ref[0][5:6] * acc
    if final:
        out = out * lax.rsqrt(jnp.mean(out * out, axis=-1, keepdims=True) + EPS) * fn_ref[...]
    o_ref[0] = out


def _scatter(x, y, slot, gate, mod, final_norm, cap, final):
    b, t, d = x.shape
    e = slot.shape[1]
    tt = min(t, 512)
    return pl.pallas_call(
        functools.partial(_scatter_kernel, final=final),
        grid=(b, t // tt),
        in_specs=[pl.BlockSpec((1, tt, d), lambda i, j: (i, j, 0)),
                  pl.BlockSpec((e, cap, d), lambda i, j: (0, i, 0)),
                  pl.BlockSpec((1, e, tt), lambda i, j: (i, 0, j)),
                  pl.BlockSpec((1, e, tt), lambda i, j: (i, 0, j)),
                  pl.BlockSpec((1, 8, d), lambda i, j: (i, 0, 0)),
                  pl.BlockSpec((1, d), lambda i, j: (0, 0))],
        out_specs=pl.BlockSpec((1, tt, d), lambda i, j: (i, j, 0)),
        out_shape=jax.ShapeDtypeStruct((b, t, d), F32),
        compiler_params=_params(("arbitrary", "arbitrary"), 48),
        name="scatter",
    )(x, y, slot, gate, mod, final_norm)


def _moe(streams, layer, w_gate, w_up, w_down, final_norm, final):
    routed = []
    for x1, hx2, logits, mod in streams:
        t = x1.shape[1]
        cap = CAPACITY_FACTOR * t // N_EXPERTS
        slot, gate, offs = _route(logits, cap)
        nblk = t // TOKEN_BLOCK
        if nblk > 1:
            offs = offs[:, :, :nblk + 1].astype(jnp.int32).reshape(offs.shape[0], -1)
            slot, gate = _by_token_block(slot), _by_token_block(gate)
            xg = _sparse_gather(hx2, slot, offs, cap)
        else:
            offs = None
            xg = _gather(hx2, slot, cap)
        routed.append((slot, gate, offs, cap, xg))
    ys = _expert_ffn([r[4] for r in routed], layer, w_gate, w_up, w_down)
    outs = []
    for (x1, _, _, mod), (slot, gate, offs, cap, _), y in zip(streams, routed, ys):
        if offs is None:
            outs.append(_scatter(x1, y, slot, gate, mod, final_norm, cap, final))
        else:
            outs.append(_sparse_scatter(x1, y, slot, gate, offs, mod, final_norm, cap, final))
    return outs


def _inproj_kernel(x_ref, mod_ref, nmix_ref, wko_ref, wqvt_ref, wg_ref, bg_ref,
                   qt_ref, k_ref, vt_ref, o_ref, gr_ref, *, chunk, qk, dk):
    tt = x_ref.shape[1]
    mod = mod_ref[0]
    h = _norm_mod(x_ref[0], nmix_ref[...], mod[0:1], mod[1:2])
    hb = h.astype(BF16)
    p = _dot(hb, wko_ref[...])
    k_ref[0] = (p[:, :qk] * (dk ** -0.5)).astype(BF16)
    o_ref[0] = _sigmoid(p[:, qk:]).astype(o_ref.dtype)
    p_t = _dot_nt(wqvt_ref[...], hb)
    for ci in range(tt // chunk):
        qt_ref[0, ci] = p_t[:qk, ci * chunk:(ci + 1) * chunk].astype(BF16)
        vt_ref[0, ci] = p_t[qk:, ci * chunk:(ci + 1) * chunk].astype(BF16)

    nh = MLSTM_HEADS
    g_t = _transpose_exact(_dot_split(h, wg_ref[...]) + bg_ref[...])
    row = _iota((4 * nh, chunk), 0)
    is_f = (row & nh) == nh
    a = _iota((chunk, chunk), 0)
    c = _iota((chunk, chunk), 1)
    tri = jnp.concatenate([(a <= c).astype(BF16), (a >= c).astype(BF16)], axis=1)
    ng = 4 * nh
    for c0 in range(0, tt, chunk):
        gc = g_t[:, c0:c0 + chunk]
        pieces = _split_bf16(jnp.where(is_f, _log_sigmoid(gc), 0.0), 3)
        cum = _dot(jnp.concatenate(pieces, axis=0), tri)
        cum = cum[:ng] + (cum[ng:2 * ng] + cum[2 * ng:])
        gr_ref[0, :, c0:c0 + chunk] = jnp.where(is_f, jnp.where(row < 2 * nh, cum[:, :chunk], cum[:, chunk:]), gc)


def _inproj(x, mod, nmix, w_ko, w_qvt, w_g, b_g, qk, vd):
    b, t, d = x.shape
    tt = min(t, 1024)
    ng = w_g.shape[1]
    ln = SCAN_CHUNK
    full = lambda shape: pl.BlockSpec(shape, lambda i, j: (0,) * len(shape))
    tok = lambda n: pl.BlockSpec((1, tt, n), lambda i, j: (i, j, 0))
    slab = lambda n: pl.BlockSpec((1, tt // ln, n, ln), lambda i, j: (i, j, 0, 0))
    return pl.pallas_call(
        functools.partial(_inproj_kernel, chunk=ln, qk=qk, dk=qk // MLSTM_HEADS),
        grid=(b, t // tt),
        in_specs=[tok(d), pl.BlockSpec((1, 8, d), lambda i, j: (i, 0, 0)), full((1, d)),
                  full(w_ko.shape), full(w_qvt.shape), full(w_g.shape), full(b_g.shape)],
        out_specs=[slab(qk), tok(qk), slab(vd), tok(vd),
                   pl.BlockSpec((1, ng, tt), lambda i, j: (i, 0, j))],
        out_shape=[jax.ShapeDtypeStruct((b, t // ln, qk, ln), BF16), jax.ShapeDtypeStruct((b, t, qk), BF16),
                   jax.ShapeDtypeStruct((b, t // ln, vd, ln), BF16), jax.ShapeDtypeStruct((b, t, vd), BF16),
                   jax.ShapeDtypeStruct((b, ng, t), F32)],
        compiler_params=_params(("arbitrary", "arbitrary"), 56),
        name="mlstm_inproj",
    )(x, mod, nmix, w_ko, w_qvt, w_g, b_g)


def _scan_kernel(kc_ref, vc_ref, grc_ref, qx_ref, kx_ref, vx_ref, grx_ref,
                 out_ref, s_ref, sall_ref, mall_ref, *, chunk):
    ln = chunk
    nh = MLSTM_HEADS
    pair = pl.program_id(1)
    kl = kx_ref.shape[2]
    dv = vx_ref.shape[2] // 2
    lane = _iota((ln, kl), 1)
    kmask = (lane < kl // 2, lane >= kl // 2)
    klane = _iota((2 * dv, kl), 1)
    kcols = (klane < kl // 2, klane >= kl // 2)
    ones = jnp.ones((dv, ln), BF16)
    si = _iota((ln, ln), 0)
    ji = _iota((ln, ln), 1)
    causal = (si <= ji, si >= ji)
    ncc = kc_ref.shape[1] // ln
    ncx = kx_ref.shape[1] // ln

    def gate_rows(gr_ref, hh, dirn, c):
        base = 2 * nh * dirn + 2 * pair + hh
        return gr_ref[0, base, pl.ds(c, 1), :], gr_ref[0, base + nh, pl.ds(c, 1), :]

    def keys(k_ref, c):
        k = k_ref[0, pl.ds(pl.multiple_of(c * ln, ln), ln), :]
        return [jnp.where(kmask[hh], k, jnp.zeros((), BF16)) for hh in range(2)]

    def values_t(v_ref, c, hh):
        return jnp.concatenate([v_ref[0, c, hh * dv:(hh + 1) * dv, :], ones], axis=0)

    def advance(refs, n, record):
        k_ref, v_ref, gr_ref = refs

        def body(i, ms):
            new_ms = [None] * 4
            cs = (i, n - 1 - i)
            for dirn in range(2):
                c = cs[dirn]
                vws, decays = [], []
                for hh in range(2):
                    ch = 2 * hh + dirn
                    ig, bc = gate_rows(gr_ref, hh, dirn, c)
                    m = ms[ch]
                    if record:
                        s_own = jnp.where(kcols[hh], s_ref[ch], 0.0)
                        sall_ref[ch * ncx + c] = s_own.astype(BF16)
                        mall_ref[ch * ncx + c] = jnp.broadcast_to(m, (8, ln))
                    b_end = bc[:, ln - 1:ln] if dirn == 0 else bc[:, 0:1]
                    gl = b_end - bc + ig
                    m_new = jnp.maximum(b_end + m, jnp.max(gl, axis=1, keepdims=True))
                    vws.append((values_t(v_ref, c, hh) * jnp.exp(gl - m_new)).astype(BF16))
                    decays.append(jnp.exp(b_end + m - m_new))
                    new_ms[ch] = m_new
                upd = _dot(jnp.concatenate(vws, axis=0), k_ref[0, pl.ds(pl.multiple_of(c * ln, ln), ln), :])
                for hh in range(2):
                    ch = 2 * hh + dirn
                    s_ref[ch] = decays[hh] * s_ref[ch] + upd[hh * 2 * dv:(hh + 1) * 2 * dv, :]
            return tuple(new_ms)

        return body

    s_ref[...] = jnp.zeros(s_ref.shape, F32)
    ms = tuple(jnp.zeros((1, 1), F32) for _ in range(4))
    ms = lax.fori_loop(0, ncc, advance((kc_ref, vc_ref, grc_ref), ncc, False), ms, unroll=min(ncc, SCAN_UNROLL))
    lax.fori_loop(0, ncx, advance((kx_ref, vx_ref, grx_ref), ncx, True), ms, unroll=min(ncx, SCAN_UNROLL))

    def emit(c, _):
        q_t = qx_ref[0, c]
        q_tf = q_t.astype(F32)
        s_pair = _dot(jnp.concatenate(keys(kx_ref, c), axis=0), q_t)
        for hh in range(2):
            s_kq = s_pair[hh * ln:(hh + 1) * ln, :]
            v_t = values_t(vx_ref, c, hh)
            hsum = None
            for dirn in range(2):
                ch = 2 * hh + dirn
                ig, bc = gate_rows(grx_ref, hh, dirn, c)
                u_col = jnp.broadcast_to(ig - bc, (ln, ln)).T
                a = bc + mall_ref[ch * ncx + c][0:1, :]
                dm = jnp.where(causal[dirn], u_col + bc, -jnp.inf)
                mj = jnp.maximum(a, jnp.max(dm, axis=0, keepdims=True))
                sm = (s_kq * jnp.exp(dm - mj)).astype(BF16)
                qw = (q_tf * jnp.exp(a - mj)).astype(BF16)
                num = _dot(jnp.concatenate([v_t, sall_ref[ch * ncx + c]], axis=1),
                           jnp.concatenate([sm, qw], axis=0))
                h = num[:dv, :] * (1.0 / jnp.maximum(jnp.abs(num[dv:dv + 1, :]), jnp.exp(-mj)))
                hsum = h if hsum is None else hsum + h
            hn = hsum * lax.rsqrt(jnp.mean(hsum * hsum, axis=0, keepdims=True) + EPS)
            out_ref[0, pl.ds(pl.multiple_of(c * ln, ln), ln), hh * dv:(hh + 1) * dv] = hn.T.astype(out_ref.dtype)
        return 0

    lax.fori_loop(0, ncx, emit, 0, unroll=min(ncx, SCAN_UNROLL))


def _scan(ctx_parts, lat_parts):
    kc, vc, grc = ctx_parts
    qx, kx, vx, grx = lat_parts
    b, t, qk = kx.shape
    vd = vx.shape[2]
    nh = MLSTM_HEADS
    ln = SCAN_CHUNK
    dv = vd // nh
    kl = 2 * (qk // nh)
    assert kl == ln and dv == ln, "scan kernel assumes key-pair lanes = head value dim = chunk"
    ncx = t // ln

    def specs(k, v, gr):
        tk = k.shape[1]
        return [pl.BlockSpec((1, tk, kl), lambda i, j: (i, 0, j)),
                pl.BlockSpec((1, tk // ln, 2 * dv, ln), lambda i, j: (i, 0, j, 0)),
                pl.BlockSpec((1,) + gr.shape[1:], lambda i, j: (i, 0, 0, 0))]

    return pl.pallas_call(
        functools.partial(_scan_kernel, chunk=ln),
        grid=(b, nh // 2),
        in_specs=specs(kc, vc, grc) + [pl.BlockSpec((1, ncx, kl, ln), lambda i, j: (i, 0, j, 0))] + specs(kx, vx, grx),
        out_specs=pl.BlockSpec((1, t, 2 * dv), lambda i, j: (i, 0, j)),
        out_shape=jax.ShapeDtypeStruct((b, t, vd), BF16),
        scratch_shapes=[pltpu.VMEM((4, 2 * dv, kl), F32),
                        pltpu.VMEM((4 * ncx, 2 * dv, kl), BF16),
                        pltpu.VMEM((4 * ncx, 8, ln), F32)],
        compiler_params=_params(("arbitrary", "arbitrary"), 40),
        name="mlstm_scan",
    )(kc, vc, grc, qx, kx, vx, grx)


def _mlstm_out_kernel(hn_ref, og_ref, x_ref, mod_ref, mn_ref, wout_ref, nffn_ref, wr_ref,
                      x1_ref, hx2_ref, lg_ref):
    mod = mod_ref[0]
    a = (hn_ref[0].astype(F32) * mn_ref[...]) * og_ref[0].astype(F32)
    xn = x_ref[0] + mod[2:3] * _dot(a.astype(BF16), wout_ref[...])
    x1_ref[0] = xn
    hb, lg = _ffn_pre(xn, mod, nffn_ref[...], wr_ref[...])
    hx2_ref[0] = hb
    lg_ref[0] = lg


def _mlstm_out(hn, og, x, mod, mnorm, w_out, nffn, w_router):
    b, t, d = x.shape
    vd = hn.shape[2]
    e = w_router.shape[1]
    tt = min(t, 1024)
    full = lambda shape: pl.BlockSpec(shape, lambda i, j: (0,) * len(shape))
    tok = lambda n: pl.BlockSpec((1, tt, n), lambda i, j: (i, j, 0))
    return pl.pallas_call(
        _mlstm_out_kernel,
        grid=(b, t // tt),
        in_specs=[tok(vd), tok(vd), tok(d), pl.BlockSpec((1, 8, d), lambda i, j: (i, 0, 0)),
                  full((1, vd)), full(w_out.shape), full((1, d)), full(w_router.shape)],
        out_specs=[tok(d), tok(d), pl.BlockSpec((1, e, tt), lambda i, j: (i, 0, j))],
        out_shape=[jax.ShapeDtypeStruct((b, t, d), F32), jax.ShapeDtypeStruct((b, t, d), BF16),
                   jax.ShapeDtypeStruct((b, e, t), F32)],
        compiler_params=_params(("arbitrary", "arbitrary"), 48),
        name="mlstm_out",
    )(hn, og, x, mod, mnorm, w_out, nffn, w_router)


def _chunked_rows(gr):
    b, n, t = gr.shape
    return gr.reshape(b, n, t // SCAN_CHUNK, SCAN_CHUNK)


def kernel(x, c, ctx, c_ctx, ada_w, ada_b, norm_mix, norm_ffn, pool_w, pool_scale, mlstm_w_in, mlstm_b_gates,
           mlstm_norm, mlstm_w_out, moe_router, moe_w_gate, moe_w_up, moe_w_down, final_norm):
    bsz, seq, d = x.shape
    ctx_len = ctx.shape[1]
    depth = ada_w.shape[0]
    n_mixers = 2

    cc = jnp.concatenate([c, c_ctx[None, :], jnp.zeros((16 - bsz - 1, d), F32)], axis=0)
    ada = _ada(cc, ada_w, ada_b).reshape(depth, 16, N_ADA, d)
    pad = jnp.zeros((bsz, 8 - N_ADA, d), F32)

    fn = final_norm.reshape(1, d)
    for i in range(depth):
        last = i == depth - 1
        j = i // n_mixers
        mod_x = jnp.concatenate([ada[i, :bsz], pad], axis=1)
        mod_c = jnp.concatenate([jnp.broadcast_to(ada[i, bsz][None], (bsz, N_ADA, d)), pad], axis=1)
        nmix = norm_mix[i].reshape(1, d)
        nffn = norm_ffn[i].reshape(1, d)
        w_router = moe_router[i]
        streams = []
        if i % n_mixers == 0:
            pw = pool_w[j].astype(BF16)
            ps = pool_scale[j].reshape(1, d)
            x1, hx2, lg = _pool_mixer(x, mod_x, nmix, nffn, pw, ps, w_router, GRID_W, True)
            streams.append((x1, hx2, lg, mod_x))
            if not last:
                c1, hc2, lgc = _pool_mixer(ctx.reshape(1, bsz * ctx_len, d), mod_c[:1], nmix, nffn, pw, ps, w_router,
                                           ctx_len, False)
                lgc = jnp.swapaxes(lgc.reshape(-1, bsz, ctx_len), 0, 1)
                streams.append((c1.reshape(bsz, ctx_len, d), hc2.reshape(bsz, ctx_len, d), lgc, mod_c))
        else:
            qk = mlstm_w_in.shape[2] - 2 * mlstm_w_out.shape[1] - 4 * MLSTM_HEADS
            qk //= 2
            vd = mlstm_w_out.shape[1]
            w_in = mlstm_w_in[j]
            w_ko = jnp.concatenate([w_in[:, qk:2 * qk], w_in[:, 2 * qk + vd:2 * qk + 2 * vd]], axis=1).astype(BF16)
            w_qvt = jnp.concatenate([w_in[:, :qk], w_in[:, 2 * qk:2 * qk + vd]], axis=1).T.astype(BF16)
            w_g = w_in[:, 2 * qk + 2 * vd:]
            b_g = mlstm_b_gates[j].reshape(1, -1)
            proj = lambda s, m: _inproj(s, m, nmix, w_ko, w_qvt, w_g, b_g, qk, vd)
            _, kc, vc, _, grc = proj(ctx.reshape(1, bsz * ctx_len, d), mod_c[:1])
            kc = kc.reshape(bsz, ctx_len, qk)
            vc = vc.reshape(bsz, ctx_len // SCAN_CHUNK, vd, SCAN_CHUNK)
            grc = jnp.swapaxes(grc.reshape(-1, bsz, ctx_len), 0, 1)
            qx, kx, vx, ox, grx = proj(x, mod_x)
            hn = _scan((kc, vc, _chunked_rows(grc)), (qx, kx, vx, _chunked_rows(grx)))
            x1, hx2, lg = _mlstm_out(hn, ox, x, mod_x, mlstm_norm[j].reshape(1, vd),
                                     mlstm_w_out[j].astype(BF16), nffn, w_router)
            streams.append((x1, hx2, lg, mod_x))
            assert last, "context output of the mLSTM mixer is only needed by a following layer"
        outs = _moe(streams, i, moe_w_gate, moe_w_up, moe_w_down, fn, last)
        x = outs[0]
        if not last:
            ctx = outs[1]
    return x
```

```python
import functools

import jax
import jax.numpy as jnp
import numpy as np
from jax import lax
from jax.experimental import pallas as pl
from jax.experimental.pallas import tpu as pltpu

F32 = jnp.float32
BF16 = jnp.bfloat16
HIGHEST = lax.Precision.HIGHEST

GRID_W = 64
EPS = 1e-6
N_ADA = 6
POOL_WINDOWS = (2, 4, 8, 16)
N_POOL_GROUPS = 4
MLSTM_HEADS = 8
N_EXPERTS = 16
CAPACITY_FACTOR = 2

MIB = 1024 * 1024
MXU_DIM = 256
LANES = 128
TOKEN_BLOCK = 256
SLOT_GROUP = 32
PAIRS_PER_DOT = 16
SCAN_CHUNK = 128
SCAN_UNROLL = 8
POOL_PAD_ROWS = max(POOL_WINDOWS) // 2


def _params(sem, vmem_mib):
    return pltpu.CompilerParams(dimension_semantics=sem, vmem_limit_bytes=vmem_mib * MIB)


def _dot(a, b, precision=None):
    return jnp.dot(a, b, preferred_element_type=F32, precision=precision)


def _dot_nt(a, b, precision=None):
    return lax.dot_general(a, b, (((1,), (1,)), ((), ())), preferred_element_type=F32, precision=precision)


def _iota(shape, dim, dtype=jnp.int32):
    return lax.broadcasted_iota(dtype, shape, dim)


def _sigmoid(x):
    return 1.0 / (1.0 + jnp.exp(-x))


def _log_sigmoid(x):
    return jnp.minimum(x, 0.0) - jnp.log1p(jnp.exp(-jnp.abs(x)))


def _norm_mod(x, g, shift, scale):
    inv = lax.rsqrt(jnp.mean(x * x, axis=-1, keepdims=True) + EPS)
    return (x * inv) * (g * (1.0 + scale)) + shift


def _ada_kernel(c_ref, w_ref, b_ref, o_ref):
    c = c_ref[...]
    s = c * _sigmoid(c)
    s_hi, s_lo = _split_bf16(s, 2)
    w_hi, w_lo = _split_bf16(w_ref[0], 2)
    o_ref[0] = _dot(s_hi, w_hi) + (_dot(s_hi, w_lo) + _dot(s_lo, w_hi)) + b_ref[0]


def _ada(cc, ada_w, ada_b):
    depth, d, n = ada_w.shape
    rows = cc.shape[0]
    tn = n // 4
    return pl.pallas_call(
        _ada_kernel,
        grid=(depth, n // tn),
        in_specs=[pl.BlockSpec((rows, d), lambda i, j: (0, 0)),
                  pl.BlockSpec((1, d, tn), lambda i, j: (i, 0, j)),
                  pl.BlockSpec((1, 1, tn), lambda i, j: (i, 0, j))],
        out_specs=pl.BlockSpec((1, rows, tn), lambda i, j: (i, 0, j)),
        out_shape=jax.ShapeDtypeStruct((depth, rows, n), F32),
        compiler_params=_params(("arbitrary", "arbitrary"), 40),
        name="ada",
    )(cc, ada_w, ada_b.reshape(depth, 1, n))


def _split_bf16(x, pieces):
    out = []
    for _ in range(pieces):
        p = x.astype(BF16)
        out.append(p)
        x = x - p.astype(F32)
    return out


def _dot_split(a, b):
    n = b.shape[1]
    a_hi, a_lo = _split_bf16(a, 2)
    b_hi, b_lo = _split_bf16(b, 2)
    hi = _dot(a_hi, jnp.concatenate([b_hi, b_lo], axis=1))
    return hi[:, :n] + (hi[:, n:] + _dot(a_lo, b_hi))


def _transpose_exact(x):
    m = x.shape[1]
    eye = (_iota((m, m), 0) == _iota((m, m), 1)).astype(BF16)
    hi, mid, lo = _split_bf16(x, 3)
    return _dot_nt(eye, hi) + (_dot_nt(eye, mid) + _dot_nt(eye, lo))


def _ffn_pre(xn, mod, nffn, wr):
    h2 = _norm_mod(xn, nffn, mod[3:4], mod[4:5])
    return h2.astype(BF16), _transpose_exact(_dot_split(h2, wr))


def _pool_kernel(x_ref, mod_ref, nmix_ref, nffn_ref, cmat_ref, cnt_ref, pw_ref, ps_ref, wr_ref,
                 x1_ref, hx2_ref, lg_ref, *pad, two_d, tc):
    t, d = x_ref.shape[1], x_ref.shape[2]
    gd = d // N_POOL_GROUPS
    mod = mod_ref[0]
    pad_tok = POOL_PAD_ROWS * GRID_W

    for c0 in range(0, t, tc):
        x1_ref[0, c0:c0 + tc, :] = _norm_mod(x_ref[0, c0:c0 + tc, :], nmix_ref[...], mod[0:1], mod[1:2])

    def hx_of(r0, rn, j):
        return x1_ref[0, r0:r0 + rn, j * gd:(j + 1) * gd]

    if two_d:
        pad_ref, = pad
        pad_ref[0:pad_tok, :] = jnp.zeros((pad_tok, gd), F32)
        pad_ref[pad_tok + t:pad_tok + t + pad_tok, :] = jnp.zeros((pad_tok, gd), F32)

    for j, w in enumerate(POOL_WINDOWS):
        cs = slice(j * gd, (j + 1) * gd)
        cm = cmat_ref[j]
        sums = []
        for b0 in range(0, t, MXU_DIM):
            g = hx_of(b0, MXU_DIM, j)
            g_hi = g.astype(BF16)
            g_lo = (g - g_hi.astype(F32)).astype(BF16)
            csum = _dot(cm, g_hi) + _dot(cm, g_lo)
            if two_d:
                pad_ref[pad_tok + b0:pad_tok + b0 + MXU_DIM, :] = csum
            else:
                sums.append(csum)
        for c0 in range(0, t, tc):
            if two_d:
                tot = None
                for dr in range(-(w // 2), w - w // 2):
                    o = pad_tok + c0 + dr * GRID_W
                    sl = pad_ref[o:o + tc, :]
                    tot = sl if tot is None else tot + sl
            else:
                blocks = sums[c0 // MXU_DIM:(c0 + tc) // MXU_DIM]
                tot = blocks[0] if len(blocks) == 1 else jnp.concatenate(blocks, axis=0)
            mean = tot / cnt_ref[c0:c0 + tc, j:j + 1]
            diff = mean - hx_of(c0, tc, j)
            y = _dot(diff.astype(BF16), pw_ref[j]) * ps_ref[:, cs]
            x1_ref[0, c0:c0 + tc, cs] = x_ref[0, c0:c0 + tc, cs] + mod[2:3, cs] * y

    tail = min(t, 2 * tc)
    for c0 in range(0, t, tail):
        hb, lg = _ffn_pre(x1_ref[0, c0:c0 + tail, :], mod, nffn_ref[...], wr_ref[...])
        hx2_ref[0, c0:c0 + tail, :] = hb
        lg_ref[0, :, c0:c0 + tail] = lg


def _pool_consts(t, seq, two_d):
    cm = np.zeros((len(POOL_WINDOWS), MXU_DIM, MXU_DIM), np.float32)
    cnt = np.zeros((t, len(POOL_WINDOWS)), np.float32)
    n = seq
    assert MXU_DIM % n == 0
    pos = np.arange(MXU_DIM)
    for j, w in enumerate(POOL_WINDOWS):
        col = pos % n
        lo = np.clip(col - w // 2, 0, n)
        hi = np.clip(col + w - w // 2, 0, n)
        same = (pos[:, None] // n) == (pos[None, :] // n)
        cm[j] = (same & (col[None, :] >= lo[:, None]) & (col[None, :] < hi[:, None])).astype(np.float32)
        tt = np.arange(t)
        c = tt % n
        ccnt = np.clip(c + w - w // 2, 0, n) - np.clip(c - w // 2, 0, n)
        if two_d:
            rows = t // n
            r = tt // n
            rcnt = np.clip(r + w - w // 2, 0, rows) - np.clip(r - w // 2, 0, rows)
            cnt[:, j] = ccnt * rcnt
        else:
            cnt[:, j] = ccnt
    return jnp.asarray(cm, BF16), jnp.asarray(cnt)


def _pool_mixer(x, mod, nmix, nffn, pool_w, pool_scale, w_router, seq, two_d):
    b, t, d = x.shape
    gd = d // N_POOL_GROUPS
    assert seq == GRID_W or not two_d, "row-window shifts assume GRID_W tokens per grid row"
    cmat, cnt = _pool_consts(t, seq, two_d)
    tc = min(t, 512)
    pad_tok = POOL_PAD_ROWS * GRID_W
    scratch = []
    if two_d:
        scratch.append(pltpu.VMEM((t + 2 * pad_tok, gd), F32))
    e = w_router.shape[1]
    full = lambda shape: pl.BlockSpec(shape, lambda i: (0,) * len(shape))
    return pl.pallas_call(
        functools.partial(_pool_kernel, two_d=two_d, tc=tc),
        grid=(b,),
        in_specs=[pl.BlockSpec((1, t, d), lambda i: (i, 0, 0)),
                  pl.BlockSpec((1, 8, d), lambda i: (i, 0, 0)),
                  full((1, d)), full((1, d)), full(cmat.shape), full(cnt.shape),
                  full(pool_w.shape), full((1, d)), full(w_router.shape)],
        out_specs=[pl.BlockSpec((1, t, d), lambda i: (i, 0, 0)),
                   pl.BlockSpec((1, t, d), lambda i: (i, 0, 0)),
                   pl.BlockSpec((1, e, t), lambda i: (i, 0, 0))],
        out_shape=[jax.ShapeDtypeStruct((b, t, d), F32),
                   jax.ShapeDtypeStruct((b, t, d), BF16),
                   jax.ShapeDtypeStruct((b, e, t), F32)],
        scratch_shapes=scratch,
        compiler_params=_params(("arbitrary",), 60),
        name="pool_mixer_2d" if two_d else "pool_mixer_1d",
    )(x, mod, nmix, nffn, cmat, cnt, pool_w, pool_scale, w_router)


def _route_kernel(lg_ref, slot_ref, gate_ref, offs_ref, *, cap):
    nb, e, t = lg_ref.shape
    capf = jnp.float32(cap)
    affs = []
    for s in range(nb):
        lg = lg_ref[s]
        ex = jnp.exp(lg - jnp.max(lg, axis=0, keepdims=True))
        affs.append(ex / jnp.sum(ex, axis=0, keepdims=True))

    def as_f32(v):
        return lax.bitcast_convert_type(v, F32)

    def enough(aff, cand):
        return jnp.sum(jnp.where(aff >= as_f32(cand), 1.0, 0.0), axis=1, keepdims=True) >= capf

    def bisect2(i, vs):
        hi = jnp.left_shift(jnp.int32(1), 30 - 2 * i)
        lo = jnp.left_shift(jnp.int32(1), 29 - 2 * i)
        out = []
        for aff, v in zip(affs, vs):
            c_hi, c_lo, c_both = v | hi, v | lo, v | hi | lo
            out.append(jnp.where(enough(aff, c_both), c_both,
                                 jnp.where(enough(aff, c_hi), c_hi, jnp.where(enough(aff, c_lo), c_lo, v))))
        return tuple(out)

    kths = lax.fori_loop(0, 15, bisect2, tuple(jnp.zeros((e, 1), jnp.int32) for _ in range(nb)))

    blk = min(t, MXU_DIM)
    before = (_iota((blk, blk), 0) < _iota((blk, blk), 1)).astype(BF16)
    nl = offs_ref.shape[2]
    starts = (_iota((t, nl), 0) < _iota((t, nl), 1) * TOKEN_BLOCK).astype(BF16)

    def excl_cumsum(mask):
        ones = jnp.where(mask, 1.0, 0.0)
        outs, run = [], jnp.zeros((e, 1), F32)
        for b0 in range(0, t, blk):
            mb = ones[:, b0:b0 + blk]
            outs.append(_dot(mb.astype(BF16), before) + run)
            run = run + jnp.sum(mb, axis=1, keepdims=True)
        return jnp.concatenate(outs, axis=1) if len(outs) > 1 else outs[0]

    for s, (aff, kth) in enumerate(zip(affs, kths)):
        kth = jnp.where(enough(aff, kth | 1), kth | 1, kth)
        above = as_f32(kth + 1)
        gt = aff >= above
        eq = (aff >= as_f32(kth)) & jnp.logical_not(gt)
        need = capf - jnp.sum(jnp.where(gt, 1.0, 0.0), axis=1, keepdims=True)
        sel = gt | (eq & (excl_cumsum(eq) < need))
        slot_ref[s] = jnp.where(sel, excl_cumsum(sel), -1.0)
        gate_ref[s] = aff
        offs_ref[s] = _dot(jnp.where(sel, 1.0, 0.0).astype(BF16), starts)


def _route(logits, cap):
    b, e, t = logits.shape
    nb = 2 if b % 2 == 0 else 1
    spec = pl.BlockSpec((nb, e, t), lambda i: (i, 0, 0))
    ospec = pl.BlockSpec((nb, e, LANES), lambda i: (i, 0, 0))
    return pl.pallas_call(
        functools.partial(_route_kernel, cap=cap),
        grid=(b // nb,),
        in_specs=[spec],
        out_specs=[spec, spec, ospec],
        out_shape=[jax.ShapeDtypeStruct((b, e, t), F32)] * 2 + [jax.ShapeDtypeStruct((b, e, LANES), F32)],
        compiler_params=_params(("arbitrary",), 32),
        name="route",
    )(logits)


def _gather_kernel(h_ref, slot_ref, o_ref, *, cap):
    hx = h_ref[0]
    t = hx.shape[0]
    row = _iota((cap, t), 0).astype(F32)
    for i in range(o_ref.shape[0]):
        onehot = jnp.where(slot_ref[0, i:i + 1, :] == row, 1.0, 0.0).astype(BF16)
        o_ref[i] = _dot(onehot, hx).astype(BF16)


def _gather(hx2, slot, cap):
    b, t, d = hx2.shape
    e = slot.shape[1]
    eb = 8
    return pl.pallas_call(
        functools.partial(_gather_kernel, cap=cap),
        grid=(b, e // eb),
        in_specs=[pl.BlockSpec((1, t, d), lambda i, j: (i, 0, 0)),
                  pl.BlockSpec((1, eb, t), lambda i, j: (i, j, 0))],
        out_specs=pl.BlockSpec((eb, cap, d), lambda i, j: (j, i, 0)),
        out_shape=jax.ShapeDtypeStruct((e, b * cap, d), BF16),
        compiler_params=_params(("arbitrary", "arbitrary"), 48),
        name="gather",
    )(hx2, slot)


STATIC_DOTS = 2


def _pair_list(offs_ref, kb, n_exp, nblk, le_ref, lg_ref):
    b = pl.program_id(0)
    shift = SLOT_GROUP.bit_length() - 1
    max_groups = le_ref.shape[0] // (n_exp + 1)

    cnt = jnp.int32(0)
    for e in range(n_exp):
        o0 = offs_ref[b, e * (nblk + 1) + kb]
        o1 = offs_ref[b, e * (nblk + 1) + kb + 1]
        lo = jnp.right_shift(o0, shift)
        ng = jnp.where(o1 > o0, jnp.right_shift(o1 - 1, shift) - lo + 1, 0)
        for i in range(max_groups):
            le_ref[cnt + i] = e
            lg_ref[cnt + i] = lo + i
        cnt = cnt + ng
    return cnt


def _pair(le_ref, lg_ref, idx, cnt):
    valid = idx < cnt
    safe = jnp.minimum(idx, jnp.maximum(cnt - 1, 0))
    e_p = jnp.where(valid, le_ref[safe], 0)
    g_p = jnp.where(valid, lg_ref[safe], 0)
    base = jnp.where(valid, g_p * SLOT_GROUP, -2 * SLOT_GROUP).astype(F32)
    return e_p, pl.multiple_of(g_p * SLOT_GROUP, SLOT_GROUP), base


def _for_each_dot(cnt, body):
    for c in range(STATIC_DOTS):
        body(c, 0)
    lax.fori_loop(STATIC_DOTS, (cnt + PAIRS_PER_DOT - 1) // PAIRS_PER_DOT, body, 0)


def _sparse_gather_kernel(offs_ref, h_ref, slot_ref, o_ref, le_ref, lg_ref, p_ref, *, n_exp, nblk):
    sg = SLOT_GROUP
    o_ref[...] = jnp.zeros(o_ref.shape, o_ref.dtype)
    sub = _iota((sg, TOKEN_BLOCK), 0).astype(F32)

    def per_block(kb, _):
        cnt = _pair_list(offs_ref, kb, n_exp, nblk, le_ref, lg_ref)
        tok = pl.ds(pl.multiple_of(kb * TOKEN_BLOCK, TOKEN_BLOCK), TOKEN_BLOCK)

        def dot_batch(c, _):
            dst = []
            for p in range(PAIRS_PER_DOT):
                e_p, s0, base = _pair(le_ref, lg_ref, c * PAIRS_PER_DOT + p, cnt)
                row = slot_ref[0, kb, pl.ds(e_p, 1), :]
                p_ref[p * sg:(p + 1) * sg, :] = jnp.where(row == base + sub, 1.0, 0.0).astype(BF16)
                dst.append((e_p, s0))
            z = _dot(p_ref[...], h_ref[0, tok, :])
            for p, (e_p, s0) in enumerate(dst):
                o_ref[e_p, pl.ds(s0, sg), :] += z[p * sg:(p + 1) * sg, :].astype(o_ref.dtype)
            return 0

        _for_each_dot(cnt, dot_batch)
        return 0

    lax.fori_loop(0, nblk, per_block, 0)


def _by_token_block(a):
    b, e, t = a.shape
    return jnp.swapaxes(a.reshape(b, e, t // TOKEN_BLOCK, TOKEN_BLOCK), 1, 2)


def _sparse_gather(hx2, slot_blocks, offs, cap):
    b, t, d = hx2.shape
    _, nblk, e, _ = slot_blocks.shape
    return pl.pallas_call(
        functools.partial(_sparse_gather_kernel, n_exp=e, nblk=nblk),
        grid_spec=pltpu.PrefetchScalarGridSpec(
            num_scalar_prefetch=1,
            grid=(b,),
            in_specs=[pl.BlockSpec((1, t, d), lambda i, o: (i, 0, 0)),
                      pl.BlockSpec((1, nblk, e, TOKEN_BLOCK), lambda i, o: (i, 0, 0, 0))],
            out_specs=pl.BlockSpec((e, cap, d), lambda i, o: (0, i, 0)),
            scratch_shapes=[pltpu.SMEM(((e + 1) * (cap // SLOT_GROUP),), jnp.int32),
                            pltpu.SMEM(((e + 1) * (cap // SLOT_GROUP),), jnp.int32),
                            pltpu.VMEM((PAIRS_PER_DOT * SLOT_GROUP, TOKEN_BLOCK), BF16)]),
        out_shape=jax.ShapeDtypeStruct((e, b * cap, d), BF16),
        compiler_params=_params(("arbitrary",), 40),
        name="sparse_gather",
    )(offs, hx2, slot_blocks)


def _sparse_scatter_kernel(offs_ref, x_ref, y_ref, slot_ref, gate_ref, mod_ref, fn_ref, o_ref,
                           le_ref, lg_ref, p_ref, yc_ref, acc_ref, *, n_exp, nblk, bps, final):
    sg = SLOT_GROUP
    sub = _iota((sg, TOKEN_BLOCK), 0).astype(F32)

    def per_block(i, _):
        cnt = _pair_list(offs_ref, pl.program_id(1) * bps + i, n_exp, nblk, le_ref, lg_ref)
        tok = pl.ds(pl.multiple_of(i * TOKEN_BLOCK, TOKEN_BLOCK), TOKEN_BLOCK)
        acc_ref[...] = jnp.zeros(acc_ref.shape, F32)

        def dot_batch(c, _):
            for p in range(PAIRS_PER_DOT):
                e_p, s0, base = _pair(le_ref, lg_ref, c * PAIRS_PER_DOT + p, cnt)
                row = slot_ref[0, i, pl.ds(e_p, 1), :]
                gate = gate_ref[0, i, pl.ds(e_p, 1), :]
                p_ref[p * sg:(p + 1) * sg, :] = jnp.where(row == base + sub, gate, 0.0).astype(BF16)
                yc_ref[p * sg:(p + 1) * sg, :] = y_ref[e_p, pl.ds(s0, sg), :]
            acc_ref[...] += lax.dot_general(p_ref[...], yc_ref[...], (((0,), (0,)), ((), ())),
                                            preferred_element_type=F32)
            return 0

        _for_each_dot(cnt, dot_batch)
        out = x_ref[0, tok, :] + mod_ref[0][5:6] * acc_ref[...]
        if final:
            out = out * lax.rsqrt(jnp.mean(out * out, axis=-1, keepdims=True) + EPS) * fn_ref[...]
        o_ref[0, tok, :] = out
        return 0

    lax.fori_loop(0, bps, per_block, 0)


def _sparse_scatter(x, y, slot_blocks, gate_blocks, offs, mod, final_norm, cap, final):
    b, t, d = x.shape
    _, nblk, e, _ = slot_blocks.shape
    bps = min(nblk, 4)
    rows = PAIRS_PER_DOT * SLOT_GROUP
    tok = lambda n: pl.BlockSpec((1, bps * TOKEN_BLOCK, n), lambda i, j, o: (i, j, 0))
    exp = pl.BlockSpec((1, bps, e, TOKEN_BLOCK), lambda i, j, o: (i, j, 0, 0))
    return pl.pallas_call(
        functools.partial(_sparse_scatter_kernel, n_exp=e, nblk=nblk, bps=bps, final=final),
        grid_spec=pltpu.PrefetchScalarGridSpec(
            num_scalar_prefetch=1,
            grid=(b, nblk // bps),
            in_specs=[tok(d), pl.BlockSpec((e, cap, d), lambda i, j, o: (0, i, 0)), exp, exp,
                      pl.BlockSpec((1, 8, d), lambda i, j, o: (i, 0, 0)),
                      pl.BlockSpec((1, d), lambda i, j, o: (0, 0))],
            out_specs=tok(d),
            scratch_shapes=[pltpu.SMEM(((e + 1) * (cap // SLOT_GROUP),), jnp.int32),
                            pltpu.SMEM(((e + 1) * (cap // SLOT_GROUP),), jnp.int32),
                            pltpu.VMEM((rows, TOKEN_BLOCK), BF16),
                            pltpu.VMEM((rows, d), BF16),
                            pltpu.VMEM((TOKEN_BLOCK, d), F32)]),
        out_shape=jax.ShapeDtypeStruct((b, t, d), F32),
        compiler_params=_params(("arbitrary", "arbitrary"), 48),
        name="sparse_scatter",
    )(offs, x, y, slot_blocks, gate_blocks, mod, final_norm)


def _ffn_kernel(*refs, n_sets, mc):
    x_refs = refs[:n_sets]
    wg_ref, wu_ref, wd_ref = refs[n_sets:n_sets + 3]
    y_refs = refs[n_sets + 3:2 * n_sets + 3]
    acc_refs = refs[2 * n_sets + 3:3 * n_sets + 3]
    wgb, wub, wdb = refs[3 * n_sets + 3:]
    f = pl.program_id(1)
    wgb[...] = wg_ref[0, 0].astype(BF16)
    wub[...] = wu_ref[0, 0].astype(BF16)
    wdb[...] = wd_ref[0, 0].astype(BF16)
    last = f == pl.num_programs(1) - 1

    @pl.when((pl.program_id(0) == 0) & (f == 0))
    def _():
        for acc in acc_refs:
            acc[...] = jnp.zeros(acc.shape, F32)

    for x_ref, y_ref, acc in zip(x_refs, y_refs, acc_refs):
        m = x_ref.shape[1]
        step = min(m, mc)
        for m0 in range(0, m, step):
            xs = x_ref[0, m0:m0 + step, :]
            hg = _dot(xs, wgb[...])
            hu = _dot(xs, wub[...])
            hid = (hg * _sigmoid(hg) * hu).astype(BF16)
            tot = acc[m0:m0 + step, :] + _dot(hid, wdb[...])
            acc[m0:m0 + step, :] = jnp.where(last, 0.0, tot)
            y_ref[0, m0:m0 + step, :] = tot.astype(BF16)


def _expert_ffn(xs, layer, w_gate, w_up, w_down):
    _, e, d, hidden = w_gate.shape
    tf = 512
    n = len(xs)
    xspecs = [pl.BlockSpec((1, x.shape[1], d), lambda i, j: (i, 0, 0)) for x in xs]
    return pl.pallas_call(
        functools.partial(_ffn_kernel, n_sets=n, mc=1024),
        grid=(e, hidden // tf),
        in_specs=xspecs + [pl.BlockSpec((1, 1, d, tf), lambda i, j: (layer, i, 0, j)),
                           pl.BlockSpec((1, 1, d, tf), lambda i, j: (layer, i, 0, j)),
                           pl.BlockSpec((1, 1, tf, d), lambda i, j: (layer, i, j, 0))],
        out_specs=xspecs,
        out_shape=[jax.ShapeDtypeStruct(x.shape, BF16) for x in xs],
        scratch_shapes=[pltpu.VMEM((x.shape[1], d), F32) for x in xs]
        + [pltpu.VMEM((d, tf), BF16), pltpu.VMEM((d, tf), BF16), pltpu.VMEM((tf, d), BF16)],
        compiler_params=_params(("arbitrary", "arbitrary"), 60),
        name="expert_ffn",
    )(*xs, w_gate, w_up, w_down)


def _scatter_kernel(x_ref, y_ref, slot_ref, gate_ref, mod_ref, fn_ref, o_ref, *, final):
    tt = x_ref.shape[1]
    e, cap, _ = y_ref.shape
    eye = (_iota((tt, tt), 0) == _iota((tt, tt), 1)).astype(BF16)
    slot_t = _dot_nt(eye, (slot_ref[0] + 1.0).astype(BF16))
    gate_t = _dot_nt(eye, gate_ref[0].astype(BF16))
    lane = _iota((tt, cap), 1).astype(F32) + 1.0
    acc = None
    for i in range(e):
        pt = jnp.where(slot_t[:, i:i + 1] == lane, gate_t[:, i:i + 1], 0.0).astype(BF16)
        part = _dot(pt, y_ref[i])
        acc = part if acc is None else acc + part
    out = x_ref[0] + mod_ref[0][5:6] * acc
    if final:
        out = out * lax.rsqrt(jnp.mean(out * out, axis=-1, keepdims=True) + EPS) * fn_ref[...]
    o_ref[0] = out


def _scatter(x, y, slot, gate, mod, final_norm, cap, final):
    b, t, d = x.shape
    e = slot.shape[1]
    tt = min(t, 512)
    return pl.pallas_call(
        functools.partial(_scatter_kernel, final=final),
        grid=(b, t // tt),
        in_specs=[pl.BlockSpec((1, tt, d), lambda i, j: (i, j, 0)),
                  pl.BlockSpec((e, cap, d), lambda i, j: (0, i, 0)),
                  pl.BlockSpec((1, e, tt), lambda i, j: (i, 0, j)),
                  pl.BlockSpec((1, e, tt), lambda i, j: (i, 0, j)),
                  pl.BlockSpec((1, 8, d), lambda i, j: (i, 0, 0)),
                  pl.BlockSpec((1, d), lambda i, j: (0, 0))],
        out_specs=pl.BlockSpec((1, tt, d), lambda i, j: (i, j, 0)),
        out_shape=jax.ShapeDtypeStruct((b, t, d), F32),
        compiler_params=_params(("arbitrary", "arbitrary"), 48),
        name="scatter",
    )(x, y, slot, gate, mod, final_norm)


def _moe(streams, layer, w_gate, w_up, w_down, final_norm, final):
    routed = []
    for x1, hx2, logits, mod in streams:
        t = x1.shape[1]
        cap = CAPACITY_FACTOR * t // N_EXPERTS
        slot, gate, offs = _route(logits, cap)
        nblk = t // TOKEN_BLOCK
        if nblk > 1:
            offs = offs[:, :, :nblk + 1].astype(jnp.int32).reshape(offs.shape[0], -1)
            slot, gate = _by_token_block(slot), _by_token_block(gate)
            xg = _sparse_gather(hx2, slot, offs, cap)
        else:
            offs = None
            xg = _gather(hx2, slot, cap)
        routed.append((slot, gate, offs, cap, xg))
    ys = _expert_ffn([r[4] for r in routed], layer, w_gate, w_up, w_down)
    outs = []
    for (x1, _, _, mod), (slot, gate, offs, cap, _), y in zip(streams, routed, ys):
        if offs is None:
            outs.append(_scatter(x1, y, slot, gate, mod, final_norm, cap, final))
        else:
            outs.append(_sparse_scatter(x1, y, slot, gate, offs, mod, final_norm, cap, final))
    return outs


def _inproj_kernel(x_ref, mod_ref, nmix_ref, wko_ref, wqv_ref, wg_ref, bg_ref,
                   qt_ref, k_ref, vt_ref, o_ref, gr_ref, *, chunk, qk, dk):
    tt = x_ref.shape[1]
    mod = mod_ref[0]
    h = _norm_mod(x_ref[0], nmix_ref[...], mod[0:1], mod[1:2])
    hb = h.astype(BF16)
    p = _dot(hb, wko_ref[...])
    k_ref[0] = (p[:, :qk] * (dk ** -0.5)).astype(BF16)
    o_ref[0] = _sigmoid(p[:, qk:]).astype(o_ref.dtype)
    p_t = lax.dot_general(wqv_ref[...], hb, (((0,), (1,)), ((), ())), preferred_element_type=F32)
    for ci in range(tt // chunk):
        qt_ref[0, ci] = p_t[:qk, ci * chunk:(ci + 1) * chunk].astype(BF16)
        vt_ref[0, ci] = p_t[qk:, ci * chunk:(ci + 1) * chunk].astype(BF16)

    nh = MLSTM_HEADS
    g_t = _transpose_exact(_dot_split(h, wg_ref[...]) + bg_ref[...])
    row = _iota((4 * nh, chunk), 0)
    is_f = (row & nh) == nh
    a = _iota((chunk, chunk), 0)
    c = _iota((chunk, chunk), 1)
    tri = jnp.concatenate([(a <= c).astype(BF16), (a >= c).astype(BF16)], axis=1)
    ng = 4 * nh
    for c0 in range(0, tt, chunk):
        gc = g_t[:, c0:c0 + chunk]
        pieces = _split_bf16(jnp.where(is_f, _log_sigmoid(gc), 0.0), 3)
        cum = _dot(jnp.concatenate(pieces, axis=0), tri)
        cum = cum[:ng] + (cum[ng:2 * ng] + cum[2 * ng:])
        gr_ref[0, :, c0:c0 + chunk] = jnp.where(is_f, jnp.where(row < 2 * nh, cum[:, :chunk], cum[:, chunk:]), gc)


def _inproj(x, mod, nmix, w_ko, w_qv, w_g, b_g, qk, vd):
    b, t, d = x.shape
    tt = min(t, 1024)
    ng = w_g.shape[1]
    ln = SCAN_CHUNK
    full = lambda shape: pl.BlockSpec(shape, lambda i, j: (0,) * len(shape))
    tok = lambda n: pl.BlockSpec((1, tt, n), lambda i, j: (i, j, 0))
    slab = lambda n: pl.BlockSpec((1, tt // ln, n, ln), lambda i, j: (i, j, 0, 0))
    return pl.pallas_call(
        functools.partial(_inproj_kernel, chunk=ln, qk=qk, dk=qk // MLSTM_HEADS),
        grid=(b, t // tt),
        in_specs=[tok(d), pl.BlockSpec((1, 8, d), lambda i, j: (i, 0, 0)), full((1, d)),
                  full(w_ko.shape), full(w_qv.shape), full(w_g.shape), full(b_g.shape)],
        out_specs=[slab(qk), tok(qk), slab(vd), tok(vd),
                   pl.BlockSpec((1, ng, tt), lambda i, j: (i, 0, j))],
        out_shape=[jax.ShapeDtypeStruct((b, t // ln, qk, ln), BF16), jax.ShapeDtypeStruct((b, t, qk), BF16),
                   jax.ShapeDtypeStruct((b, t // ln, vd, ln), BF16), jax.ShapeDtypeStruct((b, t, vd), BF16),
                   jax.ShapeDtypeStruct((b, ng, t), F32)],
        compiler_params=_params(("arbitrary", "arbitrary"), 56),
        name="mlstm_inproj",
    )(x, mod, nmix, w_ko, w_qv, w_g, b_g)


def _scan_kernel(kc_ref, vc_ref, grc_ref, qx_ref, kx_ref, vx_ref, grx_ref,
                 out_ref, s_ref, sall_ref, mall_ref, *, chunk):
    ln = chunk
    nh = MLSTM_HEADS
    pair = pl.program_id(1)
    kl = kx_ref.shape[2]
    dv = vx_ref.shape[2] // 2
    lane = _iota((ln, kl), 1)
    kmask = (lane < kl // 2, lane >= kl // 2)
    klane = _iota((2 * dv, kl), 1)
    kcols = (klane < kl // 2, klane >= kl // 2)
    ones = jnp.ones((dv, ln), BF16)
    si = _iota((ln, ln), 0)
    ji = _iota((ln, ln), 1)
    causal = (si <= ji, si >= ji)
    ncc = kc_ref.shape[1] // ln
    ncx = kx_ref.shape[1] // ln

    def gate_rows(gr_ref, hh, dirn, c):
        base = 2 * nh * dirn + 2 * pair + hh
        return gr_ref[0, base, pl.ds(c, 1), :], gr_ref[0, base + nh, pl.ds(c, 1), :]

    def keys(k_ref, c):
        k = k_ref[0, pl.ds(pl.multiple_of(c * ln, ln), ln), :]
        return [jnp.where(kmask[hh], k, jnp.zeros((), BF16)) for hh in range(2)]

    def values_t(v_ref, c, hh):
        return jnp.concatenate([v_ref[0, c, hh * dv:(hh + 1) * dv, :], ones], axis=0)

    def advance(refs, n, record):
        k_ref, v_ref, gr_ref = refs

        def body(i, ms):
            new_ms = [None] * 4
            cs = (i, n - 1 - i)
            for dirn in range(2):
                c = cs[dirn]
                vws, decays = [], []
                for hh in range(2):
                    ch = 2 * hh + dirn
                    ig, bc = gate_rows(gr_ref, hh, dirn, c)
                    m = ms[ch]
                    if record:
                        s_own = jnp.where(kcols[hh], s_ref[ch], 0.0)
                        sall_ref[ch * ncx + c] = s_own.astype(BF16)
                        mall_ref[ch * ncx + c] = jnp.broadcast_to(m, (8, ln))
                    b_end = bc[:, ln - 1:ln] if dirn == 0 else bc[:, 0:1]
                    gl = b_end - bc + ig
                    m_new = jnp.maximum(b_end + m, jnp.max(gl, axis=1, keepdims=True))
                    vws.append((values_t(v_ref, c, hh) * jnp.exp(gl - m_new)).astype(BF16))
                    decays.append(jnp.exp(b_end + m - m_new))
                    new_ms[ch] = m_new
                upd = _dot(jnp.concatenate(vws, axis=0), k_ref[0, pl.ds(pl.multiple_of(c * ln, ln), ln), :])
                for hh in range(2):
                    ch = 2 * hh + dirn
                    s_ref[ch] = decays[hh] * s_ref[ch] + upd[hh * 2 * dv:(hh + 1) * 2 * dv, :]
            return tuple(new_ms)

        return body

    s_ref[...] = jnp.zeros(s_ref.shape, F32)
    ms = tuple(jnp.zeros((1, 1), F32) for _ in range(4))
    ms = lax.fori_loop(0, ncc, advance((kc_ref, vc_ref, grc_ref), ncc, False), ms, unroll=min(ncc, SCAN_UNROLL))
    lax.fori_loop(0, ncx, advance((kx_ref, vx_ref, grx_ref), ncx, True), ms, unroll=min(ncx, SCAN_UNROLL))

    def emit(c, _):
        q_t = qx_ref[0, c]
        q_tf = q_t.astype(F32)
        s_pair = _dot(jnp.concatenate(keys(kx_ref, c), axis=0), q_t)
        for hh in range(2):
            s_kq = s_pair[hh * ln:(hh + 1) * ln, :]
            v_t = values_t(vx_ref, c, hh)
            hsum = None
            for dirn in range(2):
                ch = 2 * hh + dirn
                ig, bc = gate_rows(grx_ref, hh, dirn, c)
                u_col = jnp.broadcast_to(ig - bc, (ln, ln)).T
                a = bc + mall_ref[ch * ncx + c][0:1, :]
                dm = jnp.where(causal[dirn], u_col + bc, -jnp.inf)
                mj = jnp.maximum(a, jnp.max(dm, axis=0, keepdims=True))
                sm = (s_kq * jnp.exp(dm - mj)).astype(BF16)
                qw = (q_tf * jnp.exp(a - mj)).astype(BF16)
                num = _dot(jnp.concatenate([v_t, sall_ref[ch * ncx + c]], axis=1),
                           jnp.concatenate([sm, qw], axis=0))
                h = num[:dv, :] * (1.0 / jnp.maximum(jnp.abs(num[dv:dv + 1, :]), jnp.exp(-mj)))
                hsum = h if hsum is None else hsum + h
            hn = hsum * lax.rsqrt(jnp.mean(hsum * hsum, axis=0, keepdims=True) + EPS)
            out_ref[0, pl.ds(pl.multiple_of(c * ln, ln), ln), hh * dv:(hh + 1) * dv] = hn.T.astype(out_ref.dtype)
        return 0

    lax.fori_loop(0, ncx, emit, 0, unroll=min(ncx, SCAN_UNROLL))


def _scan(ctx_parts, lat_parts):
    kc, vc, grc = ctx_parts
    qx, kx, vx, grx = lat_parts
    b, t, qk = kx.shape
    vd = vx.shape[2]
    nh = MLSTM_HEADS
    ln = SCAN_CHUNK
    dv = vd // nh
    kl = 2 * (qk // nh)
    assert kl == ln and dv == ln, "scan kernel assumes key-pair lanes = head value dim = chunk"
    ncx = t // ln

    def specs(k, v, gr):
        tk = k.shape[1]
        return [pl.BlockSpec((1, tk, kl), lambda i, j: (i, 0, j)),
                pl.BlockSpec((1, tk // ln, 2 * dv, ln), lambda i, j: (i, 0, j, 0)),
                pl.BlockSpec((1,) + gr.shape[1:], lambda i, j: (i, 0, 0, 0))]

    return pl.pallas_call(
        functools.partial(_scan_kernel, chunk=ln),
        grid=(b, nh // 2),
        in_specs=specs(kc, vc, grc) + [pl.BlockSpec((1, ncx, kl, ln), lambda i, j: (i, 0, j, 0))] + specs(kx, vx, grx),
        out_specs=pl.BlockSpec((1, t, 2 * dv), lambda i, j: (i, 0, j)),
        out_shape=jax.ShapeDtypeStruct((b, t, vd), BF16),
        scratch_shapes=[pltpu.VMEM((4, 2 * dv, kl), F32),
                        pltpu.VMEM((4 * ncx, 2 * dv, kl), BF16),
                        pltpu.VMEM((4 * ncx, 8, ln), F32)],
        compiler_params=_params(("arbitrary", "arbitrary"), 40),
        name="mlstm_scan",
    )(kc, vc, grc, qx, kx, vx, grx)


def _mlstm_out_kernel(hn_ref, og_ref, x_ref, mod_ref, mn_ref, wout_ref, nffn_ref, wr_ref,
                      x1_ref, hx2_ref, lg_ref):
    mod = mod_ref[0]
    a = (hn_ref[0].astype(F32) * mn_ref[...]) * og_ref[0].astype(F32)
    xn = x_ref[0] + mod[2:3] * _dot(a.astype(BF16), wout_ref[...])
    x1_ref[0] = xn
    hb, lg = _ffn_pre(xn, mod, nffn_ref[...], wr_ref[...])
    hx2_ref[0] = hb
    lg_ref[0] = lg


def _mlstm_out(hn, og, x, mod, mnorm, w_out, nffn, w_router):
    b, t, d = x.shape
    vd = hn.shape[2]
    e = w_router.shape[1]
    tt = min(t, 1024)
    full = lambda shape: pl.BlockSpec(shape, lambda i, j: (0,) * len(shape))
    tok = lambda n: pl.BlockSpec((1, tt, n), lambda i, j: (i, j, 0))
    return pl.pallas_call(
        _mlstm_out_kernel,
        grid=(b, t // tt),
        in_specs=[tok(vd), tok(vd), tok(d), pl.BlockSpec((1, 8, d), lambda i, j: (i, 0, 0)),
                  full((1, vd)), full(w_out.shape), full((1, d)), full(w_router.shape)],
        out_specs=[tok(d), tok(d), pl.BlockSpec((1, e, tt), lambda i, j: (i, 0, j))],
        out_shape=[jax.ShapeDtypeStruct((b, t, d), F32), jax.ShapeDtypeStruct((b, t, d), BF16),
                   jax.ShapeDtypeStruct((b, e, t), F32)],
        compiler_params=_params(("arbitrary", "arbitrary"), 48),
        name="mlstm_out",
    )(hn, og, x, mod, mnorm, w_out, nffn, w_router)


def _chunked_rows(gr):
    b, n, t = gr.shape
    return gr.reshape(b, n, t // SCAN_CHUNK, SCAN_CHUNK)


def kernel(x, c, ctx, c_ctx, ada_w, ada_b, norm_mix, norm_ffn, pool_w, pool_scale, mlstm_w_in, mlstm_b_gates,
           mlstm_norm, mlstm_w_out, moe_router, moe_w_gate, moe_w_up, moe_w_down, final_norm):
    bsz, seq, d = x.shape
    ctx_len = ctx.shape[1]
    depth = ada_w.shape[0]
    n_mixers = 2

    cc = jnp.concatenate([c, c_ctx[None, :], jnp.zeros((16 - bsz - 1, d), F32)], axis=0)
    ada = _ada(cc, ada_w, ada_b).reshape(depth, 16, N_ADA, d)
    pad = jnp.zeros((bsz, 8 - N_ADA, d), F32)

    fn = final_norm.reshape(1, d)
    for i in range(depth):
        last = i == depth - 1
        j = i // n_mixers
        mod_x = jnp.concatenate([ada[i, :bsz], pad], axis=1)
        mod_c = jnp.concatenate([jnp.broadcast_to(ada[i, bsz][None], (bsz, N_ADA, d)), pad], axis=1)
        nmix = norm_mix[i].reshape(1, d)
        nffn = norm_ffn[i].reshape(1, d)
        w_router = moe_router[i]
        streams = []
        if i % n_mixers == 0:
            pw = pool_w[j].astype(BF16)
            ps = pool_scale[j].reshape(1, d)
            x1, hx2, lg = _pool_mixer(x, mod_x, nmix, nffn, pw, ps, w_router, GRID_W, True)
            streams.append((x1, hx2, lg, mod_x))
            if not last:
                c1, hc2, lgc = _pool_mixer(ctx.reshape(1, bsz * ctx_len, d), mod_c[:1], nmix, nffn, pw, ps, w_router,
                                           ctx_len, False)
                lgc = jnp.swapaxes(lgc.reshape(-1, bsz, ctx_len), 0, 1)
                streams.append((c1.reshape(bsz, ctx_len, d), hc2.reshape(bsz, ctx_len, d), lgc, mod_c))
        else:
            qk = mlstm_w_in.shape[2] - 2 * mlstm_w_out.shape[1] - 4 * MLSTM_HEADS
            qk //= 2
            vd = mlstm_w_out.shape[1]
            w_in = mlstm_w_in[j]
            w_ko = jnp.concatenate([w_in[:, qk:2 * qk], w_in[:, 2 * qk + vd:2 * qk + 2 * vd]], axis=1).astype(BF16)
            w_qv = jnp.concatenate([w_in[:, :qk], w_in[:, 2 * qk:2 * qk + vd]], axis=1).astype(BF16)
            w_g = w_in[:, 2 * qk + 2 * vd:]
            b_g = mlstm_b_gates[j].reshape(1, -1)
            proj = lambda s, m: _inproj(s, m, nmix, w_ko, w_qv, w_g, b_g, qk, vd)
            _, kc, vc, _, grc = proj(ctx.reshape(1, bsz * ctx_len, d), mod_c[:1])
            kc = kc.reshape(bsz, ctx_len, qk)
            vc = vc.reshape(bsz, ctx_len // SCAN_CHUNK, vd, SCAN_CHUNK)
            grc = jnp.swapaxes(grc.reshape(-1, bsz, ctx_len), 0, 1)
            qx, kx, vx, ox, grx = proj(x, mod_x)
            hn = _scan((kc, vc, _chunked_rows(grc)), (qx, kx, vx, _chunked_rows(grx)))
            x1, hx2, lg = _mlstm_out(hn, ox, x, mod_x, mlstm_norm[j].reshape(1, vd),
                                     mlstm_w_out[j].astype(BF16), nffn, w_router)
            streams.append((x1, hx2, lg, mod_x))
            assert last, "context output of the mLSTM mixer is only needed by a following layer"
        outs = _moe(streams, i, moe_w_gate, moe_w_up, moe_w_down, fn, last)
        x = outs[0]
        if not last:
            ctx = outs[1]
    return x
```

```python
import functools

import jax
import jax.numpy as jnp
import numpy as np
from jax import lax
from jax.experimental import pallas as pl
from jax.experimental.pallas import tpu as pltpu

F32 = jnp.float32
BF16 = jnp.bfloat16
HIGHEST = lax.Precision.HIGHEST

GRID_W = 64
EPS = 1e-6
N_ADA = 6
POOL_WINDOWS = (2, 4, 8, 16)
N_POOL_GROUPS = 4
MLSTM_HEADS = 8
N_EXPERTS = 16
CAPACITY_FACTOR = 2

MIB = 1024 * 1024
MXU_DIM = 256
LANES = 128
TOKEN_BLOCK = 256
SLOT_GROUP = 32
PAIRS_PER_DOT = 16
SCAN_CHUNK = 128
SCAN_UNROLL = 8
POOL_PAD_ROWS = max(POOL_WINDOWS) // 2


def _params(sem, vmem_mib):
    return pltpu.CompilerParams(dimension_semantics=sem, vmem_limit_bytes=vmem_mib * MIB)


def _dot(a, b, precision=None):
    return jnp.dot(a, b, preferred_element_type=F32, precision=precision)


def _dot_nt(a, b, precision=None):
    return lax.dot_general(a, b, (((1,), (1,)), ((), ())), preferred_element_type=F32, precision=precision)


def _iota(shape, dim, dtype=jnp.int32):
    return lax.broadcasted_iota(dtype, shape, dim)


def _sigmoid(x):
    return 1.0 / (1.0 + jnp.exp(-x))


def _log_sigmoid(x):
    return jnp.minimum(x, 0.0) - jnp.log1p(jnp.exp(-jnp.abs(x)))


def _norm_mod(x, g, shift, scale):
    inv = lax.rsqrt(jnp.mean(x * x, axis=-1, keepdims=True) + EPS)
    return (x * inv) * (g * (1.0 + scale)) + shift


def _ada_kernel(c_ref, w_ref, b_ref, o_ref):
    c = c_ref[...]
    s = c * _sigmoid(c)
    s_hi, s_lo = _split_bf16(s, 2)
    w_hi, w_lo = _split_bf16(w_ref[0], 2)
    o_ref[0] = _dot(s_hi, w_hi) + (_dot(s_hi, w_lo) + _dot(s_lo, w_hi)) + b_ref[0]


def _ada(cc, ada_w, ada_b):
    depth, d, n = ada_w.shape
    rows = cc.shape[0]
    tn = n // 4
    return pl.pallas_call(
        _ada_kernel,
        grid=(depth, n // tn),
        in_specs=[pl.BlockSpec((rows, d), lambda i, j: (0, 0)),
                  pl.BlockSpec((1, d, tn), lambda i, j: (i, 0, j)),
                  pl.BlockSpec((1, 1, tn), lambda i, j: (i, 0, j))],
        out_specs=pl.BlockSpec((1, rows, tn), lambda i, j: (i, 0, j)),
        out_shape=jax.ShapeDtypeStruct((depth, rows, n), F32),
        compiler_params=_params(("arbitrary", "arbitrary"), 40),
        name="ada",
    )(cc, ada_w, ada_b.reshape(depth, 1, n))


def _split_bf16(x, pieces):
    out = []
    for _ in range(pieces):
        p = x.astype(BF16)
        out.append(p)
        x = x - p.astype(F32)
    return out


def _dot_split(a, b):
    n = b.shape[1]
    a_hi, a_lo = _split_bf16(a, 2)
    b_hi, b_lo = _split_bf16(b, 2)
    hi = _dot(a_hi, jnp.concatenate([b_hi, b_lo], axis=1))
    return hi[:, :n] + (hi[:, n:] + _dot(a_lo, b_hi))


def _transpose_exact(x):
    m = x.shape[1]
    eye = (_iota((m, m), 0) == _iota((m, m), 1)).astype(BF16)
    hi, mid, lo = _split_bf16(x, 3)
    return _dot_nt(eye, hi) + (_dot_nt(eye, mid) + _dot_nt(eye, lo))


def _ffn_pre(xn, mod, nffn, wr):
    h2 = _norm_mod(xn, nffn, mod[3:4], mod[4:5])
    return h2.astype(BF16), _transpose_exact(_dot_split(h2, wr))


def _pool_kernel(x_ref, mod_ref, nmix_ref, nffn_ref, cmat_ref, cnt_ref, pw_ref, ps_ref, wr_ref,
                 x1_ref, hx2_ref, lg_ref, *pad, two_d, tc):
    t, d = x_ref.shape[1], x_ref.shape[2]
    gd = d // N_POOL_GROUPS
    mod = mod_ref[0]
    pad_tok = POOL_PAD_ROWS * GRID_W

    for c0 in range(0, t, tc):
        x1_ref[0, c0:c0 + tc, :] = _norm_mod(x_ref[0, c0:c0 + tc, :], nmix_ref[...], mod[0:1], mod[1:2])

    def hx_of(r0, rn, j):
        return x1_ref[0, r0:r0 + rn, j * gd:(j + 1) * gd]

    if two_d:
        pad_ref, = pad
        pad_ref[0:pad_tok, :] = jnp.zeros((pad_tok, gd), F32)
        pad_ref[pad_tok + t:pad_tok + t + pad_tok, :] = jnp.zeros((pad_tok, gd), F32)

    for j, w in enumerate(POOL_WINDOWS):
        cs = slice(j * gd, (j + 1) * gd)
        cm = cmat_ref[j]
        sums = []
        for b0 in range(0, t, MXU_DIM):
            g = hx_of(b0, MXU_DIM, j)
            g_hi = g.astype(BF16)
            g_lo = (g - g_hi.astype(F32)).astype(BF16)
            csum = _dot(cm, g_hi) + _dot(cm, g_lo)
            if two_d:
                pad_ref[pad_tok + b0:pad_tok + b0 + MXU_DIM, :] = csum
            else:
                sums.append(csum)
        for c0 in range(0, t, tc):
            if two_d:
                tot = None
                for dr in range(-(w // 2), w - w // 2):
                    o = pad_tok + c0 + dr * GRID_W
                    sl = pad_ref[o:o + tc, :]
                    tot = sl if tot is None else tot + sl
            else:
                blocks = sums[c0 // MXU_DIM:(c0 + tc) // MXU_DIM]
                tot = blocks[0] if len(blocks) == 1 else jnp.concatenate(blocks, axis=0)
            mean = tot / cnt_ref[c0:c0 + tc, j:j + 1]
            diff = mean - hx_of(c0, tc, j)
            y = _dot(diff.astype(BF16), pw_ref[j]) * ps_ref[:, cs]
            x1_ref[0, c0:c0 + tc, cs] = x_ref[0, c0:c0 + tc, cs] + mod[2:3, cs] * y

    tail = min(t, 2 * tc)
    for c0 in range(0, t, tail):
        hb, lg = _ffn_pre(x1_ref[0, c0:c0 + tail, :], mod, nffn_ref[...], wr_ref[...])
        hx2_ref[0, c0:c0 + tail, :] = hb
        lg_ref[0, :, c0:c0 + tail] = lg


def _pool_consts(t, seq, two_d):
    cm = np.zeros((len(POOL_WINDOWS), MXU_DIM, MXU_DIM), np.float32)
    cnt = np.zeros((t, len(POOL_WINDOWS)), np.float32)
    n = seq
    assert MXU_DIM % n == 0
    pos = np.arange(MXU_DIM)
    for j, w in enumerate(POOL_WINDOWS):
        col = pos % n
        lo = np.clip(col - w // 2, 0, n)
        hi = np.clip(col + w - w // 2, 0, n)
        same = (pos[:, None] // n) == (pos[None, :] // n)
        cm[j] = (same & (col[None, :] >= lo[:, None]) & (col[None, :] < hi[:, None])).astype(np.float32)
        tt = np.arange(t)
        c = tt % n
        ccnt = np.clip(c + w - w // 2, 0, n) - np.clip(c - w // 2, 0, n)
        if two_d:
            rows = t // n
            r = tt // n
            rcnt = np.clip(r + w - w // 2, 0, rows) - np.clip(r - w // 2, 0, rows)
            cnt[:, j] = ccnt * rcnt
        else:
            cnt[:, j] = ccnt
    return jnp.asarray(cm, BF16), jnp.asarray(cnt)


def _pool_mixer(x, mod, nmix, nffn, pool_w, pool_scale, w_router, seq, two_d):
    b, t, d = x.shape
    gd = d // N_POOL_GROUPS
    assert seq == GRID_W or not two_d, "row-window shifts assume GRID_W tokens per grid row"
    cmat, cnt = _pool_consts(t, seq, two_d)
    tc = min(t, 512)
    pad_tok = POOL_PAD_ROWS * GRID_W
    scratch = []
    if two_d:
        scratch.append(pltpu.VMEM((t + 2 * pad_tok, gd), F32))
    e = w_router.shape[1]
    full = lambda shape: pl.BlockSpec(shape, lambda i: (0,) * len(shape))
    return pl.pallas_call(
        functools.partial(_pool_kernel, two_d=two_d, tc=tc),
        grid=(b,),
        in_specs=[pl.BlockSpec((1, t, d), lambda i: (i, 0, 0)),
                  pl.BlockSpec((1, 8, d), lambda i: (i, 0, 0)),
                  full((1, d)), full((1, d)), full(cmat.shape), full(cnt.shape),
                  full(pool_w.shape), full((1, d)), full(w_router.shape)],
        out_specs=[pl.BlockSpec((1, t, d), lambda i: (i, 0, 0)),
                   pl.BlockSpec((1, t, d), lambda i: (i, 0, 0)),
                   pl.BlockSpec((1, e, t), lambda i: (i, 0, 0))],
        out_shape=[jax.ShapeDtypeStruct((b, t, d), F32),
                   jax.ShapeDtypeStruct((b, t, d), BF16),
                   jax.ShapeDtypeStruct((b, e, t), F32)],
        scratch_shapes=scratch,
        compiler_params=_params(("arbitrary",), 60),
        name="pool_mixer_2d" if two_d else "pool_mixer_1d",
    )(x, mod, nmix, nffn, cmat, cnt, pool_w, pool_scale, w_router)


def _route_kernel(lg_ref, slot_ref, gate_ref, offs_ref, *, cap):
    nb, e, t = lg_ref.shape
    capf = jnp.float32(cap)
    affs = []
    for s in range(nb):
        lg = lg_ref[s]
        ex = jnp.exp(lg - jnp.max(lg, axis=0, keepdims=True))
        affs.append(ex / jnp.sum(ex, axis=0, keepdims=True))

    def as_f32(v):
        return lax.bitcast_convert_type(v, F32)

    def enough(aff, cand):
        return jnp.sum(jnp.where(aff >= as_f32(cand), 1.0, 0.0), axis=1, keepdims=True) >= capf

    def bisect2(i, vs):
        hi = jnp.left_shift(jnp.int32(1), 30 - 2 * i)
        lo = jnp.left_shift(jnp.int32(1), 29 - 2 * i)
        out = []
        for aff, v in zip(affs, vs):
            c_hi, c_lo, c_both = v | hi, v | lo, v | hi | lo
            out.append(jnp.where(enough(aff, c_both), c_both,
                                 jnp.where(enough(aff, c_hi), c_hi, jnp.where(enough(aff, c_lo), c_lo, v))))
        return tuple(out)

    kths = lax.fori_loop(0, 15, bisect2, tuple(jnp.zeros((e, 1), jnp.int32) for _ in range(nb)))

    blk = min(t, MXU_DIM)
    before = (_iota((blk, blk), 0) < _iota((blk, blk), 1)).astype(BF16)
    nl = offs_ref.shape[2]
    starts = (_iota((t, nl), 0) < _iota((t, nl), 1) * TOKEN_BLOCK).astype(BF16)

    def excl_cumsum(mask):
        ones = jnp.where(mask, 1.0, 0.0)
        outs, run = [], jnp.zeros((e, 1), F32)
        for b0 in range(0, t, blk):
            mb = ones[:, b0:b0 + blk]
            outs.append(_dot(mb.astype(BF16), before) + run)
            run = run + jnp.sum(mb, axis=1, keepdims=True)
        return jnp.concatenate(outs, axis=1) if len(outs) > 1 else outs[0]

    for s, (aff, kth) in enumerate(zip(affs, kths)):
        kth = jnp.where(enough(aff, kth | 1), kth | 1, kth)
        above = as_f32(kth + 1)
        gt = aff >= above
        eq = (aff >= as_f32(kth)) & jnp.logical_not(gt)
        need = capf - jnp.sum(jnp.where(gt, 1.0, 0.0), axis=1, keepdims=True)
        sel = gt | (eq & (excl_cumsum(eq) < need))
        slot_ref[s] = jnp.where(sel, excl_cumsum(sel), -1.0)
        gate_ref[s] = aff
        offs_ref[s] = _dot(jnp.where(sel, 1.0, 0.0).astype(BF16), starts)


def _route(logits, cap):
    b, e, t = logits.shape
    nb = 2 if b % 2 == 0 else 1
    spec = pl.BlockSpec((nb, e, t), lambda i: (i, 0, 0))
    ospec = pl.BlockSpec((nb, e, LANES), lambda i: (i, 0, 0))
    return pl.pallas_call(
        functools.partial(_route_kernel, cap=cap),
        grid=(b // nb,),
        in_specs=[spec],
        out_specs=[spec, spec, ospec],
        out_shape=[jax.ShapeDtypeStruct((b, e, t), F32)] * 2 + [jax.ShapeDtypeStruct((b, e, LANES), F32)],
        compiler_params=_params(("arbitrary",), 32),
        name="route",
    )(logits)


def _gather_kernel(h_ref, slot_ref, o_ref, *, cap):
    hx = h_ref[0]
    t = hx.shape[0]
    row = _iota((cap, t), 0).astype(F32)
    for i in range(o_ref.shape[0]):
        onehot = jnp.where(slot_ref[0, i:i + 1, :] == row, 1.0, 0.0).astype(BF16)
        o_ref[i] = _dot(onehot, hx).astype(BF16)


def _gather(hx2, slot, cap):
    b, t, d = hx2.shape
    e = slot.shape[1]
    eb = 8
    return pl.pallas_call(
        functools.partial(_gather_kernel, cap=cap),
        grid=(b, e // eb),
        in_specs=[pl.BlockSpec((1, t, d), lambda i, j: (i, 0, 0)),
                  pl.BlockSpec((1, eb, t), lambda i, j: (i, j, 0))],
        out_specs=pl.BlockSpec((eb, cap, d), lambda i, j: (j, i, 0)),
        out_shape=jax.ShapeDtypeStruct((e, b * cap, d), BF16),
        compiler_params=_params(("arbitrary", "arbitrary"), 48),
        name="gather",
    )(hx2, slot)


STATIC_DOTS = 2


def _pair_list(offs_ref, kb, n_exp, nblk, le_ref, lg_ref):
    b = pl.program_id(0)
    shift = SLOT_GROUP.bit_length() - 1
    max_groups = le_ref.shape[0] // (n_exp + 1)

    cnt = jnp.int32(0)
    for e in range(n_exp):
        o0 = offs_ref[b, e * (nblk + 1) + kb]
        o1 = offs_ref[b, e * (nblk + 1) + kb + 1]
        lo = jnp.right_shift(o0, shift)
        ng = jnp.where(o1 > o0, jnp.right_shift(o1 - 1, shift) - lo + 1, 0)
        for i in range(max_groups):
            le_ref[cnt + i] = e
            lg_ref[cnt + i] = lo + i
        cnt = cnt + ng
    return cnt


def _pair(le_ref, lg_ref, idx, cnt):
    valid = idx < cnt
    safe = jnp.minimum(idx, jnp.maximum(cnt - 1, 0))
    e_p = jnp.where(valid, le_ref[safe], 0)
    g_p = jnp.where(valid, lg_ref[safe], 0)
    base = jnp.where(valid, g_p * SLOT_GROUP, -2 * SLOT_GROUP).astype(F32)
    return e_p, pl.multiple_of(g_p * SLOT_GROUP, SLOT_GROUP), base


def _for_each_dot(cnt, body):
    for c in range(STATIC_DOTS):
        body(c, 0)
    lax.fori_loop(STATIC_DOTS, (cnt + PAIRS_PER_DOT - 1) // PAIRS_PER_DOT, body, 0)


def _sparse_gather_kernel(offs_ref, h_ref, slot_ref, o_ref, le_ref, lg_ref, p_ref, *, n_exp, nblk):
    sg = SLOT_GROUP
    o_ref[...] = jnp.zeros(o_ref.shape, o_ref.dtype)
    sub = _iota((sg, TOKEN_BLOCK), 0).astype(F32)

    def per_block(kb, _):
        cnt = _pair_list(offs_ref, kb, n_exp, nblk, le_ref, lg_ref)
        tok = pl.ds(pl.multiple_of(kb * TOKEN_BLOCK, TOKEN_BLOCK), TOKEN_BLOCK)

        def dot_batch(c, _):
            dst = []
            for p in range(PAIRS_PER_DOT):
                e_p, s0, base = _pair(le_ref, lg_ref, c * PAIRS_PER_DOT + p, cnt)
                row = slot_ref[0, kb, pl.ds(e_p, 1), :]
                p_ref[p * sg:(p + 1) * sg, :] = jnp.where(row == base + sub, 1.0, 0.0).astype(BF16)
                dst.append((e_p, s0))
            z = _dot(p_ref[...], h_ref[0, tok, :])
            for p, (e_p, s0) in enumerate(dst):
                o_ref[e_p, pl.ds(s0, sg), :] += z[p * sg:(p + 1) * sg, :].astype(o_ref.dtype)
            return 0

        _for_each_dot(cnt, dot_batch)
        return 0

    lax.fori_loop(0, nblk, per_block, 0)


def _by_token_block(a):
    b, e, t = a.shape
    return jnp.swapaxes(a.reshape(b, e, t // TOKEN_BLOCK, TOKEN_BLOCK), 1, 2)


def _sparse_gather(hx2, slot_blocks, offs, cap):
    b, t, d = hx2.shape
    _, nblk, e, _ = slot_blocks.shape
    return pl.pallas_call(
        functools.partial(_sparse_gather_kernel, n_exp=e, nblk=nblk),
        grid_spec=pltpu.PrefetchScalarGridSpec(
            num_scalar_prefetch=1,
            grid=(b,),
            in_specs=[pl.BlockSpec((1, t, d), lambda i, o: (i, 0, 0)),
                      pl.BlockSpec((1, nblk, e, TOKEN_BLOCK), lambda i, o: (i, 0, 0, 0))],
            out_specs=pl.BlockSpec((e, cap, d), lambda i, o: (0, i, 0)),
            scratch_shapes=[pltpu.SMEM(((e + 1) * (cap // SLOT_GROUP),), jnp.int32),
                            pltpu.SMEM(((e + 1) * (cap // SLOT_GROUP),), jnp.int32),
                            pltpu.VMEM((PAIRS_PER_DOT * SLOT_GROUP, TOKEN_BLOCK), BF16)]),
        out_shape=jax.ShapeDtypeStruct((e, b * cap, d), BF16),
        compiler_params=_params(("arbitrary",), 40),
        name="sparse_gather",
    )(offs, hx2, slot_blocks)


def _sparse_scatter_kernel(offs_ref, x_ref, y_ref, slot_ref, gate_ref, mod_ref, fn_ref, o_ref,
                           le_ref, lg_ref, p_ref, yc_ref, acc_ref, *, n_exp, nblk, bps, final):
    sg = SLOT_GROUP
    sub = _iota((sg, TOKEN_BLOCK), 0).astype(F32)

    def per_block(i, _):
        cnt = _pair_list(offs_ref, pl.program_id(1) * bps + i, n_exp, nblk, le_ref, lg_ref)
        tok = pl.ds(pl.multiple_of(i * TOKEN_BLOCK, TOKEN_BLOCK), TOKEN_BLOCK)
        acc_ref[...] = jnp.zeros(acc_ref.shape, F32)

        def dot_batch(c, _):
            for p in range(PAIRS_PER_DOT):
                e_p, s0, base = _pair(le_ref, lg_ref, c * PAIRS_PER_DOT + p, cnt)
                row = slot_ref[0, i, pl.ds(e_p, 1), :]
                gate = gate_ref[0, i, pl.ds(e_p, 1), :]
                p_ref[p * sg:(p + 1) * sg, :] = jnp.where(row == base + sub, gate, 0.0).astype(BF16)
                yc_ref[p * sg:(p + 1) * sg, :] = y_ref[e_p, pl.ds(s0, sg), :]
            acc_ref[...] += lax.dot_general(p_ref[...], yc_ref[...], (((0,), (0,)), ((), ())),
                                            preferred_element_type=F32)
            return 0

        _for_each_dot(cnt, dot_batch)
        out = x_ref[0, tok, :] + mod_ref[0][5:6] * acc_ref[...]
        if final:
            out = out * lax.rsqrt(jnp.mean(out * out, axis=-1, keepdims=True) + EPS) * fn_ref[...]
        o_ref[0, tok, :] = out
        return 0

    lax.fori_loop(0, bps, per_block, 0)


def _sparse_scatter(x, y, slot_blocks, gate_blocks, offs, mod, final_norm, cap, final):
    b, t, d = x.shape
    _, nblk, e, _ = slot_blocks.shape
    bps = min(nblk, 4)
    rows = PAIRS_PER_DOT * SLOT_GROUP
    tok = lambda n: pl.BlockSpec((1, bps * TOKEN_BLOCK, n), lambda i, j, o: (i, j, 0))
    exp = pl.BlockSpec((1, bps, e, TOKEN_BLOCK), lambda i, j, o: (i, j, 0, 0))
    return pl.pallas_call(
        functools.partial(_sparse_scatter_kernel, n_exp=e, nblk=nblk, bps=bps, final=final),
        grid_spec=pltpu.PrefetchScalarGridSpec(
            num_scalar_prefetch=1,
            grid=(b, nblk // bps),
            in_specs=[tok(d), pl.BlockSpec((e, cap, d), lambda i, j, o: (0, i, 0)), exp, exp,
                      pl.BlockSpec((1, 8, d), lambda i, j, o: (i, 0, 0)),
                      pl.BlockSpec((1, d), lambda i, j, o: (0, 0))],
            out_specs=tok(d),
            scratch_shapes=[pltpu.SMEM(((e + 1) * (cap // SLOT_GROUP),), jnp.int32),
                            pltpu.SMEM(((e + 1) * (cap // SLOT_GROUP),), jnp.int32),
                            pltpu.VMEM((rows, TOKEN_BLOCK), BF16),
                            pltpu.VMEM((rows, d), BF16),
                            pltpu.VMEM((TOKEN_BLOCK, d), F32)]),
        out_shape=jax.ShapeDtypeStruct((b, t, d), F32),
        compiler_params=_params(("arbitrary", "arbitrary"), 48),
        name="sparse_scatter",
    )(offs, x, y, slot_blocks, gate_blocks, mod, final_norm)


def _ffn_kernel(*refs, n_sets, mc):
    x_refs = refs[:n_sets]
    wg_ref, wu_ref, wd_ref = refs[n_sets:n_sets + 3]
    y_refs = refs[n_sets + 3:2 * n_sets + 3]
    acc_refs = refs[2 * n_sets + 3:3 * n_sets + 3]
    f = pl.program_id(1)
    last = f == pl.num_programs(1) - 1

    @pl.when((pl.program_id(0) == 0) & (f == 0))
    def _():
        for acc in acc_refs:
            acc[...] = jnp.zeros(acc.shape, F32)

    for x_ref, y_ref, acc in zip(x_refs, y_refs, acc_refs):
        m = x_ref.shape[1]
        step = min(m, mc)
        for m0 in range(0, m, step):
            xs = x_ref[0, m0:m0 + step, :].astype(F32)
            hg = _dot(xs, wg_ref[0, 0])
            hu = _dot(xs, wu_ref[0, 0])
            hid = hg * _sigmoid(hg) * hu
            tot = acc[m0:m0 + step, :] + _dot(hid, wd_ref[0, 0])
            acc[m0:m0 + step, :] = jnp.where(last, 0.0, tot)
            y_ref[0, m0:m0 + step, :] = tot.astype(BF16)


def _expert_ffn(xs, layer, w_gate, w_up, w_down):
    _, e, d, hidden = w_gate.shape
    tf = 512
    n = len(xs)
    xspecs = [pl.BlockSpec((1, x.shape[1], d), lambda i, j: (i, 0, 0)) for x in xs]
    return pl.pallas_call(
        functools.partial(_ffn_kernel, n_sets=n, mc=1024),
        grid=(e, hidden // tf),
        in_specs=xspecs + [pl.BlockSpec((1, 1, d, tf), lambda i, j: (layer, i, 0, j)),
                           pl.BlockSpec((1, 1, d, tf), lambda i, j: (layer, i, 0, j)),
                           pl.BlockSpec((1, 1, tf, d), lambda i, j: (layer, i, j, 0))],
        out_specs=xspecs,
        out_shape=[jax.ShapeDtypeStruct(x.shape, BF16) for x in xs],
        scratch_shapes=[pltpu.VMEM((x.shape[1], d), F32) for x in xs],
        compiler_params=_params(("arbitrary", "arbitrary"), 60),
        name="expert_ffn",
    )(*xs, w_gate, w_up, w_down)


def _scatter_kernel(x_ref, y_ref, slot_ref, gate_ref, mod_ref, fn_ref, o_ref, *, final):
    tt = x_ref.shape[1]
    e, cap, _ = y_ref.shape
    eye = (_iota((tt, tt), 0) == _iota((tt, tt), 1)).astype(BF16)
    slot_t = _dot_nt(eye, (slot_ref[0] + 1.0).astype(BF16))
    gate_t = _dot_nt(eye, gate_ref[0].astype(BF16))
    lane = _iota((tt, cap), 1).astype(F32) + 1.0
    acc = None
    for i in range(e):
        pt = jnp.where(slot_t[:, i:i + 1] == lane, gate_t[:, i:i + 1], 0.0).astype(BF16)
        part = _dot(pt, y_ref[i])
        acc = part if acc is None else acc + part
    out = x_ref[0] + mod_ref[0][5:6] * acc
    if final:
        out = out * lax.rsqrt(jnp.mean(out * out, axis=-1, keepdims=True) + EPS) * fn_ref[...]
    o_ref[0] = out


def _scatter(x, y, slot, gate, mod, final_norm, cap, final):
    b, t, d = x.shape
    e = slot.shape[1]
    tt = min(t, 512)
    return pl.pallas_call(
        functools.partial(_scatter_kernel, final=final),
        grid=(b, t // tt),
        in_specs=[pl.BlockSpec((1, tt, d), lambda i, j: (i, j, 0)),
                  pl.BlockSpec((e, cap, d), lambda i, j: (0, i, 0)),
                  pl.BlockSpec((1, e, tt), lambda i, j: (i, 0, j)),
                  pl.BlockSpec((1, e, tt), lambda i, j: (i, 0, j)),
                  pl.BlockSpec((1, 8, d), lambda i, j: (i, 0, 0)),
                  pl.BlockSpec((1, d), lambda i, j: (0, 0))],
        out_specs=pl.BlockSpec((1, tt, d), lambda i, j: (i, j, 0)),
        out_shape=jax.ShapeDtypeStruct((b, t, d), F32),
        compiler_params=_params(("arbitrary", "arbitrary"), 48),
        name="scatter",
    )(x, y, slot, gate, mod, final_norm)


def _moe(streams, layer, w_gate, w_up, w_down, final_norm, final):
    routed = []
    for x1, hx2, logits, mod in streams:
        t = x1.shape[1]
        cap = CAPACITY_FACTOR * t // N_EXPERTS
        slot, gate, offs = _route(logits, cap)
        nblk = t // TOKEN_BLOCK
        if nblk > 1:
            offs = offs[:, :, :nblk + 1].astype(jnp.int32).reshape(offs.shape[0], -1)
            slot, gate = _by_token_block(slot), _by_token_block(gate)
            xg = _sparse_gather(hx2, slot, offs, cap)
        else:
            offs = None
            xg = _gather(hx2, slot, cap)
        routed.append((slot, gate, offs, cap, xg))
    ys = _expert_ffn([r[4] for r in routed], layer, w_gate, w_up, w_down)
    outs = []
    for (x1, _, _, mod), (slot, gate, offs, cap, _), y in zip(streams, routed, ys):
        if offs is None:
            outs.append(_scatter(x1, y, slot, gate, mod, final_norm, cap, final))
        else:
            outs.append(_sparse_scatter(x1, y, slot, gate, offs, mod, final_norm, cap, final))
    return outs


def _inproj_kernel(x_ref, mod_ref, nmix_ref, wko_ref, wqv_ref, wg_ref, bg_ref,
                   qt_ref, k_ref, vt_ref, o_ref, gr_ref, *, chunk, qk, dk):
    tt = x_ref.shape[1]
    mod = mod_ref[0]
    h = _norm_mod(x_ref[0], nmix_ref[...], mod[0:1], mod[1:2])
    hb = h.astype(BF16)
    p = _dot(hb, wko_ref[...])
    k_ref[0] = (p[:, :qk] * (dk ** -0.5)).astype(BF16)
    o_ref[0] = _sigmoid(p[:, qk:]).astype(o_ref.dtype)
    p_t = lax.dot_general(wqv_ref[...], hb, (((0,), (1,)), ((), ())), preferred_element_type=F32)
    for ci in range(tt // chunk):
        qt_ref[0, ci] = p_t[:qk, ci * chunk:(ci + 1) * chunk].astype(BF16)
        vt_ref[0, ci] = p_t[qk:, ci * chunk:(ci + 1) * chunk].astype(BF16)

    nh = MLSTM_HEADS
    g_t = _transpose_exact(_dot_split(h, wg_ref[...]) + bg_ref[...])
    row = _iota((4 * nh, chunk), 0)
    is_f = (row & nh) == nh
    a = _iota((chunk, chunk), 0)
    c = _iota((chunk, chunk), 1)
    tri = jnp.concatenate([(a <= c).astype(BF16), (a >= c).astype(BF16)], axis=1)
    ng = 4 * nh
    for c0 in range(0, tt, chunk):
        gc = g_t[:, c0:c0 + chunk]
        pieces = _split_bf16(jnp.where(is_f, _log_sigmoid(gc), 0.0), 3)
        cum = _dot(jnp.concatenate(pieces, axis=0), tri)
        cum = cum[:ng] + (cum[ng:2 * ng] + cum[2 * ng:])
        gr_ref[0, :, c0:c0 + chunk] = jnp.where(is_f, jnp.where(row < 2 * nh, cum[:, :chunk], cum[:, chunk:]), gc)


def _inproj(x, mod, nmix, w_ko, w_qv, w_g, b_g, qk, vd):
    b, t, d = x.shape
    tt = min(t, 1024)
    ng = w_g.shape[1]
    ln = SCAN_CHUNK
    full = lambda shape: pl.BlockSpec(shape, lambda i, j: (0,) * len(shape))
    tok = lambda n: pl.BlockSpec((1, tt, n), lambda i, j: (i, j, 0))
    slab = lambda n: pl.BlockSpec((1, tt // ln, n, ln), lambda i, j: (i, j, 0, 0))
    return pl.pallas_call(
        functools.partial(_inproj_kernel, chunk=ln, qk=qk, dk=qk // MLSTM_HEADS),
        grid=(b, t // tt),
        in_specs=[tok(d), pl.BlockSpec((1, 8, d), lambda i, j: (i, 0, 0)), full((1, d)),
                  full(w_ko.shape), full(w_qv.shape), full(w_g.shape), full(b_g.shape)],
        out_specs=[slab(qk), tok(qk), slab(vd), tok(vd),
                   pl.BlockSpec((1, ng, tt), lambda i, j: (i, 0, j))],
        out_shape=[jax.ShapeDtypeStruct((b, t // ln, qk, ln), BF16), jax.ShapeDtypeStruct((b, t, qk), BF16),
                   jax.ShapeDtypeStruct((b, t // ln, vd, ln), BF16), jax.ShapeDtypeStruct((b, t, vd), BF16),
                   jax.ShapeDtypeStruct((b, ng, t), F32)],
        compiler_params=_params(("arbitrary", "arbitrary"), 56),
        name="mlstm_inproj",
    )(x, mod, nmix, w_ko, w_qv, w_g, b_g)


def _scan_kernel(kc_ref, vc_ref, grc_ref, qx_ref, kx_ref, vx_ref, grx_ref,
                 out_ref, s_ref, sall_ref, mall_ref, *, chunk):
    ln = chunk
    nh = MLSTM_HEADS
    pair = pl.program_id(1)
    kl = kx_ref.shape[2]
    dv = vx_ref.shape[2] // 2
    lane = _iota((ln, kl), 1)
    kmask = (lane < kl // 2, lane >= kl // 2)
    klane = _iota((2 * dv, kl), 1)
    kcols = (klane < kl // 2, klane >= kl // 2)
    ones = jnp.ones((dv, ln), BF16)
    si = _iota((ln, ln), 0)
    ji = _iota((ln, ln), 1)
    causal = (si <= ji, si >= ji)
    ncc = kc_ref.shape[1] // ln
    ncx = kx_ref.shape[1] // ln

    def gate_rows(gr_ref, hh, dirn, c):
        base = 2 * nh * dirn + 2 * pair + hh
        return gr_ref[0, base, pl.ds(c, 1), :], gr_ref[0, base + nh, pl.ds(c, 1), :]

    def keys(k_ref, c):
        k = k_ref[0, pl.ds(pl.multiple_of(c * ln, ln), ln), :]
        return [jnp.where(kmask[hh], k, jnp.zeros((), BF16)) for hh in range(2)]

    def values_t(v_ref, c, hh):
        return jnp.concatenate([v_ref[0, c, hh * dv:(hh + 1) * dv, :], ones], axis=0)

    def advance(refs, n, record):
        k_ref, v_ref, gr_ref = refs

        def body(i, ms):
            new_ms = [None] * 4
            cs = (i, n - 1 - i)
            for dirn in range(2):
                c = cs[dirn]
                vws, decays = [], []
                for hh in range(2):
                    ch = 2 * hh + dirn
                    ig, bc = gate_rows(gr_ref, hh, dirn, c)
                    m = ms[ch]
                    if record:
                        s_own = jnp.where(kcols[hh], s_ref[ch], 0.0)
                        sall_ref[ch * ncx + c] = s_own.astype(BF16)
                        mall_ref[ch * ncx + c] = jnp.broadcast_to(m, (8, ln))
                    b_end = bc[:, ln - 1:ln] if dirn == 0 else bc[:, 0:1]
                    gl = b_end - bc + ig
                    m_new = jnp.maximum(b_end + m, jnp.max(gl, axis=1, keepdims=True))
                    vws.append((values_t(v_ref, c, hh) * jnp.exp(gl - m_new)).astype(BF16))
                    decays.append(jnp.exp(b_end + m - m_new))
                    new_ms[ch] = m_new
                upd = _dot(jnp.concatenate(vws, axis=0), k_ref[0, pl.ds(pl.multiple_of(c * ln, ln), ln), :])
                for hh in range(2):
                    ch = 2 * hh + dirn
                    s_ref[ch] = decays[hh] * s_ref[ch] + upd[hh * 2 * dv:(hh + 1) * 2 * dv, :]
            return tuple(new_ms)

        return body

    s_ref[...] = jnp.zeros(s_ref.shape, F32)
    ms = tuple(jnp.zeros((1, 1), F32) for _ in range(4))
    ms = lax.fori_loop(0, ncc, advance((kc_ref, vc_ref, grc_ref), ncc, False), ms, unroll=min(ncc, SCAN_UNROLL))
    lax.fori_loop(0, ncx, advance((kx_ref, vx_ref, grx_ref), ncx, True), ms, unroll=min(ncx, SCAN_UNROLL))

    def emit(c, _):
        q_t = qx_ref[0, c]
        q_tf = q_t.astype(F32)
        s_pair = _dot(jnp.concatenate(keys(kx_ref, c), axis=0), q_t)
        for hh in range(2):
            s_kq = s_pair[hh * ln:(hh + 1) * ln, :]
            v_t = values_t(vx_ref, c, hh)
            hsum = None
            for dirn in range(2):
                ch = 2 * hh + dirn
                ig, bc = gate_rows(grx_ref, hh, dirn, c)
                u_col = jnp.broadcast_to(ig - bc, (ln, ln)).T
                a = bc + mall_ref[ch * ncx + c][0:1, :]
                dm = jnp.where(causal[dirn], u_col + bc, -jnp.inf)
                mj = jnp.maximum(a, jnp.max(dm, axis=0, keepdims=True))
                sm = (s_kq * jnp.exp(dm - mj)).astype(BF16)
                qw = (q_tf * jnp.exp(a - mj)).astype(BF16)
                num = _dot(jnp.concatenate([v_t, sall_ref[ch * ncx + c]], axis=1),
                           jnp.concatenate([sm, qw], axis=0))
                h = num[:dv, :] * (1.0 / jnp.maximum(jnp.abs(num[dv:dv + 1, :]), jnp.exp(-mj)))
                hsum = h if hsum is None else hsum + h
            hn = hsum * lax.rsqrt(jnp.mean(hsum * hsum, axis=0, keepdims=True) + EPS)
            out_ref[0, pl.ds(pl.multiple_of(c * ln, ln), ln), hh * dv:(hh + 1) * dv] = hn.T.astype(out_ref.dtype)
        return 0

    lax.fori_loop(0, ncx, emit, 0, unroll=min(ncx, SCAN_UNROLL))


def _scan(ctx_parts, lat_parts):
    kc, vc, grc = ctx_parts
    qx, kx, vx, grx = lat_parts
    b, t, qk = kx.shape
    vd = vx.shape[2]
    nh = MLSTM_HEADS
    ln = SCAN_CHUNK
    dv = vd // nh
    kl = 2 * (qk // nh)
    assert kl == ln and dv == ln, "scan kernel assumes key-pair lanes = head value dim = chunk"
    ncx = t // ln

    def specs(k, v, gr):
        tk = k.shape[1]
        return [pl.BlockSpec((1, tk, kl), lambda i, j: (i, 0, j)),
                pl.BlockSpec((1, tk // ln, 2 * dv, ln), lambda i, j: (i, 0, j, 0)),
                pl.BlockSpec((1,) + gr.shape[1:], lambda i, j: (i, 0, 0, 0))]

    return pl.pallas_call(
        functools.partial(_scan_kernel, chunk=ln),
        grid=(b, nh // 2),
        in_specs=specs(kc, vc, grc) + [pl.BlockSpec((1, ncx, kl, ln), lambda i, j: (i, 0, j, 0))] + specs(kx, vx, grx),
        out_specs=pl.BlockSpec((1, t, 2 * dv), lambda i, j: (i, 0, j)),
        out_shape=jax.ShapeDtypeStruct((b, t, vd), BF16),
        scratch_shapes=[pltpu.VMEM((4, 2 * dv, kl), F32),
                        pltpu.VMEM((4 * ncx, 2 * dv, kl), BF16),
                        pltpu.VMEM((4 * ncx, 8, ln), F32)],
        compiler_params=_params(("arbitrary", "arbitrary"), 40),
        name="mlstm_scan",
    )(kc, vc, grc, qx, kx, vx, grx)


def _mlstm_out_kernel(hn_ref, og_ref, x_ref, mod_ref, mn_ref, wout_ref, nffn_ref, wr_ref,
                      x1_ref, hx2_ref, lg_ref):
    mod = mod_ref[0]
    a = (hn_ref[0].astype(F32) * mn_ref[...]) * og_ref[0].astype(F32)
    xn = x_ref[0] + mod[2:3] * _dot(a.astype(BF16), wout_ref[...])
    x1_ref[0] = xn
    hb, lg = _ffn_pre(xn, mod, nffn_ref[...], wr_ref[...])
    hx2_ref[0] = hb
    lg_ref[0] = lg


def _mlstm_out(hn, og, x, mod, mnorm, w_out, nffn, w_router):
    b, t, d = x.shape
    vd = hn.shape[2]
    e = w_router.shape[1]
    tt = min(t, 1024)
    full = lambda shape: pl.BlockSpec(shape, lambda i, j: (0,) * len(shape))
    tok = lambda n: pl.BlockSpec((1, tt, n), lambda i, j: (i, j, 0))
    return pl.pallas_call(
        _mlstm_out_kernel,
        grid=(b, t // tt),
        in_specs=[tok(vd), tok(vd), tok(d), pl.BlockSpec((1, 8, d), lambda i, j: (i, 0, 0)),
                  full((1, vd)), full(w_out.shape), full((1, d)), full(w_router.shape)],
        out_specs=[tok(d), tok(d), pl.BlockSpec((1, e, tt), lambda i, j: (i, 0, j))],
        out_shape=[jax.ShapeDtypeStruct((b, t, d), F32), jax.ShapeDtypeStruct((b, t, d), BF16),
                   jax.ShapeDtypeStruct((b, e, t), F32)],
        compiler_params=_params(("arbitrary", "arbitrary"), 48),
        name="mlstm_out",
    )(hn, og, x, mod, mnorm, w_out, nffn, w_router)


def _chunked_rows(gr):
    b, n, t = gr.shape
    return gr.reshape(b, n, t // SCAN_CHUNK, SCAN_CHUNK)


def kernel(x, c, ctx, c_ctx, ada_w, ada_b, norm_mix, norm_ffn, pool_w, pool_scale, mlstm_w_in, mlstm_b_gates,
           mlstm_norm, mlstm_w_out, moe_router, moe_w_gate, moe_w_up, moe_w_down, final_norm):
    bsz, seq, d = x.shape
    ctx_len = ctx.shape[1]
    depth = ada_w.shape[0]
    n_mixers = 2

    cc = jnp.concatenate([c, c_ctx[None, :], jnp.zeros((16 - bsz - 1, d), F32)], axis=0)
    ada = _ada(cc, ada_w, ada_b).reshape(depth, 16, N_ADA, d)
    pad = jnp.zeros((bsz, 8 - N_ADA, d), F32)

    fn = final_norm.reshape(1, d)
    for i in range(depth):
        last = i == depth - 1
        j = i // n_mixers
        mod_x = jnp.concatenate([ada[i, :bsz], pad], axis=1)
        mod_c = jnp.concatenate([jnp.broadcast_to(ada[i, bsz][None], (bsz, N_ADA, d)), pad], axis=1)
        nmix = norm_mix[i].reshape(1, d)
        nffn = norm_ffn[i].reshape(1, d)
        w_router = moe_router[i]
        streams = []
        if i % n_mixers == 0:
            pw = pool_w[j].astype(BF16)
            ps = pool_scale[j].reshape(1, d)
            x1, hx2, lg = _pool_mixer(x, mod_x, nmix, nffn, pw, ps, w_router, GRID_W, True)
            streams.append((x1, hx2, lg, mod_x))
            if not last:
                c1, hc2, lgc = _pool_mixer(ctx.reshape(1, bsz * ctx_len, d), mod_c[:1], nmix, nffn, pw, ps, w_router,
                                           ctx_len, False)
                lgc = jnp.swapaxes(lgc.reshape(-1, bsz, ctx_len), 0, 1)
                streams.append((c1.reshape(bsz, ctx_len, d), hc2.reshape(bsz, ctx_len, d), lgc, mod_c))
        else:
            qk = mlstm_w_in.shape[2] - 2 * mlstm_w_out.shape[1] - 4 * MLSTM_HEADS
            qk //= 2
            vd = mlstm_w_out.shape[1]
            w_in = mlstm_w_in[j]
            w_ko = jnp.concatenate([w_in[:, qk:2 * qk], w_in[:, 2 * qk + vd:2 * qk + 2 * vd]], axis=1).astype(BF16)
            w_qv = jnp.concatenate([w_in[:, :qk], w_in[:, 2 * qk:2 * qk + vd]], axis=1).astype(BF16)
            w_g = w_in[:, 2 * qk + 2 * vd:]
            b_g = mlstm_b_gates[j].reshape(1, -1)
            proj = lambda s, m: _inproj(s, m, nmix, w_ko, w_qv, w_g, b_g, qk, vd)
            _, kc, vc, _, grc = proj(ctx.reshape(1, bsz * ctx_len, d), mod_c[:1])
            kc = kc.reshape(bsz, ctx_len, qk)
            vc = vc.reshape(bsz, ctx_len // SCAN_CHUNK, vd, SCAN_CHUNK)
            grc = jnp.swapaxes(grc.reshape(-1, bsz, ctx_len), 0, 1)
            qx, kx, vx, ox, grx = proj(x, mod_x)
            hn = _scan((kc, vc, _chunked_rows(grc)), (qx, kx, vx, _chunked_rows(grx)))
            x1, hx2, lg = _mlstm_out(hn, ox, x, mod_x, mlstm_norm[j].reshape(1, vd),
                                     mlstm_w_out[j].astype(BF16), nffn, w_router)
            streams.append((x1, hx2, lg, mod_x))
            assert last, "context output of the mLSTM mixer is only needed by a following layer"
        outs = _moe(streams, i, moe_w_gate, moe_w_up, moe_w_down, fn, last)
        x = outs[0]
        if not last:
            ctx = outs[1]
    return x
```

```python
import functools

import jax
import jax.numpy as jnp
import numpy as np
from jax import lax
from jax.experimental import pallas as pl
from jax.experimental.pallas import tpu as pltpu

F32 = jnp.float32
BF16 = jnp.bfloat16

GRID_W = 64
EPS = 1e-6
N_ADA = 6
POOL_WINDOWS = (2, 4, 8, 16)
N_POOL_GROUPS = 4
MLSTM_HEADS = 8
N_EXPERTS = 16
CAPACITY_FACTOR = 2

MIB = 1024 * 1024
MXU_DIM = 256
LANES = 128
TOKEN_BLOCK = 256
SLOT_GROUP = 32
PAIRS_PER_DOT = 16
STATIC_DOTS = 2
SCAN_CHUNK = 128
SCAN_UNROLL = 8
POOL_PAD_ROWS = max(POOL_WINDOWS) // 2


def _params(sem, vmem_mib):
    return pltpu.CompilerParams(dimension_semantics=sem, vmem_limit_bytes=vmem_mib * MIB)


def _dot(a, b):
    return jnp.dot(a, b, preferred_element_type=F32)


def _dot_nt(a, b):
    return lax.dot_general(a, b, (((1,), (1,)), ((), ())), preferred_element_type=F32)


def _iota(shape, dim, dtype=jnp.int32):
    return lax.broadcasted_iota(dtype, shape, dim)


def _sigmoid(x):
    return 1.0 / (1.0 + jnp.exp(-x))


def _log_sigmoid(x):
    return jnp.minimum(x, 0.0) - jnp.log1p(jnp.exp(-jnp.abs(x)))


def _norm_mod(x, g, shift, scale):
    inv = lax.rsqrt(jnp.mean(x * x, axis=-1, keepdims=True) + EPS)
    return (x * inv) * (g * (1.0 + scale)) + shift


def _ada_kernel(c_ref, w_ref, b_ref, o_ref):
    c = c_ref[...]
    s = c * _sigmoid(c)
    s_hi, s_lo = _split_bf16(s, 2)
    w_hi, w_lo = _split_bf16(w_ref[0], 2)
    o_ref[0] = _dot(s_hi, w_hi) + (_dot(s_hi, w_lo) + _dot(s_lo, w_hi)) + b_ref[0]


def _ada(cc, ada_w, ada_b):
    depth, d, n = ada_w.shape
    rows = cc.shape[0]
    tn = n // 4
    return pl.pallas_call(
        _ada_kernel,
        grid=(depth, n // tn),
        in_specs=[pl.BlockSpec((rows, d), lambda i, j: (0, 0)),
                  pl.BlockSpec((1, d, tn), lambda i, j: (i, 0, j)),
                  pl.BlockSpec((1, 1, tn), lambda i, j: (i, 0, j))],
        out_specs=pl.BlockSpec((1, rows, tn), lambda i, j: (i, 0, j)),
        out_shape=jax.ShapeDtypeStruct((depth, rows, n), F32),
        compiler_params=_params(("arbitrary", "arbitrary"), 40),
        name="ada",
    )(cc, ada_w, ada_b.reshape(depth, 1, n))


def _split_bf16(x, pieces):
    out = []
    for _ in range(pieces):
        p = x.astype(BF16)
        out.append(p)
        x = x - p.astype(F32)
    return out


def _dot_split(a, b):
    n = b.shape[1]
    a_hi, a_lo = _split_bf16(a, 2)
    b_hi, b_lo = _split_bf16(b, 2)
    hi = _dot(a_hi, jnp.concatenate([b_hi, b_lo], axis=1))
    return hi[:, :n] + (hi[:, n:] + _dot(a_lo, b_hi))


def _transpose_exact(x):
    m = x.shape[1]
    eye = (_iota((m, m), 0) == _iota((m, m), 1)).astype(BF16)
    hi, mid, lo = _split_bf16(x, 3)
    return _dot_nt(eye, hi) + (_dot_nt(eye, mid) + _dot_nt(eye, lo))


def _ffn_pre(xn, mod, nffn, wr):
    h2 = _norm_mod(xn, nffn, mod[3:4], mod[4:5])
    return h2.astype(BF16), _transpose_exact(_dot_split(h2, wr))


def _pool_kernel(x_ref, mod_ref, nmix_ref, nffn_ref, cmat_ref, cnt_ref, pw_ref, ps_ref, wr_ref,
                 x1_ref, hx2_ref, lg_ref, *pad, two_d, tc):
    t, d = x_ref.shape[1], x_ref.shape[2]
    gd = d // N_POOL_GROUPS
    mod = mod_ref[0]
    pad_tok = POOL_PAD_ROWS * GRID_W

    for c0 in range(0, t, tc):
        x1_ref[0, c0:c0 + tc, :] = _norm_mod(x_ref[0, c0:c0 + tc, :], nmix_ref[...], mod[0:1], mod[1:2])

    def hx_of(r0, rn, j):
        return x1_ref[0, r0:r0 + rn, j * gd:(j + 1) * gd]

    if two_d:
        pad_ref, = pad
        pad_ref[0:pad_tok, :] = jnp.zeros((pad_tok, gd), F32)
        pad_ref[pad_tok + t:pad_tok + t + pad_tok, :] = jnp.zeros((pad_tok, gd), F32)

    for j, w in enumerate(POOL_WINDOWS):
        cs = slice(j * gd, (j + 1) * gd)
        cm = cmat_ref[j]
        sums = []
        for b0 in range(0, t, MXU_DIM):
            g = hx_of(b0, MXU_DIM, j)
            g_hi = g.astype(BF16)
            g_lo = (g - g_hi.astype(F32)).astype(BF16)
            csum = _dot(cm, g_hi) + _dot(cm, g_lo)
            if two_d:
                pad_ref[pad_tok + b0:pad_tok + b0 + MXU_DIM, :] = csum
            else:
                sums.append(csum)
        for c0 in range(0, t, tc):
            if two_d:
                tot = None
                for dr in range(-(w // 2), w - w // 2):
                    o = pad_tok + c0 + dr * GRID_W
                    sl = pad_ref[o:o + tc, :]
                    tot = sl if tot is None else tot + sl
            else:
                blocks = sums[c0 // MXU_DIM:(c0 + tc) // MXU_DIM]
                tot = blocks[0] if len(blocks) == 1 else jnp.concatenate(blocks, axis=0)
            mean = tot / cnt_ref[c0:c0 + tc, j:j + 1]
            diff = mean - hx_of(c0, tc, j)
            y = _dot(diff.astype(BF16), pw_ref[j]) * ps_ref[:, cs]
            x1_ref[0, c0:c0 + tc, cs] = x_ref[0, c0:c0 + tc, cs] + mod[2:3, cs] * y

    tail = min(t, 2 * tc)
    for c0 in range(0, t, tail):
        hb, lg = _ffn_pre(x1_ref[0, c0:c0 + tail, :], mod, nffn_ref[...], wr_ref[...])
        hx2_ref[0, c0:c0 + tail, :] = hb
        lg_ref[0, :, c0:c0 + tail] = lg


def _pool_consts(t, seq, two_d):
    cm = np.zeros((len(POOL_WINDOWS), MXU_DIM, MXU_DIM), np.float32)
    cnt = np.zeros((t, len(POOL_WINDOWS)), np.float32)
    n = seq
    assert MXU_DIM % n == 0
    pos = np.arange(MXU_DIM)
    for j, w in enumerate(POOL_WINDOWS):
        col = pos % n
        lo = np.clip(col - w // 2, 0, n)
        hi = np.clip(col + w - w // 2, 0, n)
        same = (pos[:, None] // n) == (pos[None, :] // n)
        cm[j] = (same & (col[None, :] >= lo[:, None]) & (col[None, :] < hi[:, None])).astype(np.float32)
        tt = np.arange(t)
        c = tt % n
        ccnt = np.clip(c + w - w // 2, 0, n) - np.clip(c - w // 2, 0, n)
        if two_d:
            rows = t // n
            r = tt // n
            rcnt = np.clip(r + w - w // 2, 0, rows) - np.clip(r - w // 2, 0, rows)
            cnt[:, j] = ccnt * rcnt
        else:
            cnt[:, j] = ccnt
    return jnp.asarray(cm, BF16), jnp.asarray(cnt)


def _pool_mixer(x, mod, nmix, nffn, pool_w, pool_scale, w_router, seq, two_d):
    b, t, d = x.shape
    gd = d // N_POOL_GROUPS
    assert seq == GRID_W or not two_d, "row-window shifts assume GRID_W tokens per grid row"
    cmat, cnt = _pool_consts(t, seq, two_d)
    tc = min(t, 512)
    pad_tok = POOL_PAD_ROWS * GRID_W
    scratch = []
    if two_d:
        scratch.append(pltpu.VMEM((t + 2 * pad_tok, gd), F32))
    e = w_router.shape[1]
    full = lambda shape: pl.BlockSpec(shape, lambda i: (0,) * len(shape))
    return pl.pallas_call(
        functools.partial(_pool_kernel, two_d=two_d, tc=tc),
        grid=(b,),
        in_specs=[pl.BlockSpec((1, t, d), lambda i: (i, 0, 0)),
                  pl.BlockSpec((1, 8, d), lambda i: (i, 0, 0)),
                  full((1, d)), full((1, d)), full(cmat.shape), full(cnt.shape),
                  full(pool_w.shape), full((1, d)), full(w_router.shape)],
        out_specs=[pl.BlockSpec((1, t, d), lambda i: (i, 0, 0)),
                   pl.BlockSpec((1, t, d), lambda i: (i, 0, 0)),
                   pl.BlockSpec((1, e, t), lambda i: (i, 0, 0))],
        out_shape=[jax.ShapeDtypeStruct((b, t, d), F32),
                   jax.ShapeDtypeStruct((b, t, d), BF16),
                   jax.ShapeDtypeStruct((b, e, t), F32)],
        scratch_shapes=scratch,
        compiler_params=_params(("arbitrary",), 60),
        name="pool_mixer_2d" if two_d else "pool_mixer_1d",
    )(x, mod, nmix, nffn, cmat, cnt, pool_w, pool_scale, w_router)


def _route_kernel(lg_ref, slot_ref, gate_ref, offs_ref, *, cap):
    nb, e, t = lg_ref.shape
    capf = jnp.float32(cap)
    affs = []
    for s in range(nb):
        lg = lg_ref[s]
        ex = jnp.exp(lg - jnp.max(lg, axis=0, keepdims=True))
        affs.append(ex / jnp.sum(ex, axis=0, keepdims=True))

    def as_f32(v):
        return lax.bitcast_convert_type(v, F32)

    def enough(aff, cand):
        return jnp.sum(jnp.where(aff >= as_f32(cand), 1.0, 0.0), axis=1, keepdims=True) >= capf

    def bisect2(i, vs):
        hi = jnp.left_shift(jnp.int32(1), 30 - 2 * i)
        lo = jnp.left_shift(jnp.int32(1), 29 - 2 * i)
        out = []
        for aff, v in zip(affs, vs):
            c_hi, c_lo, c_both = v | hi, v | lo, v | hi | lo
            out.append(jnp.where(enough(aff, c_both), c_both,
                                 jnp.where(enough(aff, c_hi), c_hi, jnp.where(enough(aff, c_lo), c_lo, v))))
        return tuple(out)

    kths = lax.fori_loop(0, 15, bisect2, tuple(jnp.zeros((e, 1), jnp.int32) for _ in range(nb)))

    blk = min(t, MXU_DIM)
    before = (_iota((blk, blk), 0) < _iota((blk, blk), 1)).astype(BF16)
    nl = offs_ref.shape[2]
    starts = (_iota((t, nl), 0) < _iota((t, nl), 1) * TOKEN_BLOCK).astype(BF16)

    def excl_cumsum(mask):
        ones = jnp.where(mask, 1.0, 0.0)
        outs, run = [], jnp.zeros((e, 1), F32)
        for b0 in range(0, t, blk):
            mb = ones[:, b0:b0 + blk]
            outs.append(_dot(mb.astype(BF16), before) + run)
            run = run + jnp.sum(mb, axis=1, keepdims=True)
        return jnp.concatenate(outs, axis=1) if len(outs) > 1 else outs[0]

    for s, (aff, kth) in enumerate(zip(affs, kths)):
        kth = jnp.where(enough(aff, kth | 1), kth | 1, kth)
        above = as_f32(kth + 1)
        gt = aff >= above
        eq = (aff >= as_f32(kth)) & jnp.logical_not(gt)
        need = capf - jnp.sum(jnp.where(gt, 1.0, 0.0), axis=1, keepdims=True)
        sel = gt | (eq & (excl_cumsum(eq) < need))
        slot_ref[s] = jnp.where(sel, excl_cumsum(sel), -1.0)
        gate_ref[s] = aff
        offs_ref[s] = _dot(jnp.where(sel, 1.0, 0.0).astype(BF16), starts)


def _route(logits, cap):
    b, e, t = logits.shape
    nb = 2 if b % 2 == 0 else 1
    spec = pl.BlockSpec((nb, e, t), lambda i: (i, 0, 0))
    ospec = pl.BlockSpec((nb, e, LANES), lambda i: (i, 0, 0))
    return pl.pallas_call(
        functools.partial(_route_kernel, cap=cap),
        grid=(b // nb,),
        in_specs=[spec],
        out_specs=[spec, spec, ospec],
        out_shape=[jax.ShapeDtypeStruct((b, e, t), F32)] * 2 + [jax.ShapeDtypeStruct((b, e, LANES), F32)],
        compiler_params=_params(("arbitrary",), 32),
        name="route",
    )(logits)


def _gather_kernel(h_ref, slot_ref, o_ref, *, cap):
    hx = h_ref[0]
    t = hx.shape[0]
    row = _iota((cap, t), 0).astype(F32)
    for i in range(o_ref.shape[0]):
        onehot = jnp.where(slot_ref[0, i:i + 1, :] == row, 1.0, 0.0).astype(BF16)
        o_ref[i] = _dot(onehot, hx).astype(BF16)


def _gather(hx2, slot, cap):
    b, t, d = hx2.shape
    e = slot.shape[1]
    eb = 8
    return pl.pallas_call(
        functools.partial(_gather_kernel, cap=cap),
        grid=(b, e // eb),
        in_specs=[pl.BlockSpec((1, t, d), lambda i, j: (i, 0, 0)),
                  pl.BlockSpec((1, eb, t), lambda i, j: (i, j, 0))],
        out_specs=pl.BlockSpec((eb, cap, d), lambda i, j: (j, i, 0)),
        out_shape=jax.ShapeDtypeStruct((e, b * cap, d), BF16),
        compiler_params=_params(("arbitrary", "arbitrary"), 48),
        name="gather",
    )(hx2, slot)


def _pair_list(offs_ref, kb, n_exp, nblk, le_ref, lg_ref):
    b = pl.program_id(0)
    shift = SLOT_GROUP.bit_length() - 1
    max_groups = le_ref.shape[0] // (n_exp + 1)

    cnt = jnp.int32(0)
    for e in range(n_exp):
        o0 = offs_ref[b, e * (nblk + 1) + kb]
        o1 = offs_ref[b, e * (nblk + 1) + kb + 1]
        lo = jnp.right_shift(o0, shift)
        ng = jnp.where(o1 > o0, jnp.right_shift(o1 - 1, shift) - lo + 1, 0)
        for i in range(max_groups):
            le_ref[cnt + i] = e
            lg_ref[cnt + i] = lo + i
        cnt = cnt + ng
    return cnt


def _pair(le_ref, lg_ref, idx, cnt):
    valid = idx < cnt
    safe = jnp.minimum(idx, jnp.maximum(cnt - 1, 0))
    e_p = jnp.where(valid, le_ref[safe], 0)
    g_p = jnp.where(valid, lg_ref[safe], 0)
    base = jnp.where(valid, g_p * SLOT_GROUP, -2 * SLOT_GROUP).astype(F32)
    return e_p, pl.multiple_of(g_p * SLOT_GROUP, SLOT_GROUP), base


def _for_each_dot(cnt, body):
    for c in range(STATIC_DOTS):
        body(c, 0)
    lax.fori_loop(STATIC_DOTS, (cnt + PAIRS_PER_DOT - 1) // PAIRS_PER_DOT, body, 0)


def _sparse_gather_kernel(offs_ref, h_ref, slot_ref, o_ref, le_ref, lg_ref, p_ref, *, n_exp, nblk):
    sg = SLOT_GROUP
    o_ref[...] = jnp.zeros(o_ref.shape, o_ref.dtype)
    sub = _iota((sg, TOKEN_BLOCK), 0).astype(F32)

    def per_block(kb, _):
        cnt = _pair_list(offs_ref, kb, n_exp, nblk, le_ref, lg_ref)
        tok = pl.ds(pl.multiple_of(kb * TOKEN_BLOCK, TOKEN_BLOCK), TOKEN_BLOCK)

        def dot_batch(c, _):
            dst = []
            for p in range(PAIRS_PER_DOT):
                e_p, s0, base = _pair(le_ref, lg_ref, c * PAIRS_PER_DOT + p, cnt)
                row = slot_ref[0, kb, pl.ds(e_p, 1), :]
                p_ref[p * sg:(p + 1) * sg, :] = jnp.where(row == base + sub, 1.0, 0.0).astype(BF16)
                dst.append((e_p, s0))
            z = _dot(p_ref[...], h_ref[0, tok, :])
            for p, (e_p, s0) in enumerate(dst):
                o_ref[e_p, pl.ds(s0, sg), :] += z[p * sg:(p + 1) * sg, :].astype(o_ref.dtype)
            return 0

        _for_each_dot(cnt, dot_batch)
        return 0

    lax.fori_loop(0, nblk, per_block, 0)


def _by_token_block(a):
    b, e, t = a.shape
    return jnp.swapaxes(a.reshape(b, e, t // TOKEN_BLOCK, TOKEN_BLOCK), 1, 2)


def _sparse_gather(hx2, slot_blocks, offs, cap):
    b, t, d = hx2.shape
    _, nblk, e, _ = slot_blocks.shape
    return pl.pallas_call(
        functools.partial(_sparse_gather_kernel, n_exp=e, nblk=nblk),
        grid_spec=pltpu.PrefetchScalarGridSpec(
            num_scalar_prefetch=1,
            grid=(b,),
            in_specs=[pl.BlockSpec((1, t, d), lambda i, o: (i, 0, 0)),
                      pl.BlockSpec((1, nblk, e, TOKEN_BLOCK), lambda i, o: (i, 0, 0, 0))],
            out_specs=pl.BlockSpec((e, cap, d), lambda i, o: (0, i, 0)),
            scratch_shapes=[pltpu.SMEM(((e + 1) * (cap // SLOT_GROUP),), jnp.int32),
                            pltpu.SMEM(((e + 1) * (cap // SLOT_GROUP),), jnp.int32),
                            pltpu.VMEM((PAIRS_PER_DOT * SLOT_GROUP, TOKEN_BLOCK), BF16)]),
        out_shape=jax.ShapeDtypeStruct((e, b * cap, d), BF16),
        compiler_params=_params(("arbitrary",), 40),
        name="sparse_gather",
    )(offs, hx2, slot_blocks)


def _sparse_scatter_kernel(offs_ref, x_ref, y_ref, slot_ref, gate_ref, mod_ref, fn_ref, o_ref,
                           le_ref, lg_ref, p_ref, yc_ref, acc_ref, *, n_exp, nblk, bps, final):
    sg = SLOT_GROUP
    sub = _iota((sg, TOKEN_BLOCK), 0).astype(F32)

    def per_block(i, _):
        cnt = _pair_list(offs_ref, pl.program_id(1) * bps + i, n_exp, nblk, le_ref, lg_ref)
        tok = pl.ds(pl.multiple_of(i * TOKEN_BLOCK, TOKEN_BLOCK), TOKEN_BLOCK)
        acc_ref[...] = jnp.zeros(acc_ref.shape, F32)

        def dot_batch(c, _):
            for p in range(PAIRS_PER_DOT):
                e_p, s0, base = _pair(le_ref, lg_ref, c * PAIRS_PER_DOT + p, cnt)
                row = slot_ref[0, i, pl.ds(e_p, 1), :]
                gate = gate_ref[0, i, pl.ds(e_p, 1), :]
                p_ref[p * sg:(p + 1) * sg, :] = jnp.where(row == base + sub, gate, 0.0)
                yc_ref[p * sg:(p + 1) * sg, :] = y_ref[e_p, pl.ds(s0, sg), :]
            acc_ref[...] += _dot(p_ref[...].T.astype(BF16), yc_ref[...])
            return 0

        _for_each_dot(cnt, dot_batch)
        out = x_ref[0, tok, :] + mod_ref[0][5:6] * acc_ref[...]
        if final:
            out = out * lax.rsqrt(jnp.mean(out * out, axis=-1, keepdims=True) + EPS) * fn_ref[...]
        o_ref[0, tok, :] = out
        return 0

    lax.fori_loop(0, bps, per_block, 0)


def _sparse_scatter(x, y, slot_blocks, gate_blocks, offs, mod, final_norm, cap, final):
    b, t, d = x.shape
    _, nblk, e, _ = slot_blocks.shape
    bps = min(nblk, 4)
    rows = PAIRS_PER_DOT * SLOT_GROUP
    tok = lambda n: pl.BlockSpec((1, bps * TOKEN_BLOCK, n), lambda i, j, o: (i, j, 0))
    exp = pl.BlockSpec((1, bps, e, TOKEN_BLOCK), lambda i, j, o: (i, j, 0, 0))
    return pl.pallas_call(
        functools.partial(_sparse_scatter_kernel, n_exp=e, nblk=nblk, bps=bps, final=final),
        grid_spec=pltpu.PrefetchScalarGridSpec(
            num_scalar_prefetch=1,
            grid=(b, nblk // bps),
            in_specs=[tok(d), pl.BlockSpec((e, cap, d), lambda i, j, o: (0, i, 0)), exp, exp,
                      pl.BlockSpec((1, 8, d), lambda i, j, o: (i, 0, 0)),
                      pl.BlockSpec((1, d), lambda i, j, o: (0, 0))],
            out_specs=tok(d),
            scratch_shapes=[pltpu.SMEM(((e + 1) * (cap // SLOT_GROUP),), jnp.int32),
                            pltpu.SMEM(((e + 1) * (cap // SLOT_GROUP),), jnp.int32),
                            pltpu.VMEM((rows, TOKEN_BLOCK), F32),
                            pltpu.VMEM((rows, d), BF16),
                            pltpu.VMEM((TOKEN_BLOCK, d), F32)]),
        out_shape=jax.ShapeDtypeStruct((b, t, d), F32),
        compiler_params=_params(("arbitrary", "arbitrary"), 48),
        name="sparse_scatter",
    )(offs, x, y, slot_blocks, gate_blocks, mod, final_norm)


def _ffn_kernel(*refs, n_sets, mc):
    x_refs = refs[:n_sets]
    wg_ref, wu_ref, wd_ref = refs[n_sets:n_sets + 3]
    y_refs = refs[n_sets + 3:2 * n_sets + 3]
    acc_refs = refs[2 * n_sets + 3:3 * n_sets + 3]
    f = pl.program_id(1)
    last = f == pl.num_programs(1) - 1

    @pl.when((pl.program_id(0) == 0) & (f == 0))
    def _():
        for acc in acc_refs:
            acc[...] = jnp.zeros(acc.shape, F32)

    for x_ref, y_ref, acc in zip(x_refs, y_refs, acc_refs):
        m = x_ref.shape[1]
        step = min(m, mc)
        for m0 in range(0, m, step):
            xs = x_ref[0, m0:m0 + step, :].astype(F32)
            hg = _dot(xs, wg_ref[0, 0])
            hu = _dot(xs, wu_ref[0, 0])
            hid = hg * _sigmoid(hg) * hu
            tot = acc[m0:m0 + step, :] + _dot(hid, wd_ref[0, 0])
            acc[m0:m0 + step, :] = jnp.where(last, 0.0, tot)
            y_ref[0, m0:m0 + step, :] = tot.astype(BF16)


def _expert_ffn(xs, layer, w_gate, w_up, w_down):
    _, e, d, hidden = w_gate.shape
    tf = 512
    n = len(xs)
    xspecs = [pl.BlockSpec((1, x.shape[1], d), lambda i, j: (i, 0, 0)) for x in xs]
    return pl.pallas_call(
        functools.partial(_ffn_kernel, n_sets=n, mc=1024),
        grid=(e, hidden // tf),
        in_specs=xspecs + [pl.BlockSpec((1, 1, d, tf), lambda i, j: (layer, i, 0, j)),
                           pl.BlockSpec((1, 1, d, tf), lambda i, j: (layer, i, 0, j)),
                           pl.BlockSpec((1, 1, tf, d), lambda i, j: (layer, i, j, 0))],
        out_specs=xspecs,
        out_shape=[jax.ShapeDtypeStruct(x.shape, BF16) for x in xs],
        scratch_shapes=[pltpu.VMEM((x.shape[1], d), F32) for x in xs],
        compiler_params=_params(("arbitrary", "arbitrary"), 60),
        name="expert_ffn",
    )(*xs, w_gate, w_up, w_down)


def _scatter_kernel(x_ref, y_ref, slot_ref, gate_ref, mod_ref, fn_ref, o_ref, *, final):
    tt = x_ref.shape[1]
    e, cap, _ = y_ref.shape
    eye = (_iota((tt, tt), 0) == _iota((tt, tt), 1)).astype(BF16)
    slot_t = _dot_nt(eye, (slot_ref[0] + 1.0).astype(BF16))
    gate_t = _dot_nt(eye, gate_ref[0].astype(BF16))
    lane = _iota((tt, cap), 1).astype(F32) + 1.0
    acc = None
    for i in range(e):
        pt = jnp.where(slot_t[:, i:i + 1] == lane, gate_t[:, i:i + 1], 0.0).astype(BF16)
        part = _dot(pt, y_ref[i])
        acc = part if acc is None else acc + part
    out = x_ref[0] + mod_ref[0][5:6] * acc
    if final:
        out = out * lax.rsqrt(jnp.mean(out * out, axis=-1, keepdims=True) + EPS) * fn_ref[...]
    o_ref[0] = out


def _scatter(x, y, slot, gate, mod, final_norm, cap, final):
    b, t, d = x.shape
    e = slot.shape[1]
    tt = min(t, 512)
    return pl.pallas_call(
        functools.partial(_scatter_kernel, final=final),
        grid=(b, t // tt),
        in_specs=[pl.BlockSpec((1, tt, d), lambda i, j: (i, j, 0)),
                  pl.BlockSpec((e, cap, d), lambda i, j: (0, i, 0)),
                  pl.BlockSpec((1, e, tt), lambda i, j: (i, 0, j)),
                  pl.BlockSpec((1, e, tt), lambda i, j: (i, 0, j)),
                  pl.BlockSpec((1, 8, d), lambda i, j: (i, 0, 0)),
                  pl.BlockSpec((1, d), lambda i, j: (0, 0))],
        out_specs=pl.BlockSpec((1, tt, d), lambda i, j: (i, j, 0)),
        out_shape=jax.ShapeDtypeStruct((b, t, d), F32),
        compiler_params=_params(("arbitrary", "arbitrary"), 48),
        name="scatter",
    )(x, y, slot, gate, mod, final_norm)


def _moe(streams, layer, w_gate, w_up, w_down, final_norm, final):
    routed = []
    for x1, hx2, logits, mod in streams:
        t = x1.shape[1]
        cap = CAPACITY_FACTOR * t // N_EXPERTS
        slot, gate, offs = _route(logits, cap)
        nblk = t // TOKEN_BLOCK
        if nblk > 1:
            offs = offs[:, :, :nblk + 1].astype(jnp.int32).reshape(offs.shape[0], -1)
            slot, gate = _by_token_block(slot), _by_token_block(gate)
            xg = _sparse_gather(hx2, slot, offs, cap)
        else:
            offs = None
            xg = _gather(hx2, slot, cap)
        routed.append((slot, gate, offs, cap, xg))
    ys = _expert_ffn([r[4] for r in routed], layer, w_gate, w_up, w_down)
    outs = []
    for (x1, _, _, mod), (slot, gate, offs, cap, _), y in zip(streams, routed, ys):
        if offs is None:
            outs.append(_scatter(x1, y, slot, gate, mod, final_norm, cap, final))
        else:
            outs.append(_sparse_scatter(x1, y, slot, gate, offs, mod, final_norm, cap, final))
    return outs


def _inproj_kernel(x_ref, mod_ref, nmix_ref, wko_ref, wqv_ref, wg_ref, bg_ref,
                   qt_ref, k_ref, vt_ref, o_ref, gr_ref, *, chunk, qk, dk):
    tt = x_ref.shape[1]
    mod = mod_ref[0]
    h = _norm_mod(x_ref[0], nmix_ref[...], mod[0:1], mod[1:2])
    hb = h.astype(BF16)
    p = _dot(hb, wko_ref[...])
    k_ref[0] = (p[:, :qk] * (dk ** -0.5)).astype(BF16)
    o_ref[0] = _sigmoid(p[:, qk:]).astype(o_ref.dtype)
    p_t = lax.dot_general(wqv_ref[...], hb, (((0,), (1,)), ((), ())), preferred_element_type=F32)
    for ci in range(tt // chunk):
        qt_ref[0, ci] = p_t[:qk, ci * chunk:(ci + 1) * chunk].astype(BF16)
        vt_ref[0, ci] = p_t[qk:, ci * chunk:(ci + 1) * chunk].astype(BF16)

    nh = MLSTM_HEADS
    g_t = _transpose_exact(_dot_split(h, wg_ref[...]) + bg_ref[...])
    row = _iota((4 * nh, chunk), 0)
    is_f = (row & nh) == nh
    a = _iota((chunk, chunk), 0)
    c = _iota((chunk, chunk), 1)
    tri = jnp.concatenate([(a <= c).astype(BF16), (a >= c).astype(BF16)], axis=1)
    ng = 4 * nh
    for c0 in range(0, tt, chunk):
        gc = g_t[:, c0:c0 + chunk]
        pieces = _split_bf16(jnp.where(is_f, _log_sigmoid(gc), 0.0), 3)
        cum = _dot(jnp.concatenate(pieces, axis=0), tri)
        cum = cum[:ng] + (cum[ng:2 * ng] + cum[2 * ng:])
        gr_ref[0, :, c0:c0 + chunk] = jnp.where(is_f, jnp.where(row < 2 * nh, cum[:, :chunk], cum[:, chunk:]), gc)


def _inproj(x, mod, nmix, w_ko, w_qv, w_g, b_g, qk, vd):
    b, t, d = x.shape
    tt = min(t, 1024)
    ng = w_g.shape[1]
    ln = SCAN_CHUNK
    full = lambda shape: pl.BlockSpec(shape, lambda i, j: (0,) * len(shape))
    tok = lambda n: pl.BlockSpec((1, tt, n), lambda i, j: (i, j, 0))
    slab = lambda n: pl.BlockSpec((1, tt // ln, n, ln), lambda i, j: (i, j, 0, 0))
    return pl.pallas_call(
        functools.partial(_inproj_kernel, chunk=ln, qk=qk, dk=qk // MLSTM_HEADS),
        grid=(b, t // tt),
        in_specs=[tok(d), pl.BlockSpec((1, 8, d), lambda i, j: (i, 0, 0)), full((1, d)),
                  full(w_ko.shape), full(w_qv.shape), full(w_g.shape), full(b_g.shape)],
        out_specs=[slab(qk), tok(qk), slab(vd), tok(vd),
                   pl.BlockSpec((1, ng, tt), lambda i, j: (i, 0, j))],
        out_shape=[jax.ShapeDtypeStruct((b, t // ln, qk, ln), BF16), jax.ShapeDtypeStruct((b, t, qk), BF16),
                   jax.ShapeDtypeStruct((b, t // ln, vd, ln), BF16), jax.ShapeDtypeStruct((b, t, vd), BF16),
                   jax.ShapeDtypeStruct((b, ng, t), F32)],
        compiler_params=_params(("arbitrary", "arbitrary"), 56),
        name="mlstm_inproj",
    )(x, mod, nmix, w_ko, w_qv, w_g, b_g)


def _scan_kernel(kc_ref, vc_ref, grc_ref, qx_ref, kx_ref, vx_ref, grx_ref,
                 out_ref, s_ref, sall_ref, mall_ref, *, chunk):
    ln = chunk
    nh = MLSTM_HEADS
    pair = pl.program_id(1)
    kl = kx_ref.shape[2]
    dv = vx_ref.shape[2] // 2
    lane = _iota((ln, kl), 1)
    kmask = (lane < kl // 2, lane >= kl // 2)
    klane = _iota((2 * dv, kl), 1)
    kcols = (klane < kl // 2, klane >= kl // 2)
    ones = jnp.ones((dv, ln), BF16)
    si = _iota((ln, ln), 0)
    ji = _iota((ln, ln), 1)
    causal = (si <= ji, si >= ji)
    ncc = kc_ref.shape[1] // ln
    ncx = kx_ref.shape[1] // ln

    def gate_rows(gr_ref, hh, dirn, c):
        base = 2 * nh * dirn + 2 * pair + hh
        return gr_ref[0, base, pl.ds(c, 1), :], gr_ref[0, base + nh, pl.ds(c, 1), :]

    def keys(k_ref, c):
        k = k_ref[0, pl.ds(pl.multiple_of(c * ln, ln), ln), :]
        return [jnp.where(kmask[hh], k, jnp.zeros((), BF16)) for hh in range(2)]

    def values_t(v_ref, c, hh):
        return jnp.concatenate([v_ref[0, c, hh * dv:(hh + 1) * dv, :], ones], axis=0)

    def advance(refs, n, record):
        k_ref, v_ref, gr_ref = refs

        def body(i, ms):
            new_ms = [None] * 4
            cs = (i, n - 1 - i)
            for dirn in range(2):
                c = cs[dirn]
                vws, decays = [], []
                for hh in range(2):
                    ch = 2 * hh + dirn
                    ig, bc = gate_rows(gr_ref, hh, dirn, c)
                    m = ms[ch]
                    if record:
                        s_own = jnp.where(kcols[hh], s_ref[ch], 0.0)
                        sall_ref[ch * ncx + c] = s_own.astype(BF16)
                        mall_ref[ch * ncx + c] = jnp.broadcast_to(m, (8, ln))
                    b_end = bc[:, ln - 1:ln] if dirn == 0 else bc[:, 0:1]
                    gl = b_end - bc + ig
                    m_new = jnp.maximum(b_end + m, jnp.max(gl, axis=1, keepdims=True))
                    vws.append((values_t(v_ref, c, hh) * jnp.exp(gl - m_new)).astype(BF16))
                    decays.append(jnp.exp(b_end + m - m_new))
                    new_ms[ch] = m_new
                upd = _dot(jnp.concatenate(vws, axis=0), k_ref[0, pl.ds(pl.multiple_of(c * ln, ln), ln), :])
                for hh in range(2):
                    ch = 2 * hh + dirn
                    s_ref[ch] = decays[hh] * s_ref[ch] + upd[hh * 2 * dv:(hh + 1) * 2 * dv, :]
            return tuple(new_ms)

        return body

    s_ref[...] = jnp.zeros(s_ref.shape, F32)
    ms = tuple(jnp.zeros((1, 1), F32) for _ in range(4))
    ms = lax.fori_loop(0, ncc, advance((kc_ref, vc_ref, grc_ref), ncc, False), ms, unroll=min(ncc, SCAN_UNROLL))
    lax.fori_loop(0, ncx, advance((kx_ref, vx_ref, grx_ref), ncx, True), ms, unroll=min(ncx, SCAN_UNROLL))

    def emit(c, _):
        q_t = qx_ref[0, c]
        q_tf = q_t.astype(F32)
        s_pair = _dot(jnp.concatenate(keys(kx_ref, c), axis=0), q_t)
        for hh in range(2):
            s_kq = s_pair[hh * ln:(hh + 1) * ln, :]
            v_t = values_t(vx_ref, c, hh)
            hsum = None
            for dirn in range(2):
                ch = 2 * hh + dirn
                ig, bc = gate_rows(grx_ref, hh, dirn, c)
                u_col = jnp.broadcast_to(ig - bc, (ln, ln)).T
                a = bc + mall_ref[ch * ncx + c][0:1, :]
                dm = jnp.where(causal[dirn], u_col + bc, -jnp.inf)
                mj = jnp.maximum(a, jnp.max(dm, axis=0, keepdims=True))
                sm = (s_kq * jnp.exp(dm - mj)).astype(BF16)
                qw = (q_tf * jnp.exp(a - mj)).astype(BF16)
                num = _dot(jnp.concatenate([v_t, sall_ref[ch * ncx + c]], axis=1),
                           jnp.concatenate([sm, qw], axis=0))
                h = num[:dv, :] * (1.0 / jnp.maximum(jnp.abs(num[dv:dv + 1, :]), jnp.exp(-mj)))
                hsum = h if hsum is None else hsum + h
            hn = hsum * lax.rsqrt(jnp.mean(hsum * hsum, axis=0, keepdims=True) + EPS)
            out_ref[0, pl.ds(pl.multiple_of(c * ln, ln), ln), hh * dv:(hh + 1) * dv] = hn.T.astype(out_ref.dtype)
        return 0

    lax.fori_loop(0, ncx, emit, 0, unroll=min(ncx, SCAN_UNROLL))


def _scan(ctx_parts, lat_parts):
    kc, vc, grc = ctx_parts
    qx, kx, vx, grx = lat_parts
    b, t, qk = kx.shape
    vd = vx.shape[2]
    nh = MLSTM_HEADS
    ln = SCAN_CHUNK
    dv = vd // nh
    kl = 2 * (qk // nh)
    assert kl == ln and dv == ln, "scan kernel assumes key-pair lanes = head value dim = chunk"
    ncx = t // ln

    def specs(k, v, gr):
        tk = k.shape[1]
        return [pl.BlockSpec((1, tk, kl), lambda i, j: (i, 0, j)),
                pl.BlockSpec((1, tk // ln, 2 * dv, ln), lambda i, j: (i, 0, j, 0)),
                pl.BlockSpec((1,) + gr.shape[1:], lambda i, j: (i, 0, 0, 0))]

    return pl.pallas_call(
        functools.partial(_scan_kernel, chunk=ln),
        grid=(b, nh // 2),
        in_specs=specs(kc, vc, grc) + [pl.BlockSpec((1, ncx, kl, ln), lambda i, j: (i, 0, j, 0))] + specs(kx, vx, grx),
        out_specs=pl.BlockSpec((1, t, 2 * dv), lambda i, j: (i, 0, j)),
        out_shape=jax.ShapeDtypeStruct((b, t, vd), BF16),
        scratch_shapes=[pltpu.VMEM((4, 2 * dv, kl), F32),
                        pltpu.VMEM((4 * ncx, 2 * dv, kl), BF16),
                        pltpu.VMEM((4 * ncx, 8, ln), F32)],
        compiler_params=_params(("arbitrary", "arbitrary"), 40),
        name="mlstm_scan",
    )(kc, vc, grc, qx, kx, vx, grx)


def _mlstm_out_kernel(hn_ref, og_ref, x_ref, mod_ref, mn_ref, wout_ref, nffn_ref, wr_ref,
                      x1_ref, hx2_ref, lg_ref):
    mod = mod_ref[0]
    a = (hn_ref[0].astype(F32) * mn_ref[...]) * og_ref[0].astype(F32)
    xn = x_ref[0] + mod[2:3] * _dot(a.astype(BF16), wout_ref[...])
    x1_ref[0] = xn
    hb, lg = _ffn_pre(xn, mod, nffn_ref[...], wr_ref[...])
    hx2_ref[0] = hb
    lg_ref[0] = lg


def _mlstm_out(hn, og, x, mod, mnorm, w_out, nffn, w_router):
    b, t, d = x.shape
    vd = hn.shape[2]
    e = w_router.shape[1]
    tt = min(t, 1024)
    full = lambda shape: pl.BlockSpec(shape, lambda i, j: (0,) * len(shape))
    tok = lambda n: pl.BlockSpec((1, tt, n), lambda i, j: (i, j, 0))
    return pl.pallas_call(
        _mlstm_out_kernel,
        grid=(b, t // tt),
        in_specs=[tok(vd), tok(vd), tok(d), pl.BlockSpec((1, 8, d), lambda i, j: (i, 0, 0)),
                  full((1, vd)), full(w_out.shape), full((1, d)), full(w_router.shape)],
        out_specs=[tok(d), tok(d), pl.BlockSpec((1, e, tt), lambda i, j: (i, 0, j))],
        out_shape=[jax.ShapeDtypeStruct((b, t, d), F32), jax.ShapeDtypeStruct((b, t, d), BF16),
                   jax.ShapeDtypeStruct((b, e, t), F32)],
        compiler_params=_params(("arbitrary", "arbitrary"), 48),
        name="mlstm_out",
    )(hn, og, x, mod, mnorm, w_out, nffn, w_router)


def _chunked_rows(gr):
    b, n, t = gr.shape
    return gr.reshape(b, n, t // SCAN_CHUNK, SCAN_CHUNK)


def kernel(x, c, ctx, c_ctx, ada_w, ada_b, norm_mix, norm_ffn, pool_w, pool_scale, mlstm_w_in, mlstm_b_gates,
           mlstm_norm, mlstm_w_out, moe_router, moe_w_gate, moe_w_up, moe_w_down, final_norm):
    bsz, seq, d = x.shape
    ctx_len = ctx.shape[1]
    depth = ada_w.shape[0]
    n_mixers = 2

    cc = jnp.concatenate([c, c_ctx[None, :], jnp.zeros((16 - bsz - 1, d), F32)], axis=0)
    ada = _ada(cc, ada_w, ada_b).reshape(depth, 16, N_ADA, d)
    pad = jnp.zeros((bsz, 8 - N_ADA, d), F32)

    fn = final_norm.reshape(1, d)
    for i in range(depth):
        last = i == depth - 1
        j = i // n_mixers
        mod_x = jnp.concatenate([ada[i, :bsz], pad], axis=1)
        mod_c = jnp.concatenate([jnp.broadcast_to(ada[i, bsz][None], (bsz, N_ADA, d)), pad], axis=1)
        nmix = norm_mix[i].reshape(1, d)
        nffn = norm_ffn[i].reshape(1, d)
        w_router = moe_router[i]
        streams = []
        if i % n_mixers == 0:
            pw = pool_w[j].astype(BF16)
            ps = pool_scale[j].reshape(1, d)
            x1, hx2, lg = _pool_mixer(x, mod_x, nmix, nffn, pw, ps, w_router, GRID_W, True)
            streams.append((x1, hx2, lg, mod_x))
            if not last:
                c1, hc2, lgc = _pool_mixer(ctx.reshape(1, bsz * ctx_len, d), mod_c[:1], nmix, nffn, pw, ps, w_router,
                                           ctx_len, False)
                lgc = jnp.swapaxes(lgc.reshape(-1, bsz, ctx_len), 0, 1)
                streams.append((c1.reshape(bsz, ctx_len, d), hc2.reshape(bsz, ctx_len, d), lgc, mod_c))
        else:
            qk = mlstm_w_in.shape[2] - 2 * mlstm_w_out.shape[1] - 4 * MLSTM_HEADS
            qk //= 2
            vd = mlstm_w_out.shape[1]
            w_in = mlstm_w_in[j]
            w_ko = jnp.concatenate([w_in[:, qk:2 * qk], w_in[:, 2 * qk + vd:2 * qk + 2 * vd]], axis=1).astype(BF16)
            w_qv = jnp.concatenate([w_in[:, :qk], w_in[:, 2 * qk:2 * qk + vd]], axis=1).astype(BF16)
            w_g = w_in[:, 2 * qk + 2 * vd:]
            b_g = mlstm_b_gates[j].reshape(1, -1)
            proj = lambda s, m: _inproj(s, m, nmix, w_ko, w_qv, w_g, b_g, qk, vd)
            _, kc, vc, _, grc = proj(ctx.reshape(1, bsz * ctx_len, d), mod_c[:1])
            kc = kc.reshape(bsz, ctx_len, qk)
            vc = vc.reshape(bsz, ctx_len // SCAN_CHUNK, vd, SCAN_CHUNK)
            grc = jnp.swapaxes(grc.reshape(-1, bsz, ctx_len), 0, 1)
            qx, kx, vx, ox, grx = proj(x, mod_x)
            hn = _scan((kc, vc, _chunked_rows(grc)), (qx, kx, vx, _chunked_rows(grx)))
            x1, hx2, lg = _mlstm_out(hn, ox, x, mod_x, mlstm_norm[j].reshape(1, vd),
                                     mlstm_w_out[j].astype(BF16), nffn, w_router)
            streams.append((x1, hx2, lg, mod_x))
            assert last, "context output of the mLSTM mixer is only needed by a following layer"
        outs = _moe(streams, i, moe_w_gate, moe_w_up, moe_w_down, fn, last)
        x = outs[0]
        if not last:
            ctx = outs[1]
    return x
```

```python
import functools

import jax
import jax.numpy as jnp
import numpy as np
from jax import lax
from jax.experimental import pallas as pl
from jax.experimental.pallas import tpu as pltpu

F32 = jnp.float32
BF16 = jnp.bfloat16

GRID_W = 64
EPS = 1e-6
N_ADA = 6
POOL_WINDOWS = (2, 4, 8, 16)
N_POOL_GROUPS = 4
MLSTM_HEADS = 8
N_EXPERTS = 16
CAPACITY_FACTOR = 2

MIB = 1024 * 1024
MXU_DIM = 256
LANES = 128
TOKEN_BLOCK = 256
SLOT_GROUP = 32
PAIRS_PER_DOT = 16
STATIC_DOTS = 2
SCAN_CHUNK = 128
SCAN_UNROLL = 8
POOL_PAD_ROWS = max(POOL_WINDOWS) // 2


def _params(sem, vmem_mib):
    return pltpu.CompilerParams(dimension_semantics=sem, vmem_limit_bytes=vmem_mib * MIB)


def _dot(a, b):
    return jnp.dot(a, b, preferred_element_type=F32)


def _dot_nt(a, b):
    return lax.dot_general(a, b, (((1,), (1,)), ((), ())), preferred_element_type=F32)


def _iota(shape, dim, dtype=jnp.int32):
    return lax.broadcasted_iota(dtype, shape, dim)


def _sigmoid(x):
    return 1.0 / (1.0 + jnp.exp(-x))


def _log_sigmoid(x):
    return jnp.minimum(x, 0.0) - jnp.log1p(jnp.exp(-jnp.abs(x)))


def _norm_mod(x, g, shift, scale):
    inv = lax.rsqrt(jnp.mean(x * x, axis=-1, keepdims=True) + EPS)
    return (x * inv) * (g * (1.0 + scale)) + shift


def _ada_kernel(c_ref, w_ref, b_ref, o_ref):
    c = c_ref[...]
    s = c * _sigmoid(c)
    s_hi, s_lo = _split_bf16(s, 2)
    w_hi, w_lo = _split_bf16(w_ref[0], 2)
    o_ref[0] = _dot(s_hi, w_hi) + (_dot(s_hi, w_lo) + _dot(s_lo, w_hi)) + b_ref[0]


def _ada(cc, ada_w, ada_b):
    depth, d, n = ada_w.shape
    rows = cc.shape[0]
    tn = n // 4
    return pl.pallas_call(
        _ada_kernel,
        grid=(depth, n // tn),
        in_specs=[pl.BlockSpec((rows, d), lambda i, j: (0, 0)),
                  pl.BlockSpec((1, d, tn), lambda i, j: (i, 0, j)),
                  pl.BlockSpec((1, 1, tn), lambda i, j: (i, 0, j))],
        out_specs=pl.BlockSpec((1, rows, tn), lambda i, j: (i, 0, j)),
        out_shape=jax.ShapeDtypeStruct((depth, rows, n), F32),
        compiler_params=_params(("arbitrary", "arbitrary"), 40),
        name="ada",
    )(cc, ada_w, ada_b.reshape(depth, 1, n))


def _split_bf16(x, pieces):
    out = []
    for _ in range(pieces):
        p = x.astype(BF16)
        out.append(p)
        x = x - p.astype(F32)
    return out


def _dot_split(a, b):
    n = b.shape[1]
    a_hi, a_lo = _split_bf16(a, 2)
    b_hi, b_lo = _split_bf16(b, 2)
    hi = _dot(a_hi, jnp.concatenate([b_hi, b_lo], axis=1))
    return hi[:, :n] + (hi[:, n:] + _dot(a_lo, b_hi))


def _transpose_exact(x):
    m = x.shape[1]
    eye = (_iota((m, m), 0) == _iota((m, m), 1)).astype(BF16)
    hi, mid, lo = _split_bf16(x, 3)
    return _dot_nt(eye, hi) + (_dot_nt(eye, mid) + _dot_nt(eye, lo))


def _ffn_pre(xn, mod, nffn, wr):
    h2 = _norm_mod(xn, nffn, mod[3:4], mod[4:5])
    return h2.astype(BF16), _transpose_exact(_dot_split(h2, wr))


def _pool_kernel(x_ref, mod_ref, nmix_ref, nffn_ref, cmat_ref, cnt_ref, pw_ref, ps_ref, wr_ref,
                 x1_ref, hx2_ref, lg_ref, *pad, two_d, tc):
    t, d = x_ref.shape[1], x_ref.shape[2]
    gd = d // N_POOL_GROUPS
    mod = mod_ref[0]
    pad_tok = POOL_PAD_ROWS * GRID_W

    for c0 in range(0, t, tc):
        x1_ref[0, c0:c0 + tc, :] = _norm_mod(x_ref[0, c0:c0 + tc, :], nmix_ref[...], mod[0:1], mod[1:2])

    def hx_of(r0, rn, j):
        return x1_ref[0, r0:r0 + rn, j * gd:(j + 1) * gd]

    if two_d:
        pad_ref, = pad
        pad_ref[0:pad_tok, :] = jnp.zeros((pad_tok, gd), F32)
        pad_ref[pad_tok + t:pad_tok + t + pad_tok, :] = jnp.zeros((pad_tok, gd), F32)

    for j, w in enumerate(POOL_WINDOWS):
        cs = slice(j * gd, (j + 1) * gd)
        cm = cmat_ref[j]
        sums = []
        for b0 in range(0, t, MXU_DIM):
            g = hx_of(b0, MXU_DIM, j)
            g_hi = g.astype(BF16)
            g_lo = (g - g_hi.astype(F32)).astype(BF16)
            csum = _dot(cm, g_hi) + _dot(cm, g_lo)
            if two_d:
                pad_ref[pad_tok + b0:pad_tok + b0 + MXU_DIM, :] = csum
            else:
                sums.append(csum)
        for c0 in range(0, t, tc):
            if two_d:
                tot = None
                for dr in range(-(w // 2), w - w // 2):
                    o = pad_tok + c0 + dr * GRID_W
                    sl = pad_ref[o:o + tc, :]
                    tot = sl if tot is None else tot + sl
            else:
                blocks = sums[c0 // MXU_DIM:(c0 + tc) // MXU_DIM]
                tot = blocks[0] if len(blocks) == 1 else jnp.concatenate(blocks, axis=0)
            mean = tot / cnt_ref[c0:c0 + tc, j:j + 1]
            diff = mean - hx_of(c0, tc, j)
            y = _dot(diff.astype(BF16), pw_ref[j]) * ps_ref[:, cs]
            x1_ref[0, c0:c0 + tc, cs] = x_ref[0, c0:c0 + tc, cs] + mod[2:3, cs] * y

    tail = min(t, 2 * tc)
    for c0 in range(0, t, tail):
        hb, lg = _ffn_pre(x1_ref[0, c0:c0 + tail, :], mod, nffn_ref[...], wr_ref[...])
        hx2_ref[0, c0:c0 + tail, :] = hb
        lg_ref[0, :, c0:c0 + tail] = lg


def _pool_consts(t, seq, two_d):
    cm = np.zeros((len(POOL_WINDOWS), MXU_DIM, MXU_DIM), np.float32)
    cnt = np.zeros((t, len(POOL_WINDOWS)), np.float32)
    n = seq
    assert MXU_DIM % n == 0
    pos = np.arange(MXU_DIM)
    for j, w in enumerate(POOL_WINDOWS):
        col = pos % n
        lo = np.clip(col - w // 2, 0, n)
        hi = np.clip(col + w - w // 2, 0, n)
        same = (pos[:, None] // n) == (pos[None, :] // n)
        cm[j] = (same & (col[None, :] >= lo[:, None]) & (col[None, :] < hi[:, None])).astype(np.float32)
        tt = np.arange(t)
        c = tt % n
        ccnt = np.clip(c + w - w // 2, 0, n) - np.clip(c - w // 2, 0, n)
        if two_d:
            rows = t // n
            r = tt // n
            rcnt = np.clip(r + w - w // 2, 0, rows) - np.clip(r - w // 2, 0, rows)
            cnt[:, j] = ccnt * rcnt
        else:
            cnt[:, j] = ccnt
    return jnp.asarray(cm, BF16), jnp.asarray(cnt)


def _pool_mixer(x, mod, nmix, nffn, pool_w, pool_scale, w_router, seq, two_d):
    b, t, d = x.shape
    gd = d // N_POOL_GROUPS
    assert seq == GRID_W or not two_d, "row-window shifts assume GRID_W tokens per grid row"
    cmat, cnt = _pool_consts(t, seq, two_d)
    tc = min(t, 512)
    pad_tok = POOL_PAD_ROWS * GRID_W
    scratch = []
    if two_d:
        scratch.append(pltpu.VMEM((t + 2 * pad_tok, gd), F32))
    e = w_router.shape[1]
    full = lambda shape: pl.BlockSpec(shape, lambda i: (0,) * len(shape))
    return pl.pallas_call(
        functools.partial(_pool_kernel, two_d=two_d, tc=tc),
        grid=(b,),
        in_specs=[pl.BlockSpec((1, t, d), lambda i: (i, 0, 0)),
                  pl.BlockSpec((1, 8, d), lambda i: (i, 0, 0)),
                  full((1, d)), full((1, d)), full(cmat.shape), full(cnt.shape),
                  full(pool_w.shape), full((1, d)), full(w_router.shape)],
        out_specs=[pl.BlockSpec((1, t, d), lambda i: (i, 0, 0)),
                   pl.BlockSpec((1, t, d), lambda i: (i, 0, 0)),
                   pl.BlockSpec((1, e, t), lambda i: (i, 0, 0))],
        out_shape=[jax.ShapeDtypeStruct((b, t, d), F32),
                   jax.ShapeDtypeStruct((b, t, d), BF16),
                   jax.ShapeDtypeStruct((b, e, t), F32)],
        scratch_shapes=scratch,
        compiler_params=_params(("arbitrary",), 60),
        name="pool_mixer_2d" if two_d else "pool_mixer_1d",
    )(x, mod, nmix, nffn, cmat, cnt, pool_w, pool_scale, w_router)


def _route_kernel(lg_ref, slot_ref, gate_ref, offs_ref, *, cap):
    nb, e, t = lg_ref.shape
    capf = jnp.float32(cap)
    affs = []
    for s in range(nb):
        lg = lg_ref[s]
        ex = jnp.exp(lg - jnp.max(lg, axis=0, keepdims=True))
        affs.append(ex / jnp.sum(ex, axis=0, keepdims=True))

    def as_f32(v):
        return lax.bitcast_convert_type(v, F32)

    def enough(aff, cand):
        return jnp.sum(jnp.where(aff >= as_f32(cand), 1.0, 0.0), axis=1, keepdims=True) >= capf

    def bisect2(i, vs):
        hi = jnp.left_shift(jnp.int32(1), 30 - 2 * i)
        lo = jnp.left_shift(jnp.int32(1), 29 - 2 * i)
        out = []
        for aff, v in zip(affs, vs):
            c_hi, c_lo, c_both = v | hi, v | lo, v | hi | lo
            out.append(jnp.where(enough(aff, c_both), c_both,
                                 jnp.where(enough(aff, c_hi), c_hi, jnp.where(enough(aff, c_lo), c_lo, v))))
        return tuple(out)

    kths = lax.fori_loop(0, 15, bisect2, tuple(jnp.zeros((e, 1), jnp.int32) for _ in range(nb)))

    blk = min(t, MXU_DIM)
    before = (_iota((blk, blk), 0) < _iota((blk, blk), 1)).astype(BF16)
    nl = offs_ref.shape[2]
    starts = (_iota((t, nl), 0) < _iota((t, nl), 1) * TOKEN_BLOCK).astype(BF16)

    def excl_cumsum(mask):
        ones = jnp.where(mask, 1.0, 0.0)
        outs, run = [], jnp.zeros((e, 1), F32)
        for b0 in range(0, t, blk):
            mb = ones[:, b0:b0 + blk]
            outs.append(_dot(mb.astype(BF16), before) + run)
            run = run + jnp.sum(mb, axis=1, keepdims=True)
        return jnp.concatenate(outs, axis=1) if len(outs) > 1 else outs[0]

    for s, (aff, kth) in enumerate(zip(affs, kths)):
        kth = jnp.where(enough(aff, kth | 1), kth | 1, kth)
        above = as_f32(kth + 1)
        gt = aff >= above
        eq = (aff >= as_f32(kth)) & jnp.logical_not(gt)
        need = capf - jnp.sum(jnp.where(gt, 1.0, 0.0), axis=1, keepdims=True)
        sel = gt | (eq & (excl_cumsum(eq) < need))
        slot_ref[s] = jnp.where(sel, excl_cumsum(sel), -1.0)
        gate_ref[s] = aff
        offs_ref[s] = _dot(jnp.where(sel, 1.0, 0.0).astype(BF16), starts)


def _route(logits, cap):
    b, e, t = logits.shape
    nb = 2 if b % 2 == 0 else 1
    spec = pl.BlockSpec((nb, e, t), lambda i: (i, 0, 0))
    ospec = pl.BlockSpec((nb, e, LANES), lambda i: (i, 0, 0))
    return pl.pallas_call(
        functools.partial(_route_kernel, cap=cap),
        grid=(b // nb,),
        in_specs=[spec],
        out_specs=[spec, spec, ospec],
        out_shape=[jax.ShapeDtypeStruct((b, e, t), F32)] * 2 + [jax.ShapeDtypeStruct((b, e, LANES), F32)],
        compiler_params=_params(("arbitrary",), 32),
        name="route",
    )(logits)


def _gather_kernel(h_ref, slot_ref, o_ref, *, cap):
    hx = h_ref[0]
    t = hx.shape[0]
    row = _iota((cap, t), 0).astype(F32)
    for i in range(o_ref.shape[0]):
        onehot = jnp.where(slot_ref[0, i:i + 1, :] == row, 1.0, 0.0).astype(BF16)
        o_ref[i] = _dot(onehot, hx).astype(BF16)


def _gather(hx2, slot, cap):
    b, t, d = hx2.shape
    e = slot.shape[1]
    eb = 8
    return pl.pallas_call(
        functools.partial(_gather_kernel, cap=cap),
        grid=(b, e // eb),
        in_specs=[pl.BlockSpec((1, t, d), lambda i, j: (i, 0, 0)),
                  pl.BlockSpec((1, eb, t), lambda i, j: (i, j, 0))],
        out_specs=pl.BlockSpec((eb, cap, d), lambda i, j: (j, i, 0)),
        out_shape=jax.ShapeDtypeStruct((e, b * cap, d), BF16),
        compiler_params=_params(("arbitrary", "arbitrary"), 48),
        name="gather",
    )(hx2, slot)


def _pair_list(offs_ref, kb, n_exp, nblk, le_ref, lg_ref):
    b = pl.program_id(0)
    shift = SLOT_GROUP.bit_length() - 1
    max_groups = le_ref.shape[0] // (n_exp + 1)

    cnt = jnp.int32(0)
    for e in range(n_exp):
        o0 = offs_ref[b, e * (nblk + 1) + kb]
        o1 = offs_ref[b, e * (nblk + 1) + kb + 1]
        lo = jnp.right_shift(o0, shift)
        ng = jnp.where(o1 > o0, jnp.right_shift(o1 - 1, shift) - lo + 1, 0)
        for i in range(max_groups):
            le_ref[cnt + i] = e
            lg_ref[cnt + i] = lo + i
        cnt = cnt + ng
    return cnt


def _pair(le_ref, lg_ref, idx, cnt):
    valid = idx < cnt
    safe = jnp.minimum(idx, jnp.maximum(cnt - 1, 0))
    e_p = jnp.where(valid, le_ref[safe], 0)
    g_p = jnp.where(valid, lg_ref[safe], 0)
    base = jnp.where(valid, g_p * SLOT_GROUP, -2 * SLOT_GROUP).astype(F32)
    return e_p, pl.multiple_of(g_p * SLOT_GROUP, SLOT_GROUP), base


def _for_each_dot(cnt, body):
    for c in range(STATIC_DOTS):
        body(c, 0)
    lax.fori_loop(STATIC_DOTS, (cnt + PAIRS_PER_DOT - 1) // PAIRS_PER_DOT, body, 0)


def _sparse_gather_kernel(offs_ref, h_ref, slot_ref, o_ref, le_ref, lg_ref, p_ref, *, n_exp, nblk):
    sg = SLOT_GROUP
    o_ref[...] = jnp.zeros(o_ref.shape, o_ref.dtype)
    sub = _iota((sg, TOKEN_BLOCK), 0).astype(F32)

    def per_block(kb, _):
        cnt = _pair_list(offs_ref, kb, n_exp, nblk, le_ref, lg_ref)
        tok = pl.ds(pl.multiple_of(kb * TOKEN_BLOCK, TOKEN_BLOCK), TOKEN_BLOCK)

        def dot_batch(c, _):
            dst = []
            for p in range(PAIRS_PER_DOT):
                e_p, s0, base = _pair(le_ref, lg_ref, c * PAIRS_PER_DOT + p, cnt)
                row = slot_ref[0, kb, pl.ds(e_p, 1), :]
                p_ref[p * sg:(p + 1) * sg, :] = jnp.where(row == base + sub, 1.0, 0.0).astype(BF16)
                dst.append((e_p, s0))
            z = _dot(p_ref[...], h_ref[0, tok, :])
            for p, (e_p, s0) in enumerate(dst):
                o_ref[e_p, pl.ds(s0, sg), :] += z[p * sg:(p + 1) * sg, :].astype(o_ref.dtype)
            return 0

        _for_each_dot(cnt, dot_batch)
        return 0

    lax.fori_loop(0, nblk, per_block, 0)


def _by_token_block(a):
    b, e, t = a.shape
    return jnp.swapaxes(a.reshape(b, e, t // TOKEN_BLOCK, TOKEN_BLOCK), 1, 2)


def _sparse_gather(hx2, slot_blocks, offs, cap):
    b, t, d = hx2.shape
    _, nblk, e, _ = slot_blocks.shape
    return pl.pallas_call(
        functools.partial(_sparse_gather_kernel, n_exp=e, nblk=nblk),
        grid_spec=pltpu.PrefetchScalarGridSpec(
            num_scalar_prefetch=1,
            grid=(b,),
            in_specs=[pl.BlockSpec((1, t, d), lambda i, o: (i, 0, 0)),
                      pl.BlockSpec((1, nblk, e, TOKEN_BLOCK), lambda i, o: (i, 0, 0, 0))],
            out_specs=pl.BlockSpec((e, cap, d), lambda i, o: (0, i, 0)),
            scratch_shapes=[pltpu.SMEM(((e + 1) * (cap // SLOT_GROUP),), jnp.int32),
                            pltpu.SMEM(((e + 1) * (cap // SLOT_GROUP),), jnp.int32),
                            pltpu.VMEM((PAIRS_PER_DOT * SLOT_GROUP, TOKEN_BLOCK), BF16)]),
        out_shape=jax.ShapeDtypeStruct((e, b * cap, d), BF16),
        compiler_params=_params(("arbitrary",), 40),
        name="sparse_gather",
    )(offs, hx2, slot_blocks)


def _sparse_scatter_kernel(offs_ref, x_ref, y_ref, slot_ref, gate_ref, mod_ref, fn_ref, o_ref,
                           le_ref, lg_ref, p_ref, yc_ref, acc_ref, *, n_exp, nblk, bps, final):
    sg = SLOT_GROUP
    sub = _iota((sg, TOKEN_BLOCK), 0).astype(F32)

    def per_block(i, _):
        cnt = _pair_list(offs_ref, pl.program_id(1) * bps + i, n_exp, nblk, le_ref, lg_ref)
        tok = pl.ds(pl.multiple_of(i * TOKEN_BLOCK, TOKEN_BLOCK), TOKEN_BLOCK)
        acc_ref[...] = jnp.zeros(acc_ref.shape, F32)

        def dot_batch(c, _):
            for p in range(PAIRS_PER_DOT):
                e_p, s0, base = _pair(le_ref, lg_ref, c * PAIRS_PER_DOT + p, cnt)
                row = slot_ref[0, i, pl.ds(e_p, 1), :]
                gate = gate_ref[0, i, pl.ds(e_p, 1), :]
                p_ref[p * sg:(p + 1) * sg, :] = jnp.where(row == base + sub, gate, 0.0)
                yc_ref[p * sg:(p + 1) * sg, :] = y_ref[e_p, pl.ds(s0, sg), :]
            acc_ref[...] += _dot(p_ref[...].T.astype(BF16), yc_ref[...])
            return 0

        _for_each_dot(cnt, dot_batch)
        out = x_ref[0, tok, :] + mod_ref[0][5:6] * acc_ref[...]
        if final:
            out = out * lax.rsqrt(jnp.mean(out * out, axis=-1, keepdims=True) + EPS) * fn_ref[...]
        o_ref[0, tok, :] = out
        return 0

    lax.fori_loop(0, bps, per_block, 0)


def _sparse_scatter(x, y, slot_blocks, gate_blocks, offs, mod, final_norm, cap, final):
    b, t, d = x.shape
    _, nblk, e, _ = slot_blocks.shape
    bps = min(nblk, 4)
    rows = PAIRS_PER_DOT * SLOT_GROUP
    tok = lambda n: pl.BlockSpec((1, bps * TOKEN_BLOCK, n), lambda i, j, o: (i, j, 0))
    exp = pl.BlockSpec((1, bps, e, TOKEN_BLOCK), lambda i, j, o: (i, j, 0, 0))
    return pl.pallas_call(
        functools.partial(_sparse_scatter_kernel, n_exp=e, nblk=nblk, bps=bps, final=final),
        grid_spec=pltpu.PrefetchScalarGridSpec(
            num_scalar_prefetch=1,
            grid=(b, nblk // bps),
            in_specs=[tok(d), pl.BlockSpec((e, cap, d), lambda i, j, o: (0, i, 0)), exp, exp,
                      pl.BlockSpec((1, 8, d), lambda i, j, o: (i, 0, 0)),
                      pl.BlockSpec((1, d), lambda i, j, o: (0, 0))],
            out_specs=tok(d),
            scratch_shapes=[pltpu.SMEM(((e + 1) * (cap // SLOT_GROUP),), jnp.int32),
                            pltpu.SMEM(((e + 1) * (cap // SLOT_GROUP),), jnp.int32),
                            pltpu.VMEM((rows, TOKEN_BLOCK), F32),
                            pltpu.VMEM((rows, d), BF16),
                            pltpu.VMEM((TOKEN_BLOCK, d), F32)]),
        out_shape=jax.ShapeDtypeStruct((b, t, d), F32),
        compiler_params=_params(("arbitrary", "arbitrary"), 48),
        name="sparse_scatter",
    )(offs, x, y, slot_blocks, gate_blocks, mod, final_norm)


def _ffn_kernel(*refs, n_sets, mc):
    x_refs = refs[:n_sets]
    wg_ref, wu_ref, wd_ref = refs[n_sets:n_sets + 3]
    y_refs = refs[n_sets + 3:2 * n_sets + 3]
    acc_refs = refs[2 * n_sets + 3:3 * n_sets + 3]
    f = pl.program_id(1)
    last = f == pl.num_programs(1) - 1

    @pl.when((pl.program_id(0) == 0) & (f == 0))
    def _():
        for acc in acc_refs:
            acc[...] = jnp.zeros(acc.shape, F32)

    for x_ref, y_ref, acc in zip(x_refs, y_refs, acc_refs):
        m = x_ref.shape[1]
        step = min(m, mc)
        for m0 in range(0, m, step):
            xs = x_ref[0, m0:m0 + step, :].astype(F32)
            hg = _dot(xs, wg_ref[0, 0])
            hu = _dot(xs, wu_ref[0, 0])
            hid = hg * _sigmoid(hg) * hu
            tot = acc[m0:m0 + step, :] + _dot(hid, wd_ref[0, 0])
            acc[m0:m0 + step, :] = jnp.where(last, 0.0, tot)
            y_ref[0, m0:m0 + step, :] = tot.astype(BF16)


def _expert_ffn(xs, layer, w_gate, w_up, w_down):
    _, e, d, hidden = w_gate.shape
    tf = 512
    n = len(xs)
    xspecs = [pl.BlockSpec((1, x.shape[1], d), lambda i, j: (i, 0, 0)) for x in xs]
    return pl.pallas_call(
        functools.partial(_ffn_kernel, n_sets=n, mc=1024),
        grid=(e, hidden // tf),
        in_specs=xspecs + [pl.BlockSpec((1, 1, d, tf), lambda i, j: (layer, i, 0, j)),
                           pl.BlockSpec((1, 1, d, tf), lambda i, j: (layer, i, 0, j)),
                           pl.BlockSpec((1, 1, tf, d), lambda i, j: (layer, i, j, 0))],
        out_specs=xspecs,
        out_shape=[jax.ShapeDtypeStruct(x.shape, BF16) for x in xs],
        scratch_shapes=[pltpu.VMEM((x.shape[1], d), F32) for x in xs],
        compiler_params=_params(("arbitrary", "arbitrary"), 60),
        name="expert_ffn",
    )(*xs, w_gate, w_up, w_down)


def _scatter_kernel(x_ref, y_ref, slot_ref, gate_ref, mod_ref, fn_ref, o_ref, *, final):
    tt = x_ref.shape[1]
    e, cap, _ = y_ref.shape
    eye = (_iota((tt, tt), 0) == _iota((tt, tt), 1)).astype(BF16)
    slot_t = _dot_nt(eye, (slot_ref[0] + 1.0).astype(BF16))
    gate_t = _dot_nt(eye, gate_ref[0].astype(BF16))
    lane = _iota((tt, cap), 1).astype(F32) + 1.0
    acc = None
    for i in range(e):
        pt = jnp.where(slot_t[:, i:i + 1] == lane, gate_t[:, i:i + 1], 0.0).astype(BF16)
        part = _dot(pt, y_ref[i])
        acc = part if acc is None else acc + part
    out = x_ref[0] + mod_ref[0][5:6] * acc
    if final:
        out = out * lax.rsqrt(jnp.mean(out * out, axis=-1, keepdims=True) + EPS) * fn_ref[...]
    o_ref[0] = out


def _scatter(x, y, slot, gate, mod, final_norm, cap, final):
    b, t, d = x.shape
    e = slot.shape[1]
    tt = min(t, 512)
    return pl.pallas_call(
        functools.partial(_scatter_kernel, final=final),
        grid=(b, t // tt),
        in_specs=[pl.BlockSpec((1, tt, d), lambda i, j: (i, j, 0)),
                  pl.BlockSpec((e, cap, d), lambda i, j: (0, i, 0)),
                  pl.BlockSpec((1, e, tt), lambda i, j: (i, 0, j)),
                  pl.BlockSpec((1, e, tt), lambda i, j: (i, 0, j)),
                  pl.BlockSpec((1, 8, d), lambda i, j: (i, 0, 0)),
                  pl.BlockSpec((1, d), lambda i, j: (0, 0))],
        out_specs=pl.BlockSpec((1, tt, d), lambda i, j: (i, j, 0)),
        out_shape=jax.ShapeDtypeStruct((b, t, d), F32),
        compiler_params=_params(("arbitrary", "arbitrary"), 48),
        name="scatter",
    )(x, y, slot, gate, mod, final_norm)


def _moe(streams, layer, w_gate, w_up, w_down, final_norm, final):
    routed = []
    for x1, hx2, logits, mod in streams:
        t = x1.shape[1]
        cap = CAPACITY_FACTOR * t // N_EXPERTS
        slot, gate, offs = _route(logits, cap)
        nblk = t // TOKEN_BLOCK
        if nblk > 1:
            offs = offs[:, :, :nblk + 1].astype(jnp.int32).reshape(offs.shape[0], -1)
            slot, gate = _by_token_block(slot), _by_token_block(gate)
            xg = _sparse_gather(hx2, slot, offs, cap)
        else:
            offs = None
            xg = _gather(hx2, slot, cap)
        routed.append((slot, gate, offs, cap, xg))
    ys = _expert_ffn([r[4] for r in routed], layer, w_gate, w_up, w_down)
    outs = []
    for (x1, _, _, mod), (slot, gate, offs, cap, _), y in zip(streams, routed, ys):
        if offs is None:
            outs.append(_scatter(x1, y, slot, gate, mod, final_norm, cap, final))
        else:
            outs.append(_sparse_scatter(x1, y, slot, gate, offs, mod, final_norm, cap, final))
    return outs


def _inproj_kernel(x_ref, mod_ref, nmix_ref, wko_ref, wqv_ref, wg_ref, bg_ref, *out_refs, chunk, qk, dk, queries):
    if queries:
        qt_ref, k_ref, vt_ref, o_ref, gr_ref = out_refs
    else:
        k_ref, vt_ref, gr_ref = out_refs
    tt = x_ref.shape[1]
    mod = mod_ref[0]
    h = _norm_mod(x_ref[0], nmix_ref[...], mod[0:1], mod[1:2])
    hb = h.astype(BF16)
    p = _dot(hb, wko_ref[...])
    k_ref[0] = (p[:, :qk] * (dk ** -0.5)).astype(BF16)
    if queries:
        o_ref[0] = _sigmoid(p[:, qk:]).astype(o_ref.dtype)
    p_t = lax.dot_general(wqv_ref[...], hb, (((0,), (1,)), ((), ())), preferred_element_type=F32)
    nq = qk if queries else 0
    for ci in range(tt // chunk):
        if queries:
            qt_ref[0, ci] = p_t[:qk, ci * chunk:(ci + 1) * chunk].astype(BF16)
        vt_ref[0, ci] = p_t[nq:, ci * chunk:(ci + 1) * chunk].astype(BF16)

    nh = MLSTM_HEADS
    g_t = _transpose_exact(_dot_split(h, wg_ref[...]) + bg_ref[...])
    row = _iota((4 * nh, chunk), 0)
    is_f = (row & nh) == nh
    a = _iota((chunk, chunk), 0)
    c = _iota((chunk, chunk), 1)
    tri = jnp.concatenate([(a <= c).astype(BF16), (a >= c).astype(BF16)], axis=1)
    ng = 4 * nh
    for c0 in range(0, tt, chunk):
        gc = g_t[:, c0:c0 + chunk]
        pieces = _split_bf16(jnp.where(is_f, _log_sigmoid(gc), 0.0), 3)
        cum = _dot(jnp.concatenate(pieces, axis=0), tri)
        cum = cum[:ng] + (cum[ng:2 * ng] + cum[2 * ng:])
        gr_ref[0, :, c0:c0 + chunk] = jnp.where(is_f, jnp.where(row < 2 * nh, cum[:, :chunk], cum[:, chunk:]), gc)


def _inproj(x, mod, nmix, w_ko, w_qv, w_g, b_g, qk, vd, queries):
    b, t, d = x.shape
    tt = min(t, 1024)
    ng = w_g.shape[1]
    ln = SCAN_CHUNK
    full = lambda shape: pl.BlockSpec(shape, lambda i, j: (0,) * len(shape))
    tok = lambda n: pl.BlockSpec((1, tt, n), lambda i, j: (i, j, 0))
    slab = lambda n: pl.BlockSpec((1, tt // ln, n, ln), lambda i, j: (i, j, 0, 0))
    outs = [(slab(qk), jax.ShapeDtypeStruct((b, t // ln, qk, ln), BF16))] if queries else []
    outs += [(tok(qk), jax.ShapeDtypeStruct((b, t, qk), BF16)),
             (slab(vd), jax.ShapeDtypeStruct((b, t // ln, vd, ln), BF16))]
    outs += [(tok(vd), jax.ShapeDtypeStruct((b, t, vd), BF16))] if queries else []
    outs += [(pl.BlockSpec((1, ng, tt), lambda i, j: (i, 0, j)), jax.ShapeDtypeStruct((b, ng, t), F32))]
    return pl.pallas_call(
        functools.partial(_inproj_kernel, chunk=ln, qk=qk, dk=qk // MLSTM_HEADS, queries=queries),
        grid=(b, t // tt),
        in_specs=[tok(d), pl.BlockSpec((1, 8, d), lambda i, j: (i, 0, 0)), full((1, d)),
                  full(w_ko.shape), full(w_qv.shape), full(w_g.shape), full(b_g.shape)],
        out_specs=[o[0] for o in outs],
        out_shape=[o[1] for o in outs],
        compiler_params=_params(("arbitrary", "arbitrary"), 56),
        name="mlstm_inproj",
    )(x, mod, nmix, w_ko, w_qv, w_g, b_g)


def _scan_kernel(kc_ref, vc_ref, grc_ref, qx_ref, kx_ref, vx_ref, grx_ref,
                 out_ref, s_ref, sall_ref, mall_ref, *, chunk):
    ln = chunk
    nh = MLSTM_HEADS
    pair = pl.program_id(1)
    kl = kx_ref.shape[2]
    dv = vx_ref.shape[2] // 2
    lane = _iota((ln, kl), 1)
    kmask = (lane < kl // 2, lane >= kl // 2)
    klane = _iota((2 * dv, kl), 1)
    kcols = (klane < kl // 2, klane >= kl // 2)
    ones = jnp.ones((dv, ln), BF16)
    si = _iota((ln, ln), 0)
    ji = _iota((ln, ln), 1)
    causal = (si <= ji, si >= ji)
    ncc = kc_ref.shape[1] // ln
    ncx = kx_ref.shape[1] // ln

    def gate_rows(gr_ref, hh, dirn, c):
        base = 2 * nh * dirn + 2 * pair + hh
        return gr_ref[0, base, pl.ds(c, 1), :], gr_ref[0, base + nh, pl.ds(c, 1), :]

    def keys(k_ref, c):
        k = k_ref[0, pl.ds(pl.multiple_of(c * ln, ln), ln), :]
        return [jnp.where(kmask[hh], k, jnp.zeros((), BF16)) for hh in range(2)]

    def values_t(v_ref, c, hh):
        return jnp.concatenate([v_ref[0, c, hh * dv:(hh + 1) * dv, :], ones], axis=0)

    def advance(refs, n, record):
        k_ref, v_ref, gr_ref = refs

        def body(i, ms):
            new_ms = [None] * 4
            cs = (i, n - 1 - i)
            for dirn in range(2):
                c = cs[dirn]
                vws, decays = [], []
                for hh in range(2):
                    ch = 2 * hh + dirn
                    ig, bc = gate_rows(gr_ref, hh, dirn, c)
                    m = ms[ch]
                    if record:
                        s_own = jnp.where(kcols[hh], s_ref[ch], 0.0)
                        sall_ref[ch * ncx + c] = s_own.astype(BF16)
                        mall_ref[ch * ncx + c] = jnp.broadcast_to(m, (8, ln))
                    b_end = bc[:, ln - 1:ln] if dirn == 0 else bc[:, 0:1]
                    gl = b_end - bc + ig
                    m_new = jnp.maximum(b_end + m, jnp.max(gl, axis=1, keepdims=True))
                    vws.append((values_t(v_ref, c, hh) * jnp.exp(gl - m_new)).astype(BF16))
                    decays.append(jnp.exp(b_end + m - m_new))
                    new_ms[ch] = m_new
                upd = _dot(jnp.concatenate(vws, axis=0), k_ref[0, pl.ds(pl.multiple_of(c * ln, ln), ln), :])
                for hh in range(2):
                    ch = 2 * hh + dirn
                    s_ref[ch] = decays[hh] * s_ref[ch] + upd[hh * 2 * dv:(hh + 1) * 2 * dv, :]
            return tuple(new_ms)

        return body

    s_ref[...] = jnp.zeros(s_ref.shape, F32)
    ms = tuple(jnp.zeros((1, 1), F32) for _ in range(4))
    ms = lax.fori_loop(0, ncc, advance((kc_ref, vc_ref, grc_ref), ncc, False), ms, unroll=min(ncc, SCAN_UNROLL))
    lax.fori_loop(0, ncx, advance((kx_ref, vx_ref, grx_ref), ncx, True), ms, unroll=min(ncx, SCAN_UNROLL))

    def emit(c, _):
        q_t = qx_ref[0, c]
        q_tf = q_t.astype(F32)
        s_pair = _dot(jnp.concatenate(keys(kx_ref, c), axis=0), q_t)
        for hh in range(2):
            s_kq = s_pair[hh * ln:(hh + 1) * ln, :]
            v_t = values_t(vx_ref, c, hh)
            hsum = None
            for dirn in range(2):
                ch = 2 * hh + dirn
                ig, bc = gate_rows(grx_ref, hh, dirn, c)
                u_col = jnp.broadcast_to(ig - bc, (ln, ln)).T
                a = bc + mall_ref[ch * ncx + c][0:1, :]
                dm = jnp.where(causal[dirn], u_col + bc, -jnp.inf)
                mj = jnp.maximum(a, jnp.max(dm, axis=0, keepdims=True))
                sm = (s_kq * jnp.exp(dm - mj)).astype(BF16)
                qw = (q_tf * jnp.exp(a - mj)).astype(BF16)
                num = _dot(jnp.concatenate([v_t, sall_ref[ch * ncx + c]], axis=1),
                           jnp.concatenate([sm, qw], axis=0))
                h = num[:dv, :] * (1.0 / jnp.maximum(jnp.abs(num[dv:dv + 1, :]), jnp.exp(-mj)))
                hsum = h if hsum is None else hsum + h
            hn = hsum * lax.rsqrt(jnp.mean(hsum * hsum, axis=0, keepdims=True) + EPS)
            out_ref[0, pl.ds(pl.multiple_of(c * ln, ln), ln), hh * dv:(hh + 1) * dv] = hn.T.astype(out_ref.dtype)
        return 0

    lax.fori_loop(0, ncx, emit, 0, unroll=min(ncx, SCAN_UNROLL))


def _scan(ctx_parts, lat_parts):
    kc, vc, grc = ctx_parts
    qx, kx, vx, grx = lat_parts
    b, t, qk = kx.shape
    vd = vx.shape[2]
    nh = MLSTM_HEADS
    ln = SCAN_CHUNK
    dv = vd // nh
    kl = 2 * (qk // nh)
    assert kl == ln and dv == ln, "scan kernel assumes key-pair lanes = head value dim = chunk"
    ncx = t // ln

    def specs(k, v, gr):
        tk = k.shape[1]
        return [pl.BlockSpec((1, tk, kl), lambda i, j: (i, 0, j)),
                pl.BlockSpec((1, tk // ln, 2 * dv, ln), lambda i, j: (i, 0, j, 0)),
                pl.BlockSpec((1,) + gr.shape[1:], lambda i, j: (i, 0, 0, 0))]

    return pl.pallas_call(
        functools.partial(_scan_kernel, chunk=ln),
        grid=(b, nh // 2),
        in_specs=specs(kc, vc, grc) + [pl.BlockSpec((1, ncx, kl, ln), lambda i, j: (i, 0, j, 0))] + specs(kx, vx, grx),
        out_specs=pl.BlockSpec((1, t, 2 * dv), lambda i, j: (i, 0, j)),
        out_shape=jax.ShapeDtypeStruct((b, t, vd), BF16),
        scratch_shapes=[pltpu.VMEM((4, 2 * dv, kl), F32),
                        pltpu.VMEM((4 * ncx, 2 * dv, kl), BF16),
                        pltpu.VMEM((4 * ncx, 8, ln), F32)],
        compiler_params=_params(("arbitrary", "arbitrary"), 40),
        name="mlstm_scan",
    )(kc, vc, grc, qx, kx, vx, grx)


def _mlstm_out_kernel(hn_ref, og_ref, x_ref, mod_ref, mn_ref, wout_ref, nffn_ref, wr_ref,
                      x1_ref, hx2_ref, lg_ref):
    mod = mod_ref[0]
    a = (hn_ref[0].astype(F32) * mn_ref[...]) * og_ref[0].astype(F32)
    xn = x_ref[0] + mod[2:3] * _dot(a.astype(BF16), wout_ref[...])
    x1_ref[0] = xn
    hb, lg = _ffn_pre(xn, mod, nffn_ref[...], wr_ref[...])
    hx2_ref[0] = hb
    lg_ref[0] = lg


def _mlstm_out(hn, og, x, mod, mnorm, w_out, nffn, w_router):
    b, t, d = x.shape
    vd = hn.shape[2]
    e = w_router.shape[1]
    tt = min(t, 1024)
    full = lambda shape: pl.BlockSpec(shape, lambda i, j: (0,) * len(shape))
    tok = lambda n: pl.BlockSpec((1, tt, n), lambda i, j: (i, j, 0))
    return pl.pallas_call(
        _mlstm_out_kernel,
        grid=(b, t // tt),
        in_specs=[tok(vd), tok(vd), tok(d), pl.BlockSpec((1, 8, d), lambda i, j: (i, 0, 0)),
                  full((1, vd)), full(w_out.shape), full((1, d)), full(w_router.shape)],
        out_specs=[tok(d), tok(d), pl.BlockSpec((1, e, tt), lambda i, j: (i, 0, j))],
        out_shape=[jax.ShapeDtypeStruct((b, t, d), F32), jax.ShapeDtypeStruct((b, t, d), BF16),
                   jax.ShapeDtypeStruct((b, e, t), F32)],
        compiler_params=_params(("arbitrary", "arbitrary"), 48),
        name="mlstm_out",
    )(hn, og, x, mod, mnorm, w_out, nffn, w_router)


def _chunked_rows(gr):
    b, n, t = gr.shape
    return gr.reshape(b, n, t // SCAN_CHUNK, SCAN_CHUNK)


def kernel(x, c, ctx, c_ctx, ada_w, ada_b, norm_mix, norm_ffn, pool_w, pool_scale, mlstm_w_in, mlstm_b_gates,
           mlstm_norm, mlstm_w_out, moe_router, moe_w_gate, moe_w_up, moe_w_down, final_norm):
    bsz, seq, d = x.shape
    ctx_len = ctx.shape[1]
    depth = ada_w.shape[0]
    n_mixers = 2

    cc = jnp.concatenate([c, c_ctx[None, :], jnp.zeros((16 - bsz - 1, d), F32)], axis=0)
    ada = _ada(cc, ada_w, ada_b).reshape(depth, 16, N_ADA, d)
    pad = jnp.zeros((bsz, 8 - N_ADA, d), F32)

    fn = final_norm.reshape(1, d)
    for i in range(depth):
        last = i == depth - 1
        j = i // n_mixers
        mod_x = jnp.concatenate([ada[i, :bsz], pad], axis=1)
        mod_c = jnp.concatenate([jnp.broadcast_to(ada[i, bsz][None], (bsz, N_ADA, d)), pad], axis=1)
        nmix = norm_mix[i].reshape(1, d)
        nffn = norm_ffn[i].reshape(1, d)
        w_router = moe_router[i]
        streams = []
        if i % n_mixers == 0:
            pw = pool_w[j].astype(BF16)
            ps = pool_scale[j].reshape(1, d)
            x1, hx2, lg = _pool_mixer(x, mod_x, nmix, nffn, pw, ps, w_router, GRID_W, True)
            streams.append((x1, hx2, lg, mod_x))
            if not last:
                c1, hc2, lgc = _pool_mixer(ctx.reshape(1, bsz * ctx_len, d), mod_c[:1], nmix, nffn, pw, ps, w_router,
                                           ctx_len, False)
                lgc = jnp.swapaxes(lgc.reshape(-1, bsz, ctx_len), 0, 1)
                streams.append((c1.reshape(bsz, ctx_len, d), hc2.reshape(bsz, ctx_len, d), lgc, mod_c))
        else:
            qk = mlstm_w_in.shape[2] - 2 * mlstm_w_out.shape[1] - 4 * MLSTM_HEADS
            qk //= 2
            vd = mlstm_w_out.shape[1]
            w_in = mlstm_w_in[j]
            w_ko = jnp.concatenate([w_in[:, qk:2 * qk], w_in[:, 2 * qk + vd:2 * qk + 2 * vd]], axis=1).astype(BF16)
            w_qv = jnp.concatenate([w_in[:, :qk], w_in[:, 2 * qk:2 * qk + vd]], axis=1).astype(BF16)
            w_g = w_in[:, 2 * qk + 2 * vd:]
            b_g = mlstm_b_gates[j].reshape(1, -1)
            proj = lambda s, m: _inproj(s, m, nmix, w_ko, w_qv, w_g, b_g, qk, vd, True)
            kc, vc, grc = _inproj(ctx.reshape(1, bsz * ctx_len, d), mod_c[:1], nmix, w_ko[:, :qk], w_qv[:, qk:],
                                  w_g, b_g, qk, vd, False)
            kc = kc.reshape(bsz, ctx_len, qk)
            vc = vc.reshape(bsz, ctx_len // SCAN_CHUNK, vd, SCAN_CHUNK)
            grc = jnp.swapaxes(grc.reshape(-1, bsz, ctx_len), 0, 1)
            qx, kx, vx, ox, grx = proj(x, mod_x)
            hn = _scan((kc, vc, _chunked_rows(grc)), (qx, kx, vx, _chunked_rows(grx)))
            x1, hx2, lg = _mlstm_out(hn, ox, x, mod_x, mlstm_norm[j].reshape(1, vd),
                                     mlstm_w_out[j].astype(BF16), nffn, w_router)
            streams.append((x1, hx2, lg, mod_x))
            assert last, "context output of the mLSTM mixer is only needed by a following layer"
        outs = _moe(streams, i, moe_w_gate, moe_w_up, moe_w_down, fn, last)
        x = outs[0]
        if not last:
            ctx = outs[1]
    return x
```

```python
import functools

import jax
import jax.numpy as jnp
import numpy as np
from jax import lax
from jax.experimental import pallas as pl
from jax.experimental.pallas import tpu as pltpu

F32 = jnp.float32
BF16 = jnp.bfloat16

GRID_W = 64
EPS = 1e-6
N_ADA = 6
POOL_WINDOWS = (2, 4, 8, 16)
N_POOL_GROUPS = 4
MLSTM_HEADS = 8
N_EXPERTS = 16
CAPACITY_FACTOR = 2

MIB = 1024 * 1024
MXU_DIM = 256
LANES = 128
TOKEN_BLOCK = 256
SLOT_GROUP = 32
PAIRS_PER_DOT = 16
STATIC_DOTS = 2
SCAN_CHUNK = 128
SCAN_UNROLL = 8
POOL_PAD_ROWS = max(POOL_WINDOWS) // 2


def _params(sem, vmem_mib):
    return pltpu.CompilerParams(dimension_semantics=sem, vmem_limit_bytes=vmem_mib * MIB)


def _dot(a, b):
    return jnp.dot(a, b, preferred_element_type=F32)


def _dot_nt(a, b):
    return lax.dot_general(a, b, (((1,), (1,)), ((), ())), preferred_element_type=F32)


def _iota(shape, dim, dtype=jnp.int32):
    return lax.broadcasted_iota(dtype, shape, dim)


def _sigmoid(x):
    return 1.0 / (1.0 + jnp.exp(-x))


def _log_sigmoid(x):
    return jnp.minimum(x, 0.0) - jnp.log1p(jnp.exp(-jnp.abs(x)))


def _norm_mod(x, g, shift, scale):
    inv = lax.rsqrt(jnp.mean(x * x, axis=-1, keepdims=True) + EPS)
    return (x * inv) * (g * (1.0 + scale)) + shift


def _ada_kernel(c_ref, w_ref, b_ref, o_ref):
    c = c_ref[...]
    s = c * _sigmoid(c)
    s_hi, s_lo = _split_bf16(s, 2)
    w_hi, w_lo = _split_bf16(w_ref[0], 2)
    o_ref[0] = _dot(s_hi, w_hi) + (_dot(s_hi, w_lo) + _dot(s_lo, w_hi)) + b_ref[0]


def _ada(cc, ada_w, ada_b):
    depth, d, n = ada_w.shape
    rows = cc.shape[0]
    tn = n // 4
    return pl.pallas_call(
        _ada_kernel,
        grid=(depth, n // tn),
        in_specs=[pl.BlockSpec((rows, d), lambda i, j: (0, 0)),
                  pl.BlockSpec((1, d, tn), lambda i, j: (i, 0, j)),
                  pl.BlockSpec((1, 1, tn), lambda i, j: (i, 0, j))],
        out_specs=pl.BlockSpec((1, rows, tn), lambda i, j: (i, 0, j)),
        out_shape=jax.ShapeDtypeStruct((depth, rows, n), F32),
        compiler_params=_params(("arbitrary", "arbitrary"), 40),
        name="ada",
    )(cc, ada_w, ada_b.reshape(depth, 1, n))


def _split_bf16(x, pieces):
    out = []
    for _ in range(pieces):
        p = x.astype(BF16)
        out.append(p)
        x = x - p.astype(F32)
    return out


def _dot_split(a, b):
    n = b.shape[1]
    a_hi, a_lo = _split_bf16(a, 2)
    b_hi, b_lo = _split_bf16(b, 2)
    hi = _dot(a_hi, jnp.concatenate([b_hi, b_lo], axis=1))
    return hi[:, :n] + (hi[:, n:] + _dot(a_lo, b_hi))


def _transpose_exact(x):
    m = x.shape[1]
    eye = (_iota((m, m), 0) == _iota((m, m), 1)).astype(BF16)
    hi, mid, lo = _split_bf16(x, 3)
    return _dot_nt(eye, hi) + (_dot_nt(eye, mid) + _dot_nt(eye, lo))


def _ffn_pre(xn, mod, nffn, wr):
    h2 = _norm_mod(xn, nffn, mod[3:4], mod[4:5])
    return h2.astype(BF16), _transpose_exact(_dot_split(h2, wr))


def _pool_kernel(x_ref, mod_ref, nmix_ref, nffn_ref, cmat_ref, cnt_ref, pw_ref, ps_ref, wr_ref,
                 x1_ref, hx2_ref, lg_ref, *pad, two_d, tc):
    t, d = x_ref.shape[1], x_ref.shape[2]
    gd = d // N_POOL_GROUPS
    mod = mod_ref[0]
    pad_tok = POOL_PAD_ROWS * GRID_W

    for c0 in range(0, t, tc):
        x1_ref[0, c0:c0 + tc, :] = _norm_mod(x_ref[0, c0:c0 + tc, :], nmix_ref[...], mod[0:1], mod[1:2])

    def hx_of(r0, rn, j):
        return x1_ref[0, r0:r0 + rn, j * gd:(j + 1) * gd]

    if two_d:
        pad_ref, = pad
        pad_ref[0:pad_tok, :] = jnp.zeros((pad_tok, gd), F32)
        pad_ref[pad_tok + t:pad_tok + t + pad_tok, :] = jnp.zeros((pad_tok, gd), F32)

    for j, w in enumerate(POOL_WINDOWS):
        cs = slice(j * gd, (j + 1) * gd)
        cm = cmat_ref[j]
        sums = []
        for b0 in range(0, t, MXU_DIM):
            g = hx_of(b0, MXU_DIM, j)
            g_hi = g.astype(BF16)
            g_lo = (g - g_hi.astype(F32)).astype(BF16)
            csum = _dot(cm, g_hi) + _dot(cm, g_lo)
            if two_d:
                pad_ref[pad_tok + b0:pad_tok + b0 + MXU_DIM, :] = csum
            else:
                sums.append(csum)
        for c0 in range(0, t, tc):
            if two_d:
                tot = None
                for dr in range(-(w // 2), w - w // 2):
                    o = pad_tok + c0 + dr * GRID_W
                    sl = pad_ref[o:o + tc, :]
                    tot = sl if tot is None else tot + sl
            else:
                blocks = sums[c0 // MXU_DIM:(c0 + tc) // MXU_DIM]
                tot = blocks[0] if len(blocks) == 1 else jnp.concatenate(blocks, axis=0)
            mean = tot / cnt_ref[c0:c0 + tc, j:j + 1]
            diff = mean - hx_of(c0, tc, j)
            y = _dot(diff.astype(BF16), pw_ref[j]) * ps_ref[:, cs]
            x1_ref[0, c0:c0 + tc, cs] = x_ref[0, c0:c0 + tc, cs] + mod[2:3, cs] * y

    tail = min(t, 2 * tc)
    for c0 in range(0, t, tail):
        hb, lg = _ffn_pre(x1_ref[0, c0:c0 + tail, :], mod, nffn_ref[...], wr_ref[...])
        hx2_ref[0, c0:c0 + tail, :] = hb
        lg_ref[0, :, c0:c0 + tail] = lg


def _pool_consts(t, seq, two_d):
    cm = np.zeros((len(POOL_WINDOWS), MXU_DIM, MXU_DIM), np.float32)
    cnt = np.zeros((t, len(POOL_WINDOWS)), np.float32)
    n = seq
    assert MXU_DIM % n == 0
    pos = np.arange(MXU_DIM)
    for j, w in enumerate(POOL_WINDOWS):
        col = pos % n
        lo = np.clip(col - w // 2, 0, n)
        hi = np.clip(col + w - w // 2, 0, n)
        same = (pos[:, None] // n) == (pos[None, :] // n)
        cm[j] = (same & (col[None, :] >= lo[:, None]) & (col[None, :] < hi[:, None])).astype(np.float32)
        tt = np.arange(t)
        c = tt % n
        ccnt = np.clip(c + w - w // 2, 0, n) - np.clip(c - w // 2, 0, n)
        if two_d:
            rows = t // n
            r = tt // n
            rcnt = np.clip(r + w - w // 2, 0, rows) - np.clip(r - w // 2, 0, rows)
            cnt[:, j] = ccnt * rcnt
        else:
            cnt[:, j] = ccnt
    return jnp.asarray(cm, BF16), jnp.asarray(cnt)


def _pool_mixer(x, mod, nmix, nffn, pool_w, pool_scale, w_router, seq, two_d):
    b, t, d = x.shape
    gd = d // N_POOL_GROUPS
    assert seq == GRID_W or not two_d, "row-window shifts assume GRID_W tokens per grid row"
    cmat, cnt = _pool_consts(t, seq, two_d)
    tc = min(t, 512)
    pad_tok = POOL_PAD_ROWS * GRID_W
    scratch = []
    if two_d:
        scratch.append(pltpu.VMEM((t + 2 * pad_tok, gd), F32))
    e = w_router.shape[1]
    full = lambda shape: pl.BlockSpec(shape, lambda i: (0,) * len(shape))
    return pl.pallas_call(
        functools.partial(_pool_kernel, two_d=two_d, tc=tc),
        grid=(b,),
        in_specs=[pl.BlockSpec((1, t, d), lambda i: (i, 0, 0)),
                  pl.BlockSpec((1, 8, d), lambda i: (i, 0, 0)),
                  full((1, d)), full((1, d)), full(cmat.shape), full(cnt.shape),
                  full(pool_w.shape), full((1, d)), full(w_router.shape)],
        out_specs=[pl.BlockSpec((1, t, d), lambda i: (i, 0, 0)),
                   pl.BlockSpec((1, t, d), lambda i: (i, 0, 0)),
                   pl.BlockSpec((1, e, t), lambda i: (i, 0, 0))],
        out_shape=[jax.ShapeDtypeStruct((b, t, d), F32),
                   jax.ShapeDtypeStruct((b, t, d), BF16),
                   jax.ShapeDtypeStruct((b, e, t), F32)],
        scratch_shapes=scratch,
        compiler_params=_params(("arbitrary",), 60),
        name="pool_mixer_2d" if two_d else "pool_mixer_1d",
    )(x, mod, nmix, nffn, cmat, cnt, pool_w, pool_scale, w_router)


def _route_kernel(lg_ref, slot_ref, gate_ref, offs_ref, *, cap):
    nb, e, t = lg_ref.shape
    capf = jnp.float32(cap)
    affs = []
    for s in range(nb):
        lg = lg_ref[s]
        ex = jnp.exp(lg - jnp.max(lg, axis=0, keepdims=True))
        affs.append(ex / jnp.sum(ex, axis=0, keepdims=True))

    def as_f32(v):
        return lax.bitcast_convert_type(v, F32)

    def enough(aff, cand):
        return jnp.sum(jnp.where(aff >= as_f32(cand), 1.0, 0.0), axis=1, keepdims=True) >= capf

    def bisect2(i, vs):
        hi = jnp.left_shift(jnp.int32(1), 30 - 2 * i)
        lo = jnp.left_shift(jnp.int32(1), 29 - 2 * i)
        out = []
        for aff, v in zip(affs, vs):
            c_hi, c_lo, c_both = v | hi, v | lo, v | hi | lo
            out.append(jnp.where(enough(aff, c_both), c_both,
                                 jnp.where(enough(aff, c_hi), c_hi, jnp.where(enough(aff, c_lo), c_lo, v))))
        return tuple(out)

    kths = lax.fori_loop(0, 15, bisect2, tuple(jnp.zeros((e, 1), jnp.int32) for _ in range(nb)))

    blk = min(t, MXU_DIM)
    before = (_iota((blk, blk), 0) < _iota((blk, blk), 1)).astype(BF16)
    nl = offs_ref.shape[2]
    starts = (_iota((t, nl), 0) < _iota((t, nl), 1) * TOKEN_BLOCK).astype(BF16)

    def excl_cumsum(mask):
        ones = jnp.where(mask, 1.0, 0.0)
        outs, run = [], jnp.zeros((e, 1), F32)
        for b0 in range(0, t, blk):
            mb = ones[:, b0:b0 + blk]
            outs.append(_dot(mb.astype(BF16), before) + run)
            run = run + jnp.sum(mb, axis=1, keepdims=True)
        return jnp.concatenate(outs, axis=1) if len(outs) > 1 else outs[0]

    for s, (aff, kth) in enumerate(zip(affs, kths)):
        kth = jnp.where(enough(aff, kth | 1), kth | 1, kth)
        above = as_f32(kth + 1)
        gt = aff >= above
        eq = (aff >= as_f32(kth)) & jnp.logical_not(gt)
        need = capf - jnp.sum(jnp.where(gt, 1.0, 0.0), axis=1, keepdims=True)
        sel = gt | (eq & (excl_cumsum(eq) < need))
        slot_ref[s] = jnp.where(sel, excl_cumsum(sel), -1.0)
        gate_ref[s] = aff
        offs_ref[s] = _dot(jnp.where(sel, 1.0, 0.0).astype(BF16), starts)


def _route(logits, cap):
    b, e, t = logits.shape
    nb = 2 if b % 2 == 0 else 1
    spec = pl.BlockSpec((nb, e, t), lambda i: (i, 0, 0))
    ospec = pl.BlockSpec((nb, e, LANES), lambda i: (i, 0, 0))
    return pl.pallas_call(
        functools.partial(_route_kernel, cap=cap),
        grid=(b // nb,),
        in_specs=[spec],
        out_specs=[spec, spec, ospec],
        out_shape=[jax.ShapeDtypeStruct((b, e, t), F32)] * 2 + [jax.ShapeDtypeStruct((b, e, LANES), F32)],
        compiler_params=_params(("arbitrary",), 32),
        name="route",
    )(logits)


def _gather_kernel(h_ref, slot_ref, o_ref, *, cap):
    hx = h_ref[0]
    t = hx.shape[0]
    row = _iota((cap, t), 0).astype(F32)
    for i in range(o_ref.shape[0]):
        onehot = jnp.where(slot_ref[0, i:i + 1, :] == row, 1.0, 0.0).astype(BF16)
        o_ref[i] = _dot(onehot, hx).astype(BF16)


def _gather(hx2, slot, cap):
    b, t, d = hx2.shape
    e = slot.shape[1]
    eb = 8
    return pl.pallas_call(
        functools.partial(_gather_kernel, cap=cap),
        grid=(b, e // eb),
        in_specs=[pl.BlockSpec((1, t, d), lambda i, j: (i, 0, 0)),
                  pl.BlockSpec((1, eb, t), lambda i, j: (i, j, 0))],
        out_specs=pl.BlockSpec((eb, cap, d), lambda i, j: (j, i, 0)),
        out_shape=jax.ShapeDtypeStruct((e, b * cap, d), BF16),
        compiler_params=_params(("arbitrary", "arbitrary"), 48),
        name="gather",
    )(hx2, slot)


def _pair_list(offs_ref, kb, n_exp, nblk, le_ref, lg_ref):
    b = pl.program_id(0)
    shift = SLOT_GROUP.bit_length() - 1
    max_groups = le_ref.shape[0] // (n_exp + 1)

    cnt = jnp.int32(0)
    for e in range(n_exp):
        o0 = offs_ref[b, e * (nblk + 1) + kb]
        o1 = offs_ref[b, e * (nblk + 1) + kb + 1]
        lo = jnp.right_shift(o0, shift)
        ng = jnp.where(o1 > o0, jnp.right_shift(o1 - 1, shift) - lo + 1, 0)
        for i in range(max_groups):
            le_ref[cnt + i] = e
            lg_ref[cnt + i] = lo + i
        cnt = cnt + ng
    return cnt


def _pair(le_ref, lg_ref, idx, cnt):
    valid = idx < cnt
    safe = jnp.minimum(idx, jnp.maximum(cnt - 1, 0))
    e_p = jnp.where(valid, le_ref[safe], 0)
    g_p = jnp.where(valid, lg_ref[safe], 0)
    base = jnp.where(valid, g_p * SLOT_GROUP, -2 * SLOT_GROUP).astype(F32)
    return e_p, pl.multiple_of(g_p * SLOT_GROUP, SLOT_GROUP), base


def _for_each_dot(cnt, body):
    for c in range(STATIC_DOTS):
        body(c, 0)
    lax.fori_loop(STATIC_DOTS, (cnt + PAIRS_PER_DOT - 1) // PAIRS_PER_DOT, body, 0)


def _sparse_gather_kernel(offs_ref, h_ref, slot_ref, o_ref, le_ref, lg_ref, p_ref, *, n_exp, nblk):
    sg = SLOT_GROUP
    o_ref[...] = jnp.zeros(o_ref.shape, o_ref.dtype)
    sub = _iota((sg, TOKEN_BLOCK), 0).astype(F32)

    def per_block(kb, _):
        cnt = _pair_list(offs_ref, kb, n_exp, nblk, le_ref, lg_ref)
        tok = pl.ds(pl.multiple_of(kb * TOKEN_BLOCK, TOKEN_BLOCK), TOKEN_BLOCK)

        def dot_batch(c, _):
            dst = []
            for p in range(PAIRS_PER_DOT):
                e_p, s0, base = _pair(le_ref, lg_ref, c * PAIRS_PER_DOT + p, cnt)
                row = slot_ref[0, kb, pl.ds(e_p, 1), :]
                p_ref[p * sg:(p + 1) * sg, :] = jnp.where(row == base + sub, 1.0, 0.0).astype(BF16)
                dst.append((e_p, s0))
            z = _dot(p_ref[...], h_ref[0, tok, :])
            for p, (e_p, s0) in enumerate(dst):
                o_ref[e_p, pl.ds(s0, sg), :] += z[p * sg:(p + 1) * sg, :].astype(o_ref.dtype)
            return 0

        _for_each_dot(cnt, dot_batch)
        return 0

    lax.fori_loop(0, nblk, per_block, 0)


def _by_token_block(a):
    b, e, t = a.shape
    return jnp.swapaxes(a.reshape(b, e, t // TOKEN_BLOCK, TOKEN_BLOCK), 1, 2)


def _sparse_gather(hx2, slot_blocks, offs, cap):
    b, t, d = hx2.shape
    _, nblk, e, _ = slot_blocks.shape
    return pl.pallas_call(
        functools.partial(_sparse_gather_kernel, n_exp=e, nblk=nblk),
        grid_spec=pltpu.PrefetchScalarGridSpec(
            num_scalar_prefetch=1,
            grid=(b,),
            in_specs=[pl.BlockSpec((1, t, d), lambda i, o: (i, 0, 0)),
                      pl.BlockSpec((1, nblk, e, TOKEN_BLOCK), lambda i, o: (i, 0, 0, 0))],
            out_specs=pl.BlockSpec((e, cap, d), lambda i, o: (0, i, 0)),
            scratch_shapes=[pltpu.SMEM(((e + 1) * (cap // SLOT_GROUP),), jnp.int32),
                            pltpu.SMEM(((e + 1) * (cap // SLOT_GROUP),), jnp.int32),
                            pltpu.VMEM((PAIRS_PER_DOT * SLOT_GROUP, TOKEN_BLOCK), BF16)]),
        out_shape=jax.ShapeDtypeStruct((e, b * cap, d), BF16),
        compiler_params=_params(("arbitrary",), 40),
        name="sparse_gather",
    )(offs, hx2, slot_blocks)


def _sparse_scatter_kernel(offs_ref, x_ref, y_ref, slot_ref, gate_ref, mod_ref, fn_ref, o_ref,
                           le_ref, lg_ref, p_ref, yc_ref, acc_ref, *, n_exp, nblk, bps, final):
    sg = SLOT_GROUP
    sub = _iota((sg, TOKEN_BLOCK), 0).astype(F32)

    def per_block(i, _):
        cnt = _pair_list(offs_ref, pl.program_id(1) * bps + i, n_exp, nblk, le_ref, lg_ref)
        tok = pl.ds(pl.multiple_of(i * TOKEN_BLOCK, TOKEN_BLOCK), TOKEN_BLOCK)
        acc_ref[...] = jnp.zeros(acc_ref.shape, F32)

        def dot_batch(c, _):
            for p in range(PAIRS_PER_DOT):
                e_p, s0, base = _pair(le_ref, lg_ref, c * PAIRS_PER_DOT + p, cnt)
                row = slot_ref[0, i, pl.ds(e_p, 1), :]
                gate = gate_ref[0, i, pl.ds(e_p, 1), :]
                p_ref[p * sg:(p + 1) * sg, :] = jnp.where(row == base + sub, gate, 0.0)
                yc_ref[p * sg:(p + 1) * sg, :] = y_ref[e_p, pl.ds(s0, sg), :]
            acc_ref[...] += _dot(p_ref[...].T.astype(BF16), yc_ref[...])
            return 0

        _for_each_dot(cnt, dot_batch)
        out = x_ref[0, tok, :] + mod_ref[0][5:6] * acc_ref[...]
        if final:
            out = out * lax.rsqrt(jnp.mean(out * out, axis=-1, keepdims=True) + EPS) * fn_ref[...]
        o_ref[0, tok, :] = out
        return 0

    lax.fori_loop(0, bps, per_block, 0)


def _sparse_scatter(x, y, slot_blocks, gate_blocks, offs, mod, final_norm, cap, final):
    b, t, d = x.shape
    _, nblk, e, _ = slot_blocks.shape
    bps = min(nblk, 4)
    rows = PAIRS_PER_DOT * SLOT_GROUP
    tok = lambda n: pl.BlockSpec((1, bps * TOKEN_BLOCK, n), lambda i, j, o: (i, j, 0))
    exp = pl.BlockSpec((1, bps, e, TOKEN_BLOCK), lambda i, j, o: (i, j, 0, 0))
    return pl.pallas_call(
        functools.partial(_sparse_scatter_kernel, n_exp=e, nblk=nblk, bps=bps, final=final),
        grid_spec=pltpu.PrefetchScalarGridSpec(
            num_scalar_prefetch=1,
            grid=(b, nblk // bps),
            in_specs=[tok(d), pl.BlockSpec((e, cap, d), lambda i, j, o: (0, i, 0)), exp, exp,
                      pl.BlockSpec((1, 8, d), lambda i, j, o: (i, 0, 0)),
                      pl.BlockSpec((1, d), lambda i, j, o: (0, 0))],
            out_specs=tok(d),
            scratch_shapes=[pltpu.SMEM(((e + 1) * (cap // SLOT_GROUP),), jnp.int32),
                            pltpu.SMEM(((e + 1) * (cap // SLOT_GROUP),), jnp.int32),
                            pltpu.VMEM((rows, TOKEN_BLOCK), F32),
                            pltpu.VMEM((rows, d), BF16),
                            pltpu.VMEM((TOKEN_BLOCK, d), F32)]),
        out_shape=jax.ShapeDtypeStruct((b, t, d), F32),
        compiler_params=_params(("arbitrary", "arbitrary"), 48),
        name="sparse_scatter",
    )(offs, x, y, slot_blocks, gate_blocks, mod, final_norm)


def _ffn_kernel(*refs, n_sets, mc):
    x_refs = refs[:n_sets]
    wg_ref, wu_ref, wd_ref = refs[n_sets:n_sets + 3]
    y_refs = refs[n_sets + 3:2 * n_sets + 3]
    acc_refs = refs[2 * n_sets + 3:3 * n_sets + 3]
    f = pl.program_id(1)
    last = f == pl.num_programs(1) - 1

    @pl.when((pl.program_id(0) == 0) & (f == 0))
    def _():
        for acc in acc_refs:
            acc[...] = jnp.zeros(acc.shape, F32)

    for x_ref, y_ref, acc in zip(x_refs, y_refs, acc_refs):
        m = x_ref.shape[1]
        step = min(m, mc)
        for m0 in range(0, m, step):
            xs = x_ref[0, m0:m0 + step, :].astype(F32)
            hg = _dot(xs, wg_ref[0, 0])
            hu = _dot(xs, wu_ref[0, 0])
            hid = hg * _sigmoid(hg) * hu
            tot = acc[m0:m0 + step, :] + _dot(hid, wd_ref[0, 0])
            acc[m0:m0 + step, :] = jnp.where(last, 0.0, tot)
            y_ref[0, m0:m0 + step, :] = tot.astype(BF16)


def _expert_ffn(xs, layer, w_gate, w_up, w_down):
    _, e, d, hidden = w_gate.shape
    tf = 512
    n = len(xs)
    xspecs = [pl.BlockSpec((1, x.shape[1], d), lambda i, j: (i, 0, 0)) for x in xs]
    return pl.pallas_call(
        functools.partial(_ffn_kernel, n_sets=n, mc=1024),
        grid=(e, hidden // tf),
        in_specs=xspecs + [pl.BlockSpec((1, 1, d, tf), lambda i, j: (layer, i, 0, j)),
                           pl.BlockSpec((1, 1, d, tf), lambda i, j: (layer, i, 0, j)),
                           pl.BlockSpec((1, 1, tf, d), lambda i, j: (layer, i, j, 0))],
        out_specs=xspecs,
        out_shape=[jax.ShapeDtypeStruct(x.shape, BF16) for x in xs],
        scratch_shapes=[pltpu.VMEM((x.shape[1], d), F32) for x in xs],
        compiler_params=_params(("arbitrary", "arbitrary"), 60),
        name="expert_ffn",
    )(*xs, w_gate, w_up, w_down)


def _scatter_kernel(x_ref, y_ref, slot_ref, gate_ref, mod_ref, fn_ref, o_ref, *, final):
    tt = x_ref.shape[1]
    e, cap, _ = y_ref.shape
    eye = (_iota((tt, tt), 0) == _iota((tt, tt), 1)).astype(BF16)
    slot_t = _dot_nt(eye, (slot_ref[0] + 1.0).astype(BF16))
    gate_t = _dot_nt(eye, gate_ref[0].astype(BF16))
    assert cap & (cap - 1) == 0
    shift = cap.bit_length() - 1
    spread = (jnp.right_shift(_iota((e, e * cap), 1), shift) == _iota((e, e * cap), 0)).astype(BF16)
    slot_l = _dot(slot_t.astype(BF16), spread)
    gate_l = _dot(gate_t.astype(BF16), spread)
    lane = (_iota((tt, e * cap), 1) & (cap - 1)).astype(F32) + 1.0
    pt = jnp.where(slot_l == lane, gate_l, 0.0).astype(BF16)
    acc = _dot(pt, y_ref[...].reshape(e * cap, y_ref.shape[2]))
    out = x_ref[0] + mod_ref[0][5:6] * acc
    if final:
        out = out * lax.rsqrt(jnp.mean(out * out, axis=-1, keepdims=True) + EPS) * fn_ref[...]
    o_ref[0] = out


def _scatter(x, y, slot, gate, mod, final_norm, cap, final):
    b, t, d = x.shape
    e = slot.shape[1]
    tt = min(t, 512)
    return pl.pallas_call(
        functools.partial(_scatter_kernel, final=final),
        grid=(b, t // tt),
        in_specs=[pl.BlockSpec((1, tt, d), lambda i, j: (i, j, 0)),
                  pl.BlockSpec((e, cap, d), lambda i, j: (0, i, 0)),
                  pl.BlockSpec((1, e, tt), lambda i, j: (i, 0, j)),
                  pl.BlockSpec((1, e, tt), lambda i, j: (i, 0, j)),
                  pl.BlockSpec((1, 8, d), lambda i, j: (i, 0, 0)),
                  pl.BlockSpec((1, d), lambda i, j: (0, 0))],
        out_specs=pl.BlockSpec((1, tt, d), lambda i, j: (i, j, 0)),
        out_shape=jax.ShapeDtypeStruct((b, t, d), F32),
        compiler_params=_params(("arbitrary", "arbitrary"), 48),
        name="scatter",
    )(x, y, slot, gate, mod, final_norm)


def _moe(streams, layer, w_gate, w_up, w_down, final_norm, final):
    routed = []
    for x1, hx2, logits, mod in streams:
        t = x1.shape[1]
        cap = CAPACITY_FACTOR * t // N_EXPERTS
        slot, gate, offs = _route(logits, cap)
        nblk = t // TOKEN_BLOCK
        if nblk > 1:
            offs = offs[:, :, :nblk + 1].astype(jnp.int32).reshape(offs.shape[0], -1)
            slot, gate = _by_token_block(slot), _by_token_block(gate)
            xg = _sparse_gather(hx2, slot, offs, cap)
        else:
            offs = None
            xg = _gather(hx2, slot, cap)
        routed.append((slot, gate, offs, cap, xg))
    ys = _expert_ffn([r[4] for r in routed], layer, w_gate, w_up, w_down)
    outs = []
    for (x1, _, _, mod), (slot, gate, offs, cap, _), y in zip(streams, routed, ys):
        if offs is None:
            outs.append(_scatter(x1, y, slot, gate, mod, final_norm, cap, final))
        else:
            outs.append(_sparse_scatter(x1, y, slot, gate, offs, mod, final_norm, cap, final))
    return outs


def _inproj_kernel(x_ref, mod_ref, nmix_ref, wko_ref, wqv_ref, wg_ref, bg_ref, *out_refs, chunk, qk, dk, queries):
    if queries:
        qt_ref, k_ref, vt_ref, o_ref, gr_ref = out_refs
    else:
        k_ref, vt_ref, gr_ref = out_refs
    tt = x_ref.shape[1]
    mod = mod_ref[0]
    h = _norm_mod(x_ref[0], nmix_ref[...], mod[0:1], mod[1:2])
    hb = h.astype(BF16)
    p = _dot(hb, wko_ref[...])
    k_ref[0] = (p[:, :qk] * (dk ** -0.5)).astype(BF16)
    if queries:
        o_ref[0] = _sigmoid(p[:, qk:]).astype(o_ref.dtype)
    p_t = lax.dot_general(wqv_ref[...], hb, (((0,), (1,)), ((), ())), preferred_element_type=F32)
    nq = qk if queries else 0
    for ci in range(tt // chunk):
        if queries:
            qt_ref[0, ci] = p_t[:qk, ci * chunk:(ci + 1) * chunk].astype(BF16)
        vt_ref[0, ci] = p_t[nq:, ci * chunk:(ci + 1) * chunk].astype(BF16)

    nh = MLSTM_HEADS
    g_t = _transpose_exact(_dot_split(h, wg_ref[...]) + bg_ref[...])
    row = _iota((4 * nh, chunk), 0)
    is_f = (row & nh) == nh
    a = _iota((chunk, chunk), 0)
    c = _iota((chunk, chunk), 1)
    tri = jnp.concatenate([(a <= c).astype(BF16), (a >= c).astype(BF16)], axis=1)
    ng = 4 * nh
    for c0 in range(0, tt, chunk):
        gc = g_t[:, c0:c0 + chunk]
        pieces = _split_bf16(jnp.where(is_f, _log_sigmoid(gc), 0.0), 3)
        cum = _dot(jnp.concatenate(pieces, axis=0), tri)
        cum = cum[:ng] + (cum[ng:2 * ng] + cum[2 * ng:])
        gr_ref[0, :, c0:c0 + chunk] = jnp.where(is_f, jnp.where(row < 2 * nh, cum[:, :chunk], cum[:, chunk:]), gc)


def _inproj(x, mod, nmix, w_ko, w_qv, w_g, b_g, qk, vd, queries):
    b, t, d = x.shape
    tt = min(t, 1024)
    ng = w_g.shape[1]
    ln = SCAN_CHUNK
    full = lambda shape: pl.BlockSpec(shape, lambda i, j: (0,) * len(shape))
    tok = lambda n: pl.BlockSpec((1, tt, n), lambda i, j: (i, j, 0))
    slab = lambda n: pl.BlockSpec((1, tt // ln, n, ln), lambda i, j: (i, j, 0, 0))
    outs = [(slab(qk), jax.ShapeDtypeStruct((b, t // ln, qk, ln), BF16))] if queries else []
    outs += [(tok(qk), jax.ShapeDtypeStruct((b, t, qk), BF16)),
             (slab(vd), jax.ShapeDtypeStruct((b, t // ln, vd, ln), BF16))]
    outs += [(tok(vd), jax.ShapeDtypeStruct((b, t, vd), BF16))] if queries else []
    outs += [(pl.BlockSpec((1, ng, tt), lambda i, j: (i, 0, j)), jax.ShapeDtypeStruct((b, ng, t), F32))]
    return pl.pallas_call(
        functools.partial(_inproj_kernel, chunk=ln, qk=qk, dk=qk // MLSTM_HEADS, queries=queries),
        grid=(b, t // tt),
        in_specs=[tok(d), pl.BlockSpec((1, 8, d), lambda i, j: (i, 0, 0)), full((1, d)),
                  full(w_ko.shape), full(w_qv.shape), full(w_g.shape), full(b_g.shape)],
        out_specs=[o[0] for o in outs],
        out_shape=[o[1] for o in outs],
        compiler_params=_params(("arbitrary", "arbitrary"), 56),
        name="mlstm_inproj",
    )(x, mod, nmix, w_ko, w_qv, w_g, b_g)


def _scan_kernel(kc_ref, vc_ref, grc_ref, qx_ref, kx_ref, vx_ref, grx_ref,
                 out_ref, s_ref, sall_ref, mall_ref, *, chunk):
    ln = chunk
    nh = MLSTM_HEADS
    pair = pl.program_id(1)
    kl = kx_ref.shape[2]
    dv = vx_ref.shape[2] // 2
    lane = _iota((ln, kl), 1)
    kmask = (lane < kl // 2, lane >= kl // 2)
    klane = _iota((2 * dv, kl), 1)
    kcols = (klane < kl // 2, klane >= kl // 2)
    ones = jnp.ones((dv, ln), BF16)
    si = _iota((ln, ln), 0)
    ji = _iota((ln, ln), 1)
    causal = (si <= ji, si >= ji)
    ncc = kc_ref.shape[1] // ln
    ncx = kx_ref.shape[1] // ln

    def gate_rows(gr_ref, hh, dirn, c):
        base = 2 * nh * dirn + 2 * pair + hh
        return gr_ref[0, base, pl.ds(c, 1), :], gr_ref[0, base + nh, pl.ds(c, 1), :]

    def keys(k_ref, c):
        k = k_ref[0, pl.ds(pl.multiple_of(c * ln, ln), ln), :]
        return [jnp.where(kmask[hh], k, jnp.zeros((), BF16)) for hh in range(2)]

    def values_t(v_ref, c, hh):
        return jnp.concatenate([v_ref[0, c, hh * dv:(hh + 1) * dv, :], ones], axis=0)

    def advance(refs, n, record):
        k_ref, v_ref, gr_ref = refs

        def body(i, ms):
            new_ms = [None] * 4
            cs = (i, n - 1 - i)
            for dirn in range(2):
                c = cs[dirn]
                vws, decays = [], []
                for hh in range(2):
                    ch = 2 * hh + dirn
                    ig, bc = gate_rows(gr_ref, hh, dirn, c)
                    m = ms[ch]
                    if record:
                        s_own = jnp.where(kcols[hh], s_ref[ch], 0.0)
                        sall_ref[ch * ncx + c] = s_own.astype(BF16)
                        mall_ref[ch * ncx + c] = jnp.broadcast_to(m, (8, ln))
                    b_end = bc[:, ln - 1:ln] if dirn == 0 else bc[:, 0:1]
                    gl = b_end - bc + ig
                    m_new = jnp.maximum(b_end + m, jnp.max(gl, axis=1, keepdims=True))
                    vws.append((values_t(v_ref, c, hh) * jnp.exp(gl - m_new)).astype(BF16))
                    decays.append(jnp.exp(b_end + m - m_new))
                    new_ms[ch] = m_new
                upd = _dot(jnp.concatenate(vws, axis=0), k_ref[0, pl.ds(pl.multiple_of(c * ln, ln), ln), :])
                for hh in range(2):
                    ch = 2 * hh + dirn
                    s_ref[ch] = decays[hh] * s_ref[ch] + upd[hh * 2 * dv:(hh + 1) * 2 * dv, :]
            return tuple(new_ms)

        return body

    s_ref[...] = jnp.zeros(s_ref.shape, F32)
    ms = tuple(jnp.zeros((1, 1), F32) for _ in range(4))
    ms = lax.fori_loop(0, ncc, advance((kc_ref, vc_ref, grc_ref), ncc, False), ms, unroll=min(ncc, SCAN_UNROLL))
    lax.fori_loop(0, ncx, advance((kx_ref, vx_ref, grx_ref), ncx, True), ms, unroll=min(ncx, SCAN_UNROLL))

    def emit(c, _):
        q_t = qx_ref[0, c]
        q_tf = q_t.astype(F32)
        s_pair = _dot(jnp.concatenate(keys(kx_ref, c), axis=0), q_t)
        for hh in range(2):
            s_kq = s_pair[hh * ln:(hh + 1) * ln, :]
            v_t = values_t(vx_ref, c, hh)
            hsum = None
            for dirn in range(2):
                ch = 2 * hh + dirn
                ig, bc = gate_rows(grx_ref, hh, dirn, c)
                u_col = jnp.broadcast_to(ig - bc, (ln, ln)).T
                a = bc + mall_ref[ch * ncx + c][0:1, :]
                dm = jnp.where(causal[dirn], u_col + bc, -jnp.inf)
                mj = jnp.maximum(a, jnp.max(dm, axis=0, keepdims=True))
                sm = (s_kq * jnp.exp(dm - mj)).astype(BF16)
                qw = (q_tf * jnp.exp(a - mj)).astype(BF16)
                num = _dot(jnp.concatenate([v_t, sall_ref[ch * ncx + c]], axis=1),
                           jnp.concatenate([sm, qw], axis=0))
                h = num[:dv, :] * (1.0 / jnp.maximum(jnp.abs(num[dv:dv + 1, :]), jnp.exp(-mj)))
                hsum = h if hsum is None else hsum + h
            hn = hsum * lax.rsqrt(jnp.mean(hsum * hsum, axis=0, keepdims=True) + EPS)
            out_ref[0, pl.ds(pl.multiple_of(c * ln, ln), ln), hh * dv:(hh + 1) * dv] = hn.T.astype(out_ref.dtype)
        return 0

    lax.fori_loop(0, ncx, emit, 0, unroll=min(ncx, SCAN_UNROLL))


def _scan(ctx_parts, lat_parts):
    kc, vc, grc = ctx_parts
    qx, kx, vx, grx = lat_parts
    b, t, qk = kx.shape
    vd = vx.shape[2]
    nh = MLSTM_HEADS
    ln = SCAN_CHUNK
    dv = vd // nh
    kl = 2 * (qk // nh)
    assert kl == ln and dv == ln, "scan kernel assumes key-pair lanes = head value dim = chunk"
    ncx = t // ln

    def specs(k, v, gr):
        tk = k.shape[1]
        return [pl.BlockSpec((1, tk, kl), lambda i, j: (i, 0, j)),
                pl.BlockSpec((1, tk // ln, 2 * dv, ln), lambda i, j: (i, 0, j, 0)),
                pl.BlockSpec((1,) + gr.shape[1:], lambda i, j: (i, 0, 0, 0))]

    return pl.pallas_call(
        functools.partial(_scan_kernel, chunk=ln),
        grid=(b, nh // 2),
        in_specs=specs(kc, vc, grc) + [pl.BlockSpec((1, ncx, kl, ln), lambda i, j: (i, 0, j, 0))] + specs(kx, vx, grx),
        out_specs=pl.BlockSpec((1, t, 2 * dv), lambda i, j: (i, 0, j)),
        out_shape=jax.ShapeDtypeStruct((b, t, vd), BF16),
        scratch_shapes=[pltpu.VMEM((4, 2 * dv, kl), F32),
                        pltpu.VMEM((4 * ncx, 2 * dv, kl), BF16),
                        pltpu.VMEM((4 * ncx, 8, ln), F32)],
        compiler_params=_params(("arbitrary", "arbitrary"), 40),
        name="mlstm_scan",
    )(kc, vc, grc, qx, kx, vx, grx)


def _mlstm_out_kernel(hn_ref, og_ref, x_ref, mod_ref, mn_ref, wout_ref, nffn_ref, wr_ref,
                      x1_ref, hx2_ref, lg_ref):
    mod = mod_ref[0]
    a = (hn_ref[0].astype(F32) * mn_ref[...]) * og_ref[0].astype(F32)
    xn = x_ref[0] + mod[2:3] * _dot(a.astype(BF16), wout_ref[...])
    x1_ref[0] = xn
    hb, lg = _ffn_pre(xn, mod, nffn_ref[...], wr_ref[...])
    hx2_ref[0] = hb
    lg_ref[0] = lg


def _mlstm_out(hn, og, x, mod, mnorm, w_out, nffn, w_router):
    b, t, d = x.shape
    vd = hn.shape[2]
    e = w_router.shape[1]
    tt = min(t, 1024)
    full = lambda shape: pl.BlockSpec(shape, lambda i, j: (0,) * len(shape))
    tok = lambda n: pl.BlockSpec((1, tt, n), lambda i, j: (i, j, 0))
    return pl.pallas_call(
        _mlstm_out_kernel,
        grid=(b, t // tt),
        in_specs=[tok(vd), tok(vd), tok(d), pl.BlockSpec((1, 8, d), lambda i, j: (i, 0, 0)),
                  full((1, vd)), full(w_out.shape), full((1, d)), full(w_router.shape)],
        out_specs=[tok(d), tok(d), pl.BlockSpec((1, e, tt), lambda i, j: (i, 0, j))],
        out_shape=[jax.ShapeDtypeStruct((b, t, d), F32), jax.ShapeDtypeStruct((b, t, d), BF16),
                   jax.ShapeDtypeStruct((b, e, t), F32)],
        compiler_params=_params(("arbitrary", "arbitrary"), 48),
        name="mlstm_out",
    )(hn, og, x, mod, mnorm, w_out, nffn, w_router)


def _chunked_rows(gr):
    b, n, t = gr.shape
    return gr.reshape(b, n, t // SCAN_CHUNK, SCAN_CHUNK)


def kernel(x, c, ctx, c_ctx, ada_w, ada_b, norm_mix, norm_ffn, pool_w, pool_scale, mlstm_w_in, mlstm_b_gates,
           mlstm_norm, mlstm_w_out, moe_router, moe_w_gate, moe_w_up, moe_w_down, final_norm):
    bsz, seq, d = x.shape
    ctx_len = ctx.shape[1]
    depth = ada_w.shape[0]
    n_mixers = 2

    cc = jnp.concatenate([c, c_ctx[None, :], jnp.zeros((16 - bsz - 1, d), F32)], axis=0)
    ada = _ada(cc, ada_w, ada_b).reshape(depth, 16, N_ADA, d)
    pad = jnp.zeros((bsz, 8 - N_ADA, d), F32)

    fn = final_norm.reshape(1, d)
    for i in range(depth):
        last = i == depth - 1
        j = i // n_mixers
        mod_x = jnp.concatenate([ada[i, :bsz], pad], axis=1)
        mod_c = jnp.concatenate([jnp.broadcast_to(ada[i, bsz][None], (bsz, N_ADA, d)), pad], axis=1)
        nmix = norm_mix[i].reshape(1, d)
        nffn = norm_ffn[i].reshape(1, d)
        w_router = moe_router[i]
        streams = []
        if i % n_mixers == 0:
            pw = pool_w[j].astype(BF16)
            ps = pool_scale[j].reshape(1, d)
            x1, hx2, lg = _pool_mixer(x, mod_x, nmix, nffn, pw, ps, w_router, GRID_W, True)
            streams.append((x1, hx2, lg, mod_x))
            if not last:
                c1, hc2, lgc = _pool_mixer(ctx.reshape(1, bsz * ctx_len, d), mod_c[:1], nmix, nffn, pw, ps, w_router,
                                           ctx_len, False)
                lgc = jnp.swapaxes(lgc.reshape(-1, bsz, ctx_len), 0, 1)
                streams.append((c1.reshape(bsz, ctx_len, d), hc2.reshape(bsz, ctx_len, d), lgc, mod_c))
        else:
            qk = mlstm_w_in.shape[2] - 2 * mlstm_w_out.shape[1] - 4 * MLSTM_HEADS
            qk //= 2
            vd = mlstm_w_out.shape[1]
            w_in = mlstm_w_in[j]
            w_ko = jnp.concatenate([w_in[:, qk:2 * qk], w_in[:, 2 * qk + vd:2 * qk + 2 * vd]], axis=1).astype(BF16)
            w_qv = jnp.concatenate([w_in[:, :qk], w_in[:, 2 * qk:2 * qk + vd]], axis=1).astype(BF16)
            w_g = w_in[:, 2 * qk + 2 * vd:]
            b_g = mlstm_b_gates[j].reshape(1, -1)
            proj = lambda s, m: _inproj(s, m, nmix, w_ko, w_qv, w_g, b_g, qk, vd, True)
            kc, vc, grc = _inproj(ctx.reshape(1, bsz * ctx_len, d), mod_c[:1], nmix, w_ko[:, :qk], w_qv[:, qk:],
                                  w_g, b_g, qk, vd, False)
            kc = kc.reshape(bsz, ctx_len, qk)
            vc = vc.reshape(bsz, ctx_len // SCAN_CHUNK, vd, SCAN_CHUNK)
            grc = jnp.swapaxes(grc.reshape(-1, bsz, ctx_len), 0, 1)
            qx, kx, vx, ox, grx = proj(x, mod_x)
            hn = _scan((kc, vc, _chunked_rows(grc)), (qx, kx, vx, _chunked_rows(grx)))
            x1, hx2, lg = _mlstm_out(hn, ox, x, mod_x, mlstm_norm[j].reshape(1, vd),
                                     mlstm_w_out[j].astype(BF16), nffn, w_router)
            streams.append((x1, hx2, lg, mod_x))
            assert last, "context output of the mLSTM mixer is only needed by a following layer"
        outs = _moe(streams, i, moe_w_gate, moe_w_up, moe_w_down, fn, last)
        x = outs[0]
        if not last:
            ctx = outs[1]
    return x
```

```python
import functools

import jax
import jax.numpy as jnp
import numpy as np
from jax import lax
from jax.experimental import pallas as pl
from jax.experimental.pallas import tpu as pltpu

F32 = jnp.float32
BF16 = jnp.bfloat16

GRID_W = 64
EPS = 1e-6
N_ADA = 6
POOL_WINDOWS = (2, 4, 8, 16)
N_POOL_GROUPS = 4
MLSTM_HEADS = 8
N_EXPERTS = 16
CAPACITY_FACTOR = 2

MIB = 1024 * 1024
MXU_DIM = 256
LANES = 128
TOKEN_BLOCK = 256
SLOT_GROUP = 32
PAIRS_PER_DOT = 16
STATIC_DOTS = 2
SCAN_CHUNK = 128
SCAN_UNROLL = 8
POOL_PAD_ROWS = max(POOL_WINDOWS) // 2


def _params(sem, vmem_mib):
    return pltpu.CompilerParams(dimension_semantics=sem, vmem_limit_bytes=vmem_mib * MIB)


def _dot(a, b):
    return jnp.dot(a, b, preferred_element_type=F32)


def _dot_nt(a, b):
    return lax.dot_general(a, b, (((1,), (1,)), ((), ())), preferred_element_type=F32)


def _iota(shape, dim, dtype=jnp.int32):
    return lax.broadcasted_iota(dtype, shape, dim)


def _sigmoid(x):
    return 1.0 / (1.0 + jnp.exp(-x))


def _log_sigmoid(x):
    return jnp.minimum(x, 0.0) - jnp.log1p(jnp.exp(-jnp.abs(x)))


def _norm_mod(x, g, shift, scale):
    inv = lax.rsqrt(jnp.mean(x * x, axis=-1, keepdims=True) + EPS)
    return (x * inv) * (g * (1.0 + scale)) + shift


def _ada_kernel(c_ref, w_ref, b_ref, o_ref):
    c = c_ref[...]
    s = c * _sigmoid(c)
    s_hi, s_lo = _split_bf16(s, 2)
    w_hi, w_lo = _split_bf16(w_ref[0], 2)
    o_ref[0] = _dot(s_hi, w_hi) + (_dot(s_hi, w_lo) + _dot(s_lo, w_hi)) + b_ref[0]


def _ada(cc, ada_w, ada_b):
    depth, d, n = ada_w.shape
    rows = cc.shape[0]
    tn = n // 4
    return pl.pallas_call(
        _ada_kernel,
        grid=(depth, n // tn),
        in_specs=[pl.BlockSpec((rows, d), lambda i, j: (0, 0)),
                  pl.BlockSpec((1, d, tn), lambda i, j: (i, 0, j)),
                  pl.BlockSpec((1, 1, tn), lambda i, j: (i, 0, j))],
        out_specs=pl.BlockSpec((1, rows, tn), lambda i, j: (i, 0, j)),
        out_shape=jax.ShapeDtypeStruct((depth, rows, n), F32),
        compiler_params=_params(("arbitrary", "arbitrary"), 40),
        name="ada",
    )(cc, ada_w, ada_b.reshape(depth, 1, n))


def _split_bf16(x, pieces):
    out = []
    for _ in range(pieces):
        p = x.astype(BF16)
        out.append(p)
        x = x - p.astype(F32)
    return out


def _dot_split(a, b):
    n = b.shape[1]
    a_hi, a_lo = _split_bf16(a, 2)
    b_hi, b_lo = _split_bf16(b, 2)
    hi = _dot(a_hi, jnp.concatenate([b_hi, b_lo], axis=1))
    return hi[:, :n] + (hi[:, n:] + _dot(a_lo, b_hi))


def _transpose_exact(x):
    m = x.shape[1]
    eye = (_iota((m, m), 0) == _iota((m, m), 1)).astype(BF16)
    hi, mid, lo = _split_bf16(x, 3)
    return _dot_nt(eye, hi) + (_dot_nt(eye, mid) + _dot_nt(eye, lo))


def _ffn_pre(xn, mod, nffn, wr):
    h2 = _norm_mod(xn, nffn, mod[3:4], mod[4:5])
    return h2.astype(BF16), _transpose_exact(_dot_split(h2, wr))


def _pool_kernel(x_ref, mod_ref, nmix_ref, nffn_ref, cmat_ref, cnt_ref, pw_ref, ps_ref, wr_ref,
                 x1_ref, hx2_ref, lg_ref, *pad, two_d, tc):
    t, d = x_ref.shape[1], x_ref.shape[2]
    gd = d // N_POOL_GROUPS
    mod = mod_ref[0]
    pad_tok = POOL_PAD_ROWS * GRID_W

    for c0 in range(0, t, tc):
        x1_ref[0, c0:c0 + tc, :] = _norm_mod(x_ref[0, c0:c0 + tc, :], nmix_ref[...], mod[0:1], mod[1:2])

    def hx_of(r0, rn, j):
        return x1_ref[0, r0:r0 + rn, j * gd:(j + 1) * gd]

    if two_d:
        pad_ref, = pad
        pad_ref[0:pad_tok, :] = jnp.zeros((pad_tok, gd), F32)
        pad_ref[pad_tok + t:pad_tok + t + pad_tok, :] = jnp.zeros((pad_tok, gd), F32)

    for j, w in enumerate(POOL_WINDOWS):
        cs = slice(j * gd, (j + 1) * gd)
        cm = cmat_ref[j]
        sums = []
        for b0 in range(0, t, MXU_DIM):
            g = hx_of(b0, MXU_DIM, j)
            g_hi = g.astype(BF16)
            g_lo = (g - g_hi.astype(F32)).astype(BF16)
            csum = _dot(cm, g_hi) + _dot(cm, g_lo)
            if two_d:
                pad_ref[pad_tok + b0:pad_tok + b0 + MXU_DIM, :] = csum
            else:
                sums.append(csum)
        for c0 in range(0, t, tc):
            if two_d:
                tot = None
                for dr in range(-(w // 2), w - w // 2):
                    o = pad_tok + c0 + dr * GRID_W
                    sl = pad_ref[o:o + tc, :]
                    tot = sl if tot is None else tot + sl
            else:
                blocks = sums[c0 // MXU_DIM:(c0 + tc) // MXU_DIM]
                tot = blocks[0] if len(blocks) == 1 else jnp.concatenate(blocks, axis=0)
            mean = tot / cnt_ref[c0:c0 + tc, j:j + 1]
            diff = mean - hx_of(c0, tc, j)
            y = _dot(diff.astype(BF16), pw_ref[j]) * ps_ref[:, cs]
            x1_ref[0, c0:c0 + tc, cs] = x_ref[0, c0:c0 + tc, cs] + mod[2:3, cs] * y

    tail = min(t, 2 * tc)
    for c0 in range(0, t, tail):
        hb, lg = _ffn_pre(x1_ref[0, c0:c0 + tail, :], mod, nffn_ref[...], wr_ref[...])
        hx2_ref[0, c0:c0 + tail, :] = hb
        lg_ref[0, :, c0:c0 + tail] = lg


def _pool_consts(t, seq, two_d):
    cm = np.zeros((len(POOL_WINDOWS), MXU_DIM, MXU_DIM), np.float32)
    cnt = np.zeros((t, len(POOL_WINDOWS)), np.float32)
    n = seq
    assert MXU_DIM % n == 0
    pos = np.arange(MXU_DIM)
    for j, w in enumerate(POOL_WINDOWS):
        col = pos % n
        lo = np.clip(col - w // 2, 0, n)
        hi = np.clip(col + w - w // 2, 0, n)
        same = (pos[:, None] // n) == (pos[None, :] // n)
        cm[j] = (same & (col[None, :] >= lo[:, None]) & (col[None, :] < hi[:, None])).astype(np.float32)
        tt = np.arange(t)
        c = tt % n
        ccnt = np.clip(c + w - w // 2, 0, n) - np.clip(c - w // 2, 0, n)
        if two_d:
            rows = t // n
            r = tt // n
            rcnt = np.clip(r + w - w // 2, 0, rows) - np.clip(r - w // 2, 0, rows)
            cnt[:, j] = ccnt * rcnt
        else:
            cnt[:, j] = ccnt
    return jnp.asarray(cm, BF16), jnp.asarray(cnt)


def _pool_mixer(x, mod, nmix, nffn, pool_w, pool_scale, w_router, seq, two_d):
    b, t, d = x.shape
    gd = d // N_POOL_GROUPS
    assert seq == GRID_W or not two_d, "row-window shifts assume GRID_W tokens per grid row"
    cmat, cnt = _pool_consts(t, seq, two_d)
    tc = min(t, 512)
    pad_tok = POOL_PAD_ROWS * GRID_W
    scratch = []
    if two_d:
        scratch.append(pltpu.VMEM((t + 2 * pad_tok, gd), F32))
    e = w_router.shape[1]
    full = lambda shape: pl.BlockSpec(shape, lambda i: (0,) * len(shape))
    return pl.pallas_call(
        functools.partial(_pool_kernel, two_d=two_d, tc=tc),
        grid=(b,),
        in_specs=[pl.BlockSpec((1, t, d), lambda i: (i, 0, 0)),
                  pl.BlockSpec((1, 8, d), lambda i: (i, 0, 0)),
                  full((1, d)), full((1, d)), full(cmat.shape), full(cnt.shape),
                  full(pool_w.shape), full((1, d)), full(w_router.shape)],
        out_specs=[pl.BlockSpec((1, t, d), lambda i: (i, 0, 0)),
                   pl.BlockSpec((1, t, d), lambda i: (i, 0, 0)),
                   pl.BlockSpec((1, e, t), lambda i: (i, 0, 0))],
        out_shape=[jax.ShapeDtypeStruct((b, t, d), F32),
                   jax.ShapeDtypeStruct((b, t, d), BF16),
                   jax.ShapeDtypeStruct((b, e, t), F32)],
        scratch_shapes=scratch,
        compiler_params=_params(("arbitrary",), 60),
        name="pool_mixer_2d" if two_d else "pool_mixer_1d",
    )(x, mod, nmix, nffn, cmat, cnt, pool_w, pool_scale, w_router)


def _route_kernel(lg_ref, slot_ref, gate_ref, offs_ref, *, cap):
    nb, e, t = lg_ref.shape
    capf = jnp.float32(cap)
    affs = []
    for s in range(nb):
        lg = lg_ref[s]
        ex = jnp.exp(lg - jnp.max(lg, axis=0, keepdims=True))
        affs.append(ex / jnp.sum(ex, axis=0, keepdims=True))

    def as_f32(v):
        return lax.bitcast_convert_type(v, F32)

    def enough(aff, cand):
        return jnp.sum(jnp.where(aff >= as_f32(cand), 1.0, 0.0), axis=1, keepdims=True) >= capf

    def bisect2(i, vs):
        hi = jnp.left_shift(jnp.int32(1), 30 - 2 * i)
        lo = jnp.left_shift(jnp.int32(1), 29 - 2 * i)
        out = []
        for aff, v in zip(affs, vs):
            c_hi, c_lo, c_both = v | hi, v | lo, v | hi | lo
            out.append(jnp.where(enough(aff, c_both), c_both,
                                 jnp.where(enough(aff, c_hi), c_hi, jnp.where(enough(aff, c_lo), c_lo, v))))
        return tuple(out)

    kths = lax.fori_loop(0, 15, bisect2, tuple(jnp.zeros((e, 1), jnp.int32) for _ in range(nb)))

    blk = min(t, MXU_DIM)
    before = (_iota((blk, blk), 0) < _iota((blk, blk), 1)).astype(BF16)
    nl = offs_ref.shape[2]
    starts = (_iota((t, nl), 0) < _iota((t, nl), 1) * TOKEN_BLOCK).astype(BF16)

    def excl_cumsum(mask):
        ones = jnp.where(mask, 1.0, 0.0)
        outs, run = [], jnp.zeros((e, 1), F32)
        for b0 in range(0, t, blk):
            mb = ones[:, b0:b0 + blk]
            outs.append(_dot(mb.astype(BF16), before) + run)
            run = run + jnp.sum(mb, axis=1, keepdims=True)
        return jnp.concatenate(outs, axis=1) if len(outs) > 1 else outs[0]

    for s, (aff, kth) in enumerate(zip(affs, kths)):
        kth = jnp.where(enough(aff, kth | 1), kth | 1, kth)
        above = as_f32(kth + 1)
        gt = aff >= above
        eq = (aff >= as_f32(kth)) & jnp.logical_not(gt)
        need = capf - jnp.sum(jnp.where(gt, 1.0, 0.0), axis=1, keepdims=True)
        sel = gt | (eq & (excl_cumsum(eq) < need))
        slot_ref[s] = jnp.where(sel, excl_cumsum(sel), -1.0)
        gate_ref[s] = aff
        offs_ref[s] = _dot(jnp.where(sel, 1.0, 0.0).astype(BF16), starts)


def _route(logits, cap):
    b, e, t = logits.shape
    nb = 2 if b % 2 == 0 else 1
    spec = pl.BlockSpec((nb, e, t), lambda i: (i, 0, 0))
    ospec = pl.BlockSpec((nb, e, LANES), lambda i: (i, 0, 0))
    return pl.pallas_call(
        functools.partial(_route_kernel, cap=cap),
        grid=(b // nb,),
        in_specs=[spec],
        out_specs=[spec, spec, ospec],
        out_shape=[jax.ShapeDtypeStruct((b, e, t), F32)] * 2 + [jax.ShapeDtypeStruct((b, e, LANES), F32)],
        compiler_params=_params(("arbitrary",), 32),
        name="route",
    )(logits)


def _gather_kernel(h_ref, slot_ref, o_ref, *, cap):
    hx = h_ref[0]
    t = hx.shape[0]
    row = _iota((cap, t), 0).astype(F32)
    ne = o_ref.shape[0]
    onehot = jnp.concatenate([jnp.where(slot_ref[0, i:i + 1, :] == row, 1.0, 0.0).astype(BF16) for i in range(ne)],
                             axis=0)
    o_ref[...] = _dot(onehot, hx).astype(BF16).reshape(o_ref.shape)


def _gather(hx2, slot, cap):
    b, t, d = hx2.shape
    e = slot.shape[1]
    eb = 8
    return pl.pallas_call(
        functools.partial(_gather_kernel, cap=cap),
        grid=(b, e // eb),
        in_specs=[pl.BlockSpec((1, t, d), lambda i, j: (i, 0, 0)),
                  pl.BlockSpec((1, eb, t), lambda i, j: (i, j, 0))],
        out_specs=pl.BlockSpec((eb, cap, d), lambda i, j: (j, i, 0)),
        out_shape=jax.ShapeDtypeStruct((e, b * cap, d), BF16),
        compiler_params=_params(("arbitrary", "arbitrary"), 48),
        name="gather",
    )(hx2, slot)


def _pair_list(offs_ref, kb, n_exp, nblk, le_ref, lg_ref):
    b = pl.program_id(0)
    shift = SLOT_GROUP.bit_length() - 1
    max_groups = le_ref.shape[0] // (n_exp + 1)

    cnt = jnp.int32(0)
    for e in range(n_exp):
        o0 = offs_ref[b, e * (nblk + 1) + kb]
        o1 = offs_ref[b, e * (nblk + 1) + kb + 1]
        lo = jnp.right_shift(o0, shift)
        ng = jnp.where(o1 > o0, jnp.right_shift(o1 - 1, shift) - lo + 1, 0)
        for i in range(max_groups):
            le_ref[cnt + i] = e
            lg_ref[cnt + i] = lo + i
        cnt = cnt + ng
    return cnt


def _pair(le_ref, lg_ref, idx, cnt):
    valid = idx < cnt
    safe = jnp.minimum(idx, jnp.maximum(cnt - 1, 0))
    e_p = jnp.where(valid, le_ref[safe], 0)
    g_p = jnp.where(valid, lg_ref[safe], 0)
    base = jnp.where(valid, g_p * SLOT_GROUP, -2 * SLOT_GROUP).astype(F32)
    return e_p, pl.multiple_of(g_p * SLOT_GROUP, SLOT_GROUP), base


def _for_each_dot(cnt, body):
    for c in range(STATIC_DOTS):
        body(c, 0)
    lax.fori_loop(STATIC_DOTS, (cnt + PAIRS_PER_DOT - 1) // PAIRS_PER_DOT, body, 0)


def _sparse_gather_kernel(offs_ref, h_ref, slot_ref, o_ref, le_ref, lg_ref, p_ref, *, n_exp, nblk):
    sg = SLOT_GROUP
    o_ref[...] = jnp.zeros(o_ref.shape, o_ref.dtype)
    sub = _iota((sg, TOKEN_BLOCK), 0).astype(F32)

    def per_block(kb, _):
        cnt = _pair_list(offs_ref, kb, n_exp, nblk, le_ref, lg_ref)
        tok = pl.ds(pl.multiple_of(kb * TOKEN_BLOCK, TOKEN_BLOCK), TOKEN_BLOCK)

        def dot_batch(c, _):
            dst = []
            for p in range(PAIRS_PER_DOT):
                e_p, s0, base = _pair(le_ref, lg_ref, c * PAIRS_PER_DOT + p, cnt)
                row = slot_ref[0, kb, pl.ds(e_p, 1), :]
                p_ref[p * sg:(p + 1) * sg, :] = jnp.where(row == base + sub, 1.0, 0.0).astype(BF16)
                dst.append((e_p, s0))
            z = _dot(p_ref[...], h_ref[0, tok, :])
            for p, (e_p, s0) in enumerate(dst):
                o_ref[e_p, pl.ds(s0, sg), :] += z[p * sg:(p + 1) * sg, :].astype(o_ref.dtype)
            return 0

        _for_each_dot(cnt, dot_batch)
        return 0

    lax.fori_loop(0, nblk, per_block, 0)


def _by_token_block(a):
    b, e, t = a.shape
    return jnp.swapaxes(a.reshape(b, e, t // TOKEN_BLOCK, TOKEN_BLOCK), 1, 2)


def _sparse_gather(hx2, slot_blocks, offs, cap):
    b, t, d = hx2.shape
    _, nblk, e, _ = slot_blocks.shape
    return pl.pallas_call(
        functools.partial(_sparse_gather_kernel, n_exp=e, nblk=nblk),
        grid_spec=pltpu.PrefetchScalarGridSpec(
            num_scalar_prefetch=1,
            grid=(b,),
            in_specs=[pl.BlockSpec((1, t, d), lambda i, o: (i, 0, 0)),
                      pl.BlockSpec((1, nblk, e, TOKEN_BLOCK), lambda i, o: (i, 0, 0, 0))],
            out_specs=pl.BlockSpec((e, cap, d), lambda i, o: (0, i, 0)),
            scratch_shapes=[pltpu.SMEM(((e + 1) * (cap // SLOT_GROUP),), jnp.int32),
                            pltpu.SMEM(((e + 1) * (cap // SLOT_GROUP),), jnp.int32),
                            pltpu.VMEM((PAIRS_PER_DOT * SLOT_GROUP, TOKEN_BLOCK), BF16)]),
        out_shape=jax.ShapeDtypeStruct((e, b * cap, d), BF16),
        compiler_params=_params(("arbitrary",), 40),
        name="sparse_gather",
    )(offs, hx2, slot_blocks)


def _sparse_scatter_kernel(offs_ref, x_ref, y_ref, slot_ref, gate_ref, mod_ref, fn_ref, o_ref,
                           le_ref, lg_ref, p_ref, yc_ref, acc_ref, *, n_exp, nblk, bps, final):
    sg = SLOT_GROUP
    sub = _iota((sg, TOKEN_BLOCK), 0).astype(F32)

    def per_block(i, _):
        cnt = _pair_list(offs_ref, pl.program_id(1) * bps + i, n_exp, nblk, le_ref, lg_ref)
        tok = pl.ds(pl.multiple_of(i * TOKEN_BLOCK, TOKEN_BLOCK), TOKEN_BLOCK)
        acc_ref[...] = jnp.zeros(acc_ref.shape, F32)

        def dot_batch(c, _):
            for p in range(PAIRS_PER_DOT):
                e_p, s0, base = _pair(le_ref, lg_ref, c * PAIRS_PER_DOT + p, cnt)
                row = slot_ref[0, i, pl.ds(e_p, 1), :]
                gate = gate_ref[0, i, pl.ds(e_p, 1), :]
                p_ref[p * sg:(p + 1) * sg, :] = jnp.where(row == base + sub, gate, 0.0)
                yc_ref[p * sg:(p + 1) * sg, :] = y_ref[e_p, pl.ds(s0, sg), :]
            acc_ref[...] += _dot(p_ref[...].T.astype(BF16), yc_ref[...])
            return 0

        _for_each_dot(cnt, dot_batch)
        out = x_ref[0, tok, :] + mod_ref[0][5:6] * acc_ref[...]
        if final:
            out = out * lax.rsqrt(jnp.mean(out * out, axis=-1, keepdims=True) + EPS) * fn_ref[...]
        o_ref[0, tok, :] = out
        return 0

    lax.fori_loop(0, bps, per_block, 0)


def _sparse_scatter(x, y, slot_blocks, gate_blocks, offs, mod, final_norm, cap, final):
    b, t, d = x.shape
    _, nblk, e, _ = slot_blocks.shape
    bps = min(nblk, 4)
    rows = PAIRS_PER_DOT * SLOT_GROUP
    tok = lambda n: pl.BlockSpec((1, bps * TOKEN_BLOCK, n), lambda i, j, o: (i, j, 0))
    exp = pl.BlockSpec((1, bps, e, TOKEN_BLOCK), lambda i, j, o: (i, j, 0, 0))
    return pl.pallas_call(
        functools.partial(_sparse_scatter_kernel, n_exp=e, nblk=nblk, bps=bps, final=final),
        grid_spec=pltpu.PrefetchScalarGridSpec(
            num_scalar_prefetch=1,
            grid=(b, nblk // bps),
            in_specs=[tok(d), pl.BlockSpec((e, cap, d), lambda i, j, o: (0, i, 0)), exp, exp,
                      pl.BlockSpec((1, 8, d), lambda i, j, o: (i, 0, 0)),
                      pl.BlockSpec((1, d), lambda i, j, o: (0, 0))],
            out_specs=tok(d),
            scratch_shapes=[pltpu.SMEM(((e + 1) * (cap // SLOT_GROUP),), jnp.int32),
                            pltpu.SMEM(((e + 1) * (cap // SLOT_GROUP),), jnp.int32),
                            pltpu.VMEM((rows, TOKEN_BLOCK), F32),
                            pltpu.VMEM((rows, d), BF16),
                            pltpu.VMEM((TOKEN_BLOCK, d), F32)]),
        out_shape=jax.ShapeDtypeStruct((b, t, d), F32),
        compiler_params=_params(("arbitrary", "arbitrary"), 48),
        name="sparse_scatter",
    )(offs, x, y, slot_blocks, gate_blocks, mod, final_norm)


def _ffn_kernel(*refs, n_sets, mc):
    x_refs = refs[:n_sets]
    wg_ref, wu_ref, wd_ref = refs[n_sets:n_sets + 3]
    y_refs = refs[n_sets + 3:2 * n_sets + 3]
    acc_refs = refs[2 * n_sets + 3:3 * n_sets + 3]
    f = pl.program_id(1)
    last = f == pl.num_programs(1) - 1

    @pl.when((pl.program_id(0) == 0) & (f == 0))
    def _():
        for acc in acc_refs:
            acc[...] = jnp.zeros(acc.shape, F32)

    for x_ref, y_ref, acc in zip(x_refs, y_refs, acc_refs):
        m = x_ref.shape[1]
        step = min(m, mc)
        for m0 in range(0, m, step):
            xs = x_ref[0, m0:m0 + step, :].astype(F32)
            hg = _dot(xs, wg_ref[0, 0])
            hu = _dot(xs, wu_ref[0, 0])
            hid = hg * _sigmoid(hg) * hu
            tot = acc[m0:m0 + step, :] + _dot(hid, wd_ref[0, 0])
            acc[m0:m0 + step, :] = jnp.where(last, 0.0, tot)
            y_ref[0, m0:m0 + step, :] = tot.astype(BF16)


def _expert_ffn(xs, layer, w_gate, w_up, w_down):
    _, e, d, hidden = w_gate.shape
    tf = 512
    n = len(xs)
    xspecs = [pl.BlockSpec((1, x.shape[1], d), lambda i, j: (i, 0, 0)) for x in xs]
    return pl.pallas_call(
        functools.partial(_ffn_kernel, n_sets=n, mc=1024),
        grid=(e, hidden // tf),
        in_specs=xspecs + [pl.BlockSpec((1, 1, d, tf), lambda i, j: (layer, i, 0, j)),
                           pl.BlockSpec((1, 1, d, tf), lambda i, j: (layer, i, 0, j)),
                           pl.BlockSpec((1, 1, tf, d), lambda i, j: (layer, i, j, 0))],
        out_specs=xspecs,
        out_shape=[jax.ShapeDtypeStruct(x.shape, BF16) for x in xs],
        scratch_shapes=[pltpu.VMEM((x.shape[1], d), F32) for x in xs],
        compiler_params=_params(("arbitrary", "arbitrary"), 60),
        name="expert_ffn",
    )(*xs, w_gate, w_up, w_down)


def _scatter_kernel(x_ref, y_ref, slot_ref, gate_ref, mod_ref, fn_ref, o_ref, *, final):
    tt = x_ref.shape[1]
    e, cap, _ = y_ref.shape
    eye = (_iota((tt, tt), 0) == _iota((tt, tt), 1)).astype(BF16)
    slot_t = _dot_nt(eye, (slot_ref[0] + 1.0).astype(BF16))
    gate_t = _dot_nt(eye, gate_ref[0].astype(BF16))
    assert cap & (cap - 1) == 0
    shift = cap.bit_length() - 1
    spread = (jnp.right_shift(_iota((e, e * cap), 1), shift) == _iota((e, e * cap), 0)).astype(BF16)
    slot_l = _dot(slot_t.astype(BF16), spread)
    gate_l = _dot(gate_t.astype(BF16), spread)
    lane = (_iota((tt, e * cap), 1) & (cap - 1)).astype(F32) + 1.0
    pt = jnp.where(slot_l == lane, gate_l, 0.0).astype(BF16)
    acc = _dot(pt, y_ref[...].reshape(e * cap, y_ref.shape[2]))
    out = x_ref[0] + mod_ref[0][5:6] * acc
    if final:
        out = out * lax.rsqrt(jnp.mean(out * out, axis=-1, keepdims=True) + EPS) * fn_ref[...]
    o_ref[0] = out


def _scatter(x, y, slot, gate, mod, final_norm, cap, final):
    b, t, d = x.shape
    e = slot.shape[1]
    tt = min(t, 512)
    return pl.pallas_call(
        functools.partial(_scatter_kernel, final=final),
        grid=(b, t // tt),
        in_specs=[pl.BlockSpec((1, tt, d), lambda i, j: (i, j, 0)),
                  pl.BlockSpec((e, cap, d), lambda i, j: (0, i, 0)),
                  pl.BlockSpec((1, e, tt), lambda i, j: (i, 0, j)),
                  pl.BlockSpec((1, e, tt), lambda i, j: (i, 0, j)),
                  pl.BlockSpec((1, 8, d), lambda i, j: (i, 0, 0)),
                  pl.BlockSpec((1, d), lambda i, j: (0, 0))],
        out_specs=pl.BlockSpec((1, tt, d), lambda i, j: (i, j, 0)),
        out_shape=jax.ShapeDtypeStruct((b, t, d), F32),
        compiler_params=_params(("arbitrary", "arbitrary"), 48),
        name="scatter",
    )(x, y, slot, gate, mod, final_norm)


def _moe(streams, layer, w_gate, w_up, w_down, final_norm, final):
    routed = []
    for x1, hx2, logits, mod in streams:
        t = x1.shape[1]
        cap = CAPACITY_FACTOR * t // N_EXPERTS
        slot, gate, offs = _route(logits, cap)
        nblk = t // TOKEN_BLOCK
        if nblk > 1:
            offs = offs[:, :, :nblk + 1].astype(jnp.int32).reshape(offs.shape[0], -1)
            slot, gate = _by_token_block(slot), _by_token_block(gate)
            xg = _sparse_gather(hx2, slot, offs, cap)
        else:
            offs = None
            xg = _gather(hx2, slot, cap)
        routed.append((slot, gate, offs, cap, xg))
    ys = _expert_ffn([r[4] for r in routed], layer, w_gate, w_up, w_down)
    outs = []
    for (x1, _, _, mod), (slot, gate, offs, cap, _), y in zip(streams, routed, ys):
        if offs is None:
            outs.append(_scatter(x1, y, slot, gate, mod, final_norm, cap, final))
        else:
            outs.append(_sparse_scatter(x1, y, slot, gate, offs, mod, final_norm, cap, final))
    return outs


def _inproj_kernel(x_ref, mod_ref, nmix_ref, wko_ref, wqv_ref, wg_ref, bg_ref, *out_refs, chunk, qk, dk, queries):
    if queries:
        qt_ref, k_ref, vt_ref, o_ref, gr_ref = out_refs
    else:
        k_ref, vt_ref, gr_ref = out_refs
    tt = x_ref.shape[1]
    mod = mod_ref[0]
    h = _norm_mod(x_ref[0], nmix_ref[...], mod[0:1], mod[1:2])
    hb = h.astype(BF16)
    p = _dot(hb, wko_ref[...])
    k_ref[0] = (p[:, :qk] * (dk ** -0.5)).astype(BF16)
    if queries:
        o_ref[0] = _sigmoid(p[:, qk:]).astype(o_ref.dtype)
    p_t = lax.dot_general(wqv_ref[...], hb, (((0,), (1,)), ((), ())), preferred_element_type=F32)
    nq = qk if queries else 0
    for ci in range(tt // chunk):
        if queries:
            qt_ref[0, ci] = p_t[:qk, ci * chunk:(ci + 1) * chunk].astype(BF16)
        vt_ref[0, ci] = p_t[nq:, ci * chunk:(ci + 1) * chunk].astype(BF16)

    nh = MLSTM_HEADS
    g_t = _transpose_exact(_dot_split(h, wg_ref[...]) + bg_ref[...])
    row = _iota((4 * nh, chunk), 0)
    is_f = (row & nh) == nh
    a = _iota((chunk, chunk), 0)
    c = _iota((chunk, chunk), 1)
    tri = jnp.concatenate([(a <= c).astype(BF16), (a >= c).astype(BF16)], axis=1)
    ng = 4 * nh
    for c0 in range(0, tt, chunk):
        gc = g_t[:, c0:c0 + chunk]
        pieces = _split_bf16(jnp.where(is_f, _log_sigmoid(gc), 0.0), 3)
        cum = _dot(jnp.concatenate(pieces, axis=0), tri)
        cum = cum[:ng] + (cum[ng:2 * ng] + cum[2 * ng:])
        gr_ref[0, :, c0:c0 + chunk] = jnp.where(is_f, jnp.where(row < 2 * nh, cum[:, :chunk], cum[:, chunk:]), gc)


def _inproj(x, mod, nmix, w_ko, w_qv, w_g, b_g, qk, vd, queries):
    b, t, d = x.shape
    tt = min(t, 1024)
    ng = w_g.shape[1]
    ln = SCAN_CHUNK
    full = lambda shape: pl.BlockSpec(shape, lambda i, j: (0,) * len(shape))
    tok = lambda n: pl.BlockSpec((1, tt, n), lambda i, j: (i, j, 0))
    slab = lambda n: pl.BlockSpec((1, tt // ln, n, ln), lambda i, j: (i, j, 0, 0))
    outs = [(slab(qk), jax.ShapeDtypeStruct((b, t // ln, qk, ln), BF16))] if queries else []
    outs += [(tok(qk), jax.ShapeDtypeStruct((b, t, qk), BF16)),
             (slab(vd), jax.ShapeDtypeStruct((b, t // ln, vd, ln), BF16))]
    outs += [(tok(vd), jax.ShapeDtypeStruct((b, t, vd), BF16))] if queries else []
    outs += [(pl.BlockSpec((1, ng, tt), lambda i, j: (i, 0, j)), jax.ShapeDtypeStruct((b, ng, t), F32))]
    return pl.pallas_call(
        functools.partial(_inproj_kernel, chunk=ln, qk=qk, dk=qk // MLSTM_HEADS, queries=queries),
        grid=(b, t // tt),
        in_specs=[tok(d), pl.BlockSpec((1, 8, d), lambda i, j: (i, 0, 0)), full((1, d)),
                  full(w_ko.shape), full(w_qv.shape), full(w_g.shape), full(b_g.shape)],
        out_specs=[o[0] for o in outs],
        out_shape=[o[1] for o in outs],
        compiler_params=_params(("arbitrary", "arbitrary"), 56),
        name="mlstm_inproj",
    )(x, mod, nmix, w_ko, w_qv, w_g, b_g)


def _scan_kernel(kc_ref, vc_ref, grc_ref, qx_ref, kx_ref, vx_ref, grx_ref,
                 out_ref, s_ref, sall_ref, mall_ref, *, chunk):
    ln = chunk
    nh = MLSTM_HEADS
    pair = pl.program_id(1)
    kl = kx_ref.shape[2]
    dv = vx_ref.shape[2] // 2
    lane = _iota((ln, kl), 1)
    kmask = (lane < kl // 2, lane >= kl // 2)
    klane = _iota((2 * dv, kl), 1)
    kcols = (klane < kl // 2, klane >= kl // 2)
    ones = jnp.ones((dv, ln), BF16)
    si = _iota((ln, ln), 0)
    ji = _iota((ln, ln), 1)
    causal = (si <= ji, si >= ji)
    ncc = kc_ref.shape[1] // ln
    ncx = kx_ref.shape[1] // ln

    def gate_rows(gr_ref, hh, dirn, c):
        base = 2 * nh * dirn + 2 * pair + hh
        return gr_ref[0, base, pl.ds(c, 1), :], gr_ref[0, base + nh, pl.ds(c, 1), :]

    def keys(k_ref, c):
        k = k_ref[0, pl.ds(pl.multiple_of(c * ln, ln), ln), :]
        return [jnp.where(kmask[hh], k, jnp.zeros((), BF16)) for hh in range(2)]

    def values_t(v_ref, c, hh):
        return jnp.concatenate([v_ref[0, c, hh * dv:(hh + 1) * dv, :], ones], axis=0)

    def advance(refs, n, record):
        k_ref, v_ref, gr_ref = refs

        def body(i, ms):
            new_ms = [None] * 4
            cs = (i, n - 1 - i)
            for dirn in range(2):
                c = cs[dirn]
                vws, decays = [], []
                for hh in range(2):
                    ch = 2 * hh + dirn
                    ig, bc = gate_rows(gr_ref, hh, dirn, c)
                    m = ms[ch]
                    if record:
                        s_own = jnp.where(kcols[hh], s_ref[ch], 0.0)
                        sall_ref[ch * ncx + c] = s_own.astype(BF16)
                        mall_ref[ch * ncx + c] = jnp.broadcast_to(m, (8, ln))
                    b_end = bc[:, ln - 1:ln] if dirn == 0 else bc[:, 0:1]
                    gl = b_end - bc + ig
                    m_new = jnp.maximum(b_end + m, jnp.max(gl, axis=1, keepdims=True))
                    vws.append((values_t(v_ref, c, hh) * jnp.exp(gl - m_new)).astype(BF16))
                    decays.append(jnp.exp(b_end + m - m_new))
                    new_ms[ch] = m_new
                upd = _dot(jnp.concatenate(vws, axis=0), k_ref[0, pl.ds(pl.multiple_of(c * ln, ln), ln), :])
                for hh in range(2):
                    ch = 2 * hh + dirn
                    s_ref[ch] = decays[hh] * s_ref[ch] + upd[hh * 2 * dv:(hh + 1) * 2 * dv, :]
            return tuple(new_ms)

        return body

    s_ref[...] = jnp.zeros(s_ref.shape, F32)
    ms = tuple(jnp.zeros((1, 1), F32) for _ in range(4))
    ms = lax.fori_loop(0, ncc, advance((kc_ref, vc_ref, grc_ref), ncc, False), ms, unroll=min(ncc, SCAN_UNROLL))
    lax.fori_loop(0, ncx, advance((kx_ref, vx_ref, grx_ref), ncx, True), ms, unroll=min(ncx, SCAN_UNROLL))

    def emit(c, _):
        q_t = qx_ref[0, c]
        q_tf = q_t.astype(F32)
        s_pair = _dot(jnp.concatenate(keys(kx_ref, c), axis=0), q_t)
        for hh in range(2):
            s_kq = s_pair[hh * ln:(hh + 1) * ln, :]
            v_t = values_t(vx_ref, c, hh)
            hsum = None
            for dirn in range(2):
                ch = 2 * hh + dirn
                ig, bc = gate_rows(grx_ref, hh, dirn, c)
                u_col = jnp.broadcast_to(ig - bc, (ln, ln)).T
                a = bc + mall_ref[ch * ncx + c][0:1, :]
                dm = jnp.where(causal[dirn], u_col + bc, -jnp.inf)
                mj = jnp.maximum(a, jnp.max(dm, axis=0, keepdims=True))
                sm = (s_kq * jnp.exp(dm - mj)).astype(BF16)
                qw = (q_tf * jnp.exp(a - mj)).astype(BF16)
                num = _dot(jnp.concatenate([v_t, sall_ref[ch * ncx + c]], axis=1),
                           jnp.concatenate([sm, qw], axis=0))
                h = num[:dv, :] * (1.0 / jnp.maximum(jnp.abs(num[dv:dv + 1, :]), jnp.exp(-mj)))
                hsum = h if hsum is None else hsum + h
            hn = hsum * lax.rsqrt(jnp.mean(hsum * hsum, axis=0, keepdims=True) + EPS)
            out_ref[0, pl.ds(pl.multiple_of(c * ln, ln), ln), hh * dv:(hh + 1) * dv] = hn.T.astype(out_ref.dtype)
        return 0

    lax.fori_loop(0, ncx, emit, 0, unroll=min(ncx, SCAN_UNROLL))


def _scan(ctx_parts, lat_parts):
    kc, vc, grc = ctx_parts
    qx, kx, vx, grx = lat_parts
    b, t, qk = kx.shape
    vd = vx.shape[2]
    nh = MLSTM_HEADS
    ln = SCAN_CHUNK
    dv = vd // nh
    kl = 2 * (qk // nh)
    assert kl == ln and dv == ln, "scan kernel assumes key-pair lanes = head value dim = chunk"
    ncx = t // ln

    def specs(k, v, gr):
        tk = k.shape[1]
        return [pl.BlockSpec((1, tk, kl), lambda i, j: (i, 0, j)),
                pl.BlockSpec((1, tk // ln, 2 * dv, ln), lambda i, j: (i, 0, j, 0)),
                pl.BlockSpec((1,) + gr.shape[1:], lambda i, j: (i, 0, 0, 0))]

    return pl.pallas_call(
        functools.partial(_scan_kernel, chunk=ln),
        grid=(b, nh // 2),
        in_specs=specs(kc, vc, grc) + [pl.BlockSpec((1, ncx, kl, ln), lambda i, j: (i, 0, j, 0))] + specs(kx, vx, grx),
        out_specs=pl.BlockSpec((1, t, 2 * dv), lambda i, j: (i, 0, j)),
        out_shape=jax.ShapeDtypeStruct((b, t, vd), BF16),
        scratch_shapes=[pltpu.VMEM((4, 2 * dv, kl), F32),
                        pltpu.VMEM((4 * ncx, 2 * dv, kl), BF16),
                        pltpu.VMEM((4 * ncx, 8, ln), F32)],
        compiler_params=_params(("arbitrary", "arbitrary"), 40),
        name="mlstm_scan",
    )(kc, vc, grc, qx, kx, vx, grx)


def _mlstm_out_kernel(hn_ref, og_ref, x_ref, mod_ref, mn_ref, wout_ref, nffn_ref, wr_ref,
                      x1_ref, hx2_ref, lg_ref):
    mod = mod_ref[0]
    a = (hn_ref[0].astype(F32) * mn_ref[...]) * og_ref[0].astype(F32)
    xn = x_ref[0] + mod[2:3] * _dot(a.astype(BF16), wout_ref[...])
    x1_ref[0] = xn
    hb, lg = _ffn_pre(xn, mod, nffn_ref[...], wr_ref[...])
    hx2_ref[0] = hb
    lg_ref[0] = lg


def _mlstm_out(hn, og, x, mod, mnorm, w_out, nffn, w_router):
    b, t, d = x.shape
    vd = hn.shape[2]
    e = w_router.shape[1]
    tt = min(t, 1024)
    full = lambda shape: pl.BlockSpec(shape, lambda i, j: (0,) * len(shape))
    tok = lambda n: pl.BlockSpec((1, tt, n), lambda i, j: (i, j, 0))
    return pl.pallas_call(
        _mlstm_out_kernel,
        grid=(b, t // tt),
        in_specs=[tok(vd), tok(vd), tok(d), pl.BlockSpec((1, 8, d), lambda i, j: (i, 0, 0)),
                  full((1, vd)), full(w_out.shape), full((1, d)), full(w_router.shape)],
        out_specs=[tok(d), tok(d), pl.BlockSpec((1, e, tt), lambda i, j: (i, 0, j))],
        out_shape=[jax.ShapeDtypeStruct((b, t, d), F32), jax.ShapeDtypeStruct((b, t, d), BF16),
                   jax.ShapeDtypeStruct((b, e, t), F32)],
        compiler_params=_params(("arbitrary", "arbitrary"), 48),
        name="mlstm_out",
    )(hn, og, x, mod, mnorm, w_out, nffn, w_router)


def _chunked_rows(gr):
    b, n, t = gr.shape
    return gr.reshape(b, n, t // SCAN_CHUNK, SCAN_CHUNK)


def kernel(x, c, ctx, c_ctx, ada_w, ada_b, norm_mix, norm_ffn, pool_w, pool_scale, mlstm_w_in, mlstm_b_gates,
           mlstm_norm, mlstm_w_out, moe_router, moe_w_gate, moe_w_up, moe_w_down, final_norm):
    bsz, seq, d = x.shape
    ctx_len = ctx.shape[1]
    depth = ada_w.shape[0]
    n_mixers = 2

    cc = jnp.concatenate([c, c_ctx[None, :], jnp.zeros((16 - bsz - 1, d), F32)], axis=0)
    ada = _ada(cc, ada_w, ada_b).reshape(depth, 16, N_ADA, d)
    pad = jnp.zeros((bsz, 8 - N_ADA, d), F32)

    fn = final_norm.reshape(1, d)
    for i in range(depth):
        last = i == depth - 1
        j = i // n_mixers
        mod_x = jnp.concatenate([ada[i, :bsz], pad], axis=1)
        mod_c = jnp.concatenate([jnp.broadcast_to(ada[i, bsz][None], (bsz, N_ADA, d)), pad], axis=1)
        nmix = norm_mix[i].reshape(1, d)
        nffn = norm_ffn[i].reshape(1, d)
        w_router = moe_router[i]
        streams = []
        if i % n_mixers == 0:
            pw = pool_w[j].astype(BF16)
            ps = pool_scale[j].reshape(1, d)
            x1, hx2, lg = _pool_mixer(x, mod_x, nmix, nffn, pw, ps, w_router, GRID_W, True)
            streams.append((x1, hx2, lg, mod_x))
            if not last:
                c1, hc2, lgc = _pool_mixer(ctx.reshape(1, bsz * ctx_len, d), mod_c[:1], nmix, nffn, pw, ps, w_router,
                                           ctx_len, False)
                lgc = jnp.swapaxes(lgc.reshape(-1, bsz, ctx_len), 0, 1)
                streams.append((c1.reshape(bsz, ctx_len, d), hc2.reshape(bsz, ctx_len, d), lgc, mod_c))
        else:
            qk = mlstm_w_in.shape[2] - 2 * mlstm_w_out.shape[1] - 4 * MLSTM_HEADS
            qk //= 2
            vd = mlstm_w_out.shape[1]
            w_in = mlstm_w_in[j]
            w_ko = jnp.concatenate([w_in[:, qk:2 * qk], w_in[:, 2 * qk + vd:2 * qk + 2 * vd]], axis=1).astype(BF16)
            w_qv = jnp.concatenate([w_in[:, :qk], w_in[:, 2 * qk:2 * qk + vd]], axis=1).astype(BF16)
            w_g = w_in[:, 2 * qk + 2 * vd:]
            b_g = mlstm_b_gates[j].reshape(1, -1)
            proj = lambda s, m: _inproj(s, m, nmix, w_ko, w_qv, w_g, b_g, qk, vd, True)
            kc, vc, grc = _inproj(ctx.reshape(1, bsz * ctx_len, d), mod_c[:1], nmix, w_ko[:, :qk], w_qv[:, qk:],
                                  w_g, b_g, qk, vd, False)
            kc = kc.reshape(bsz, ctx_len, qk)
            vc = vc.reshape(bsz, ctx_len // SCAN_CHUNK, vd, SCAN_CHUNK)
            grc = jnp.swapaxes(grc.reshape(-1, bsz, ctx_len), 0, 1)
            qx, kx, vx, ox, grx = proj(x, mod_x)
            hn = _scan((kc, vc, _chunked_rows(grc)), (qx, kx, vx, _chunked_rows(grx)))
            x1, hx2, lg = _mlstm_out(hn, ox, x, mod_x, mlstm_norm[j].reshape(1, vd),
                                     mlstm_w_out[j].astype(BF16), nffn, w_router)
            streams.append((x1, hx2, lg, mod_x))
            assert last, "context output of the mLSTM mixer is only needed by a following layer"
        outs = _moe(streams, i, moe_w_gate, moe_w_up, moe_w_down, fn, last)
        x = outs[0]
        if not last:
            ctx = outs[1]
    return x
```
